```python
import jax, jax.numpy as jnp
from jax import lax
import numpy as np

D_MODEL = 1024
BATCH = 8
SEQ = 2048
DEPTH = 1
DEC_BATCH = 128
DEC_SEQ = 8
PAST_LEN = 16384
PAGE_SIZE = 128

MIX_WIDTH = D_MODEL
GM_WIDTH = MIX_WIDTH // 2
ML_WIDTH = MIX_WIDTH - GM_WIDTH
GM_HEADS = 4
GM_HEAD_DIM = GM_WIDTH // GM_HEADS
ML_HEADS = 4
ML_HEAD_DIM = ML_WIDTH // ML_HEADS
CHUNK = 128
ML_CHUNK = 128
D_FF = 4 * D_MODEL
PLE_DIM = 256
EPS = 1e-6
SPLITS = (GM_WIDTH, 2 * GM_WIDTH, 2 * GM_WIDTH + ML_WIDTH, 2 * GM_WIDTH + 2 * ML_WIDTH,
          2 * GM_WIDTH + 3 * ML_WIDTH, 2 * GM_WIDTH + 4 * ML_WIDTH, 2 * GM_WIDTH + 4 * ML_WIDTH + ML_HEADS)
IN_COLS = 2 * GM_WIDTH + 4 * ML_WIDTH + 2 * ML_HEADS

kernel_name = "hymba_gmlp_mlstm_decoder_step"


def rmsnorm(x, g):
    xf = x.astype(jnp.float32)
    y = xf * lax.rsqrt(jnp.mean(xf * xf, axis=-1, keepdims=True) + EPS)
    return (y * g.astype(jnp.float32)).astype(x.dtype)


def layernorm(x, g, b):
    xf = x.astype(jnp.float32)
    mu = jnp.mean(xf, axis=-1, keepdims=True)
    var = jnp.mean(jnp.square(xf - mu), axis=-1, keepdims=True)
    y = (xf - mu) * lax.rsqrt(var + EPS)
    return (y * g.astype(jnp.float32) + b.astype(jnp.float32)).astype(x.dtype)


def _gmlp_mix(u, vn, w_s, b_s, L):
    B, T, _ = vn.shape
    n = T // L
    vh = vn.reshape(B, n, L, GM_HEADS, GM_HEAD_DIM)
    w = w_s[:, :L, :L]
    w = jnp.where(jnp.tril(jnp.ones((L, L), dtype=bool)), w, jnp.zeros_like(w))
    s = jnp.einsum('htj,bnjhd->bnthd', w, vh) + jnp.transpose(b_s[:, :L])[None, None, :, :, None]
    return u * s.reshape(B, T, GM_WIDTH)


def _mlstm_chunk(carry, inp):
    C, n, m = carry
    q, k, v, ig, lf = inp
    L = q.shape[2]
    b = jnp.cumsum(lf, axis=-1)
    g = b + m[..., None]
    D = b[..., :, None] - b[..., None, :] + ig[..., None, :]
    D = jnp.where(jnp.tril(jnp.ones((L, L), dtype=bool)), D, -jnp.inf)
    m_t = jnp.maximum(g, jnp.max(D, axis=-1))
    w_intra = jnp.exp(D - m_t[..., None])
    w_inter = jnp.exp(g - m_t)
    s = jnp.einsum('bhtd,bhsd->bhts', q, k) * w_intra
    num = jnp.einsum('bhts,bhse->bhte', s, v) + w_inter[..., None] * jnp.einsum('bhtd,bhde->bhte', q, C)
    den = jnp.sum(s, axis=-1) + w_inter * jnp.einsum('bhtd,bhd->bht', q, n)
    h = num / jnp.maximum(jnp.abs(den), jnp.exp(-m_t))[..., None]
    m_new = m_t[..., -1]
    w_end = jnp.exp(D[..., -1, :] - m_new[..., None])
    dec = jnp.exp(g[..., -1] - m_new)
    C_new = dec[..., None, None] * C + jnp.einsum('bhs,bhsd,bhse->bhde', w_end, k, v)
    n_new = dec[..., None] * n + jnp.einsum('bhs,bhsd->bhd', w_end, k)
    return (C_new, n_new, m_new), h


def _mlstm(q, k, v, ig, lf, state, L):
    B, T, _ = q.shape
    n = T // L

    def to_chunks(a):
        return a.astype(jnp.float32).reshape(B, n, L, ML_HEADS, ML_HEAD_DIM).transpose(1, 0, 3, 2, 4)

    def gate_chunks(a):
        return a.reshape(B, n, L, ML_HEADS).transpose(1, 0, 3, 2)

    xs = (to_chunks(q), to_chunks(k) * (ML_HEAD_DIM ** -0.5), to_chunks(v), gate_chunks(ig), gate_chunks(lf))
    state, h = lax.scan(_mlstm_chunk, state, xs)
    h = h.transpose(1, 0, 3, 2, 4).reshape(B, T, ML_WIDTH)
    return h, state


def _layer(h, pe, state, gm_len, ml_len, norm_mix, w_in, gm_ln_g, gm_ln_b, gm_ws, gm_bs,
           ml_b_i, ml_b_f, ml_norm, w_out, norm_ffn, w_up, w_down, norm_ple, w_ple_gate, w_ple_proj):
    B, T, _ = h.shape
    a = rmsnorm(h, norm_mix)
    z = a @ w_in
    u, vg, q, k, vm, o, zi, zf = jnp.split(z, SPLITS, axis=-1)
    u = jax.nn.gelu(u)
    vg = layernorm(jax.nn.gelu(vg), gm_ln_g, gm_ln_b)
    y_gm = _gmlp_mix(u, vg, gm_ws, gm_bs, gm_len)
    ig = (zi + ml_b_i).astype(jnp.float32)
    lf = jax.nn.log_sigmoid((zf + ml_b_f).astype(jnp.float32))
    hm, state = _mlstm(q, k, vm, ig, lf, state, ml_len)
    hm = rmsnorm(hm.reshape(B, T, ML_HEADS, ML_HEAD_DIM), ml_norm.reshape(ML_HEADS, ML_HEAD_DIM))
    y_ml = jax.nn.sigmoid(o) * hm.reshape(B, T, ML_WIDTH).astype(o.dtype)
    h = h + jnp.concatenate([y_gm, y_ml], axis=-1) @ w_out
    f = rmsnorm(h, norm_ffn) @ w_up
    h = h + jnp.square(jax.nn.relu(f)) @ w_down
    gate = jax.nn.sigmoid(rmsnorm(h, norm_ple) @ w_ple_gate)
    h = h + gate * (pe @ w_ple_proj)
    return h, state, vg


def setup_inputs(seed: int = 0) -> dict:
    key = jax.random.key(seed)
    ks = jax.random.split(key, 32)
    f32 = jnp.float32

    def nrm(k, shape, scale):
        return jax.random.normal(k, shape, f32) * scale

    def gain(k, shape):
        return 1.0 + 0.05 * jax.random.normal(k, shape, f32)

    f_bias = jnp.linspace(3.0, 6.0, ML_HEADS, dtype=f32)[None, :] + 0.1 * jax.random.normal(ks[11], (DEPTH, ML_HEADS), f32)
    return {
        "x_prompt": nrm(ks[0], (BATCH, SEQ, D_MODEL), 1.0),
        "x_sample": nrm(ks[1], (DEC_BATCH, DEC_SEQ, D_MODEL), 1.0),
        "p_prompt": nrm(ks[2], (DEPTH, BATCH, SEQ, PLE_DIM), 1.0),
        "p_sample": nrm(ks[3], (DEPTH, DEC_BATCH, DEC_SEQ, PLE_DIM), 1.0),
        "state_C": nrm(ks[4], (DEPTH, DEC_BATCH, ML_HEADS, ML_HEAD_DIM, ML_HEAD_DIM), 0.1),
        "state_n": jnp.abs(nrm(ks[5], (DEPTH, DEC_BATCH, ML_HEADS, ML_HEAD_DIM), 0.1)),
        "state_m": nrm(ks[6], (DEPTH, DEC_BATCH, ML_HEADS), 1.0),
        "norm_mix": gain(ks[7], (DEPTH, D_MODEL)),
        "w_in": nrm(ks[8], (DEPTH, D_MODEL, IN_COLS), D_MODEL ** -0.5),
        "gm_ln_g": gain(ks[9], (DEPTH, GM_WIDTH)),
        "gm_ln_b": nrm(ks[10], (DEPTH, GM_WIDTH), 0.02),
        "gm_ws": nrm(ks[12], (DEPTH, GM_HEADS, CHUNK, CHUNK), 0.5 * CHUNK ** -0.5),
        "gm_bs": gain(ks[13], (DEPTH, GM_HEADS, CHUNK)),
        "ml_b_i": nrm(ks[14], (DEPTH, ML_HEADS), 0.1),
        "ml_b_f": f_bias,
        "ml_norm": gain(ks[15], (DEPTH, ML_WIDTH)),
        "w_out": nrm(ks[16], (DEPTH, MIX_WIDTH, D_MODEL), MIX_WIDTH ** -0.5),
        "norm_ffn": gain(ks[17], (DEPTH, D_MODEL)),
        "w_up": nrm(ks[18], (DEPTH, D_MODEL, D_FF), D_MODEL ** -0.5),
        "w_down": nrm(ks[19], (DEPTH, D_FF, D_MODEL), D_FF ** -0.5),
        "norm_ple": gain(ks[20], (DEPTH, D_MODEL)),
        "w_ple_gate": nrm(ks[21], (DEPTH, D_MODEL, D_MODEL), D_MODEL ** -0.5),
        "w_ple_proj": nrm(ks[22], (DEPTH, PLE_DIM, D_MODEL), PLE_DIM ** -0.5),
        "norm_final": gain(ks[23], (D_MODEL,)),
    }


def reference(x_prompt, x_sample, p_prompt, p_sample, state_C, state_n, state_m,
              norm_mix, w_in, gm_ln_g, gm_ln_b, gm_ws, gm_bs, ml_b_i, ml_b_f, ml_norm,
              w_out, norm_ffn, w_up, w_down, norm_ple, w_ple_gate, w_ple_proj, norm_final):
    hp, hs = x_prompt, x_sample
    B = x_prompt.shape[0]
    Cp, Np, Mp, Vp, Cs, Ns, Ms, Vs = [], [], [], [], [], [], [], []
    for l in range(DEPTH):
        w = (norm_mix[l], w_in[l], gm_ln_g[l], gm_ln_b[l], gm_ws[l], gm_bs[l], ml_b_i[l], ml_b_f[l],
             ml_norm[l], w_out[l], norm_ffn[l], w_up[l], w_down[l], norm_ple[l], w_ple_gate[l], w_ple_proj[l])
        st0 = (jnp.zeros((B, ML_HEADS, ML_HEAD_DIM, ML_HEAD_DIM), jnp.float32),
               jnp.zeros((B, ML_HEADS, ML_HEAD_DIM), jnp.float32),
               jnp.zeros((B, ML_HEADS), jnp.float32))
        hp, (c, n, m), vg = _layer(hp, p_prompt[l], st0, CHUNK, ML_CHUNK, *w)
        Cp.append(c); Np.append(n); Mp.append(m); Vp.append(vg[:, -CHUNK:])
        sts = (state_C[l].astype(jnp.float32), state_n[l].astype(jnp.float32), state_m[l].astype(jnp.float32))
        hs, (c, n, m), vg = _layer(hs, p_sample[l], sts, DEC_SEQ, DEC_SEQ, *w)
        Cs.append(c); Ns.append(n); Ms.append(m); Vs.append(vg)
    y_prompt = rmsnorm(hp, norm_final)
    y_sample = rmsnorm(hs, norm_final)
    return (y_prompt, y_sample, jnp.stack(Cp), jnp.stack(Np), jnp.stack(Mp), jnp.stack(Vp),
            jnp.stack(Cs), jnp.stack(Ns), jnp.stack(Ms), jnp.stack(Vs))
```

```python
import functools

import jax
import jax.numpy as jnp
from jax import lax
from jax.experimental import pallas as pl
from jax.experimental.pallas import tpu as pltpu

F32 = jnp.float32
BF16 = jnp.bfloat16

D_MODEL = 1024
GM_WIDTH = 512
ML_WIDTH = 512
HEADS = 4
HEAD_DIM = 128
D_FF = 4096
PLE_DIM = 256
EPS = 1e-6
CHUNK = 128
MAIN_COLS = 2 * GM_WIDTH + 4 * ML_WIDTH
GATE_LANES = 128
GATE_ROWS = 16
FF_SPLIT = 4
VMEM_LIMIT = 56 * 1024 * 1024


def _dot(a, b):
    return jnp.dot(a, b, preferred_element_type=F32)


def _dot_nt(a, b):
    return lax.dot_general(a, b, (((1,), (1,)), ((), ())), preferred_element_type=F32)


def _split3(x):
    x1 = x.astype(BF16)
    r = x - x1.astype(F32)
    x2 = r.astype(BF16)
    r = r - x2.astype(F32)
    return x1, x2, r.astype(BF16)


def _sel_dot(sel, x):
    p1, p2, p3 = _split3(x)
    return _dot(sel, p1) + _dot(sel, p2) + _dot(sel, p3)


def _dot_sel(x, sel):
    p1, p2, p3 = _split3(x)
    return _dot(p1, sel) + _dot(p2, sel) + _dot(p3, sel)


def _rms(x, g):
    return x * lax.rsqrt(jnp.mean(x * x, axis=-1, keepdims=True) + EPS) * g


def _log_sigmoid(x):
    return -(jnp.maximum(-x, 0.0) + jnp.log1p(jnp.exp(-jnp.abs(x))))


def _const_spec(shape):
    nd = len(shape)
    return pl.BlockSpec(shape, lambda *_: (0,) * nd, pipeline_mode=pl.Buffered(1))


def _front_kernel(x_ref, nmix_ref, wmain_ref, wgate_ref, wgate_t_ref, bcol_ref, brow_ref,
                  lng_ref, lnb_ref, mixw_ref, mixb_ref,
                  ygm_ref, q_ref, k_ref, v_ref, og_ref, gcol_ref, grow_ref, vgn_ref):
    tb = x_ref.shape[0]
    a = _rms(x_ref[...], nmix_ref[...]).astype(BF16)

    def proj(lo, width):
        return _dot(a, wmain_ref[:, lo:lo + width])

    u = jax.nn.gelu(proj(0, GM_WIDTH))
    vg = jax.nn.gelu(proj(GM_WIDTH, GM_WIDTH))
    mu = jnp.mean(vg, axis=-1, keepdims=True)
    var = jnp.mean(jnp.square(vg - mu), axis=-1, keepdims=True)
    vgn = (vg - mu) * lax.rsqrt(var + EPS) * lng_ref[...] + lnb_ref[...]
    vgn_ref[...] = vgn
    vgb = vgn.astype(BF16)
    for c in range(tb // CHUNK):
        rows = slice(c * CHUNK, (c + 1) * CHUNK)
        for h in range(HEADS):
            cols = slice(h * HEAD_DIM, (h + 1) * HEAD_DIM)
            s = _dot(mixw_ref[h], vgb[rows, cols]) + mixb_ref[h]
            ygm_ref[rows, cols] = (u[rows, cols] * s).astype(BF16)

    q_ref[...] = proj(2 * GM_WIDTH, ML_WIDTH).astype(BF16)
    k_ref[...] = (proj(2 * GM_WIDTH + ML_WIDTH, ML_WIDTH) * (HEAD_DIM ** -0.5)).astype(BF16)
    v_ref[...] = proj(2 * GM_WIDTH + 2 * ML_WIDTH, ML_WIDTH).astype(BF16)
    og_ref[...] = jax.nn.sigmoid(proj(2 * GM_WIDTH + 3 * ML_WIDTH, ML_WIDTH))

    zc = _dot(a, wgate_ref[...]) + bcol_ref[...]
    lane = lax.broadcasted_iota(jnp.int32, zc.shape, 1)
    gcol_ref[...] = jnp.where(lane >= HEADS, _log_sigmoid(zc), zc)
    zr = _dot_nt(wgate_t_ref[...], a) + brow_ref[...]
    sub = lax.broadcasted_iota(jnp.int32, zr.shape, 0)
    grow_ref[...] = jnp.where(sub >= HEADS, _log_sigmoid(zr), zr)


def _front(x2d, nmix, wmain, wgate, wgate_t, bcol, brow, lng, lnb, mixw, mixb, tb):
    n = x2d.shape[0]
    tok = lambda w: pl.BlockSpec((tb, w), lambda i: (i, 0))
    out_shape = (
        jax.ShapeDtypeStruct((n, GM_WIDTH), BF16),
        jax.ShapeDtypeStruct((n, ML_WIDTH), BF16),
        jax.ShapeDtypeStruct((n, ML_WIDTH), BF16),
        jax.ShapeDtypeStruct((n, ML_WIDTH), BF16),
        jax.ShapeDtypeStruct((n, ML_WIDTH), F32),
        jax.ShapeDtypeStruct((n, GATE_LANES), F32),
        jax.ShapeDtypeStruct((GATE_ROWS, n), F32),
        jax.ShapeDtypeStruct((n, GM_WIDTH), F32),
    )
    return pl.pallas_call(
        _front_kernel,
        grid=(n // tb,),
        in_specs=[
            tok(D_MODEL),
            _const_spec((1, D_MODEL)),
            _const_spec((D_MODEL, MAIN_COLS)),
            _const_spec((D_MODEL, GATE_LANES)),
            _const_spec((GATE_ROWS, D_MODEL)),
            _const_spec((1, GATE_LANES)),
            _const_spec((GATE_ROWS, tb)),
            _const_spec((1, GM_WIDTH)),
            _const_spec((1, GM_WIDTH)),
            _const_spec((HEADS, CHUNK, CHUNK)),
            _const_spec((HEADS, CHUNK, CHUNK)),
        ],
        out_specs=(tok(GM_WIDTH), tok(ML_WIDTH), tok(ML_WIDTH), tok(ML_WIDTH), tok(ML_WIDTH),
                   tok(GATE_LANES), pl.BlockSpec((GATE_ROWS, tb), lambda i: (0, i)), tok(GM_WIDTH)),
        out_shape=out_shape,
        compiler_params=pltpu.CompilerParams(dimension_semantics=("arbitrary",),
                                             vmem_limit_bytes=VMEM_LIMIT),
        name="front",
    )(x2d, nmix, wmain, wgate, wgate_t, bcol, brow, lng, lnb, mixw, mixb)


def _gate_terms(gcol, grow, tri, triu):
    return _sel_dot(tri, gcol), _dot_sel(grow, triu)


def _intra(q, ks, igcol, bcol, igrow, brow, mprev, mask):
    d = bcol + (igrow - brow)
    g = bcol + mprev
    m_t = jnp.maximum(g, jnp.max(jnp.where(mask, d, -jnp.inf), axis=-1, keepdims=True))
    w_intra = jnp.where(mask, jnp.exp(d - m_t), 0.0)
    w_inter = jnp.exp(g - m_t)
    s = _dot_nt(q, ks) * w_intra
    return s, w_inter, m_t, g


def _head_out(num, den, m_t, gain, og):
    hh = num / jnp.maximum(jnp.abs(den), jnp.exp(-m_t))
    return (og * _rms(hh, gain)).astype(BF16)


def _mlstm_prompt_kernel(q_ref, k_ref, v_ref, og_ref, gcol_ref, grow_ref, mln_ref, tri_ref, triu_ref,
                         yml_ref, cn_out_ref, m_out_ref, cn_scr, m_scr):
    j = pl.program_id(1)

    @pl.when(j == 0)
    def _():
        cn_scr[...] = jnp.zeros_like(cn_scr)
        m_scr[...] = jnp.zeros_like(m_scr)

    tri = tri_ref[...]
    mask = tri.astype(F32) > 0.0
    gcol = gcol_ref[...]
    grow = grow_ref[...]
    bcol_all, brow_all = _gate_terms(gcol, grow, tri, triu_ref[...])
    lane = lax.broadcasted_iota(jnp.int32, (CHUNK, HEAD_DIM), 1)
    ones_col = (lane == 0).astype(BF16)
    last = slice(CHUNK - 1, CHUNK)
    for h in range(HEADS):
        cols = slice(h * HEAD_DIM, (h + 1) * HEAD_DIM)
        q, ks, v = q_ref[:, cols], k_ref[:, cols], v_ref[:, cols]
        igcol = gcol[:, h:h + 1]
        bcol = bcol_all[:, HEADS + h:HEADS + h + 1]
        igrow = grow[h:h + 1, :]
        brow = brow_all[HEADS + h:HEADS + h + 1, :]
        mprev = m_scr[h:h + 1, 0:1]
        s, w_inter, m_t, g = _intra(q, ks, igcol, bcol, igrow, brow, mprev, mask)
        cn = cn_scr[h]
        vext = jnp.concatenate([v, ones_col], axis=1)
        qc = _dot(q, cn.astype(BF16))
        num = _dot(s.astype(BF16), v) + w_inter * qc[:, :HEAD_DIM]
        den = jnp.sum(s, axis=-1, keepdims=True) + w_inter * qc[:, HEAD_DIM:HEAD_DIM + 1]
        yml_ref[:, cols] = _head_out(num, den, m_t, mln_ref[:, cols], og_ref[:, cols])
        m_new = m_t[last, :]
        w_end = jnp.exp(bcol[last, :] - bcol + igcol - m_new)
        dec = jnp.exp(g[last, :] - m_new)
        kw_t = (ks.astype(F32) * w_end).T.astype(BF16)
        cn_scr[h] = dec * cn + _dot(kw_t, vext)
        m_scr[h:h + 1, :] = jnp.broadcast_to(m_new, (1, HEAD_DIM))

    @pl.when(j == pl.num_programs(1) - 1)
    def _():
        cn_out_ref[...] = cn_scr[...]
        m_out_ref[...] = m_scr[...]


def _mlstm_prompt(q, k, v, og, gcol, grow, mln, tri, triu, batch, n_chunks):
    tok = lambda w: pl.BlockSpec((CHUNK, w), lambda b, j: (b * n_chunks + j, 0))
    return pl.pallas_call(
        _mlstm_prompt_kernel,
        grid=(batch, n_chunks),
        in_specs=[
            tok(ML_WIDTH), tok(ML_WIDTH), tok(ML_WIDTH), tok(ML_WIDTH), tok(GATE_LANES),
            pl.BlockSpec((GATE_ROWS, CHUNK), lambda b, j: (0, b * n_chunks + j)),
            _const_spec((1, ML_WIDTH)),
            _const_spec((CHUNK, CHUNK)),
            _const_spec((CHUNK, CHUNK)),
        ],
        out_specs=(
            tok(ML_WIDTH),
            pl.BlockSpec((None, HEADS, HEAD_DIM, 2 * HEAD_DIM), lambda b, j: (b, 0, 0, 0)),
            pl.BlockSpec((None, 8, HEAD_DIM), lambda b, j: (b, 0, 0)),
        ),
        out_shape=(
            jax.ShapeDtypeStruct((batch * n_chunks * CHUNK, ML_WIDTH), BF16),
            jax.ShapeDtypeStruct((batch, HEADS, HEAD_DIM, 2 * HEAD_DIM), F32),
            jax.ShapeDtypeStruct((batch, 8, HEAD_DIM), F32),
        ),
        scratch_shapes=[pltpu.VMEM((HEADS, HEAD_DIM, 2 * HEAD_DIM), F32),
                        pltpu.VMEM((8, HEAD_DIM), F32)],
        compiler_params=pltpu.CompilerParams(dimension_semantics=("arbitrary", "arbitrary"),
                                             vmem_limit_bytes=VMEM_LIMIT),
        name="mlstm_prompt",
    )(q, k, v, og, gcol, grow, mln, tri, triu)


def _mlstm_sample_kernel(seq_len, q_ref, k_ref, v_ref, og_ref, gcol_ref, grow_ref, mrep_ref, c_ref, n_ref,
                         mln_ref, tri_ref, triu_ref, sel_last_ref, expand_ref, pick_last_ref, seq_sum_ref,
                         yml_ref, c_out_ref, n_out_ref, m_out_ref):
    n_seq = CHUNK // seq_len
    tri = tri_ref[...]
    mask = tri.astype(F32) > 0.0
    gcol = gcol_ref[...]
    grow = grow_ref[...]
    bcol_all, brow_all = _gate_terms(gcol, grow, tri, triu_ref[...])
    sel_last = sel_last_ref[...]
    expand = expand_ref[...]
    pick_last = pick_last_ref[...]
    seq_sum = seq_sum_ref[...]
    row = lax.broadcasted_iota(jnp.int32, (CHUNK, HEAD_DIM), 0)
    lane = lax.broadcasted_iota(jnp.int32, (CHUNK, HEAD_DIM), 1)
    seq_rows = [(row >= i * seq_len) & (row < (i + 1) * seq_len) for i in range(n_seq)]
    for h in range(HEADS):
        cols = slice(h * HEAD_DIM, (h + 1) * HEAD_DIM)
        q, ks, v = q_ref[:, cols], k_ref[:, cols], v_ref[:, cols]
        igcol = gcol[:, h:h + 1]
        bcol = bcol_all[:, HEADS + h:HEADS + h + 1]
        igrow = grow[h:h + 1, :]
        brow = brow_all[HEADS + h:HEADS + h + 1, :]
        mprev = mrep_ref[:, h:h + 1]
        s, w_inter, m_t, g = _intra(q, ks, igcol, bcol, igrow, brow, mprev, mask)
        c_all = c_ref[:, h]
        c_stack = c_all.reshape(n_seq * HEAD_DIM, HEAD_DIM).astype(BF16)
        zero = jnp.zeros_like(q)
        q_exp = jnp.concatenate([jnp.where(m, q, zero) for m in seq_rows], axis=1)
        qc = _dot(q_exp, c_stack)
        n_all = n_ref[:, h]
        qn = jnp.sum(q.astype(F32) * _sel_dot(expand, n_all), axis=-1, keepdims=True)
        num = _dot(s.astype(BF16), v) + w_inter * qc
        den = jnp.sum(s, axis=-1, keepdims=True) + w_inter * qn
        yml_ref[:, cols] = _head_out(num, den, m_t, mln_ref[:, cols], og_ref[:, cols])
        packed = jnp.where(lane == 0, m_t, jnp.where(lane == 1, g, jnp.where(lane == 2, bcol, 0.0)))
        ends = _sel_dot(sel_last, packed)
        m_new, g_last, b_last = ends[:, 0:1], ends[:, 1:2], ends[:, 2:3]
        w_end = jnp.exp(b_last - bcol + igcol - m_new)
        dec = jnp.exp(g_last - m_new)
        kw = ks.astype(F32) * w_end
        v_exp = jnp.concatenate([jnp.where(m, v, zero) for m in seq_rows], axis=1)
        upd = _dot(kw.T.astype(BF16), v_exp)
        dec_seq = _sel_dot(pick_last, jnp.broadcast_to(dec, (CHUNK, HEAD_DIM)))
        for i in range(n_seq):
            c_out_ref[i, h] = dec_seq[i:i + 1, 0:1] * c_all[i] + upd[:, i * HEAD_DIM:(i + 1) * HEAD_DIM]
        n_out_ref[:, h] = dec_seq * n_all + _sel_dot(seq_sum, kw)
        m_out_ref[:, h] = _sel_dot(pick_last, jnp.broadcast_to(m_t, (CHUNK, HEAD_DIM)))


def _mlstm_sample(q, k, v, og, gcol, grow, mrep, c0, n0, mln, tri, triu, sel_last, expand, pick_last,
                  seq_sum, seq_len):
    n = q.shape[0]
    n_seq = CHUNK // seq_len
    n_batch = n // seq_len
    tok = lambda w: pl.BlockSpec((CHUNK, w), lambda i: (i, 0))
    c_spec = pl.BlockSpec((n_seq, HEADS, HEAD_DIM, HEAD_DIM), lambda i: (i, 0, 0, 0))
    n_spec = pl.BlockSpec((n_seq, HEADS, HEAD_DIM), lambda i: (i, 0, 0))
    return pl.pallas_call(
        functools.partial(_mlstm_sample_kernel, seq_len),
        grid=(n // CHUNK,),
        in_specs=[
            tok(ML_WIDTH), tok(ML_WIDTH), tok(ML_WIDTH), tok(ML_WIDTH), tok(GATE_LANES),
            pl.BlockSpec((GATE_ROWS, CHUNK), lambda i: (0, i)),
            pl.BlockSpec((CHUNK, HEADS), lambda i: (i, 0)),
            c_spec, n_spec,
            _const_spec((1, ML_WIDTH)),
            _const_spec((CHUNK, CHUNK)), _const_spec((CHUNK, CHUNK)), _const_spec((CHUNK, CHUNK)),
            _const_spec((CHUNK, n_seq)), _const_spec((n_seq, CHUNK)), _const_spec((n_seq, CHUNK)),
        ],
        out_specs=(tok(ML_WIDTH), c_spec, n_spec, n_spec),
        out_shape=(
            jax.ShapeDtypeStruct((n, ML_WIDTH), BF16),
            jax.ShapeDtypeStruct((n_batch, HEADS, HEAD_DIM, HEAD_DIM), F32),
            jax.ShapeDtypeStruct((n_batch, HEADS, HEAD_DIM), F32),
            jax.ShapeDtypeStruct((n_batch, HEADS, HEAD_DIM), F32),
        ),
        compiler_params=pltpu.CompilerParams(dimension_semantics=("arbitrary",),
                                             vmem_limit_bytes=VMEM_LIMIT),
        name="mlstm_sample",
    )(q, k, v, og, gcol, grow, mrep, c0, n0, mln, tri, triu, sel_last, expand, pick_last, seq_sum)


def _back_kernel(final_norm, x_ref, ygm_ref, yml_ref, pe_ref, wout_a_ref, wout_b_ref, nffn_ref, wup_ref,
                 wdown_ref, nple_ref, wpg_ref, wpp_ref, nfin_ref, out_ref):
    h = x_ref[...] + _dot(ygm_ref[...], wout_a_ref[...]) + _dot(yml_ref[...], wout_b_ref[...])
    a = _rms(h, nffn_ref[...]).astype(BF16)
    ff = D_FF // FF_SPLIT

    def mlp_part(c):
        f = _dot(a, wup_ref[:, c * ff:(c + 1) * ff])
        f = jnp.square(jnp.maximum(f, 0.0)).astype(BF16)
        return _dot(f, wdown_ref[c * ff:(c + 1) * ff, :])

    mlp = mlp_part(0)
    for c in range(1, FF_SPLIT):
        mlp = mlp + mlp_part(c)
    h = h + mlp
    gate = jax.nn.sigmoid(_dot(_rms(h, nple_ref[...]).astype(BF16), wpg_ref[...]))
    h = h + gate * _dot(pe_ref[...].astype(BF16), wpp_ref[...])
    if final_norm:
        h = _rms(h, nfin_ref[...])
    out_ref[...] = h


def _back(x2d, ygm, yml, pe2d, wout_a, wout_b, nffn, wup, wdown, nple, wpg, wpp, nfin, final_norm, tb):
    n = x2d.shape[0]
    tok = lambda w: pl.BlockSpec((tb, w), lambda i: (i, 0))
    return pl.pallas_call(
        functools.partial(_back_kernel, final_norm),
        grid=(n // tb,),
        in_specs=[
            tok(D_MODEL), tok(GM_WIDTH), tok(ML_WIDTH), tok(PLE_DIM),
            _const_spec((GM_WIDTH, D_MODEL)), _const_spec((ML_WIDTH, D_MODEL)),
            _const_spec((1, D_MODEL)),
            _const_spec((D_MODEL, D_FF)), _const_spec((D_FF, D_MODEL)),
            _const_spec((1, D_MODEL)),
            _const_spec((D_MODEL, D_MODEL)), _const_spec((PLE_DIM, D_MODEL)),
            _const_spec((1, D_MODEL)),
        ],
        out_specs=tok(D_MODEL),
        out_shape=jax.ShapeDtypeStruct((n, D_MODEL), F32),
        compiler_params=pltpu.CompilerParams(dimension_semantics=("arbitrary",),
                                             vmem_limit_bytes=VMEM_LIMIT),
        name="back",
    )(x2d, ygm, yml, pe2d, wout_a, wout_b, nffn, wup, wdown, nple, wpg, wpp, nfin)


def _block_tri(block, dtype):
    r = jnp.arange(CHUNK)[:, None]
    c = jnp.arange(CHUNK)[None, :]
    return ((r // block == c // block) & (c <= r)).astype(dtype)


def _token_block(n):
    return 512 if n % 512 == 0 else CHUNK


def kernel(x_prompt, x_sample, p_prompt, p_sample, state_C, state_n, state_m, norm_mix, w_in, gm_ln_g,
           gm_ln_b, gm_ws, gm_bs, ml_b_i, ml_b_f, ml_norm, w_out, norm_ffn, w_up, w_down, norm_ple,
           w_ple_gate, w_ple_proj, norm_final):
    depth = w_in.shape[0]
    batch, seq, _ = x_prompt.shape
    dec_batch, dec_seq, _ = x_sample.shape
    n_p, n_s = batch * seq, dec_batch * dec_seq
    n_chunks = seq // CHUNK
    n_seq = CHUNK // dec_seq
    assert seq % CHUNK == 0 and CHUNK % dec_seq == 0 and n_s % CHUNK == 0
    tb_p, tb_s = _token_block(n_p), _token_block(n_s)

    hp = x_prompt.reshape(n_p, D_MODEL)
    hs = x_sample.reshape(n_s, D_MODEL)

    tri_p = _block_tri(CHUNK, BF16)
    tri_s = _block_tri(dec_seq, BF16)
    r = jnp.arange(CHUNK)
    i = jnp.arange(n_seq)
    sel_last = (r[None, :] == (r[:, None] // dec_seq) * dec_seq + dec_seq - 1).astype(BF16)
    expand = (r[:, None] // dec_seq == i[None, :]).astype(BF16)
    pick_last = (r[None, :] == i[:, None] * dec_seq + dec_seq - 1).astype(BF16)
    seq_sum = (r[None, :] // dec_seq == i[:, None]).astype(BF16)

    outs = {k: [] for k in ("Cp", "Np", "Mp", "Vp", "Cs", "Ns", "Ms", "Vs")}
    for l in range(depth):
        row = lambda a: a[l].reshape(1, -1).astype(F32)
        wmain = w_in[l][:, :MAIN_COLS].astype(BF16)
        wg = w_in[l][:, MAIN_COLS:]
        wgate = jnp.pad(wg, ((0, 0), (0, GATE_LANES - 2 * HEADS))).astype(BF16)
        wgate_t = jnp.pad(wg.T, ((0, GATE_ROWS - 2 * HEADS), (0, 0))).astype(BF16)
        gbias = jnp.concatenate([ml_b_i[l], ml_b_f[l]]).astype(F32)
        bcol = jnp.pad(gbias, (0, GATE_LANES - 2 * HEADS)).reshape(1, GATE_LANES)
        brow = jnp.pad(gbias, (0, GATE_ROWS - 2 * HEADS)).reshape(GATE_ROWS, 1)
        ws = gm_ws[l]
        mixw_p = (ws[:, :CHUNK, :CHUNK] * tri_p.astype(F32)).astype(BF16)
        mixb_p = jnp.broadcast_to(gm_bs[l][:, :CHUNK, None], (HEADS, CHUNK, CHUNK)).astype(F32)
        mixw_s = (jnp.tile(ws[:, :dec_seq, :dec_seq], (1, n_seq, n_seq)) * tri_s.astype(F32)).astype(BF16)
        mixb_s = jnp.broadcast_to(jnp.tile(gm_bs[l][:, :dec_seq], (1, n_seq))[:, :, None],
                                  (HEADS, CHUNK, CHUNK)).astype(F32)
        wout_a = w_out[l][:GM_WIDTH].astype(BF16)
        wout_b = w_out[l][GM_WIDTH:].astype(BF16)
        wup, wdown = w_up[l].astype(BF16), w_down[l].astype(BF16)
        wpg, wpp = w_ple_gate[l].astype(BF16), w_ple_proj[l].astype(BF16)
        last = l == depth - 1
        nfin = norm_final.reshape(1, D_MODEL).astype(F32)

        def front(x2d, mixw, mixb, tb):
            return _front(x2d, row(norm_mix), wmain, wgate, wgate_t, bcol,
                          jnp.broadcast_to(brow, (GATE_ROWS, tb)), row(gm_ln_g), row(gm_ln_b), mixw, mixb, tb)

        def back(x2d, ygm, yml, pe, tb):
            return _back(x2d, ygm, yml, pe, wout_a, wout_b, row(norm_ffn), wup, wdown, row(norm_ple),
                         wpg, wpp, nfin, last, tb)

        ygm, q, k, v, og, gcol, grow, vgn = front(hp, mixw_p, mixb_p, tb_p)
        yml, cn, m8 = _mlstm_prompt(q, k, v, og, gcol, grow, row(ml_norm), tri_p, tri_p.T, batch, n_chunks)
        hp = back(hp, ygm, yml, p_prompt[l].reshape(n_p, PLE_DIM), tb_p)
        outs["Cp"].append(cn[..., :HEAD_DIM])
        outs["Np"].append(cn[..., HEAD_DIM])
        outs["Mp"].append(m8[:, :HEADS, 0])
        outs["Vp"].append(vgn.reshape(batch, seq, GM_WIDTH)[:, -CHUNK:])

        ygm, q, k, v, og, gcol, grow, vgn = front(hs, mixw_s, mixb_s, tb_s)
        mrep = jnp.repeat(state_m[l].astype(F32), dec_seq, axis=0)
        yml, c_new, n_new, m_new = _mlstm_sample(
            q, k, v, og, gcol, grow, mrep, state_C[l].astype(F32), state_n[l].astype(F32), row(ml_norm),
            tri_s, tri_s.T, sel_last, expand, pick_last, seq_sum, dec_seq)
        hs = back(hs, ygm, yml, p_sample[l].reshape(n_s, PLE_DIM), tb_s)
        outs["Cs"].append(c_new)
        outs["Ns"].append(n_new)
        outs["Ms"].append(m_new[..., 0])
        outs["Vs"].append(vgn.reshape(dec_batch, dec_seq, GM_WIDTH))

    st = lambda k: jnp.stack(outs[k])
    return (hp.reshape(batch, seq, D_MODEL), hs.reshape(dec_batch, dec_seq, D_MODEL),
            st("Cp"), st("Np"), st("Mp"), st("Vp"), st("Cs"), st("Ns"), st("Ms"), st("Vs"))
```

```python
import functools

import jax
import jax.numpy as jnp
from jax import lax
from jax.experimental import pallas as pl
from jax.experimental.pallas import tpu as pltpu

F32 = jnp.float32
BF16 = jnp.bfloat16

D_MODEL = 1024
GM_WIDTH = 512
ML_WIDTH = 512
HEADS = 4
HEAD_DIM = 128
D_FF = 4096
PLE_DIM = 256
EPS = 1e-6
CHUNK = 128
MAIN_COLS = 2 * GM_WIDTH + 4 * ML_WIDTH
GATE_LANES = 128
GATE_ROWS = 16
FF_SPLIT = 4
VMEM_LIMIT = 56 * 1024 * 1024


def _dot(a, b):
    return jnp.dot(a, b, preferred_element_type=F32)


def _dot_nt(a, b):
    return lax.dot_general(a, b, (((1,), (1,)), ((), ())), preferred_element_type=F32)


def _split3(x):
    x1 = x.astype(BF16)
    r = x - x1.astype(F32)
    x2 = r.astype(BF16)
    r = r - x2.astype(F32)
    return x1, x2, r.astype(BF16)


def _sel_dot(sel, x):
    p1, p2, p3 = _split3(x)
    return _dot(sel, p1) + _dot(sel, p2) + _dot(sel, p3)


def _dot_sel(x, sel):
    p1, p2, p3 = _split3(x)
    return _dot(p1, sel) + _dot(p2, sel) + _dot(p3, sel)


def _rms(x, g):
    return x * lax.rsqrt(jnp.mean(x * x, axis=-1, keepdims=True) + EPS) * g


def _log_sigmoid(x):
    return -(jnp.maximum(-x, 0.0) + jnp.log1p(jnp.exp(-jnp.abs(x))))


def _const_spec(shape):
    nd = len(shape)
    return pl.BlockSpec(shape, lambda *_: (0,) * nd, pipeline_mode=pl.Buffered(1))


def _project_and_gmlp(x_ref, nmix_ref, wmain_ref, lng_ref, lnb_ref, mixw_ref, mixb_ref, ygm_ref):
    tb = x_ref.shape[0]
    a = _rms(x_ref[...], nmix_ref[...]).astype(BF16)

    def proj(lo, width):
        return _dot(a, wmain_ref[:, lo:lo + width])

    u = jax.nn.gelu(proj(0, GM_WIDTH))
    vg = jax.nn.gelu(proj(GM_WIDTH, GM_WIDTH))
    mu = jnp.mean(vg, axis=-1, keepdims=True)
    var = jnp.mean(jnp.square(vg - mu), axis=-1, keepdims=True)
    vgn = (vg - mu) * lax.rsqrt(var + EPS) * lng_ref[...] + lnb_ref[...]
    vgb = vgn.astype(BF16)
    for c in range(tb // CHUNK):
        rows = slice(c * CHUNK, (c + 1) * CHUNK)
        for h in range(HEADS):
            cols = slice(h * HEAD_DIM, (h + 1) * HEAD_DIM)
            s = _dot(mixw_ref[h], vgb[rows, cols]) + mixb_ref[h]
            ygm_ref[rows, cols] = (u[rows, cols] * s).astype(BF16)
    return a, proj, vgn


Q_LO = 2 * GM_WIDTH
K_LO = Q_LO + ML_WIDTH
V_LO = K_LO + ML_WIDTH
O_LO = V_LO + ML_WIDTH


def _mixer_prompt_kernel(steps_per_seq, x_ref, nmix_ref, wmain_ref, wgc_ref, wgr_ref, bgc_ref, bgr_ref,
                         lng_ref, lnb_ref, mixw_ref, mixb_ref, mln_ref, tri_ref, triu_ref,
                         ygm_ref, yml_ref, vlast_ref, cn_out_ref, m_out_ref, cn_scr, m_scr):
    tb = x_ref.shape[0]
    step = pl.program_id(0) % steps_per_seq

    @pl.when(step == 0)
    def _():
        cn_scr[...] = jnp.zeros_like(cn_scr)
        m_scr[...] = jnp.zeros_like(m_scr)

    a, proj, vgn = _project_and_gmlp(x_ref, nmix_ref, wmain_ref, lng_ref, lnb_ref, mixw_ref, mixb_ref, ygm_ref)

    @pl.when(step == steps_per_seq - 1)
    def _():
        vlast_ref[...] = vgn[tb - CHUNK:, :]

    q = proj(Q_LO, ML_WIDTH).astype(BF16)
    k = proj(K_LO, ML_WIDTH) * (HEAD_DIM ** -0.5)
    v = proj(V_LO, ML_WIDTH).astype(BF16)
    og = jax.nn.sigmoid(proj(O_LO, ML_WIDTH))

    zc = _dot(a, wgc_ref[...]) + bgc_ref[...]
    ig_c = zc[:, :GATE_LANES]
    lf_c = _log_sigmoid(zc[:, GATE_LANES:])
    zr = _dot_nt(wgr_ref[...], a) + bgr_ref[...]
    ig_r = zr[:GATE_ROWS]
    lf_r = _log_sigmoid(zr[GATE_ROWS:])

    tri = tri_ref[...]
    triu = triu_ref[...]
    mask = tri.astype(F32) > 0.0
    lane = lax.broadcasted_iota(jnp.int32, (CHUNK, HEAD_DIM), 1)
    ones_col = (lane == 0).astype(BF16)
    last = slice(CHUNK - 1, CHUNK)
    gain = mln_ref[...]
    cn = [cn_scr[h] for h in range(HEADS)]
    m_prev = [m_scr[h:h + 1, 0:1] for h in range(HEADS)]

    for c in range(tb // CHUNK):
        rows = slice(c * CHUNK, (c + 1) * CHUNK)
        b_c = _sel_dot(tri, lf_c[rows])
        r_c = ig_c[rows] - b_c
        r_r = ig_r[:, rows] - _dot_sel(lf_r[:, rows], triu)
        for h in range(HEADS):
            cols = slice(h * HEAD_DIM, (h + 1) * HEAD_DIM)
            qh, kh, vh = q[rows, cols], k[rows, cols], v[rows, cols]
            bcol, rcol, rrow = b_c[:, h:h + 1], r_c[:, h:h + 1], r_r[h:h + 1, :]
            cm = jnp.maximum(m_prev[h], jnp.max(jnp.where(mask, rrow, -jnp.inf), axis=-1, keepdims=True))
            w_intra = jnp.where(mask, jnp.exp(rrow - cm), 0.0)
            w_inter = jnp.exp(m_prev[h] - cm)
            s = _dot_nt(qh, kh.astype(BF16)) * w_intra
            qc = _dot(qh, cn[h].astype(BF16))
            num = _dot(s.astype(BF16), vh) + w_inter * qc[:, :HEAD_DIM]
            den = jnp.sum(s, axis=-1, keepdims=True) + w_inter * qc[:, HEAD_DIM:HEAD_DIM + 1]
            hh = num * (1.0 / jnp.maximum(jnp.abs(den), jnp.exp(-(bcol + cm))))
            yml_ref[rows, cols] = (og[rows, cols] * _rms(hh, gain[:, cols])).astype(BF16)
            cm_last = cm[last, :]
            w_end = jnp.exp(rcol - cm_last)
            dec = jnp.exp(m_prev[h] - cm_last)
            kw_t = (kh * w_end).T.astype(BF16)
            cn[h] = dec * cn[h] + _dot(kw_t, jnp.concatenate([vh, ones_col], axis=1))
            m_prev[h] = bcol[last, :] + cm_last

    for h in range(HEADS):
        cn_scr[h] = cn[h]
        m_scr[h:h + 1, :] = jnp.broadcast_to(m_prev[h], (1, HEAD_DIM))

    @pl.when(step == steps_per_seq - 1)
    def _():
        cn_out_ref[...] = cn_scr[...]
        m_out_ref[...] = m_scr[...]


def _mixer_prompt(x2d, nmix, wmain, wgc, wgr, bgc, bgr, lng, lnb, mixw, mixb, mln, tri, triu, batch, tb):
    n = x2d.shape[0]
    steps_per_seq = n // batch // tb
    tok = lambda w: pl.BlockSpec((tb, w), lambda i: (i, 0))
    per_seq = lambda *shape: pl.BlockSpec((None,) + shape, lambda i: (i // steps_per_seq,) + (0,) * len(shape))
    return pl.pallas_call(
        functools.partial(_mixer_prompt_kernel, steps_per_seq),
        grid=(n // tb,),
        in_specs=[
            tok(D_MODEL),
            _const_spec((1, D_MODEL)),
            _const_spec((D_MODEL, MAIN_COLS)),
            _const_spec((D_MODEL, 2 * GATE_LANES)),
            _const_spec((2 * GATE_ROWS, D_MODEL)),
            _const_spec((1, 2 * GATE_LANES)),
            _const_spec((2 * GATE_ROWS, tb)),
            _const_spec((1, GM_WIDTH)),
            _const_spec((1, GM_WIDTH)),
            _const_spec((HEADS, CHUNK, CHUNK)),
            _const_spec((HEADS, CHUNK, CHUNK)),
            _const_spec((1, ML_WIDTH)),
            _const_spec((CHUNK, CHUNK)),
            _const_spec((CHUNK, CHUNK)),
        ],
        out_specs=(tok(GM_WIDTH), tok(ML_WIDTH), per_seq(CHUNK, GM_WIDTH),
                   per_seq(HEADS, HEAD_DIM, 2 * HEAD_DIM), per_seq(8, HEAD_DIM)),
        out_shape=(
            jax.ShapeDtypeStruct((n, GM_WIDTH), BF16),
            jax.ShapeDtypeStruct((n, ML_WIDTH), BF16),
            jax.ShapeDtypeStruct((batch, CHUNK, GM_WIDTH), F32),
            jax.ShapeDtypeStruct((batch, HEADS, HEAD_DIM, 2 * HEAD_DIM), F32),
            jax.ShapeDtypeStruct((batch, 8, HEAD_DIM), F32),
        ),
        scratch_shapes=[pltpu.VMEM((HEADS, HEAD_DIM, 2 * HEAD_DIM), F32),
                        pltpu.VMEM((8, HEAD_DIM), F32)],
        compiler_params=pltpu.CompilerParams(dimension_semantics=("arbitrary",),
                                             vmem_limit_bytes=VMEM_LIMIT),
        name="mixer_prompt",
    )(x2d, nmix, wmain, wgc, wgr, bgc, bgr, lng, lnb, mixw, mixb, mln, tri, triu)


def _front_kernel(x_ref, nmix_ref, wmain_ref, wgate_ref, wgate_t_ref, bcol_ref, brow_ref,
                  lng_ref, lnb_ref, mixw_ref, mixb_ref,
                  ygm_ref, q_ref, k_ref, v_ref, og_ref, gcol_ref, grow_ref, vgn_ref):
    a, proj, vgn = _project_and_gmlp(x_ref, nmix_ref, wmain_ref, lng_ref, lnb_ref, mixw_ref, mixb_ref, ygm_ref)
    vgn_ref[...] = vgn
    q_ref[...] = proj(Q_LO, ML_WIDTH).astype(BF16)
    k_ref[...] = (proj(K_LO, ML_WIDTH) * (HEAD_DIM ** -0.5)).astype(BF16)
    v_ref[...] = proj(V_LO, ML_WIDTH).astype(BF16)
    og_ref[...] = jax.nn.sigmoid(proj(O_LO, ML_WIDTH))

    zc = _dot(a, wgate_ref[...]) + bcol_ref[...]
    lane = lax.broadcasted_iota(jnp.int32, zc.shape, 1)
    gcol_ref[...] = jnp.where(lane >= HEADS, _log_sigmoid(zc), zc)
    zr = _dot_nt(wgate_t_ref[...], a) + brow_ref[...]
    sub = lax.broadcasted_iota(jnp.int32, zr.shape, 0)
    grow_ref[...] = jnp.where(sub >= HEADS, _log_sigmoid(zr), zr)


def _front(x2d, nmix, wmain, wgate, wgate_t, bcol, brow, lng, lnb, mixw, mixb, tb):
    n = x2d.shape[0]
    tok = lambda w: pl.BlockSpec((tb, w), lambda i: (i, 0))
    out_shape = (
        jax.ShapeDtypeStruct((n, GM_WIDTH), BF16),
        jax.ShapeDtypeStruct((n, ML_WIDTH), BF16),
        jax.ShapeDtypeStruct((n, ML_WIDTH), BF16),
        jax.ShapeDtypeStruct((n, ML_WIDTH), BF16),
        jax.ShapeDtypeStruct((n, ML_WIDTH), F32),
        jax.ShapeDtypeStruct((n, GATE_LANES), F32),
        jax.ShapeDtypeStruct((GATE_ROWS, n), F32),
        jax.ShapeDtypeStruct((n, GM_WIDTH), F32),
    )
    return pl.pallas_call(
        _front_kernel,
        grid=(n // tb,),
        in_specs=[
            tok(D_MODEL),
            _const_spec((1, D_MODEL)),
            _const_spec((D_MODEL, MAIN_COLS)),
            _const_spec((D_MODEL, GATE_LANES)),
            _const_spec((GATE_ROWS, D_MODEL)),
            _const_spec((1, GATE_LANES)),
            _const_spec((GATE_ROWS, tb)),
            _const_spec((1, GM_WIDTH)),
            _const_spec((1, GM_WIDTH)),
            _const_spec((HEADS, CHUNK, CHUNK)),
            _const_spec((HEADS, CHUNK, CHUNK)),
        ],
        out_specs=(tok(GM_WIDTH), tok(ML_WIDTH), tok(ML_WIDTH), tok(ML_WIDTH), tok(ML_WIDTH),
                   tok(GATE_LANES), pl.BlockSpec((GATE_ROWS, tb), lambda i: (0, i)), tok(GM_WIDTH)),
        out_shape=out_shape,
        compiler_params=pltpu.CompilerParams(dimension_semantics=("arbitrary",),
                                             vmem_limit_bytes=VMEM_LIMIT),
        name="front",
    )(x2d, nmix, wmain, wgate, wgate_t, bcol, brow, lng, lnb, mixw, mixb)


def _intra(q, ks, igcol, bcol, igrow, brow, mprev, mask):
    d = bcol + (igrow - brow)
    g = bcol + mprev
    m_t = jnp.maximum(g, jnp.max(jnp.where(mask, d, -jnp.inf), axis=-1, keepdims=True))
    w_intra = jnp.where(mask, jnp.exp(d - m_t), 0.0)
    w_inter = jnp.exp(g - m_t)
    s = _dot_nt(q, ks) * w_intra
    return s, w_inter, m_t, g


def _head_out(num, den, m_t, gain, og):
    hh = num / jnp.maximum(jnp.abs(den), jnp.exp(-m_t))
    return (og * _rms(hh, gain)).astype(BF16)


def _mlstm_sample_kernel(seq_len, q_ref, k_ref, v_ref, og_ref, gcol_ref, grow_ref, mrep_ref, c_ref, n_ref,
                         mln_ref, tri_ref, triu_ref, sel_last_ref, expand_ref, pick_last_ref, seq_sum_ref,
                         yml_ref, c_out_ref, n_out_ref, m_out_ref):
    n_seq = CHUNK // seq_len
    tri = tri_ref[...]
    mask = tri.astype(F32) > 0.0
    gcol = gcol_ref[...]
    grow = grow_ref[...]
    bcol_all = _sel_dot(tri, gcol)
    brow_all = _dot_sel(grow, triu_ref[...])
    sel_last = sel_last_ref[...]
    expand = expand_ref[...]
    pick_last = pick_last_ref[...]
    seq_sum = seq_sum_ref[...]
    row = lax.broadcasted_iota(jnp.int32, (CHUNK, HEAD_DIM), 0)
    lane = lax.broadcasted_iota(jnp.int32, (CHUNK, HEAD_DIM), 1)
    seq_rows = [(row >= i * seq_len) & (row < (i + 1) * seq_len) for i in range(n_seq)]
    for h in range(HEADS):
        cols = slice(h * HEAD_DIM, (h + 1) * HEAD_DIM)
        q, ks, v = q_ref[:, cols], k_ref[:, cols], v_ref[:, cols]
        igcol = gcol[:, h:h + 1]
        bcol = bcol_all[:, HEADS + h:HEADS + h + 1]
        igrow = grow[h:h + 1, :]
        brow = brow_all[HEADS + h:HEADS + h + 1, :]
        mprev = mrep_ref[:, h:h + 1]
        s, w_inter, m_t, g = _intra(q, ks, igcol, bcol, igrow, brow, mprev, mask)
        c_all = c_ref[:, h]
        c_stack = c_all.reshape(n_seq * HEAD_DIM, HEAD_DIM).astype(BF16)
        zero = jnp.zeros_like(q)
        q_exp = jnp.concatenate([jnp.where(m, q, zero) for m in seq_rows], axis=1)
        qc = _dot(q_exp, c_stack)
        n_all = n_ref[:, h]
        qn = jnp.sum(q.astype(F32) * _sel_dot(expand, n_all), axis=-1, keepdims=True)
        num = _dot(s.astype(BF16), v) + w_inter * qc
        den = jnp.sum(s, axis=-1, keepdims=True) + w_inter * qn
        yml_ref[:, cols] = _head_out(num, den, m_t, mln_ref[:, cols], og_ref[:, cols])
        packed = jnp.where(lane == 0, m_t, jnp.where(lane == 1, g, jnp.where(lane == 2, bcol, 0.0)))
        ends = _sel_dot(sel_last, packed)
        m_new, g_last, b_last = ends[:, 0:1], ends[:, 1:2], ends[:, 2:3]
        w_end = jnp.exp(b_last - bcol + igcol - m_new)
        dec = jnp.exp(g_last - m_new)
        kw = ks.astype(F32) * w_end
        v_exp = jnp.concatenate([jnp.where(m, v, zero) for m in seq_rows], axis=1)
        upd = _dot(kw.T.astype(BF16), v_exp)
        dec_seq = _sel_dot(pick_last, jnp.broadcast_to(dec, (CHUNK, HEAD_DIM)))
        for i in range(n_seq):
            c_out_ref[i, h] = dec_seq[i:i + 1, 0:1] * c_all[i] + upd[:, i * HEAD_DIM:(i + 1) * HEAD_DIM]
        n_out_ref[:, h] = dec_seq * n_all + _sel_dot(seq_sum, kw)
        m_out_ref[:, h] = _sel_dot(pick_last, jnp.broadcast_to(m_t, (CHUNK, HEAD_DIM)))


def _mlstm_sample(q, k, v, og, gcol, grow, mrep, c0, n0, mln, tri, triu, sel_last, expand, pick_last,
                  seq_sum, seq_len):
    n = q.shape[0]
    n_seq = CHUNK // seq_len
    n_batch = n // seq_len
    tok = lambda w: pl.BlockSpec((CHUNK, w), lambda i: (i, 0))
    c_spec = pl.BlockSpec((n_seq, HEADS, HEAD_DIM, HEAD_DIM), lambda i: (i, 0, 0, 0))
    n_spec = pl.BlockSpec((n_seq, HEADS, HEAD_DIM), lambda i: (i, 0, 0))
    return pl.pallas_call(
        functools.partial(_mlstm_sample_kernel, seq_len),
        grid=(n // CHUNK,),
        in_specs=[
            tok(ML_WIDTH), tok(ML_WIDTH), tok(ML_WIDTH), tok(ML_WIDTH), tok(GATE_LANES),
            pl.BlockSpec((GATE_ROWS, CHUNK), lambda i: (0, i)),
            pl.BlockSpec((CHUNK, HEADS), lambda i: (i, 0)),
            c_spec, n_spec,
            _const_spec((1, ML_WIDTH)),
            _const_spec((CHUNK, CHUNK)), _const_spec((CHUNK, CHUNK)), _const_spec((CHUNK, CHUNK)),
            _const_spec((CHUNK, n_seq)), _const_spec((n_seq, CHUNK)), _const_spec((n_seq, CHUNK)),
        ],
        out_specs=(tok(ML_WIDTH), c_spec, n_spec, n_spec),
        out_shape=(
            jax.ShapeDtypeStruct((n, ML_WIDTH), BF16),
            jax.ShapeDtypeStruct((n_batch, HEADS, HEAD_DIM, HEAD_DIM), F32),
            jax.ShapeDtypeStruct((n_batch, HEADS, HEAD_DIM), F32),
            jax.ShapeDtypeStruct((n_batch, HEADS, HEAD_DIM), F32),
        ),
        compiler_params=pltpu.CompilerParams(dimension_semantics=("arbitrary",),
                                             vmem_limit_bytes=VMEM_LIMIT),
        name="mlstm_sample",
    )(q, k, v, og, gcol, grow, mrep, c0, n0, mln, tri, triu, sel_last, expand, pick_last, seq_sum)


def _back_kernel(final_norm, x_ref, ygm_ref, yml_ref, pe_ref, wout_a_ref, wout_b_ref, nffn_ref, wup_ref,
                 wdown_ref, nple_ref, wpg_ref, wpp_ref, nfin_ref, out_ref):
    h = x_ref[...] + _dot(ygm_ref[...], wout_a_ref[...]) + _dot(yml_ref[...], wout_b_ref[...])
    a = _rms(h, nffn_ref[...]).astype(BF16)
    ff = D_FF // FF_SPLIT

    def mlp_part(c):
        f = _dot(a, wup_ref[:, c * ff:(c + 1) * ff])
        f = jnp.square(jnp.maximum(f, 0.0)).astype(BF16)
        return _dot(f, wdown_ref[c * ff:(c + 1) * ff, :])

    mlp = mlp_part(0)
    for c in range(1, FF_SPLIT):
        mlp = mlp + mlp_part(c)
    h = h + mlp
    gate = jax.nn.sigmoid(_dot(_rms(h, nple_ref[...]).astype(BF16), wpg_ref[...]))
    h = h + gate * _dot(pe_ref[...].astype(BF16), wpp_ref[...])
    if final_norm:
        h = _rms(h, nfin_ref[...])
    out_ref[...] = h


def _back(x2d, ygm, yml, pe2d, wout_a, wout_b, nffn, wup, wdown, nple, wpg, wpp, nfin, final_norm, tb):
    n = x2d.shape[0]
    tok = lambda w: pl.BlockSpec((tb, w), lambda i: (i, 0))
    return pl.pallas_call(
        functools.partial(_back_kernel, final_norm),
        grid=(n // tb,),
        in_specs=[
            tok(D_MODEL), tok(GM_WIDTH), tok(ML_WIDTH), tok(PLE_DIM),
            _const_spec((GM_WIDTH, D_MODEL)), _const_spec((ML_WIDTH, D_MODEL)),
            _const_spec((1, D_MODEL)),
            _const_spec((D_MODEL, D_FF)), _const_spec((D_FF, D_MODEL)),
            _const_spec((1, D_MODEL)),
            _const_spec((D_MODEL, D_MODEL)), _const_spec((PLE_DIM, D_MODEL)),
            _const_spec((1, D_MODEL)),
        ],
        out_specs=tok(D_MODEL),
        out_shape=jax.ShapeDtypeStruct((n, D_MODEL), F32),
        compiler_params=pltpu.CompilerParams(dimension_semantics=("arbitrary",),
                                             vmem_limit_bytes=VMEM_LIMIT),
        name="back",
    )(x2d, ygm, yml, pe2d, wout_a, wout_b, nffn, wup, wdown, nple, wpg, wpp, nfin)


def _block_tri(block, dtype):
    r = jnp.arange(CHUNK)[:, None]
    c = jnp.arange(CHUNK)[None, :]
    return ((r // block == c // block) & (c <= r)).astype(dtype)


def _token_block(n):
    return 512 if n % 512 == 0 else CHUNK


def kernel(x_prompt, x_sample, p_prompt, p_sample, state_C, state_n, state_m, norm_mix, w_in, gm_ln_g,
           gm_ln_b, gm_ws, gm_bs, ml_b_i, ml_b_f, ml_norm, w_out, norm_ffn, w_up, w_down, norm_ple,
           w_ple_gate, w_ple_proj, norm_final):
    depth = w_in.shape[0]
    batch, seq, _ = x_prompt.shape
    dec_batch, dec_seq, _ = x_sample.shape
    n_p, n_s = batch * seq, dec_batch * dec_seq
    n_seq = CHUNK // dec_seq
    assert seq % CHUNK == 0 and CHUNK % dec_seq == 0 and n_s % CHUNK == 0
    tb_p, tb_s = _token_block(seq), _token_block(n_s)

    hp = x_prompt.reshape(n_p, D_MODEL)
    hs = x_sample.reshape(n_s, D_MODEL)

    tri_p = _block_tri(CHUNK, BF16)
    tri_s = _block_tri(dec_seq, BF16)
    r = jnp.arange(CHUNK)
    i = jnp.arange(n_seq)
    sel_last = (r[None, :] == (r[:, None] // dec_seq) * dec_seq + dec_seq - 1).astype(BF16)
    expand = (r[:, None] // dec_seq == i[None, :]).astype(BF16)
    pick_last = (r[None, :] == i[:, None] * dec_seq + dec_seq - 1).astype(BF16)
    seq_sum = (r[None, :] // dec_seq == i[:, None]).astype(BF16)

    outs = {k: [] for k in ("Cp", "Np", "Mp", "Vp", "Cs", "Ns", "Ms", "Vs")}
    for l in range(depth):
        row = lambda a: a[l].reshape(1, -1).astype(F32)
        wmain = w_in[l][:, :MAIN_COLS].astype(BF16)
        wg = w_in[l][:, MAIN_COLS:]
        wg_i, wg_f = wg[:, :HEADS], wg[:, HEADS:]
        b_i, b_f = ml_b_i[l].astype(F32), ml_b_f[l].astype(F32)
        lane_pad = lambda a: jnp.pad(a, ((0, 0), (0, GATE_LANES - HEADS)))
        row_pad = lambda a: jnp.pad(a, ((0, GATE_ROWS - HEADS), (0, 0)))
        wgc = jnp.concatenate([lane_pad(wg_i), lane_pad(wg_f)], axis=1).astype(BF16)
        wgr = jnp.concatenate([row_pad(wg_i.T), row_pad(wg_f.T)], axis=0).astype(BF16)
        bgc = jnp.concatenate([lane_pad(b_i[None]), lane_pad(b_f[None])], axis=1)
        bgr = jnp.broadcast_to(jnp.concatenate([row_pad(b_i[:, None]), row_pad(b_f[:, None])], axis=0),
                               (2 * GATE_ROWS, tb_p))
        wgate = jnp.pad(wg, ((0, 0), (0, GATE_LANES - 2 * HEADS))).astype(BF16)
        wgate_t = jnp.pad(wg.T, ((0, GATE_ROWS - 2 * HEADS), (0, 0))).astype(BF16)
        gbias = jnp.concatenate([b_i, b_f])
        bcol = jnp.pad(gbias, (0, GATE_LANES - 2 * HEADS)).reshape(1, GATE_LANES)
        brow = jnp.broadcast_to(jnp.pad(gbias, (0, GATE_ROWS - 2 * HEADS)).reshape(GATE_ROWS, 1),
                                (GATE_ROWS, tb_s))
        ws = gm_ws[l]
        mixw_p = (ws[:, :CHUNK, :CHUNK] * tri_p.astype(F32)).astype(BF16)
        mixb_p = jnp.broadcast_to(gm_bs[l][:, :CHUNK, None], (HEADS, CHUNK, CHUNK)).astype(F32)
        mixw_s = (jnp.tile(ws[:, :dec_seq, :dec_seq], (1, n_seq, n_seq)) * tri_s.astype(F32)).astype(BF16)
        mixb_s = jnp.broadcast_to(jnp.tile(gm_bs[l][:, :dec_seq], (1, n_seq))[:, :, None],
                                  (HEADS, CHUNK, CHUNK)).astype(F32)
        wout_a = w_out[l][:GM_WIDTH].astype(BF16)
        wout_b = w_out[l][GM_WIDTH:].astype(BF16)
        wup, wdown = w_up[l].astype(BF16), w_down[l].astype(BF16)
        wpg, wpp = w_ple_gate[l].astype(BF16), w_ple_proj[l].astype(BF16)
        last = l == depth - 1
        nfin = norm_final.reshape(1, D_MODEL).astype(F32)

        def back(x2d, ygm, yml, pe, tb):
            return _back(x2d, ygm, yml, pe, wout_a, wout_b, row(norm_ffn), wup, wdown, row(norm_ple),
                         wpg, wpp, nfin, last, tb)

        ygm, yml, vlast, cn, m8 = _mixer_prompt(
            hp, row(norm_mix), wmain, wgc, wgr, bgc, bgr, row(gm_ln_g), row(gm_ln_b), mixw_p, mixb_p,
            row(ml_norm), tri_p, tri_p.T, batch, tb_p)
        hp = back(hp, ygm, yml, p_prompt[l].reshape(n_p, PLE_DIM), tb_p)
        outs["Cp"].append(cn[..., :HEAD_DIM])
        outs["Np"].append(cn[..., HEAD_DIM])
        outs["Mp"].append(m8[:, :HEADS, 0])
        outs["Vp"].append(vlast)

        ygm, q, k, v, og, gcol, grow, vgn = _front(
            hs, row(norm_mix), wmain, wgate, wgate_t, bcol, brow, row(gm_ln_g), row(gm_ln_b), mixw_s, mixb_s,
            tb_s)
        mrep = jnp.repeat(state_m[l].astype(F32), dec_seq, axis=0)
        yml, c_new, n_new, m_new = _mlstm_sample(
            q, k, v, og, gcol, grow, mrep, state_C[l].astype(F32), state_n[l].astype(F32), row(ml_norm),
            tri_s, tri_s.T, sel_last, expand, pick_last, seq_sum, dec_seq)
        hs = back(hs, ygm, yml, p_sample[l].reshape(n_s, PLE_DIM), tb_s)
        outs["Cs"].append(c_new)
        outs["Ns"].append(n_new)
        outs["Ms"].append(m_new[..., 0])
        outs["Vs"].append(vgn.reshape(dec_batch, dec_seq, GM_WIDTH))

    st = lambda k: jnp.stack(outs[k])
    return (hp.reshape(batch, seq, D_MODEL), hs.reshape(dec_batch, dec_seq, D_MODEL),
            st("Cp"), st("Np"), st("Mp"), st("Vp"), st("Cs"), st("Ns"), st("Ms"), st("Vs"))
```

```python
import functools

import jax
import jax.numpy as jnp
from jax import lax
from jax.experimental import pallas as pl
from jax.experimental.pallas import tpu as pltpu

F32 = jnp.float32
BF16 = jnp.bfloat16

D_MODEL = 1024
GM_WIDTH = 512
ML_WIDTH = 512
HEADS = 4
HEAD_DIM = 128
D_FF = 4096
PLE_DIM = 256
EPS = 1e-6
CHUNK = 128
MAIN_COLS = 2 * GM_WIDTH + 4 * ML_WIDTH
Q_LO = 2 * GM_WIDTH
K_LO = Q_LO + ML_WIDTH
V_LO = K_LO + ML_WIDTH
O_LO = V_LO + ML_WIDTH
GATE_LANES = 128
GATE_ROWS = 16
STATE_ROWS = HEAD_DIM + 16
FF_SPLIT = 4
VMEM_LIMIT = 56 * 1024 * 1024


def _dot(a, b):
    return jnp.dot(a, b, preferred_element_type=F32)


def _dot_nt(a, b):
    return lax.dot_general(a, b, (((1,), (1,)), ((), ())), preferred_element_type=F32)


def _dot_tn(a, b):
    return lax.dot_general(a, b, (((0,), (0,)), ((), ())), preferred_element_type=F32)


def _split3(x):
    x1 = x.astype(BF16)
    r = x - x1.astype(F32)
    x2 = r.astype(BF16)
    r = r - x2.astype(F32)
    return x1, x2, r.astype(BF16)


def _sel_dot(sel, x):
    p1, p2, p3 = _split3(x)
    return _dot(sel, p1) + _dot(sel, p2) + _dot(sel, p3)


def _dot_sel(x, sel):
    p1, p2, p3 = _split3(x)
    return _dot(p1, sel) + _dot(p2, sel) + _dot(p3, sel)


def _sel_dot_nt(sel, x):
    p1, p2, p3 = _split3(x)
    return _dot_nt(sel, p1) + _dot_nt(sel, p2) + _dot_nt(sel, p3)


def _rms(x, g):
    return x * lax.rsqrt(jnp.mean(x * x, axis=-1, keepdims=True) + EPS) * g


def _log_sigmoid(x):
    return -(jnp.maximum(-x, 0.0) + jnp.log1p(jnp.exp(-jnp.abs(x))))


def _const_spec(shape):
    nd = len(shape)
    return pl.BlockSpec(shape, lambda *_: (0,) * nd, pipeline_mode=pl.Buffered(1))


def _gmlp_pieces(x_ref, nmix_ref, wmain_ref, lng_ref, lnb_ref, mixw_ref, mixb_ref, ygm_ref, vgn_ref, st):
    tb = x_ref.shape[0]

    def norm():
        st["a"] = _rms(x_ref[...], nmix_ref[...]).astype(BF16)

    def u_part():
        st["u"] = jax.nn.gelu(_dot(st["a"], wmain_ref[:, :GM_WIDTH]))

    def v_part():
        vg = jax.nn.gelu(_dot(st["a"], wmain_ref[:, GM_WIDTH:2 * GM_WIDTH]))
        mu = jnp.mean(vg, axis=-1, keepdims=True)
        var = jnp.mean(jnp.square(vg - mu), axis=-1, keepdims=True)
        vgn = (vg - mu) * lax.rsqrt(var + EPS) * lng_ref[...] + lnb_ref[...]
        vgn_ref[...] = vgn[tb - vgn_ref.shape[0]:, :]
        st["vgb"] = vgn.astype(BF16)

    def mix(c):
        rows = slice(c * CHUNK, (c + 1) * CHUNK)
        for h in range(HEADS):
            cols = slice(h * HEAD_DIM, (h + 1) * HEAD_DIM)
            s = _dot(mixw_ref[h], st["vgb"][rows, cols]) + mixb_ref[h]
            ygm_ref[rows, cols] = (st["u"][rows, cols] * s).astype(BF16)

    return [norm, u_part, v_part] + [functools.partial(mix, c) for c in range(tb // CHUNK)]


def _mixer_prompt_kernel(steps_per_seq, x_ref, nmix_ref, wmain_ref, wqvo_t_ref, wgr_ref, bgr_ref,
                         lng_ref, lnb_ref, mixw_ref, mixb_ref, gain_t_ref, triu_ref, eye_ref,
                         ygm_ref, yml_t_ref, vlast_ref, cn_out_ref, m_out_ref, cn_scr, m_scr):
    tb = x_ref.shape[0]
    n_chunks = tb // CHUNK
    blocks = [slice(c * CHUNK, (c + 1) * CHUNK) for c in range(n_chunks)]
    heads = [slice(h * HEAD_DIM, (h + 1) * HEAD_DIM) for h in range(HEADS)]
    unit_ids = [(c, h) for c in range(n_chunks) for h in range(HEADS)]
    new_seq = pl.program_id(0) % steps_per_seq == 0
    triu = triu_ref[...]
    mask_t = triu.astype(F32) > 0.0
    eye = eye_ref[...]
    lane_g = lax.broadcasted_iota(jnp.int32, (GATE_ROWS, tb), 1) % CHUNK
    sub_g = lax.broadcasted_iota(jnp.int32, (GATE_ROWS, CHUNK), 0)
    ones_row = (sub_g == 0).astype(BF16)
    last = slice(CHUNK - 1, CHUNK)

    st = {}
    front = _gmlp_pieces(x_ref, nmix_ref, wmain_ref, lng_ref, lnb_ref, mixw_ref, mixb_ref, ygm_ref, vlast_ref, st)
    front[0]()
    a = st["a"]

    zr = _dot_nt(wgr_ref[...], a) + bgr_ref[...]
    lf_r = _log_sigmoid(zr[GATE_ROWS:])
    b_r = jnp.concatenate([_dot_sel(lf_r[:, t], triu) for t in blocks], axis=1)
    r_r = zr[:GATE_ROWS] - b_r
    p_r = r_r
    shift = 1
    while shift < CHUNK:
        p_r = jnp.maximum(p_r, jnp.where(lane_g >= shift, pltpu.roll(p_r, shift, axis=1), -jnp.inf))
        shift *= 2
    r_c = [_sel_dot_nt(eye, r_r[:, t]) for t in blocks]
    e_end = [jnp.exp(r_r[:, t] - p_r[:, t][:, last]) for t in blocks]

    for piece in front[1:]:
        piece()
    k = (_dot(a, wmain_ref[:, K_LO:K_LO + ML_WIDTH]) * (HEAD_DIM ** -0.5)).astype(BF16)
    q_t = _dot_nt(wqvo_t_ref[:ML_WIDTH], a).astype(BF16)
    v_t = _dot_nt(wqvo_t_ref[ML_WIDTH:2 * ML_WIDTH], a)
    og_t = jax.nn.sigmoid(_dot_nt(wqvo_t_ref[2 * ML_WIDTH:], a)) * gain_t_ref[...]

    kq, e_intra, x1, upd = {}, {}, {}, {}
    for c, h in unit_ids:
        t, hd = blocks[c], heads[h]
        kq[c, h] = _dot(k[t, hd], q_t[hd, t])
        e_intra[c, h] = jnp.where(mask_t, jnp.exp(r_c[c][:, h:h + 1] - p_r[h:h + 1, t]), 0.0)
    for c, h in unit_ids:
        t, hd = blocks[c], heads[h]
        s0 = (kq[c, h] * e_intra[c, h]).astype(BF16)
        x1[c, h] = _dot(jnp.concatenate([v_t[hd, t].astype(BF16), ones_row], axis=0), s0)
        e_row = e_end[c][h:h + 1, :]
        vw = jnp.concatenate([(v_t[hd, t] * e_row).astype(BF16), jnp.where(sub_g == 0, e_row, 0.0).astype(BF16)],
                             axis=0)
        upd[c, h] = _dot(vw, k[t, hd])

    cn_in, cm, w_inter = {}, {}, {}
    for h in range(HEADS):
        cn = jnp.where(new_seq, 0.0, cn_scr[h])
        m_prev = jnp.where(new_seq, 0.0, m_scr[h:h + 1, 0:1])
        for c in range(n_chunks):
            prow = p_r[h:h + 1, blocks[c]]
            cn_in[c, h] = cn
            cm[c, h] = jnp.maximum(m_prev, prow)
            w_inter[c, h] = jnp.exp(m_prev - cm[c, h])
            cm_last = cm[c, h][:, last]
            cn = jnp.exp(m_prev - cm_last) * cn + jnp.exp(prow[:, last] - cm_last) * upd[c, h]
            m_prev = b_r[h:h + 1, blocks[c]][:, last] + cm_last
        cn_scr[h] = cn
        m_scr[h:h + 1, :] = jnp.broadcast_to(m_prev, (1, HEAD_DIM))
    x2 = {u: _dot(cn_in[u].astype(BF16), q_t[heads[u[1]], blocks[u[0]]]) for u in unit_ids}

    for c, h in unit_ids:
        t, hd = blocks[c], heads[h]
        brow, prow = b_r[h:h + 1, t], p_r[h:h + 1, t]
        nd = x1[c, h] * jnp.exp(prow - cm[c, h]) + x2[c, h] * w_inter[c, h]
        num_t, den = nd[:HEAD_DIM], nd[HEAD_DIM:HEAD_DIM + 1]
        inv = 1.0 / jnp.maximum(jnp.abs(den), jnp.exp(-(brow + cm[c, h])))
        ssq = jnp.sum(num_t * num_t, axis=0, keepdims=True)
        scale = inv * lax.rsqrt(ssq * (inv * inv) * (1.0 / HEAD_DIM) + EPS)
        yml_t_ref[hd, t] = (num_t * scale * og_t[hd, t]).astype(BF16)

    @pl.when(pl.program_id(0) % steps_per_seq == steps_per_seq - 1)
    def _():
        cn_out_ref[...] = cn_scr[...]
        m_out_ref[...] = m_scr[...]


def _mixer_prompt(x2d, nmix, wmain, wqvo_t, wgr, bgr, lng, lnb, mixw, mixb, gain_t, triu, eye, batch, tb):
    n = x2d.shape[0]
    steps_per_seq = n // batch // tb
    tok = lambda w: pl.BlockSpec((tb, w), lambda i: (i, 0))
    per_seq = lambda *shape: pl.BlockSpec((None,) + shape, lambda i: (i // steps_per_seq,) + (0,) * len(shape))
    return pl.pallas_call(
        functools.partial(_mixer_prompt_kernel, steps_per_seq),
        grid=(n // tb,),
        in_specs=[
            tok(D_MODEL),
            _const_spec((1, D_MODEL)),
            _const_spec((D_MODEL, MAIN_COLS)),
            _const_spec((3 * ML_WIDTH, D_MODEL)),
            _const_spec((2 * GATE_ROWS, D_MODEL)),
            _const_spec((2 * GATE_ROWS, tb)),
            _const_spec((1, GM_WIDTH)),
            _const_spec((1, GM_WIDTH)),
            _const_spec((HEADS, CHUNK, CHUNK)),
            _const_spec((HEADS, CHUNK, CHUNK)),
            _const_spec((ML_WIDTH, tb)),
            _const_spec((CHUNK, CHUNK)),
            _const_spec((CHUNK, CHUNK)),
        ],
        out_specs=(tok(GM_WIDTH), pl.BlockSpec((ML_WIDTH, tb), lambda i: (0, i)), per_seq(CHUNK, GM_WIDTH),
                   per_seq(HEADS, STATE_ROWS, HEAD_DIM), per_seq(8, HEAD_DIM)),
        out_shape=(
            jax.ShapeDtypeStruct((n, GM_WIDTH), BF16),
            jax.ShapeDtypeStruct((ML_WIDTH, n), BF16),
            jax.ShapeDtypeStruct((batch, CHUNK, GM_WIDTH), F32),
            jax.ShapeDtypeStruct((batch, HEADS, STATE_ROWS, HEAD_DIM), F32),
            jax.ShapeDtypeStruct((batch, 8, HEAD_DIM), F32),
        ),
        scratch_shapes=[pltpu.VMEM((HEADS, STATE_ROWS, HEAD_DIM), F32), pltpu.VMEM((8, HEAD_DIM), F32)],
        compiler_params=pltpu.CompilerParams(dimension_semantics=("arbitrary",),
                                             vmem_limit_bytes=VMEM_LIMIT),
        name="mixer_prompt",
    )(x2d, nmix, wmain, wqvo_t, wgr, bgr, lng, lnb, mixw, mixb, gain_t, triu, eye)


def _front_kernel(x_ref, nmix_ref, wmain_ref, wgate_ref, wgate_t_ref, bcol_ref, brow_ref,
                  lng_ref, lnb_ref, mixw_ref, mixb_ref,
                  ygm_ref, q_ref, k_ref, v_ref, og_ref, gcol_ref, grow_ref, vgn_ref):
    st = {}
    for piece in _gmlp_pieces(x_ref, nmix_ref, wmain_ref, lng_ref, lnb_ref, mixw_ref, mixb_ref,
                              ygm_ref, vgn_ref, st):
        piece()
    a = st["a"]

    def proj(lo):
        return _dot(a, wmain_ref[:, lo:lo + ML_WIDTH])

    q_ref[...] = proj(Q_LO).astype(BF16)
    k_ref[...] = (proj(K_LO) * (HEAD_DIM ** -0.5)).astype(BF16)
    v_ref[...] = proj(V_LO).astype(BF16)
    og_ref[...] = jax.nn.sigmoid(proj(O_LO))
    zc = _dot(a, wgate_ref[...]) + bcol_ref[...]
    lane = lax.broadcasted_iota(jnp.int32, zc.shape, 1)
    gcol_ref[...] = jnp.where(lane >= HEADS, _log_sigmoid(zc), zc)
    zr = _dot_nt(wgate_t_ref[...], a) + brow_ref[...]
    sub = lax.broadcasted_iota(jnp.int32, zr.shape, 0)
    grow_ref[...] = jnp.where(sub >= HEADS, _log_sigmoid(zr), zr)


def _front(x2d, nmix, wmain, wgate, wgate_t, bcol, brow, lng, lnb, mixw, mixb, tb):
    n = x2d.shape[0]
    tok = lambda w: pl.BlockSpec((tb, w), lambda i: (i, 0))
    out_shape = (
        jax.ShapeDtypeStruct((n, GM_WIDTH), BF16),
        jax.ShapeDtypeStruct((n, ML_WIDTH), BF16),
        jax.ShapeDtypeStruct((n, ML_WIDTH), BF16),
        jax.ShapeDtypeStruct((n, ML_WIDTH), BF16),
        jax.ShapeDtypeStruct((n, ML_WIDTH), F32),
        jax.ShapeDtypeStruct((n, GATE_LANES), F32),
        jax.ShapeDtypeStruct((GATE_ROWS, n), F32),
        jax.ShapeDtypeStruct((n, GM_WIDTH), F32),
    )
    return pl.pallas_call(
        _front_kernel,
        grid=(n // tb,),
        in_specs=[
            tok(D_MODEL),
            _const_spec((1, D_MODEL)),
            _const_spec((D_MODEL, MAIN_COLS)),
            _const_spec((D_MODEL, GATE_LANES)),
            _const_spec((GATE_ROWS, D_MODEL)),
            _const_spec((1, GATE_LANES)),
            _const_spec((GATE_ROWS, tb)),
            _const_spec((1, GM_WIDTH)),
            _const_spec((1, GM_WIDTH)),
            _const_spec((HEADS, CHUNK, CHUNK)),
            _const_spec((HEADS, CHUNK, CHUNK)),
        ],
        out_specs=(tok(GM_WIDTH), tok(ML_WIDTH), tok(ML_WIDTH), tok(ML_WIDTH), tok(ML_WIDTH),
                   tok(GATE_LANES), pl.BlockSpec((GATE_ROWS, tb), lambda i: (0, i)), tok(GM_WIDTH)),
        out_shape=out_shape,
        compiler_params=pltpu.CompilerParams(dimension_semantics=("arbitrary",),
                                             vmem_limit_bytes=VMEM_LIMIT),
        name="front",
    )(x2d, nmix, wmain, wgate, wgate_t, bcol, brow, lng, lnb, mixw, mixb)


def _intra(q, ks, igcol, bcol, igrow, brow, mprev, mask):
    d = bcol + (igrow - brow)
    g = bcol + mprev
    m_t = jnp.maximum(g, jnp.max(jnp.where(mask, d, -jnp.inf), axis=-1, keepdims=True))
    w_intra = jnp.where(mask, jnp.exp(d - m_t), 0.0)
    w_inter = jnp.exp(g - m_t)
    s = _dot_nt(q, ks) * w_intra
    return s, w_inter, m_t, g


def _head_out(num, den, m_t, gain, og):
    hh = num / jnp.maximum(jnp.abs(den), jnp.exp(-m_t))
    return (og * _rms(hh, gain)).astype(BF16)


def _mlstm_sample_kernel(seq_len, q_ref, k_ref, v_ref, og_ref, gcol_ref, grow_ref, mrep_ref, c_ref, n_ref,
                         mln_ref, tri_ref, triu_ref, sel_last_ref, expand_ref, pick_last_ref, seq_sum_ref,
                         yml_ref, c_out_ref, n_out_ref, m_out_ref):
    n_seq = CHUNK // seq_len
    tri = tri_ref[...]
    mask = tri.astype(F32) > 0.0
    gcol = gcol_ref[...]
    grow = grow_ref[...]
    bcol_all = _sel_dot(tri, gcol)
    brow_all = _dot_sel(grow, triu_ref[...])
    sel_last = sel_last_ref[...]
    expand = expand_ref[...]
    pick_last = pick_last_ref[...]
    seq_sum = seq_sum_ref[...]
    row = lax.broadcasted_iota(jnp.int32, (CHUNK, HEAD_DIM), 0)
    lane = lax.broadcasted_iota(jnp.int32, (CHUNK, HEAD_DIM), 1)
    seq_rows = [(row >= i * seq_len) & (row < (i + 1) * seq_len) for i in range(n_seq)]
    for h in range(HEADS):
        cols = slice(h * HEAD_DIM, (h + 1) * HEAD_DIM)
        q, ks, v = q_ref[:, cols], k_ref[:, cols], v_ref[:, cols]
        igcol = gcol[:, h:h + 1]
        bcol = bcol_all[:, HEADS + h:HEADS + h + 1]
        igrow = grow[h:h + 1, :]
        brow = brow_all[HEADS + h:HEADS + h + 1, :]
        mprev = mrep_ref[:, h:h + 1]
        s, w_inter, m_t, g = _intra(q, ks, igcol, bcol, igrow, brow, mprev, mask)
        c_all = c_ref[:, h]
        c_stack = c_all.reshape(n_seq * HEAD_DIM, HEAD_DIM).astype(BF16)
        zero = jnp.zeros_like(q)
        q_exp = jnp.concatenate([jnp.where(m, q, zero) for m in seq_rows], axis=1)
        qc = _dot(q_exp, c_stack)
        n_all = n_ref[:, h]
        qn = jnp.sum(q.astype(F32) * _sel_dot(expand, n_all), axis=-1, keepdims=True)
        num = _dot(s.astype(BF16), v) + w_inter * qc
        den = jnp.sum(s, axis=-1, keepdims=True) + w_inter * qn
        yml_ref[:, cols] = _head_out(num, den, m_t, mln_ref[:, cols], og_ref[:, cols])
        packed = jnp.where(lane == 0, m_t, jnp.where(lane == 1, g, jnp.where(lane == 2, bcol, 0.0)))
        ends = _sel_dot(sel_last, packed)
        m_new, g_last, b_last = ends[:, 0:1], ends[:, 1:2], ends[:, 2:3]
        w_end = jnp.exp(b_last - bcol + igcol - m_new)
        dec = jnp.exp(g_last - m_new)
        kw = ks.astype(F32) * w_end
        v_exp = jnp.concatenate([jnp.where(m, v, zero) for m in seq_rows], axis=1)
        upd = _dot(kw.T.astype(BF16), v_exp)
        dec_seq = _sel_dot(pick_last, jnp.broadcast_to(dec, (CHUNK, HEAD_DIM)))
        for i in range(n_seq):
            c_out_ref[i, h] = dec_seq[i:i + 1, 0:1] * c_all[i] + upd[:, i * HEAD_DIM:(i + 1) * HEAD_DIM]
        n_out_ref[:, h] = dec_seq * n_all + _sel_dot(seq_sum, kw)
        m_out_ref[:, h] = _sel_dot(pick_last, jnp.broadcast_to(m_t, (CHUNK, HEAD_DIM)))


def _mlstm_sample(q, k, v, og, gcol, grow, mrep, c0, n0, mln, tri, triu, sel_last, expand, pick_last,
                  seq_sum, seq_len):
    n = q.shape[0]
    n_seq = CHUNK // seq_len
    n_batch = n // seq_len
    tok = lambda w: pl.BlockSpec((CHUNK, w), lambda i: (i, 0))
    c_spec = pl.BlockSpec((n_seq, HEADS, HEAD_DIM, HEAD_DIM), lambda i: (i, 0, 0, 0))
    n_spec = pl.BlockSpec((n_seq, HEADS, HEAD_DIM), lambda i: (i, 0, 0))
    return pl.pallas_call(
        functools.partial(_mlstm_sample_kernel, seq_len),
        grid=(n // CHUNK,),
        in_specs=[
            tok(ML_WIDTH), tok(ML_WIDTH), tok(ML_WIDTH), tok(ML_WIDTH), tok(GATE_LANES),
            pl.BlockSpec((GATE_ROWS, CHUNK), lambda i: (0, i)),
            pl.BlockSpec((CHUNK, HEADS), lambda i: (i, 0)),
            c_spec, n_spec,
            _const_spec((1, ML_WIDTH)),
            _const_spec((CHUNK, CHUNK)), _const_spec((CHUNK, CHUNK)), _const_spec((CHUNK, CHUNK)),
            _const_spec((CHUNK, n_seq)), _const_spec((n_seq, CHUNK)), _const_spec((n_seq, CHUNK)),
        ],
        out_specs=(tok(ML_WIDTH), c_spec, n_spec, n_spec),
        out_shape=(
            jax.ShapeDtypeStruct((n, ML_WIDTH), BF16),
            jax.ShapeDtypeStruct((n_batch, HEADS, HEAD_DIM, HEAD_DIM), F32),
            jax.ShapeDtypeStruct((n_batch, HEADS, HEAD_DIM), F32),
            jax.ShapeDtypeStruct((n_batch, HEADS, HEAD_DIM), F32),
        ),
        compiler_params=pltpu.CompilerParams(dimension_semantics=("arbitrary",),
                                             vmem_limit_bytes=VMEM_LIMIT),
        name="mlstm_sample",
    )(q, k, v, og, gcol, grow, mrep, c0, n0, mln, tri, triu, sel_last, expand, pick_last, seq_sum)


def _back_kernel(final_norm, yml_transposed, x_ref, ygm_ref, yml_ref, pe_ref, wout_a_ref, wout_b_ref, nffn_ref,
                 wup_ref, wdown_ref, nple_ref, wpg_ref, wpp_ref, nfin_ref, out_ref):
    ml_dot = _dot_tn if yml_transposed else _dot
    h = x_ref[...] + _dot(ygm_ref[...], wout_a_ref[...]) + ml_dot(yml_ref[...], wout_b_ref[...])
    a = _rms(h, nffn_ref[...]).astype(BF16)
    ff = D_FF // FF_SPLIT

    def mlp_part(c):
        f = _dot(a, wup_ref[:, c * ff:(c + 1) * ff])
        f = jnp.square(jnp.maximum(f, 0.0)).astype(BF16)
        return _dot(f, wdown_ref[c * ff:(c + 1) * ff, :])

    mlp = mlp_part(0)
    for c in range(1, FF_SPLIT):
        mlp = mlp + mlp_part(c)
    h = h + mlp
    gate = jax.nn.sigmoid(_dot(_rms(h, nple_ref[...]).astype(BF16), wpg_ref[...]))
    h = h + gate * _dot(pe_ref[...].astype(BF16), wpp_ref[...])
    if final_norm:
        h = _rms(h, nfin_ref[...])
    out_ref[...] = h


def _back(x2d, ygm, yml, pe2d, wout_a, wout_b, nffn, wup, wdown, nple, wpg, wpp, nfin, final_norm, tb):
    n = x2d.shape[0]
    yml_transposed = yml.shape[0] != n
    tok = lambda w: pl.BlockSpec((tb, w), lambda i: (i, 0))
    yml_spec = pl.BlockSpec((ML_WIDTH, tb), lambda i: (0, i)) if yml_transposed else tok(ML_WIDTH)
    return pl.pallas_call(
        functools.partial(_back_kernel, final_norm, yml_transposed),
        grid=(n // tb,),
        in_specs=[
            tok(D_MODEL), tok(GM_WIDTH), yml_spec, tok(PLE_DIM),
            _const_spec((GM_WIDTH, D_MODEL)), _const_spec((ML_WIDTH, D_MODEL)),
            _const_spec((1, D_MODEL)),
            _const_spec((D_MODEL, D_FF)), _const_spec((D_FF, D_MODEL)),
            _const_spec((1, D_MODEL)),
            _const_spec((D_MODEL, D_MODEL)), _const_spec((PLE_DIM, D_MODEL)),
            _const_spec((1, D_MODEL)),
        ],
        out_specs=tok(D_MODEL),
        out_shape=jax.ShapeDtypeStruct((n, D_MODEL), F32),
        compiler_params=pltpu.CompilerParams(dimension_semantics=("arbitrary",),
                                             vmem_limit_bytes=VMEM_LIMIT),
        name="back",
    )(x2d, ygm, yml, pe2d, wout_a, wout_b, nffn, wup, wdown, nple, wpg, wpp, nfin)


def _block_tri(block, dtype):
    r = jnp.arange(CHUNK)[:, None]
    c = jnp.arange(CHUNK)[None, :]
    return ((r // block == c // block) & (c <= r)).astype(dtype)


def _token_block(n):
    return 512 if n % 512 == 0 else CHUNK


def kernel(x_prompt, x_sample, p_prompt, p_sample, state_C, state_n, state_m, norm_mix, w_in, gm_ln_g,
           gm_ln_b, gm_ws, gm_bs, ml_b_i, ml_b_f, ml_norm, w_out, norm_ffn, w_up, w_down, norm_ple,
           w_ple_gate, w_ple_proj, norm_final):
    depth = w_in.shape[0]
    batch, seq, _ = x_prompt.shape
    dec_batch, dec_seq, _ = x_sample.shape
    n_p, n_s = batch * seq, dec_batch * dec_seq
    n_seq = CHUNK // dec_seq
    assert seq % CHUNK == 0 and CHUNK % dec_seq == 0 and n_s % CHUNK == 0
    tb_p, tb_s = _token_block(seq), _token_block(n_s)

    hp = x_prompt.reshape(n_p, D_MODEL)
    hs = x_sample.reshape(n_s, D_MODEL)

    tri_p = _block_tri(CHUNK, BF16)
    tri_s = _block_tri(dec_seq, BF16)
    eye = jnp.eye(CHUNK, dtype=BF16)
    r = jnp.arange(CHUNK)
    i = jnp.arange(n_seq)
    sel_last = (r[None, :] == (r[:, None] // dec_seq) * dec_seq + dec_seq - 1).astype(BF16)
    expand = (r[:, None] // dec_seq == i[None, :]).astype(BF16)
    pick_last = (r[None, :] == i[:, None] * dec_seq + dec_seq - 1).astype(BF16)
    seq_sum = (r[None, :] // dec_seq == i[:, None]).astype(BF16)

    outs = {k: [] for k in ("Cp", "Np", "Mp", "Vp", "Cs", "Ns", "Ms", "Vs")}
    for l in range(depth):
        row = lambda a: a[l].reshape(1, -1).astype(F32)
        wmain = w_in[l][:, :MAIN_COLS].astype(BF16)
        wg = w_in[l][:, MAIN_COLS:]
        wg_i, wg_f = wg[:, :HEADS], wg[:, HEADS:]
        b_i, b_f = ml_b_i[l].astype(F32), ml_b_f[l].astype(F32)
        row_pad = lambda a: jnp.pad(a, ((0, GATE_ROWS - HEADS), (0, 0)))
        wqvo_t = jnp.concatenate([w_in[l][:, Q_LO:K_LO], w_in[l][:, V_LO:MAIN_COLS]], axis=1).T.astype(BF16)
        wgr = jnp.concatenate([row_pad(wg_i.T), row_pad(wg_f.T)], axis=0).astype(BF16)
        bgr = jnp.broadcast_to(jnp.concatenate([row_pad(b_i[:, None]), row_pad(b_f[:, None])], axis=0),
                               (2 * GATE_ROWS, tb_p))
        gain_t = jnp.broadcast_to(ml_norm[l].astype(F32)[:, None], (ML_WIDTH, tb_p))
        wgate = jnp.pad(wg, ((0, 0), (0, GATE_LANES - 2 * HEADS))).astype(BF16)
        wgate_t = jnp.pad(wg.T, ((0, GATE_ROWS - 2 * HEADS), (0, 0))).astype(BF16)
        gbias = jnp.concatenate([b_i, b_f])
        bcol = jnp.pad(gbias, (0, GATE_LANES - 2 * HEADS)).reshape(1, GATE_LANES)
        brow = jnp.broadcast_to(jnp.pad(gbias, (0, GATE_ROWS - 2 * HEADS)).reshape(GATE_ROWS, 1),
                                (GATE_ROWS, tb_s))
        ws = gm_ws[l]
        mixw_p = (ws[:, :CHUNK, :CHUNK] * tri_p.astype(F32)).astype(BF16)
        mixb_p = jnp.broadcast_to(gm_bs[l][:, :CHUNK, None], (HEADS, CHUNK, CHUNK)).astype(F32)
        mixw_s = (jnp.tile(ws[:, :dec_seq, :dec_seq], (1, n_seq, n_seq)) * tri_s.astype(F32)).astype(BF16)
        mixb_s = jnp.broadcast_to(jnp.tile(gm_bs[l][:, :dec_seq], (1, n_seq))[:, :, None],
                                  (HEADS, CHUNK, CHUNK)).astype(F32)
        wout_a = w_out[l][:GM_WIDTH].astype(BF16)
        wout_b = w_out[l][GM_WIDTH:].astype(BF16)
        wup, wdown = w_up[l].astype(BF16), w_down[l].astype(BF16)
        wpg, wpp = w_ple_gate[l].astype(BF16), w_ple_proj[l].astype(BF16)
        last = l == depth - 1
        nfin = norm_final.reshape(1, D_MODEL).astype(F32)

        def back(x2d, ygm, yml, pe, tb):
            return _back(x2d, ygm, yml, pe, wout_a, wout_b, row(norm_ffn), wup, wdown, row(norm_ple),
                         wpg, wpp, nfin, last, tb)

        ygm, yml_t, vlast, cn_t, m8 = _mixer_prompt(
            hp, row(norm_mix), wmain, wqvo_t, wgr, bgr, row(gm_ln_g), row(gm_ln_b), mixw_p, mixb_p,
            gain_t, tri_p.T, eye, batch, tb_p)
        hp = back(hp, ygm, yml_t, p_prompt[l].reshape(n_p, PLE_DIM), tb_p)
        outs["Cp"].append(jnp.swapaxes(cn_t[:, :, :HEAD_DIM, :], -1, -2))
        outs["Np"].append(cn_t[:, :, HEAD_DIM, :])
        outs["Mp"].append(m8[:, :HEADS, 0])
        outs["Vp"].append(vlast)

        ygm, q, k, v, og, gcol, grow, vgn = _front(
            hs, row(norm_mix), wmain, wgate, wgate_t, bcol, brow, row(gm_ln_g), row(gm_ln_b), mixw_s, mixb_s,
            tb_s)
        mrep = jnp.repeat(state_m[l].astype(F32), dec_seq, axis=0)
        yml, c_new, n_new, m_new = _mlstm_sample(
            q, k, v, og, gcol, grow, mrep, state_C[l].astype(F32), state_n[l].astype(F32), row(ml_norm),
            tri_s, tri_s.T, sel_last, expand, pick_last, seq_sum, dec_seq)
        hs = back(hs, ygm, yml, p_sample[l].reshape(n_s, PLE_DIM), tb_s)
        outs["Cs"].append(c_new)
        outs["Ns"].append(n_new)
        outs["Ms"].append(m_new[..., 0])
        outs["Vs"].append(vgn.reshape(dec_batch, dec_seq, GM_WIDTH))

    st = lambda k: outs[k][0][None] if depth == 1 else jnp.stack(outs[k])
    return (hp.reshape(batch, seq, D_MODEL), hs.reshape(dec_batch, dec_seq, D_MODEL),
            st("Cp"), st("Np"), st("Mp"), st("Vp"), st("Cs"), st("Ns"), st("Ms"), st("Vs"))
```

```python
import functools

import jax
import jax.numpy as jnp
from jax import lax
from jax.experimental import pallas as pl
from jax.experimental.pallas import tpu as pltpu

F32 = jnp.float32
BF16 = jnp.bfloat16

D_MODEL = 1024
GM_WIDTH = 512
ML_WIDTH = 512
HEADS = 4
HEAD_DIM = 128
D_FF = 4096
PLE_DIM = 256
EPS = 1e-6
CHUNK = 128
MAIN_COLS = 2 * GM_WIDTH + 4 * ML_WIDTH
Q_LO = 2 * GM_WIDTH
K_LO = Q_LO + ML_WIDTH
V_LO = K_LO + ML_WIDTH
O_LO = V_LO + ML_WIDTH
IN_COLS = MAIN_COLS + 2 * HEADS
GATE_LANES = 128
GATE_ROWS = 16
STATE_ROWS = HEAD_DIM + 16
FF_SPLIT = 4
VMEM_LIMIT = 56 * 1024 * 1024


def _dot(a, b):
    return jnp.dot(a, b, preferred_element_type=F32)


def _dot_nt(a, b):
    return lax.dot_general(a, b, (((1,), (1,)), ((), ())), preferred_element_type=F32)


def _dot_tn(a, b):
    return lax.dot_general(a, b, (((0,), (0,)), ((), ())), preferred_element_type=F32)


def _split3(x):
    x1 = x.astype(BF16)
    r = x - x1.astype(F32)
    x2 = r.astype(BF16)
    r = r - x2.astype(F32)
    return x1, x2, r.astype(BF16)


def _sel_dot(sel, x):
    p1, p2, p3 = _split3(x)
    return _dot(sel, p1) + _dot(sel, p2) + _dot(sel, p3)


def _dot_sel(x, sel):
    p1, p2, p3 = _split3(x)
    return _dot(p1, sel) + _dot(p2, sel) + _dot(p3, sel)


def _sel_dot_nt(sel, x):
    p1, p2, p3 = _split3(x)
    return _dot_nt(sel, p1) + _dot_nt(sel, p2) + _dot_nt(sel, p3)


def _rms(x, g):
    return x * lax.rsqrt(jnp.mean(x * x, axis=-1, keepdims=True) + EPS) * g


def _log_sigmoid(x):
    return -(jnp.maximum(-x, 0.0) + jnp.log1p(jnp.exp(-jnp.abs(x))))


def _const_spec(shape):
    nd = len(shape)
    return pl.BlockSpec(shape, lambda *_: (0,) * nd, pipeline_mode=pl.Buffered(1))


def _gmlp_pieces(x_ref, nmix_ref, w_t_ref, lng_ref, lnb_ref, mixw_ref, mixb_ref, ygm_ref, vgn_ref, st):
    tb = x_ref.shape[0]

    def norm():
        st["a"] = _rms(x_ref[...], nmix_ref[...]).astype(BF16)

    def u_part():
        st["u"] = jax.nn.gelu(_dot_nt(st["a"], w_t_ref[:GM_WIDTH]))

    def v_part():
        vg = jax.nn.gelu(_dot_nt(st["a"], w_t_ref[GM_WIDTH:2 * GM_WIDTH]))
        mu = jnp.mean(vg, axis=-1, keepdims=True)
        var = jnp.mean(jnp.square(vg - mu), axis=-1, keepdims=True)
        vgn = (vg - mu) * lax.rsqrt(var + EPS) * lng_ref[...] + lnb_ref[...]
        vgn_ref[...] = vgn[tb - vgn_ref.shape[0]:, :]
        st["vgb"] = vgn.astype(BF16)

    def mix(c):
        rows = slice(c * CHUNK, (c + 1) * CHUNK)
        for h in range(HEADS):
            cols = slice(h * HEAD_DIM, (h + 1) * HEAD_DIM)
            s = _dot(mixw_ref[h], st["vgb"][rows, cols]) + mixb_ref[h]
            ygm_ref[rows, cols] = (st["u"][rows, cols] * s).astype(BF16)

    return [norm, u_part, v_part] + [functools.partial(mix, c) for c in range(tb // CHUNK)]


def _mixer_prompt_kernel(steps_per_seq, x_ref, nmix_ref, w_t_ref, wgr_ref, bgr_ref,
                         lng_ref, lnb_ref, mixw_ref, mixb_ref, gain_t_ref, triu_ref, eye_ref,
                         ygm_ref, yml_t_ref, vlast_ref, cn_out_ref, m_out_ref, cn_scr, m_scr):
    tb = x_ref.shape[0]
    n_chunks = tb // CHUNK
    blocks = [slice(c * CHUNK, (c + 1) * CHUNK) for c in range(n_chunks)]
    heads = [slice(h * HEAD_DIM, (h + 1) * HEAD_DIM) for h in range(HEADS)]
    unit_ids = [(c, h) for c in range(n_chunks) for h in range(HEADS)]
    new_seq = pl.program_id(0) % steps_per_seq == 0
    triu = triu_ref[...]
    mask_t = triu.astype(F32) > 0.0
    eye = eye_ref[...]
    lane_g = lax.broadcasted_iota(jnp.int32, (GATE_ROWS, tb), 1) % CHUNK
    sub_g = lax.broadcasted_iota(jnp.int32, (GATE_ROWS, CHUNK), 0)
    ones_row = (sub_g == 0).astype(BF16)
    last = slice(CHUNK - 1, CHUNK)

    st = {}
    norm, u_part, v_part, *mix = _gmlp_pieces(x_ref, nmix_ref, w_t_ref, lng_ref, lnb_ref, mixw_ref, mixb_ref,
                                              ygm_ref, vlast_ref, st)
    norm()
    a = st["a"]

    zr = _dot_nt(wgr_ref[...], a) + bgr_ref[...]
    lf_r = _log_sigmoid(zr[GATE_ROWS:])
    b_r = jnp.concatenate([_dot_sel(lf_r[:, t], triu) for t in blocks], axis=1)
    r_r = zr[:GATE_ROWS] - b_r
    k = (_dot_nt(a, w_t_ref[K_LO:V_LO]) * (HEAD_DIM ** -0.5)).astype(BF16)
    p_r = r_r
    shift = 1
    while shift < CHUNK:
        p_r = jnp.maximum(p_r, jnp.where(lane_g >= shift, pltpu.roll(p_r, shift, axis=1), -jnp.inf))
        shift *= 2
    r_c = [_sel_dot_nt(eye, r_r[:, t]) for t in blocks]
    e_end = [jnp.exp(r_r[:, t] - p_r[:, t][:, last]) for t in blocks]
    q_t = _dot_nt(w_t_ref[Q_LO:K_LO], a).astype(BF16)
    v_t = _dot_nt(w_t_ref[V_LO:O_LO], a)
    og_t = jax.nn.sigmoid(_dot_nt(w_t_ref[O_LO:MAIN_COLS], a)) * gain_t_ref[...]

    zero = jnp.zeros((HEAD_DIM, HEAD_DIM), BF16)

    def block_diag(x, y):
        return jnp.concatenate([jnp.concatenate([x, zero], axis=1), jnp.concatenate([zero, y], axis=1)], axis=0)

    def pair_dot(lhs, rhs):
        out = _dot(jnp.concatenate(lhs, axis=1), block_diag(*rhs))
        return out[:, :HEAD_DIM], out[:, HEAD_DIM:]

    pair_ids = [(c, h) for c in range(n_chunks) for h in range(0, HEADS, 2)]
    kq, e_intra, x1, upd = {}, {}, {}, {}
    for c, h in pair_ids:
        t = blocks[c]
        kq[c, h], kq[c, h + 1] = pair_dot([k[t, heads[h]], k[t, heads[h + 1]]],
                                          [q_t[heads[h], t], q_t[heads[h + 1], t]])
        for g in (h, h + 1):
            e_intra[c, g] = jnp.where(mask_t, jnp.exp(r_c[c][:, g:g + 1] - p_r[g:g + 1, t]), 0.0)
    u_part()
    for c, h in pair_ids:
        t = blocks[c]
        s0, vext, vw = [], [], []
        for g in (h, h + 1):
            s0.append((kq[c, g] * e_intra[c, g]).astype(BF16))
            vext.append(jnp.concatenate([v_t[heads[g], t].astype(BF16), ones_row], axis=0))
            e_row = e_end[c][g:g + 1, :]
            vw.append(jnp.concatenate([(v_t[heads[g], t] * e_row).astype(BF16),
                                       jnp.where(sub_g == 0, e_row, 0.0).astype(BF16)], axis=0))
        x1[c, h], x1[c, h + 1] = pair_dot(vext, s0)
        upd[c, h], upd[c, h + 1] = pair_dot(vw, [k[t, heads[h]], k[t, heads[h + 1]]])
    v_part()

    cn_in, cm, w_inter = {}, {}, {}
    for h in range(HEADS):
        cn = jnp.where(new_seq, 0.0, cn_scr[h])
        m_prev = jnp.where(new_seq, 0.0, m_scr[h:h + 1, 0:1])
        for c in range(n_chunks):
            prow = p_r[h:h + 1, blocks[c]]
            cn_in[c, h] = cn
            cm[c, h] = jnp.maximum(m_prev, prow)
            w_inter[c, h] = jnp.exp(m_prev - cm[c, h])
            cm_last = cm[c, h][:, last]
            cn = jnp.exp(m_prev - cm_last) * cn + jnp.exp(prow[:, last] - cm_last) * upd[c, h]
            m_prev = b_r[h:h + 1, blocks[c]][:, last] + cm_last
        cn_scr[h] = cn
        m_scr[h:h + 1, :] = jnp.broadcast_to(m_prev, (1, HEAD_DIM))
    x2 = {}

    def state_matmuls(c):
        t = blocks[c]
        for h in range(0, HEADS, 2):
            x2[c, h], x2[c, h + 1] = pair_dot([cn_in[c, h].astype(BF16), cn_in[c, h + 1].astype(BF16)],
                                              [q_t[heads[h], t], q_t[heads[h + 1], t]])

    def head_outputs(c):
        t = blocks[c]
        for h, hd in enumerate(heads):
            brow, prow = b_r[h:h + 1, t], p_r[h:h + 1, t]
            nd = x1[c, h] * jnp.exp(prow - cm[c, h]) + x2[c, h] * w_inter[c, h]
            num_t, den = nd[:HEAD_DIM], nd[HEAD_DIM:HEAD_DIM + 1]
            inv = 1.0 / jnp.maximum(jnp.abs(den), jnp.exp(-(brow + cm[c, h])))
            ssq = jnp.sum(num_t * num_t, axis=0, keepdims=True)
            scale = inv * lax.rsqrt(ssq * (inv * inv) * (1.0 / HEAD_DIM) + EPS)
            yml_t_ref[hd, t] = (num_t * scale * og_t[hd, t]).astype(BF16)

    for c in range(n_chunks):
        state_matmuls(c)
    for c in range(n_chunks):
        head_outputs(c)
        mix[c]()

    @pl.when(pl.program_id(0) % steps_per_seq == steps_per_seq - 1)
    def _():
        cn_out_ref[...] = cn_scr[...]
        m_out_ref[...] = m_scr[...]


def _mixer_prompt(x2d, nmix, w_t, wgr, bgr, lng, lnb, mixw, mixb, gain_t, triu, eye, batch, tb):
    n = x2d.shape[0]
    steps_per_seq = n // batch // tb
    tok = lambda w: pl.BlockSpec((tb, w), lambda i: (i, 0))
    per_seq = lambda *shape: pl.BlockSpec((None,) + shape, lambda i: (i // steps_per_seq,) + (0,) * len(shape))
    return pl.pallas_call(
        functools.partial(_mixer_prompt_kernel, steps_per_seq),
        grid=(n // tb,),
        in_specs=[
            tok(D_MODEL),
            _const_spec((1, D_MODEL)),
            _const_spec((IN_COLS, D_MODEL)),
            _const_spec((2 * GATE_ROWS, D_MODEL)),
            _const_spec((2 * GATE_ROWS, tb)),
            _const_spec((1, GM_WIDTH)),
            _const_spec((1, GM_WIDTH)),
            _const_spec((HEADS, CHUNK, CHUNK)),
            _const_spec((HEADS, CHUNK, CHUNK)),
            _const_spec((ML_WIDTH, tb)),
            _const_spec((CHUNK, CHUNK)),
            _const_spec((CHUNK, CHUNK)),
        ],
        out_specs=(tok(GM_WIDTH), pl.BlockSpec((ML_WIDTH, tb), lambda i: (0, i)), per_seq(CHUNK, GM_WIDTH),
                   per_seq(HEADS, STATE_ROWS, HEAD_DIM), per_seq(8, HEAD_DIM)),
        out_shape=(
            jax.ShapeDtypeStruct((n, GM_WIDTH), BF16),
            jax.ShapeDtypeStruct((ML_WIDTH, n), BF16),
            jax.ShapeDtypeStruct((batch, CHUNK, GM_WIDTH), F32),
            jax.ShapeDtypeStruct((batch, HEADS, STATE_ROWS, HEAD_DIM), F32),
            jax.ShapeDtypeStruct((batch, 8, HEAD_DIM), F32),
        ),
        scratch_shapes=[pltpu.VMEM((HEADS, STATE_ROWS, HEAD_DIM), F32), pltpu.VMEM((8, HEAD_DIM), F32)],
        compiler_params=pltpu.CompilerParams(dimension_semantics=("arbitrary",),
                                             vmem_limit_bytes=VMEM_LIMIT),
        name="mixer_prompt",
    )(x2d, nmix, w_t, wgr, bgr, lng, lnb, mixw, mixb, gain_t, triu, eye)


def _front_kernel(x_ref, nmix_ref, w_t_ref, wgate_ref, wgate_t_ref, bcol_ref, brow_ref,
                  lng_ref, lnb_ref, mixw_ref, mixb_ref,
                  ygm_ref, q_ref, k_ref, v_ref, og_ref, gcol_ref, grow_ref, vgn_ref):
    st = {}
    for piece in _gmlp_pieces(x_ref, nmix_ref, w_t_ref, lng_ref, lnb_ref, mixw_ref, mixb_ref,
                              ygm_ref, vgn_ref, st):
        piece()
    a = st["a"]

    def proj(lo):
        return _dot_nt(a, w_t_ref[lo:lo + ML_WIDTH])

    q_ref[...] = proj(Q_LO).astype(BF16)
    k_ref[...] = (proj(K_LO) * (HEAD_DIM ** -0.5)).astype(BF16)
    v_ref[...] = proj(V_LO).astype(BF16)
    og_ref[...] = jax.nn.sigmoid(proj(O_LO))
    zc = _dot(a, wgate_ref[...]) + bcol_ref[...]
    lane = lax.broadcasted_iota(jnp.int32, zc.shape, 1)
    gcol_ref[...] = jnp.where(lane >= HEADS, _log_sigmoid(zc), zc)
    zr = _dot_nt(wgate_t_ref[...], a) + brow_ref[...]
    sub = lax.broadcasted_iota(jnp.int32, zr.shape, 0)
    grow_ref[...] = jnp.where(sub >= HEADS, _log_sigmoid(zr), zr)


def _front(x2d, nmix, w_t, wgate, wgate_t, bcol, brow, lng, lnb, mixw, mixb, tb):
    n = x2d.shape[0]
    tok = lambda w: pl.BlockSpec((tb, w), lambda i: (i, 0))
    out_shape = (
        jax.ShapeDtypeStruct((n, GM_WIDTH), BF16),
        jax.ShapeDtypeStruct((n, ML_WIDTH), BF16),
        jax.ShapeDtypeStruct((n, ML_WIDTH), BF16),
        jax.ShapeDtypeStruct((n, ML_WIDTH), BF16),
        jax.ShapeDtypeStruct((n, ML_WIDTH), F32),
        jax.ShapeDtypeStruct((n, GATE_LANES), F32),
        jax.ShapeDtypeStruct((GATE_ROWS, n), F32),
        jax.ShapeDtypeStruct((n, GM_WIDTH), F32),
    )
    return pl.pallas_call(
        _front_kernel,
        grid=(n // tb,),
        in_specs=[
            tok(D_MODEL),
            _const_spec((1, D_MODEL)),
            _const_spec((IN_COLS, D_MODEL)),
            _const_spec((D_MODEL, GATE_LANES)),
            _const_spec((GATE_ROWS, D_MODEL)),
            _const_spec((1, GATE_LANES)),
            _const_spec((GATE_ROWS, tb)),
            _const_spec((1, GM_WIDTH)),
            _const_spec((1, GM_WIDTH)),
            _const_spec((HEADS, CHUNK, CHUNK)),
            _const_spec((HEADS, CHUNK, CHUNK)),
        ],
        out_specs=(tok(GM_WIDTH), tok(ML_WIDTH), tok(ML_WIDTH), tok(ML_WIDTH), tok(ML_WIDTH),
                   tok(GATE_LANES), pl.BlockSpec((GATE_ROWS, tb), lambda i: (0, i)), tok(GM_WIDTH)),
        out_shape=out_shape,
        compiler_params=pltpu.CompilerParams(dimension_semantics=("arbitrary",),
                                             vmem_limit_bytes=VMEM_LIMIT),
        name="front",
    )(x2d, nmix, w_t, wgate, wgate_t, bcol, brow, lng, lnb, mixw, mixb)


def _intra(q, ks, igcol, bcol, igrow, brow, mprev, mask):
    d = bcol + (igrow - brow)
    g = bcol + mprev
    m_t = jnp.maximum(g, jnp.max(jnp.where(mask, d, -jnp.inf), axis=-1, keepdims=True))
    w_intra = jnp.where(mask, jnp.exp(d - m_t), 0.0)
    w_inter = jnp.exp(g - m_t)
    s = _dot_nt(q, ks) * w_intra
    return s, w_inter, m_t, g


def _head_out(num, den, m_t, gain, og):
    hh = num / jnp.maximum(jnp.abs(den), jnp.exp(-m_t))
    return (og * _rms(hh, gain)).astype(BF16)


def _mlstm_sample_kernel(seq_len, q_ref, k_ref, v_ref, og_ref, gcol_ref, grow_ref, mrep_ref, c_ref, n_ref,
                         mln_ref, tri_ref, triu_ref, sel_last_ref, expand_ref, pick_last_ref, seq_sum_ref,
                         yml_ref, c_out_ref, n_out_ref, m_out_ref):
    n_seq = CHUNK // seq_len
    tri = tri_ref[...]
    mask = tri.astype(F32) > 0.0
    gcol = gcol_ref[...]
    grow = grow_ref[...]
    bcol_all = _sel_dot(tri, gcol)
    brow_all = _dot_sel(grow, triu_ref[...])
    sel_last = sel_last_ref[...]
    expand = expand_ref[...]
    pick_last = pick_last_ref[...]
    seq_sum = seq_sum_ref[...]
    row = lax.broadcasted_iota(jnp.int32, (CHUNK, HEAD_DIM), 0)
    lane = lax.broadcasted_iota(jnp.int32, (CHUNK, HEAD_DIM), 1)
    seq_rows = [(row >= i * seq_len) & (row < (i + 1) * seq_len) for i in range(n_seq)]
    for h in range(HEADS):
        cols = slice(h * HEAD_DIM, (h + 1) * HEAD_DIM)
        q, ks, v = q_ref[:, cols], k_ref[:, cols], v_ref[:, cols]
        igcol = gcol[:, h:h + 1]
        bcol = bcol_all[:, HEADS + h:HEADS + h + 1]
        igrow = grow[h:h + 1, :]
        brow = brow_all[HEADS + h:HEADS + h + 1, :]
        mprev = mrep_ref[:, h:h + 1]
        s, w_inter, m_t, g = _intra(q, ks, igcol, bcol, igrow, brow, mprev, mask)
        c_all = c_ref[:, h]
        c_stack = c_all.reshape(n_seq * HEAD_DIM, HEAD_DIM).astype(BF16)
        zero = jnp.zeros_like(q)
        q_exp = jnp.concatenate([jnp.where(m, q, zero) for m in seq_rows], axis=1)
        qc = _dot(q_exp, c_stack)
        n_all = n_ref[:, h]
        qn = jnp.sum(q.astype(F32) * _sel_dot(expand, n_all), axis=-1, keepdims=True)
        num = _dot(s.astype(BF16), v) + w_inter * qc
        den = jnp.sum(s, axis=-1, keepdims=True) + w_inter * qn
        yml_ref[:, cols] = _head_out(num, den, m_t, mln_ref[:, cols], og_ref[:, cols])
        packed = jnp.where(lane == 0, m_t, jnp.where(lane == 1, g, jnp.where(lane == 2, bcol, 0.0)))
        ends = _sel_dot(sel_last, packed)
        m_new, g_last, b_last = ends[:, 0:1], ends[:, 1:2], ends[:, 2:3]
        w_end = jnp.exp(b_last - bcol + igcol - m_new)
        dec = jnp.exp(g_last - m_new)
        kw = ks.astype(F32) * w_end
        v_exp = jnp.concatenate([jnp.where(m, v, zero) for m in seq_rows], axis=1)
        upd = _dot(kw.T.astype(BF16), v_exp)
        dec_seq = _sel_dot(pick_last, jnp.broadcast_to(dec, (CHUNK, HEAD_DIM)))
        for i in range(n_seq):
            c_out_ref[i, h] = dec_seq[i:i + 1, 0:1] * c_all[i] + upd[:, i * HEAD_DIM:(i + 1) * HEAD_DIM]
        n_out_ref[:, h] = dec_seq * n_all + _sel_dot(seq_sum, kw)
        m_out_ref[:, h] = _sel_dot(pick_last, jnp.broadcast_to(m_t, (CHUNK, HEAD_DIM)))


def _mlstm_sample(q, k, v, og, gcol, grow, mrep, c0, n0, mln, tri, triu, sel_last, expand, pick_last,
                  seq_sum, seq_len):
    n = q.shape[0]
    n_seq = CHUNK // seq_len
    n_batch = n // seq_len
    tok = lambda w: pl.BlockSpec((CHUNK, w), lambda i: (i, 0))
    c_spec = pl.BlockSpec((n_seq, HEADS, HEAD_DIM, HEAD_DIM), lambda i: (i, 0, 0, 0))
    n_spec = pl.BlockSpec((n_seq, HEADS, HEAD_DIM), lambda i: (i, 0, 0))
    return pl.pallas_call(
        functools.partial(_mlstm_sample_kernel, seq_len),
        grid=(n // CHUNK,),
        in_specs=[
            tok(ML_WIDTH), tok(ML_WIDTH), tok(ML_WIDTH), tok(ML_WIDTH), tok(GATE_LANES),
            pl.BlockSpec((GATE_ROWS, CHUNK), lambda i: (0, i)),
            pl.BlockSpec((CHUNK, HEADS), lambda i: (i, 0)),
            c_spec, n_spec,
            _const_spec((1, ML_WIDTH)),
            _const_spec((CHUNK, CHUNK)), _const_spec((CHUNK, CHUNK)), _const_spec((CHUNK, CHUNK)),
            _const_spec((CHUNK, n_seq)), _const_spec((n_seq, CHUNK)), _const_spec((n_seq, CHUNK)),
        ],
        out_specs=(tok(ML_WIDTH), c_spec, n_spec, n_spec),
        out_shape=(
            jax.ShapeDtypeStruct((n, ML_WIDTH), BF16),
            jax.ShapeDtypeStruct((n_batch, HEADS, HEAD_DIM, HEAD_DIM), F32),
            jax.ShapeDtypeStruct((n_batch, HEADS, HEAD_DIM), F32),
            jax.ShapeDtypeStruct((n_batch, HEADS, HEAD_DIM), F32),
        ),
        compiler_params=pltpu.CompilerParams(dimension_semantics=("arbitrary",),
                                             vmem_limit_bytes=VMEM_LIMIT),
        name="mlstm_sample",
    )(q, k, v, og, gcol, grow, mrep, c0, n0, mln, tri, triu, sel_last, expand, pick_last, seq_sum)


def _back_kernel(final_norm, yml_transposed, x_ref, ygm_ref, yml_ref, pe_ref, wout_ref, nffn_ref,
                 wup_ref, wdown_ref, nple_ref, wpg_ref, wpp_ref, nfin_ref, out_ref):
    ml_dot = _dot_tn if yml_transposed else _dot
    h = x_ref[...] + _dot(ygm_ref[...], wout_ref[:GM_WIDTH]) + ml_dot(yml_ref[...], wout_ref[GM_WIDTH:])
    a = _rms(h, nffn_ref[...]).astype(BF16)
    ff = D_FF // FF_SPLIT

    def mlp_part(c):
        f = _dot(a, wup_ref[:, c * ff:(c + 1) * ff])
        f = jnp.square(jnp.maximum(f, 0.0)).astype(BF16)
        return _dot(f, wdown_ref[c * ff:(c + 1) * ff, :])

    mlp = mlp_part(0)
    for c in range(1, FF_SPLIT):
        mlp = mlp + mlp_part(c)
    h = h + mlp
    gate = jax.nn.sigmoid(_dot(_rms(h, nple_ref[...]).astype(BF16), wpg_ref[...]))
    h = h + gate * _dot(pe_ref[...].astype(BF16), wpp_ref[...])
    if final_norm:
        h = _rms(h, nfin_ref[...])
    out_ref[...] = h


def _back(x2d, ygm, yml, pe2d, wout, nffn, wup, wdown, nple, wpg, wpp, nfin, final_norm, tb):
    n = x2d.shape[0]
    yml_transposed = yml.shape[0] != n
    tok = lambda w: pl.BlockSpec((tb, w), lambda i: (i, 0))
    yml_spec = pl.BlockSpec((ML_WIDTH, tb), lambda i: (0, i)) if yml_transposed else tok(ML_WIDTH)
    return pl.pallas_call(
        functools.partial(_back_kernel, final_norm, yml_transposed),
        grid=(n // tb,),
        in_specs=[
            tok(D_MODEL), tok(GM_WIDTH), yml_spec, tok(PLE_DIM),
            _const_spec((GM_WIDTH + ML_WIDTH, D_MODEL)),
            _const_spec((1, D_MODEL)),
            _const_spec((D_MODEL, D_FF)), _const_spec((D_FF, D_MODEL)),
            _const_spec((1, D_MODEL)),
            _const_spec((D_MODEL, D_MODEL)), _const_spec((PLE_DIM, D_MODEL)),
            _const_spec((1, D_MODEL)),
        ],
        out_specs=tok(D_MODEL),
        out_shape=jax.ShapeDtypeStruct((n, D_MODEL), F32),
        compiler_params=pltpu.CompilerParams(dimension_semantics=("arbitrary",),
                                             vmem_limit_bytes=VMEM_LIMIT),
        name="back",
    )(x2d, ygm, yml, pe2d, wout, nffn, wup, wdown, nple, wpg, wpp, nfin)


def _block_tri(block, dtype):
    r = jnp.arange(CHUNK)[:, None]
    c = jnp.arange(CHUNK)[None, :]
    return ((r // block == c // block) & (c <= r)).astype(dtype)


def _token_block(n):
    return 512 if n % 512 == 0 else CHUNK


def kernel(x_prompt, x_sample, p_prompt, p_sample, state_C, state_n, state_m, norm_mix, w_in, gm_ln_g,
           gm_ln_b, gm_ws, gm_bs, ml_b_i, ml_b_f, ml_norm, w_out, norm_ffn, w_up, w_down, norm_ple,
           w_ple_gate, w_ple_proj, norm_final):
    depth = w_in.shape[0]
    batch, seq, _ = x_prompt.shape
    dec_batch, dec_seq, _ = x_sample.shape
    n_p, n_s = batch * seq, dec_batch * dec_seq
    n_seq = CHUNK // dec_seq
    assert seq % CHUNK == 0 and CHUNK % dec_seq == 0 and n_s % CHUNK == 0
    tb_p, tb_s = _token_block(seq), _token_block(n_s)

    hp = x_prompt.reshape(n_p, D_MODEL)
    hs = x_sample.reshape(n_s, D_MODEL)

    tri_p = _block_tri(CHUNK, BF16)
    tri_s = _block_tri(dec_seq, BF16)
    eye = jnp.eye(CHUNK, dtype=BF16)
    r = jnp.arange(CHUNK)
    i = jnp.arange(n_seq)
    sel_last = (r[None, :] == (r[:, None] // dec_seq) * dec_seq + dec_seq - 1).astype(BF16)
    expand = (r[:, None] // dec_seq == i[None, :]).astype(BF16)
    pick_last = (r[None, :] == i[:, None] * dec_seq + dec_seq - 1).astype(BF16)
    seq_sum = (r[None, :] // dec_seq == i[:, None]).astype(BF16)

    outs = {k: [] for k in ("Cp", "Np", "Mp", "Vp", "Cs", "Ns", "Ms", "Vs")}
    for l in range(depth):
        row = lambda a: a[l].reshape(1, -1).astype(F32)
        w_t = w_in[l].T.astype(BF16)
        wg = w_in[l][:, MAIN_COLS:]
        wg_i, wg_f = wg[:, :HEADS], wg[:, HEADS:]
        b_i, b_f = ml_b_i[l].astype(F32), ml_b_f[l].astype(F32)
        row_pad = lambda a: jnp.pad(a, ((0, GATE_ROWS - HEADS), (0, 0)))
        wgr = jnp.concatenate([row_pad(wg_i.T), row_pad(wg_f.T)], axis=0).astype(BF16)
        bgr = jnp.broadcast_to(jnp.concatenate([row_pad(b_i[:, None]), row_pad(b_f[:, None])], axis=0),
                               (2 * GATE_ROWS, tb_p))
        gain_t = jnp.broadcast_to(ml_norm[l].astype(F32)[:, None], (ML_WIDTH, tb_p))
        wgate = jnp.pad(wg, ((0, 0), (0, GATE_LANES - 2 * HEADS))).astype(BF16)
        wgate_t = jnp.pad(wg.T, ((0, GATE_ROWS - 2 * HEADS), (0, 0))).astype(BF16)
        gbias = jnp.concatenate([b_i, b_f])
        bcol = jnp.pad(gbias, (0, GATE_LANES - 2 * HEADS)).reshape(1, GATE_LANES)
        brow = jnp.broadcast_to(jnp.pad(gbias, (0, GATE_ROWS - 2 * HEADS)).reshape(GATE_ROWS, 1),
                                (GATE_ROWS, tb_s))
        ws = gm_ws[l]
        mixw_p = (ws[:, :CHUNK, :CHUNK] * tri_p.astype(F32)).astype(BF16)
        mixb_p = jnp.broadcast_to(gm_bs[l][:, :CHUNK, None], (HEADS, CHUNK, CHUNK)).astype(F32)
        mixw_s = (jnp.tile(ws[:, :dec_seq, :dec_seq], (1, n_seq, n_seq)) * tri_s.astype(F32)).astype(BF16)
        mixb_s = jnp.broadcast_to(jnp.tile(gm_bs[l][:, :dec_seq], (1, n_seq))[:, :, None],
                                  (HEADS, CHUNK, CHUNK)).astype(F32)
        wout = w_out[l].astype(BF16)
        wup, wdown = w_up[l].astype(BF16), w_down[l].astype(BF16)
        wpg, wpp = w_ple_gate[l].astype(BF16), w_ple_proj[l].astype(BF16)
        last = l == depth - 1
        nfin = norm_final.reshape(1, D_MODEL).astype(F32)

        def back(x2d, ygm, yml, pe, tb):
            return _back(x2d, ygm, yml, pe, wout, row(norm_ffn), wup, wdown, row(norm_ple),
                         wpg, wpp, nfin, last, tb)

        ygm, yml_t, vlast, cn_t, m8 = _mixer_prompt(
            hp, row(norm_mix), w_t, wgr, bgr, row(gm_ln_g), row(gm_ln_b), mixw_p, mixb_p,
            gain_t, tri_p.T, eye, batch, tb_p)
        hp = back(hp, ygm, yml_t, p_prompt[l].reshape(n_p, PLE_DIM), 2 * tb_p)
        outs["Cp"].append(jnp.swapaxes(cn_t[:, :, :HEAD_DIM, :], -1, -2))
        outs["Np"].append(cn_t[:, :, HEAD_DIM, :])
        outs["Mp"].append(m8[:, :HEADS, 0])
        outs["Vp"].append(vlast)

        ygm, q, k, v, og, gcol, grow, vgn = _front(
            hs, row(norm_mix), w_t, wgate, wgate_t, bcol, brow, row(gm_ln_g), row(gm_ln_b), mixw_s, mixb_s,
            tb_s)
        mrep = jnp.repeat(state_m[l].astype(F32), dec_seq, axis=0)
        yml, c_new, n_new, m_new = _mlstm_sample(
            q, k, v, og, gcol, grow, mrep, state_C[l].astype(F32), state_n[l].astype(F32), row(ml_norm),
            tri_s, tri_s.T, sel_last, expand, pick_last, seq_sum, dec_seq)
        hs = back(hs, ygm, yml, p_sample[l].reshape(n_s, PLE_DIM), tb_s)
        outs["Cs"].append(c_new)
        outs["Ns"].append(n_new)
        outs["Ms"].append(m_new[..., 0])
        outs["Vs"].append(vgn.reshape(dec_batch, dec_seq, GM_WIDTH))

    st = lambda k: outs[k][0][None] if depth == 1 else jnp.stack(outs[k])
    return (hp.reshape(batch, seq, D_MODEL), hs.reshape(dec_batch, dec_seq, D_MODEL),
            st("Cp"), st("Np"), st("Mp"), st("Vp"), st("Cs"), st("Ns"), st("Ms"), st("Vs"))
```

```python
import functools

import jax
import numpy as np
import jax.numpy as jnp
from jax import lax
from jax.experimental import pallas as pl
from jax.experimental.pallas import tpu as pltpu

F32 = jnp.float32
BF16 = jnp.bfloat16

D_MODEL = 1024
GM_WIDTH = 512
ML_WIDTH = 512
HEADS = 4
HEAD_DIM = 128
D_FF = 4096
PLE_DIM = 256
EPS = 1e-6
CHUNK = 128
MAIN_COLS = 2 * GM_WIDTH + 4 * ML_WIDTH
Q_LO = 2 * GM_WIDTH
K_LO = Q_LO + ML_WIDTH
V_LO = K_LO + ML_WIDTH
O_LO = V_LO + ML_WIDTH
IN_COLS = MAIN_COLS + 2 * HEADS
GATE_LANES = 128
GATE_ROWS = 16
STATE_ROWS = HEAD_DIM + 16
FF_SPLIT = 4
CAST_ROWS = 256
VMEM_LIMIT = 60 * 1024 * 1024


def _dot(a, b):
    return jnp.dot(a, b, preferred_element_type=F32)


def _dot_nt(a, b):
    return lax.dot_general(a, b, (((1,), (1,)), ((), ())), preferred_element_type=F32)


def _dot_tn(a, b):
    return lax.dot_general(a, b, (((0,), (0,)), ((), ())), preferred_element_type=F32)


def _split3(x):
    x1 = x.astype(BF16)
    r = x - x1.astype(F32)
    x2 = r.astype(BF16)
    r = r - x2.astype(F32)
    return x1, x2, r.astype(BF16)


def _sel_dot(sel, x):
    p1, p2, p3 = _split3(x)
    return _dot(sel, p1) + _dot(sel, p2) + _dot(sel, p3)


def _dot_sel(x, sel):
    p1, p2, p3 = _split3(x)
    return _dot(p1, sel) + _dot(p2, sel) + _dot(p3, sel)


def _sel_dot_nt(sel, x):
    p1, p2, p3 = _split3(x)
    return _dot_nt(sel, p1) + _dot_nt(sel, p2) + _dot_nt(sel, p3)


def _rms(x, g):
    return x * lax.rsqrt(jnp.mean(x * x, axis=-1, keepdims=True) + EPS) * g


def _log_sigmoid(x):
    return -(jnp.maximum(-x, 0.0) + jnp.log1p(jnp.exp(-jnp.abs(x))))


def _const_spec(shape):
    nd = len(shape)
    return pl.BlockSpec(shape, lambda *_: (0,) * nd, pipeline_mode=pl.Buffered(1))


def _gmlp_pieces(x_ref, nmix_ref, w_t_ref, lng_ref, lnb_ref, mixw_ref, mixb_ref, ygm_ref, vgn_ref, st):
    tb = x_ref.shape[0]

    def norm():
        st["a"] = _rms(x_ref[...], nmix_ref[...]).astype(BF16)

    def u_part():
        st["u"] = jax.nn.gelu(_dot_nt(st["a"], w_t_ref[:GM_WIDTH]))

    def v_part():
        vg = jax.nn.gelu(_dot_nt(st["a"], w_t_ref[GM_WIDTH:2 * GM_WIDTH]))
        mu = jnp.mean(vg, axis=-1, keepdims=True)
        var = jnp.mean(jnp.square(vg - mu), axis=-1, keepdims=True)
        vgn = (vg - mu) * lax.rsqrt(var + EPS) * lng_ref[...] + lnb_ref[...]
        vgn_ref[...] = vgn[tb - vgn_ref.shape[0]:, :]
        st["vgb"] = vgn.astype(BF16)

    def mix(c):
        rows = slice(c * CHUNK, (c + 1) * CHUNK)
        for h in range(HEADS):
            cols = slice(h * HEAD_DIM, (h + 1) * HEAD_DIM)
            s = _dot(mixw_ref[h], st["vgb"][rows, cols]) + mixb_ref[h]
            ygm_ref[rows, cols] = (st["u"][rows, cols] * s).astype(BF16)

    return [norm, u_part, v_part] + [functools.partial(mix, c) for c in range(tb // CHUNK)]


def _mixer_prompt_kernel(steps_per_seq, x_ref, nmix_ref, w_t_ref, wgr_ref, bgr_ref,
                         lng_ref, lnb_ref, mixw_ref, mixb_ref, gain_t_ref, triu_ref, eye_ref,
                         ygm_ref, yml_t_ref, vlast_ref, cn_out_ref, m_out_ref, cn_scr, m_scr):
    tb = x_ref.shape[0]
    n_chunks = tb // CHUNK
    blocks = [slice(c * CHUNK, (c + 1) * CHUNK) for c in range(n_chunks)]
    heads = [slice(h * HEAD_DIM, (h + 1) * HEAD_DIM) for h in range(HEADS)]
    unit_ids = [(c, h) for c in range(n_chunks) for h in range(HEADS)]
    new_seq = pl.program_id(0) % steps_per_seq == 0
    triu = triu_ref[...]
    mask_t = triu.astype(F32) > 0.0
    eye = eye_ref[...]
    lane_g = lax.broadcasted_iota(jnp.int32, (GATE_ROWS, tb), 1) % CHUNK
    sub_g = lax.broadcasted_iota(jnp.int32, (GATE_ROWS, CHUNK), 0)
    ones_row = (sub_g == 0).astype(BF16)
    last = slice(CHUNK - 1, CHUNK)

    st = {}
    norm, u_part, v_part, *mix = _gmlp_pieces(x_ref, nmix_ref, w_t_ref, lng_ref, lnb_ref, mixw_ref, mixb_ref,
                                              ygm_ref, vlast_ref, st)
    norm()
    a = st["a"]

    zr = _dot_nt(wgr_ref[...], a) + bgr_ref[...]
    lf_r = _log_sigmoid(zr[GATE_ROWS:])
    b_r = jnp.concatenate([_dot_sel(lf_r[:, t], triu) for t in blocks], axis=1)
    r_r = zr[:GATE_ROWS] - b_r
    k = (_dot_nt(a, w_t_ref[K_LO:V_LO]) * (HEAD_DIM ** -0.5)).astype(BF16)
    p_r = r_r
    shift = 1
    while shift < CHUNK:
        p_r = jnp.maximum(p_r, jnp.where(lane_g >= shift, pltpu.roll(p_r, shift, axis=1), -jnp.inf))
        shift *= 2
    r_c = [_sel_dot_nt(eye, r_r[:, t]) for t in blocks]
    e_end = [jnp.exp(r_r[:, t] - p_r[:, t][:, last]) for t in blocks]
    q_t = _dot_nt(w_t_ref[Q_LO:K_LO], a).astype(BF16)
    v_t = _dot_nt(w_t_ref[V_LO:O_LO], a)
    og_t = jax.nn.sigmoid(_dot_nt(w_t_ref[O_LO:MAIN_COLS], a)) * gain_t_ref[...]

    zero = jnp.zeros((HEAD_DIM, HEAD_DIM), BF16)

    def block_diag(x, y):
        return jnp.concatenate([jnp.concatenate([x, zero], axis=1), jnp.concatenate([zero, y], axis=1)], axis=0)

    def pair_dot(lhs, rhs):
        out = _dot(jnp.concatenate(lhs, axis=1), block_diag(*rhs))
        return out[:, :HEAD_DIM], out[:, HEAD_DIM:]

    pair_ids = [(c, h) for c in range(n_chunks) for h in range(0, HEADS, 2)]
    kq, e_intra, x1, upd = {}, {}, {}, {}
    for c, h in pair_ids:
        t = blocks[c]
        kq[c, h], kq[c, h + 1] = pair_dot([k[t, heads[h]], k[t, heads[h + 1]]],
                                          [q_t[heads[h], t], q_t[heads[h + 1], t]])
        for g in (h, h + 1):
            e_intra[c, g] = jnp.where(mask_t, jnp.exp(r_c[c][:, g:g + 1] - p_r[g:g + 1, t]), 0.0)
    u_part()
    for c, h in pair_ids:
        t = blocks[c]
        s0, vext, vw = [], [], []
        for g in (h, h + 1):
            s0.append((kq[c, g] * e_intra[c, g]).astype(BF16))
            vext.append(jnp.concatenate([v_t[heads[g], t].astype(BF16), ones_row], axis=0))
            e_row = e_end[c][g:g + 1, :]
            vw.append(jnp.concatenate([(v_t[heads[g], t] * e_row).astype(BF16),
                                       jnp.where(sub_g == 0, e_row, 0.0).astype(BF16)], axis=0))
        x1[c, h], x1[c, h + 1] = pair_dot(vext, s0)
        upd[c, h], upd[c, h + 1] = pair_dot(vw, [k[t, heads[h]], k[t, heads[h + 1]]])
    v_part()

    cn_in, cm, w_inter = {}, {}, {}
    for h in range(HEADS):
        cn = jnp.where(new_seq, 0.0, cn_scr[h])
        m_prev = jnp.where(new_seq, 0.0, m_scr[h:h + 1, 0:1])
        for c in range(n_chunks):
            prow = p_r[h:h + 1, blocks[c]]
            cn_in[c, h] = cn
            cm[c, h] = jnp.maximum(m_prev, prow)
            w_inter[c, h] = jnp.exp(m_prev - cm[c, h])
            cm_last = cm[c, h][:, last]
            cn = jnp.exp(m_prev - cm_last) * cn + jnp.exp(prow[:, last] - cm_last) * upd[c, h]
            m_prev = b_r[h:h + 1, blocks[c]][:, last] + cm_last
        cn_scr[h] = cn
        m_scr[h:h + 1, :] = jnp.broadcast_to(m_prev, (1, HEAD_DIM))
    x2 = {}

    def state_matmuls(c):
        t = blocks[c]
        for h in range(0, HEADS, 2):
            x2[c, h], x2[c, h + 1] = pair_dot([cn_in[c, h].astype(BF16), cn_in[c, h + 1].astype(BF16)],
                                              [q_t[heads[h], t], q_t[heads[h + 1], t]])

    def head_outputs(c):
        t = blocks[c]
        for h, hd in enumerate(heads):
            brow, prow = b_r[h:h + 1, t], p_r[h:h + 1, t]
            nd = x1[c, h] * jnp.exp(prow - cm[c, h]) + x2[c, h] * w_inter[c, h]
            num_t, den = nd[:HEAD_DIM], nd[HEAD_DIM:HEAD_DIM + 1]
            inv = 1.0 / jnp.maximum(jnp.abs(den), jnp.exp(-(brow + cm[c, h])))
            ssq = jnp.sum(num_t * num_t, axis=0, keepdims=True)
            scale = inv * lax.rsqrt(ssq * (inv * inv) * (1.0 / HEAD_DIM) + EPS)
            yml_t_ref[hd, t] = (num_t * scale * og_t[hd, t]).astype(BF16)

    for c in range(n_chunks):
        state_matmuls(c)
    for c in range(n_chunks):
        head_outputs(c)
        mix[c]()

    @pl.when(pl.program_id(0) % steps_per_seq == steps_per_seq - 1)
    def _():
        cn_out_ref[...] = cn_scr[...]
        m_out_ref[...] = m_scr[...]


def _mixer_prompt(x2d, nmix, w_t, wgr, bgr, lng, lnb, mixw, mixb, gain_t, triu, eye, batch, tb):
    n = x2d.shape[0]
    steps_per_seq = n // batch // tb
    tok = lambda w: pl.BlockSpec((tb, w), lambda i: (i, 0))
    per_seq = lambda *shape: pl.BlockSpec((None,) + shape, lambda i: (i // steps_per_seq,) + (0,) * len(shape))
    return pl.pallas_call(
        functools.partial(_mixer_prompt_kernel, steps_per_seq),
        grid=(n // tb,),
        in_specs=[
            tok(D_MODEL),
            _const_spec((1, D_MODEL)),
            _const_spec((IN_COLS, D_MODEL)),
            _const_spec((2 * GATE_ROWS, D_MODEL)),
            _const_spec((2 * GATE_ROWS, tb)),
            _const_spec((1, GM_WIDTH)),
            _const_spec((1, GM_WIDTH)),
            _const_spec((HEADS, CHUNK, CHUNK)),
            _const_spec((HEADS, CHUNK, CHUNK)),
            _const_spec((ML_WIDTH, tb)),
            _const_spec((CHUNK, CHUNK)),
            _const_spec((CHUNK, CHUNK)),
        ],
        out_specs=(tok(GM_WIDTH), pl.BlockSpec((ML_WIDTH, tb), lambda i: (0, i)), per_seq(CHUNK, GM_WIDTH),
                   per_seq(HEADS, STATE_ROWS, HEAD_DIM), per_seq(HEADS, HEAD_DIM)),
        out_shape=(
            jax.ShapeDtypeStruct((n, GM_WIDTH), BF16),
            jax.ShapeDtypeStruct((ML_WIDTH, n), BF16),
            jax.ShapeDtypeStruct((batch, CHUNK, GM_WIDTH), F32),
            jax.ShapeDtypeStruct((batch, HEADS, STATE_ROWS, HEAD_DIM), F32),
            jax.ShapeDtypeStruct((batch, HEADS, HEAD_DIM), F32),
        ),
        scratch_shapes=[pltpu.VMEM((HEADS, STATE_ROWS, HEAD_DIM), F32), pltpu.VMEM((HEADS, HEAD_DIM), F32)],
        compiler_params=pltpu.CompilerParams(dimension_semantics=("arbitrary",),
                                             vmem_limit_bytes=VMEM_LIMIT),
        name="mixer_prompt",
    )(x2d, nmix, w_t, wgr, bgr, lng, lnb, mixw, mixb, gain_t, triu, eye)


def _front_kernel(x_ref, nmix_ref, w_t_ref, wgate_ref, wgate_t_ref, bcol_ref, brow_ref,
                  lng_ref, lnb_ref, mixw_ref, mixb_ref,
                  ygm_ref, q_ref, k_ref, v_ref, og_ref, gcol_ref, grow_ref, vgn_ref):
    st = {}
    for piece in _gmlp_pieces(x_ref, nmix_ref, w_t_ref, lng_ref, lnb_ref, mixw_ref, mixb_ref,
                              ygm_ref, vgn_ref, st):
        piece()
    a = st["a"]

    def proj(lo):
        return _dot_nt(a, w_t_ref[lo:lo + ML_WIDTH])

    q_ref[...] = proj(Q_LO).astype(BF16)
    k_ref[...] = (proj(K_LO) * (HEAD_DIM ** -0.5)).astype(BF16)
    v_ref[...] = proj(V_LO).astype(BF16)
    og_ref[...] = jax.nn.sigmoid(proj(O_LO))
    zc = _dot(a, wgate_ref[...]) + bcol_ref[...]
    lane = lax.broadcasted_iota(jnp.int32, zc.shape, 1)
    gcol_ref[...] = jnp.where(lane >= HEADS, _log_sigmoid(zc), zc)
    zr = _dot_nt(wgate_t_ref[...], a) + brow_ref[...]
    sub = lax.broadcasted_iota(jnp.int32, zr.shape, 0)
    grow_ref[...] = jnp.where(sub >= HEADS, _log_sigmoid(zr), zr)


def _front(x2d, nmix, w_t, wgate, wgate_t, bcol, brow, lng, lnb, mixw, mixb, tb):
    n = x2d.shape[0]
    tok = lambda w: pl.BlockSpec((tb, w), lambda i: (i, 0))
    out_shape = (
        jax.ShapeDtypeStruct((n, GM_WIDTH), BF16),
        jax.ShapeDtypeStruct((n, ML_WIDTH), BF16),
        jax.ShapeDtypeStruct((n, ML_WIDTH), BF16),
        jax.ShapeDtypeStruct((n, ML_WIDTH), BF16),
        jax.ShapeDtypeStruct((n, ML_WIDTH), F32),
        jax.ShapeDtypeStruct((n, GATE_LANES), F32),
        jax.ShapeDtypeStruct((GATE_ROWS, n), F32),
        jax.ShapeDtypeStruct((n, GM_WIDTH), F32),
    )
    return pl.pallas_call(
        _front_kernel,
        grid=(n // tb,),
        in_specs=[
            tok(D_MODEL),
            _const_spec((1, D_MODEL)),
            _const_spec((IN_COLS, D_MODEL)),
            _const_spec((D_MODEL, GATE_LANES)),
            _const_spec((GATE_ROWS, D_MODEL)),
            _const_spec((1, GATE_LANES)),
            _const_spec((GATE_ROWS, tb)),
            _const_spec((1, GM_WIDTH)),
            _const_spec((1, GM_WIDTH)),
            _const_spec((HEADS, CHUNK, CHUNK)),
            _const_spec((HEADS, CHUNK, CHUNK)),
        ],
        out_specs=(tok(GM_WIDTH), tok(ML_WIDTH), tok(ML_WIDTH), tok(ML_WIDTH), tok(ML_WIDTH),
                   tok(GATE_LANES), pl.BlockSpec((GATE_ROWS, tb), lambda i: (0, i)), tok(GM_WIDTH)),
        out_shape=out_shape,
        compiler_params=pltpu.CompilerParams(dimension_semantics=("arbitrary",),
                                             vmem_limit_bytes=VMEM_LIMIT),
        name="front",
    )(x2d, nmix, w_t, wgate, wgate_t, bcol, brow, lng, lnb, mixw, mixb)


def _intra(q, ks, igcol, bcol, igrow, brow, mprev, mask):
    d = bcol + (igrow - brow)
    g = bcol + mprev
    m_t = jnp.maximum(g, jnp.max(jnp.where(mask, d, -jnp.inf), axis=-1, keepdims=True))
    w_intra = jnp.where(mask, jnp.exp(d - m_t), 0.0)
    w_inter = jnp.exp(g - m_t)
    s = _dot_nt(q, ks) * w_intra
    return s, w_inter, m_t, g


def _head_out(num, den, m_t, gain, og):
    hh = num / jnp.maximum(jnp.abs(den), jnp.exp(-m_t))
    return (og * _rms(hh, gain)).astype(BF16)


def _mlstm_sample_kernel(seq_len, q_ref, k_ref, v_ref, og_ref, gcol_ref, grow_ref, mrep_ref, c_ref, n_ref,
                         mln_ref, tri_ref, triu_ref, sel_last_ref, expand_ref, pick_last_ref, seq_sum_ref,
                         yml_ref, c_out_ref, n_out_ref, m_out_ref):
    n_seq = CHUNK // seq_len
    tri = tri_ref[...]
    mask = tri.astype(F32) > 0.0
    gcol = gcol_ref[...]
    grow = grow_ref[...]
    bcol_all = _sel_dot(tri, gcol)
    brow_all = _dot_sel(grow, triu_ref[...])
    sel_last = sel_last_ref[...]
    expand = expand_ref[...]
    pick_last = pick_last_ref[...]
    seq_sum = seq_sum_ref[...]
    row = lax.broadcasted_iota(jnp.int32, (CHUNK, HEAD_DIM), 0)
    lane = lax.broadcasted_iota(jnp.int32, (CHUNK, HEAD_DIM), 1)
    seq_rows = [(row >= i * seq_len) & (row < (i + 1) * seq_len) for i in range(n_seq)]
    for h in range(HEADS):
        cols = slice(h * HEAD_DIM, (h + 1) * HEAD_DIM)
        q, ks, v = q_ref[:, cols], k_ref[:, cols], v_ref[:, cols]
        igcol = gcol[:, h:h + 1]
        bcol = bcol_all[:, HEADS + h:HEADS + h + 1]
        igrow = grow[h:h + 1, :]
        brow = brow_all[HEADS + h:HEADS + h + 1, :]
        mprev = mrep_ref[:, h:h + 1]
        s, w_inter, m_t, g = _intra(q, ks, igcol, bcol, igrow, brow, mprev, mask)
        c_all = c_ref[:, h]
        c_stack = c_all.reshape(n_seq * HEAD_DIM, HEAD_DIM).astype(BF16)
        zero = jnp.zeros_like(q)
        q_exp = jnp.concatenate([jnp.where(m, q, zero) for m in seq_rows], axis=1)
        qc = _dot(q_exp, c_stack)
        n_all = n_ref[:, h]
        qn = jnp.sum(q.astype(F32) * _sel_dot(expand, n_all), axis=-1, keepdims=True)
        num = _dot(s.astype(BF16), v) + w_inter * qc
        den = jnp.sum(s, axis=-1, keepdims=True) + w_inter * qn
        yml_ref[:, cols] = _head_out(num, den, m_t, mln_ref[:, cols], og_ref[:, cols])
        packed = jnp.where(lane == 0, m_t, jnp.where(lane == 1, g, jnp.where(lane == 2, bcol, 0.0)))
        ends = _sel_dot(sel_last, packed)
        m_new, g_last, b_last = ends[:, 0:1], ends[:, 1:2], ends[:, 2:3]
        w_end = jnp.exp(b_last - bcol + igcol - m_new)
        dec = jnp.exp(g_last - m_new)
        kw = ks.astype(F32) * w_end
        v_exp = jnp.concatenate([jnp.where(m, v, zero) for m in seq_rows], axis=1)
        upd = _dot(kw.T.astype(BF16), v_exp)
        dec_seq = _sel_dot(pick_last, jnp.broadcast_to(dec, (CHUNK, HEAD_DIM)))
        for i in range(n_seq):
            c_out_ref[i, h] = dec_seq[i:i + 1, 0:1] * c_all[i] + upd[:, i * HEAD_DIM:(i + 1) * HEAD_DIM]
        n_out_ref[:, h] = dec_seq * n_all + _sel_dot(seq_sum, kw)
        m_out_ref[:, h] = _sel_dot(pick_last, jnp.broadcast_to(m_t, (CHUNK, HEAD_DIM)))


def _mlstm_sample(q, k, v, og, gcol, grow, mrep, c0, n0, mln, tri, triu, sel_last, expand, pick_last,
                  seq_sum, seq_len):
    n = q.shape[0]
    n_seq = CHUNK // seq_len
    n_batch = n // seq_len
    tok = lambda w: pl.BlockSpec((CHUNK, w), lambda i: (i, 0))
    c_spec = pl.BlockSpec((n_seq, HEADS, HEAD_DIM, HEAD_DIM), lambda i: (i, 0, 0, 0))
    n_spec = pl.BlockSpec((n_seq, HEADS, HEAD_DIM), lambda i: (i, 0, 0))
    return pl.pallas_call(
        functools.partial(_mlstm_sample_kernel, seq_len),
        grid=(n // CHUNK,),
        in_specs=[
            tok(ML_WIDTH), tok(ML_WIDTH), tok(ML_WIDTH), tok(ML_WIDTH), tok(GATE_LANES),
            pl.BlockSpec((GATE_ROWS, CHUNK), lambda i: (0, i)),
            pl.BlockSpec((CHUNK, HEADS), lambda i: (i, 0)),
            c_spec, n_spec,
            _const_spec((1, ML_WIDTH)),
            _const_spec((CHUNK, CHUNK)), _const_spec((CHUNK, CHUNK)), _const_spec((CHUNK, CHUNK)),
            _const_spec((CHUNK, n_seq)), _const_spec((n_seq, CHUNK)), _const_spec((n_seq, CHUNK)),
        ],
        out_specs=(tok(ML_WIDTH), c_spec, n_spec, n_spec),
        out_shape=(
            jax.ShapeDtypeStruct((n, ML_WIDTH), BF16),
            jax.ShapeDtypeStruct((n_batch, HEADS, HEAD_DIM, HEAD_DIM), F32),
            jax.ShapeDtypeStruct((n_batch, HEADS, HEAD_DIM), F32),
            jax.ShapeDtypeStruct((n_batch, HEADS, HEAD_DIM), F32),
        ),
        compiler_params=pltpu.CompilerParams(dimension_semantics=("arbitrary",),
                                             vmem_limit_bytes=VMEM_LIMIT),
        name="mlstm_sample",
    )(q, k, v, og, gcol, grow, mrep, c0, n0, mln, tri, triu, sel_last, expand, pick_last, seq_sum)


def _back_math(final_norm, yml_transposed, x_ref, ygm_ref, yml_ref, pe_ref, wout_ref, nffn_ref,
               wup_ref, wdown_ref, nple_ref, wpg_ref, wpp_ref, nfin_ref, out_ref):
    ml_dot = _dot_tn if yml_transposed else _dot
    h = x_ref[...] + _dot(ygm_ref[...], wout_ref[:GM_WIDTH]) + ml_dot(yml_ref[...], wout_ref[GM_WIDTH:])
    a = _rms(h, nffn_ref[...]).astype(BF16)
    ff = D_FF // FF_SPLIT

    def mlp_part(c):
        f = _dot(a, wup_ref[:, c * ff:(c + 1) * ff])
        f = jnp.square(jnp.maximum(f, 0.0)).astype(BF16)
        return _dot(f, wdown_ref[c * ff:(c + 1) * ff, :])

    mlp = mlp_part(0)
    for c in range(1, FF_SPLIT):
        mlp = mlp + mlp_part(c)
    h = h + mlp
    gate = jax.nn.sigmoid(_dot(_rms(h, nple_ref[...]).astype(BF16), wpg_ref[...]))
    h = h + gate * _dot(pe_ref[...].astype(BF16), wpp_ref[...])
    if final_norm:
        h = _rms(h, nfin_ref[...])
    out_ref[...] = h


def _weight_chunks(pairs):
    chunks = []
    for src, dst in pairs:
        rows, cols = src.shape
        for r0 in range(0, rows, CAST_ROWS):
            r1 = min(rows, r0 + CAST_ROWS)
            for c0 in range(0, cols, D_MODEL):
                window = (slice(r0, r1), slice(c0, c0 + D_MODEL))
                chunks.append((src.at[window], dst.at[window], r1 - r0))
    return chunks


def _back_cast_kernel(final_norm, x_ref, ygm_ref, yml_ref, pe_ref, wout_hbm, nffn_ref, wup_hbm, wdown_hbm, nple_ref,
                      wpg_hbm, wpp_hbm, nfin_ref,
                      out_ref, wout_exp, wup_exp, wdown_exp, wpg_exp, wpp_exp,
                      wout_v, wup_v, wdown_v, wpg_v, wpp_v, stage, load_sem, export_sem):
    i = pl.program_id(0)
    weights = ((wout_hbm, wout_v, wout_exp), (wup_hbm, wup_v, wup_exp), (wdown_hbm, wdown_v, wdown_exp),
               (wpg_hbm, wpg_v, wpg_exp), (wpp_hbm, wpp_v, wpp_exp))
    exports = [pltpu.make_async_copy(v, e, export_sem.at[j]) for j, (_, v, e) in enumerate(weights)]

    @pl.when(i == 0)
    def _():
        chunks = _weight_chunks([(h, v) for h, v, _ in weights])
        loads = [pltpu.make_async_copy(src, stage.at[j % 2, :rows], load_sem.at[j % 2])
                 for j, (src, _, rows) in enumerate(chunks)]
        loads[0].start()
        for j, (_, dst, rows) in enumerate(chunks):
            if j + 1 < len(chunks):
                loads[j + 1].start()
            loads[j].wait()
            dst[...] = stage[j % 2, :rows].astype(BF16)
        for e in exports:
            e.start()

    _back_math(final_norm, True, x_ref, ygm_ref, yml_ref, pe_ref, wout_v, nffn_ref, wup_v, wdown_v, nple_ref,
               wpg_v, wpp_v, nfin_ref, out_ref)

    @pl.when(i == pl.num_programs(0) - 1)
    def _():
        for e in exports:
            e.wait()


def _back_cast(x2d, ygm, yml_t, pe2d, wout, nffn, wup, wdown, nple, wpg, wpp, nfin, final_norm, tb):
    n = x2d.shape[0]
    tok = lambda w: pl.BlockSpec((tb, w), lambda i: (i, 0))
    hbm = pl.BlockSpec(memory_space=pl.ANY)
    weights = (wout, wup, wdown, wpg, wpp)
    return pl.pallas_call(
        functools.partial(_back_cast_kernel, final_norm),
        grid=(n // tb,),
        in_specs=[
            tok(D_MODEL), tok(GM_WIDTH), pl.BlockSpec((ML_WIDTH, tb), lambda i: (0, i)), tok(PLE_DIM),
            hbm, _const_spec((1, D_MODEL)), hbm, hbm, _const_spec((1, D_MODEL)), hbm, hbm,
            _const_spec((1, D_MODEL)),
        ],
        out_specs=(tok(D_MODEL),) + (hbm,) * len(weights),
        out_shape=(jax.ShapeDtypeStruct((n, D_MODEL), F32),)
        + tuple(jax.ShapeDtypeStruct(w.shape, BF16) for w in weights),
        scratch_shapes=[pltpu.VMEM(w.shape, BF16) for w in weights]
        + [pltpu.VMEM((2, CAST_ROWS, D_MODEL), F32), pltpu.SemaphoreType.DMA((2,)),
           pltpu.SemaphoreType.DMA((len(weights),))],
        compiler_params=pltpu.CompilerParams(dimension_semantics=("arbitrary",),
                                             vmem_limit_bytes=VMEM_LIMIT),
        name="back_cast",
    )(x2d, ygm, yml_t, pe2d, wout, nffn, wup, wdown, nple, wpg, wpp, nfin)


def _back_kernel(final_norm, yml_transposed, *refs):
    _back_math(final_norm, yml_transposed, *refs)


def _back(x2d, ygm, yml, pe2d, wout, nffn, wup, wdown, nple, wpg, wpp, nfin, final_norm, tb):
    n = x2d.shape[0]
    yml_transposed = yml.shape[0] != n
    tok = lambda w: pl.BlockSpec((tb, w), lambda i: (i, 0))
    yml_spec = pl.BlockSpec((ML_WIDTH, tb), lambda i: (0, i)) if yml_transposed else tok(ML_WIDTH)
    return pl.pallas_call(
        functools.partial(_back_kernel, final_norm, yml_transposed),
        grid=(n // tb,),
        in_specs=[
            tok(D_MODEL), tok(GM_WIDTH), yml_spec, tok(PLE_DIM),
            _const_spec((GM_WIDTH + ML_WIDTH, D_MODEL)),
            _const_spec((1, D_MODEL)),
            _const_spec((D_MODEL, D_FF)), _const_spec((D_FF, D_MODEL)),
            _const_spec((1, D_MODEL)),
            _const_spec((D_MODEL, D_MODEL)), _const_spec((PLE_DIM, D_MODEL)),
            _const_spec((1, D_MODEL)),
        ],
        out_specs=tok(D_MODEL),
        out_shape=jax.ShapeDtypeStruct((n, D_MODEL), F32),
        compiler_params=pltpu.CompilerParams(dimension_semantics=("arbitrary",),
                                             vmem_limit_bytes=VMEM_LIMIT),
        name="back",
    )(x2d, ygm, yml, pe2d, wout, nffn, wup, wdown, nple, wpg, wpp, nfin)


def _const(x):
    return jnp.asarray(np.asarray(x, np.float32), BF16)


def _block_tri(block):
    r = np.arange(CHUNK)[:, None]
    c = np.arange(CHUNK)[None, :]
    return (r // block == c // block) & (c <= r)


def _token_block(n):
    return 512 if n % 512 == 0 else CHUNK


def kernel(x_prompt, x_sample, p_prompt, p_sample, state_C, state_n, state_m, norm_mix, w_in, gm_ln_g,
           gm_ln_b, gm_ws, gm_bs, ml_b_i, ml_b_f, ml_norm, w_out, norm_ffn, w_up, w_down, norm_ple,
           w_ple_gate, w_ple_proj, norm_final):
    depth = w_in.shape[0]
    batch, seq, _ = x_prompt.shape
    dec_batch, dec_seq, _ = x_sample.shape
    n_p, n_s = batch * seq, dec_batch * dec_seq
    n_seq = CHUNK // dec_seq
    assert seq % CHUNK == 0 and CHUNK % dec_seq == 0 and n_s % CHUNK == 0
    tb_p, tb_s = _token_block(seq), _token_block(n_s)

    hp = x_prompt.reshape(n_p, D_MODEL)
    hs = x_sample.reshape(n_s, D_MODEL)

    tri_p, tri_s = _block_tri(CHUNK), _block_tri(dec_seq)
    triu_p, eye = _const(tri_p.T), _const(np.eye(CHUNK))
    r = np.arange(CHUNK)
    i = np.arange(n_seq)
    sel_last = _const(r[None, :] == (r[:, None] // dec_seq) * dec_seq + dec_seq - 1)
    expand = _const(r[:, None] // dec_seq == i[None, :])
    pick_last = _const(r[None, :] == i[:, None] * dec_seq + dec_seq - 1)
    seq_sum = _const(r[None, :] // dec_seq == i[:, None])

    outs = {k: [] for k in ("Cp", "Np", "Mp", "Vp", "Cs", "Ns", "Ms", "Vs")}
    for l in range(depth):
        row = lambda a: a[l].reshape(1, -1).astype(F32)
        w_t = w_in[l].T.astype(BF16)
        wg = w_in[l][:, MAIN_COLS:]
        wg_i, wg_f = wg[:, :HEADS], wg[:, HEADS:]
        b_i, b_f = ml_b_i[l].astype(F32), ml_b_f[l].astype(F32)
        row_pad = lambda a: jnp.pad(a, ((0, GATE_ROWS - HEADS), (0, 0)))
        wgr = jnp.concatenate([row_pad(wg_i.T), row_pad(wg_f.T)], axis=0).astype(BF16)
        bgr = jnp.broadcast_to(jnp.concatenate([row_pad(b_i[:, None]), row_pad(b_f[:, None])], axis=0),
                               (2 * GATE_ROWS, tb_p))
        gain_t = jnp.broadcast_to(ml_norm[l].astype(F32)[:, None], (ML_WIDTH, tb_p))
        wgate = jnp.pad(wg, ((0, 0), (0, GATE_LANES - 2 * HEADS))).astype(BF16)
        wgate_t = jnp.pad(wg.T, ((0, GATE_ROWS - 2 * HEADS), (0, 0))).astype(BF16)
        gbias = jnp.concatenate([b_i, b_f])
        bcol = jnp.pad(gbias, (0, GATE_LANES - 2 * HEADS)).reshape(1, GATE_LANES)
        brow = jnp.broadcast_to(jnp.pad(gbias, (0, GATE_ROWS - 2 * HEADS)).reshape(GATE_ROWS, 1),
                                (GATE_ROWS, tb_s))
        ws = gm_ws[l]
        mixw_p = (ws[:, :CHUNK, :CHUNK] * tri_p.astype(np.float32)).astype(BF16)
        mixb_p = jnp.broadcast_to(gm_bs[l][:, :CHUNK, None], (HEADS, CHUNK, CHUNK)).astype(F32)
        mixw_s = (jnp.tile(ws[:, :dec_seq, :dec_seq], (1, n_seq, n_seq)) * tri_s.astype(np.float32)).astype(BF16)
        mixb_s = jnp.broadcast_to(jnp.tile(gm_bs[l][:, :dec_seq], (1, n_seq))[:, :, None],
                                  (HEADS, CHUNK, CHUNK)).astype(F32)
        last = l == depth - 1
        nfin = norm_final.reshape(1, D_MODEL).astype(F32)

        ygm, yml_t, vlast, cn_t, m8 = _mixer_prompt(
            hp, row(norm_mix), w_t, wgr, bgr, row(gm_ln_g), row(gm_ln_b), mixw_p, mixb_p,
            gain_t, triu_p, eye, batch, tb_p)
        hp, *back_weights = _back_cast(
            hp, ygm, yml_t, p_prompt[l].reshape(n_p, PLE_DIM), w_out[l].astype(F32), row(norm_ffn),
            w_up[l].astype(F32), w_down[l].astype(F32), row(norm_ple), w_ple_gate[l].astype(F32),
            w_ple_proj[l].astype(F32), nfin, last, 2 * tb_p)
        wout, wup, wdown, wpg, wpp = back_weights
        outs["Cp"].append(jnp.swapaxes(cn_t[:, :, :HEAD_DIM, :], -1, -2))
        outs["Np"].append(cn_t[:, :, HEAD_DIM, :])
        outs["Mp"].append(m8[:, :, 0])
        outs["Vp"].append(vlast)

        ygm, q, k, v, og, gcol, grow, vgn = _front(
            hs, row(norm_mix), w_t, wgate, wgate_t, bcol, brow, row(gm_ln_g), row(gm_ln_b), mixw_s, mixb_s,
            tb_s)
        mrep = jnp.repeat(state_m[l].astype(F32), dec_seq, axis=0)
        yml, c_new, n_new, m_new = _mlstm_sample(
            q, k, v, og, gcol, grow, mrep, state_C[l].astype(F32), state_n[l].astype(F32), row(ml_norm),
            _const(tri_s), _const(tri_s.T), sel_last, expand, pick_last, seq_sum, dec_seq)
        hs = _back(hs, ygm, yml, p_sample[l].reshape(n_s, PLE_DIM), wout, row(norm_ffn), wup, wdown,
                   row(norm_ple), wpg, wpp, nfin, last, tb_s)
        outs["Cs"].append(c_new)
        outs["Ns"].append(n_new)
        outs["Ms"].append(m_new[..., 0])
        outs["Vs"].append(vgn.reshape(dec_batch, dec_seq, GM_WIDTH))

    st = lambda k: outs[k][0][None] if depth == 1 else jnp.stack(outs[k])
    return (hp.reshape(batch, seq, D_MODEL), hs.reshape(dec_batch, dec_seq, D_MODEL),
            st("Cp"), st("Np"), st("Mp"), st("Vp"), st("Cs"), st("Ns"), st("Ms"), st("Vs"))
```

```python
import functools

import jax
import numpy as np
import jax.numpy as jnp
from jax import lax
from jax.experimental import pallas as pl
from jax.experimental.pallas import tpu as pltpu

F32 = jnp.float32
BF16 = jnp.bfloat16

D_MODEL = 1024
GM_WIDTH = 512
ML_WIDTH = 512
HEADS = 4
HEAD_DIM = 128
D_FF = 4096
PLE_DIM = 256
EPS = 1e-6
CHUNK = 128
MAIN_COLS = 2 * GM_WIDTH + 4 * ML_WIDTH
Q_LO = 2 * GM_WIDTH
K_LO = Q_LO + ML_WIDTH
V_LO = K_LO + ML_WIDTH
O_LO = V_LO + ML_WIDTH
IN_COLS = MAIN_COLS + 2 * HEADS
GATE_LANES = 128
GATE_ROWS = 16
STATE_ROWS = HEAD_DIM + 16
FF_SPLIT = 4
BF16_ROWS = 16
VMEM_LIMIT = 60 * 1024 * 1024


def _dot(a, b):
    return jnp.dot(a, b, preferred_element_type=F32)


def _dot_nt(a, b):
    return lax.dot_general(a, b, (((1,), (1,)), ((), ())), preferred_element_type=F32)


def _dot_tn(a, b):
    return lax.dot_general(a, b, (((0,), (0,)), ((), ())), preferred_element_type=F32)


def _split3(x):
    x1 = x.astype(BF16)
    r = x - x1.astype(F32)
    x2 = r.astype(BF16)
    r = r - x2.astype(F32)
    return x1, x2, r.astype(BF16)


def _sel_dot(sel, x):
    p1, p2, p3 = _split3(x)
    return _dot(sel, p1) + _dot(sel, p2) + _dot(sel, p3)


def _dot_sel(x, sel):
    p1, p2, p3 = _split3(x)
    return _dot(p1, sel) + _dot(p2, sel) + _dot(p3, sel)


def _sel_dot_nt(sel, x):
    p1, p2, p3 = _split3(x)
    return _dot_nt(sel, p1) + _dot_nt(sel, p2) + _dot_nt(sel, p3)


def _rms(x, g):
    return x * lax.rsqrt(jnp.mean(x * x, axis=-1, keepdims=True) + EPS) * g


def _log_sigmoid(x):
    return -(jnp.maximum(-x, 0.0) + jnp.log1p(jnp.exp(-jnp.abs(x))))


def _const_spec(shape):
    nd = len(shape)
    return pl.BlockSpec(shape, lambda *_: (0,) * nd, pipeline_mode=pl.Buffered(1))


def _gmlp_pieces(x_ref, nmix_ref, w_t_ref, lng_ref, lnb_ref, mixw_ref, mixb_ref, ygm_ref, vgn_ref, st):
    tb = x_ref.shape[0]

    def norm():
        st["a"] = _rms(x_ref[...], nmix_ref[...]).astype(BF16)

    def u_part():
        st["u"] = jax.nn.gelu(_dot_nt(st["a"], w_t_ref[:GM_WIDTH]))

    def v_part():
        vg = jax.nn.gelu(_dot_nt(st["a"], w_t_ref[GM_WIDTH:2 * GM_WIDTH]))
        mu = jnp.mean(vg, axis=-1, keepdims=True)
        var = jnp.mean(jnp.square(vg - mu), axis=-1, keepdims=True)
        vgn = (vg - mu) * lax.rsqrt(var + EPS) * lng_ref[...] + lnb_ref[...]
        vgn_ref[...] = vgn[tb - vgn_ref.shape[0]:, :]
        st["vgb"] = vgn.astype(BF16)

    def mix(c):
        rows = slice(c * CHUNK, (c + 1) * CHUNK)
        for h in range(HEADS):
            cols = slice(h * HEAD_DIM, (h + 1) * HEAD_DIM)
            s = _dot(mixw_ref[h], st["vgb"][rows, cols]) + mixb_ref[h]
            ygm_ref[rows, cols] = (st["u"][rows, cols] * s).astype(BF16)

    return [norm, u_part, v_part] + [functools.partial(mix, c) for c in range(tb // CHUNK)]


def _mixer_prompt_kernel(steps_per_seq, n_cast, x_ref, nmix_ref, w_t_ref, wgr_ref, bgr_ref,
                         lng_ref, lnb_ref, mixw_ref, mixb_ref, gain_t_ref, triu_ref, eye_ref, *rest):
    cast_in, rest = rest[:n_cast], rest[n_cast:]
    ygm_ref, yml_t_ref, vlast_ref, cn_out_ref, m_out_ref = rest[:5]
    cast_out, (cn_scr, m_scr) = rest[5:5 + n_cast], rest[5 + n_cast:]
    for src, dst in zip(cast_in, cast_out):
        dst[...] = src[...].astype(BF16)

    tb = x_ref.shape[0]
    n_chunks = tb // CHUNK
    blocks = [slice(c * CHUNK, (c + 1) * CHUNK) for c in range(n_chunks)]
    heads = [slice(h * HEAD_DIM, (h + 1) * HEAD_DIM) for h in range(HEADS)]
    unit_ids = [(c, h) for c in range(n_chunks) for h in range(HEADS)]
    new_seq = pl.program_id(0) % steps_per_seq == 0
    triu = triu_ref[...]
    mask_t = triu.astype(F32) > 0.0
    eye = eye_ref[...]
    lane_g = lax.broadcasted_iota(jnp.int32, (GATE_ROWS, tb), 1) % CHUNK
    sub_g = lax.broadcasted_iota(jnp.int32, (GATE_ROWS, CHUNK), 0)
    ones_row = (sub_g == 0).astype(BF16)
    last = slice(CHUNK - 1, CHUNK)

    st = {}
    norm, u_part, v_part, *mix = _gmlp_pieces(x_ref, nmix_ref, w_t_ref, lng_ref, lnb_ref, mixw_ref, mixb_ref,
                                              ygm_ref, vlast_ref, st)
    norm()
    a = st["a"]

    zr = _dot_nt(wgr_ref[...], a) + bgr_ref[...]
    lf_r = _log_sigmoid(zr[GATE_ROWS:])
    b_r = jnp.concatenate([_dot_sel(lf_r[:, t], triu) for t in blocks], axis=1)
    r_r = zr[:GATE_ROWS] - b_r
    k = (_dot_nt(a, w_t_ref[K_LO:V_LO]) * (HEAD_DIM ** -0.5)).astype(BF16)
    p_r = r_r
    shift = 1
    while shift < CHUNK:
        p_r = jnp.maximum(p_r, jnp.where(lane_g >= shift, pltpu.roll(p_r, shift, axis=1), -jnp.inf))
        shift *= 2
    r_c = [_sel_dot_nt(eye, r_r[:, t]) for t in blocks]
    e_end = [jnp.exp(r_r[:, t] - p_r[:, t][:, last]) for t in blocks]
    q_t = _dot_nt(w_t_ref[Q_LO:K_LO], a).astype(BF16)
    v_t = _dot_nt(w_t_ref[V_LO:O_LO], a)
    og_t = jax.nn.sigmoid(_dot_nt(w_t_ref[O_LO:MAIN_COLS], a)) * gain_t_ref[...]

    zero = jnp.zeros((HEAD_DIM, HEAD_DIM), BF16)

    def block_diag(x, y):
        return jnp.concatenate([jnp.concatenate([x, zero], axis=1), jnp.concatenate([zero, y], axis=1)], axis=0)

    def pair_dot(lhs, rhs):
        out = _dot(jnp.concatenate(lhs, axis=1), block_diag(*rhs))
        return out[:, :HEAD_DIM], out[:, HEAD_DIM:]

    pair_ids = [(c, h) for c in range(n_chunks) for h in range(0, HEADS, 2)]
    kq, e_intra, x1, upd = {}, {}, {}, {}
    for c, h in pair_ids:
        t = blocks[c]
        kq[c, h], kq[c, h + 1] = pair_dot([k[t, heads[h]], k[t, heads[h + 1]]],
                                          [q_t[heads[h], t], q_t[heads[h + 1], t]])
        for g in (h, h + 1):
            e_intra[c, g] = jnp.where(mask_t, jnp.exp(r_c[c][:, g:g + 1] - p_r[g:g + 1, t]), 0.0)
    u_part()
    for c, h in pair_ids:
        t = blocks[c]
        s0, vext, vw = [], [], []
        for g in (h, h + 1):
            s0.append((kq[c, g] * e_intra[c, g]).astype(BF16))
            vext.append(jnp.concatenate([v_t[heads[g], t].astype(BF16), ones_row], axis=0))
            e_row = e_end[c][g:g + 1, :]
            vw.append(jnp.concatenate([(v_t[heads[g], t] * e_row).astype(BF16),
                                       jnp.where(sub_g == 0, e_row, 0.0).astype(BF16)], axis=0))
        x1[c, h], x1[c, h + 1] = pair_dot(vext, s0)
        upd[c, h], upd[c, h + 1] = pair_dot(vw, [k[t, heads[h]], k[t, heads[h + 1]]])
    v_part()

    cn_in, cm, w_inter = {}, {}, {}
    for h in range(HEADS):
        cn = jnp.where(new_seq, 0.0, cn_scr[h])
        m_prev = jnp.where(new_seq, 0.0, m_scr[h:h + 1, 0:1])
        for c in range(n_chunks):
            prow = p_r[h:h + 1, blocks[c]]
            cn_in[c, h] = cn
            cm[c, h] = jnp.maximum(m_prev, prow)
            w_inter[c, h] = jnp.exp(m_prev - cm[c, h])
            cm_last = cm[c, h][:, last]
            cn = jnp.exp(m_prev - cm_last) * cn + jnp.exp(prow[:, last] - cm_last) * upd[c, h]
            m_prev = b_r[h:h + 1, blocks[c]][:, last] + cm_last
        cn_scr[h] = cn
        m_scr[h:h + 1, :] = jnp.broadcast_to(m_prev, (1, HEAD_DIM))
    x2 = {}

    def state_matmuls(c):
        t = blocks[c]
        for h in range(0, HEADS, 2):
            x2[c, h], x2[c, h + 1] = pair_dot([cn_in[c, h].astype(BF16), cn_in[c, h + 1].astype(BF16)],
                                              [q_t[heads[h], t], q_t[heads[h + 1], t]])

    def head_outputs(c):
        t = blocks[c]
        for h, hd in enumerate(heads):
            brow, prow = b_r[h:h + 1, t], p_r[h:h + 1, t]
            nd = x1[c, h] * jnp.exp(prow - cm[c, h]) + x2[c, h] * w_inter[c, h]
            num_t, den = nd[:HEAD_DIM], nd[HEAD_DIM:HEAD_DIM + 1]
            inv = 1.0 / jnp.maximum(jnp.abs(den), jnp.exp(-(brow + cm[c, h])))
            ssq = jnp.sum(num_t * num_t, axis=0, keepdims=True)
            scale = inv * lax.rsqrt(ssq * (inv * inv) * (1.0 / HEAD_DIM) + EPS)
            yml_t_ref[hd, t] = (num_t * scale * og_t[hd, t]).astype(BF16)

    for c in range(n_chunks):
        state_matmuls(c)
    for c in range(n_chunks):
        head_outputs(c)
        mix[c]()

    @pl.when(pl.program_id(0) % steps_per_seq == steps_per_seq - 1)
    def _():
        cn_out_ref[...] = cn_scr[...]
        m_out_ref[...] = m_scr[...]


def _mixer_prompt(x2d, nmix, w_t, wgr, bgr, lng, lnb, mixw, mixb, gain_t, triu, eye, cast, batch, tb):
    n = x2d.shape[0]
    n_steps = n // tb
    steps_per_seq = n_steps // batch

    def window(w):
        rows = max(BF16_ROWS, w.shape[0] // n_steps)
        return pl.BlockSpec((rows, w.shape[1]), lambda i: (jnp.minimum(i, w.shape[0] // rows - 1), 0))

    tok = lambda w: pl.BlockSpec((tb, w), lambda i: (i, 0))
    per_seq = lambda *shape: pl.BlockSpec((None,) + shape, lambda i: (i // steps_per_seq,) + (0,) * len(shape))
    return pl.pallas_call(
        functools.partial(_mixer_prompt_kernel, steps_per_seq, len(cast)),
        grid=(n_steps,),
        in_specs=[
            tok(D_MODEL),
            _const_spec((1, D_MODEL)),
            _const_spec((IN_COLS, D_MODEL)),
            _const_spec((2 * GATE_ROWS, D_MODEL)),
            _const_spec((2 * GATE_ROWS, tb)),
            _const_spec((1, GM_WIDTH)),
            _const_spec((1, GM_WIDTH)),
            _const_spec((HEADS, CHUNK, CHUNK)),
            _const_spec((HEADS, CHUNK, CHUNK)),
            _const_spec((ML_WIDTH, tb)),
            _const_spec((CHUNK, CHUNK)),
            _const_spec((CHUNK, CHUNK)),
        ] + [window(w) for w in cast],
        out_specs=(tok(GM_WIDTH), pl.BlockSpec((ML_WIDTH, tb), lambda i: (0, i)), per_seq(CHUNK, GM_WIDTH),
                   per_seq(HEADS, STATE_ROWS, HEAD_DIM), per_seq(HEADS, HEAD_DIM)) + tuple(window(w) for w in cast),
        out_shape=(
            jax.ShapeDtypeStruct((n, GM_WIDTH), BF16),
            jax.ShapeDtypeStruct((ML_WIDTH, n), BF16),
            jax.ShapeDtypeStruct((batch, CHUNK, GM_WIDTH), F32),
            jax.ShapeDtypeStruct((batch, HEADS, STATE_ROWS, HEAD_DIM), F32),
            jax.ShapeDtypeStruct((batch, HEADS, HEAD_DIM), F32),
        ) + tuple(jax.ShapeDtypeStruct(w.shape, BF16) for w in cast),
        scratch_shapes=[pltpu.VMEM((HEADS, STATE_ROWS, HEAD_DIM), F32), pltpu.VMEM((HEADS, HEAD_DIM), F32)],
        compiler_params=pltpu.CompilerParams(dimension_semantics=("arbitrary",),
                                             vmem_limit_bytes=VMEM_LIMIT),
        name="mixer_prompt",
    )(x2d, nmix, w_t, wgr, bgr, lng, lnb, mixw, mixb, gain_t, triu, eye, *cast)


def _front_kernel(x_ref, nmix_ref, w_t_ref, wgate_ref, wgate_t_ref, bcol_ref, brow_ref,
                  lng_ref, lnb_ref, mixw_ref, mixb_ref,
                  ygm_ref, q_ref, k_ref, v_ref, og_ref, gcol_ref, grow_ref, vgn_ref):
    st = {}
    for piece in _gmlp_pieces(x_ref, nmix_ref, w_t_ref, lng_ref, lnb_ref, mixw_ref, mixb_ref,
                              ygm_ref, vgn_ref, st):
        piece()
    a = st["a"]

    def proj(lo):
        return _dot_nt(a, w_t_ref[lo:lo + ML_WIDTH])

    q_ref[...] = proj(Q_LO).astype(BF16)
    k_ref[...] = (proj(K_LO) * (HEAD_DIM ** -0.5)).astype(BF16)
    v_ref[...] = proj(V_LO).astype(BF16)
    og_ref[...] = jax.nn.sigmoid(proj(O_LO))
    zc = _dot(a, wgate_ref[...]) + bcol_ref[...]
    lane = lax.broadcasted_iota(jnp.int32, zc.shape, 1)
    gcol_ref[...] = jnp.where(lane >= HEADS, _log_sigmoid(zc), zc)
    zr = _dot_nt(wgate_t_ref[...], a) + brow_ref[...]
    sub = lax.broadcasted_iota(jnp.int32, zr.shape, 0)
    grow_ref[...] = jnp.where(sub >= HEADS, _log_sigmoid(zr), zr)


def _front(x2d, nmix, w_t, wgate, wgate_t, bcol, brow, lng, lnb, mixw, mixb, tb):
    n = x2d.shape[0]
    tok = lambda w: pl.BlockSpec((tb, w), lambda i: (i, 0))
    out_shape = (
        jax.ShapeDtypeStruct((n, GM_WIDTH), BF16),
        jax.ShapeDtypeStruct((n, ML_WIDTH), BF16),
        jax.ShapeDtypeStruct((n, ML_WIDTH), BF16),
        jax.ShapeDtypeStruct((n, ML_WIDTH), BF16),
        jax.ShapeDtypeStruct((n, ML_WIDTH), F32),
        jax.ShapeDtypeStruct((n, GATE_LANES), F32),
        jax.ShapeDtypeStruct((GATE_ROWS, n), F32),
        jax.ShapeDtypeStruct((n, GM_WIDTH), F32),
    )
    return pl.pallas_call(
        _front_kernel,
        grid=(n // tb,),
        in_specs=[
            tok(D_MODEL),
            _const_spec((1, D_MODEL)),
            _const_spec((IN_COLS, D_MODEL)),
            _const_spec((D_MODEL, GATE_LANES)),
            _const_spec((GATE_ROWS, D_MODEL)),
            _const_spec((1, GATE_LANES)),
            _const_spec((GATE_ROWS, tb)),
            _const_spec((1, GM_WIDTH)),
            _const_spec((1, GM_WIDTH)),
            _const_spec((HEADS, CHUNK, CHUNK)),
            _const_spec((HEADS, CHUNK, CHUNK)),
        ],
        out_specs=(tok(GM_WIDTH), tok(ML_WIDTH), tok(ML_WIDTH), tok(ML_WIDTH), tok(ML_WIDTH),
                   tok(GATE_LANES), pl.BlockSpec((GATE_ROWS, tb), lambda i: (0, i)), tok(GM_WIDTH)),
        out_shape=out_shape,
        compiler_params=pltpu.CompilerParams(dimension_semantics=("arbitrary",),
                                             vmem_limit_bytes=VMEM_LIMIT),
        name="front",
    )(x2d, nmix, w_t, wgate, wgate_t, bcol, brow, lng, lnb, mixw, mixb)


def _intra(q, ks, igcol, bcol, igrow, brow, mprev, mask):
    d = bcol + (igrow - brow)
    g = bcol + mprev
    m_t = jnp.maximum(g, jnp.max(jnp.where(mask, d, -jnp.inf), axis=-1, keepdims=True))
    w_intra = jnp.where(mask, jnp.exp(d - m_t), 0.0)
    w_inter = jnp.exp(g - m_t)
    s = _dot_nt(q, ks) * w_intra
    return s, w_inter, m_t, g


def _head_out(num, den, m_t, gain, og):
    hh = num / jnp.maximum(jnp.abs(den), jnp.exp(-m_t))
    return (og * _rms(hh, gain)).astype(BF16)


def _mlstm_sample_kernel(seq_len, q_ref, k_ref, v_ref, og_ref, gcol_ref, grow_ref, mrep_ref, c_ref, n_ref,
                         mln_ref, tri_ref, triu_ref, sel_last_ref, expand_ref, pick_last_ref, seq_sum_ref,
                         yml_ref, c_out_ref, n_out_ref, m_out_ref):
    n_seq = CHUNK // seq_len
    tri = tri_ref[...]
    mask = tri.astype(F32) > 0.0
    gcol = gcol_ref[...]
    grow = grow_ref[...]
    bcol_all = _sel_dot(tri, gcol)
    brow_all = _dot_sel(grow, triu_ref[...])
    sel_last = sel_last_ref[...]
    expand = expand_ref[...]
    pick_last = pick_last_ref[...]
    seq_sum = seq_sum_ref[...]
    row = lax.broadcasted_iota(jnp.int32, (CHUNK, HEAD_DIM), 0)
    lane = lax.broadcasted_iota(jnp.int32, (CHUNK, HEAD_DIM), 1)
    seq_rows = [(row >= i * seq_len) & (row < (i + 1) * seq_len) for i in range(n_seq)]
    for h in range(HEADS):
        cols = slice(h * HEAD_DIM, (h + 1) * HEAD_DIM)
        q, ks, v = q_ref[:, cols], k_ref[:, cols], v_ref[:, cols]
        igcol = gcol[:, h:h + 1]
        bcol = bcol_all[:, HEADS + h:HEADS + h + 1]
        igrow = grow[h:h + 1, :]
        brow = brow_all[HEADS + h:HEADS + h + 1, :]
        mprev = mrep_ref[:, h:h + 1]
        s, w_inter, m_t, g = _intra(q, ks, igcol, bcol, igrow, brow, mprev, mask)
        c_all = c_ref[:, h]
        c_stack = c_all.reshape(n_seq * HEAD_DIM, HEAD_DIM).astype(BF16)
        zero = jnp.zeros_like(q)
        q_exp = jnp.concatenate([jnp.where(m, q, zero) for m in seq_rows], axis=1)
        qc = _dot(q_exp, c_stack)
        n_all = n_ref[:, h]
        qn = jnp.sum(q.astype(F32) * _sel_dot(expand, n_all), axis=-1, keepdims=True)
        num = _dot(s.astype(BF16), v) + w_inter * qc
        den = jnp.sum(s, axis=-1, keepdims=True) + w_inter * qn
        yml_ref[:, cols] = _head_out(num, den, m_t, mln_ref[:, cols], og_ref[:, cols])
        packed = jnp.where(lane == 0, m_t, jnp.where(lane == 1, g, jnp.where(lane == 2, bcol, 0.0)))
        ends = _sel_dot(sel_last, packed)
        m_new, g_last, b_last = ends[:, 0:1], ends[:, 1:2], ends[:, 2:3]
        w_end = jnp.exp(b_last - bcol + igcol - m_new)
        dec = jnp.exp(g_last - m_new)
        kw = ks.astype(F32) * w_end
        v_exp = jnp.concatenate([jnp.where(m, v, zero) for m in seq_rows], axis=1)
        upd = _dot(kw.T.astype(BF16), v_exp)
        dec_seq = _sel_dot(pick_last, jnp.broadcast_to(dec, (CHUNK, HEAD_DIM)))
        for i in range(n_seq):
            c_out_ref[i, h] = dec_seq[i:i + 1, 0:1] * c_all[i] + upd[:, i * HEAD_DIM:(i + 1) * HEAD_DIM]
        n_out_ref[:, h] = dec_seq * n_all + _sel_dot(seq_sum, kw)
        m_out_ref[:, h] = _sel_dot(pick_last, jnp.broadcast_to(m_t, (CHUNK, HEAD_DIM)))


def _mlstm_sample(q, k, v, og, gcol, grow, mrep, c0, n0, mln, tri, triu, sel_last, expand, pick_last,
                  seq_sum, seq_len):
    n = q.shape[0]
    n_seq = CHUNK // seq_len
    n_batch = n // seq_len
    tok = lambda w: pl.BlockSpec((CHUNK, w), lambda i: (i, 0))
    c_spec = pl.BlockSpec((n_seq, HEADS, HEAD_DIM, HEAD_DIM), lambda i: (i, 0, 0, 0))
    n_spec = pl.BlockSpec((n_seq, HEADS, HEAD_DIM), lambda i: (i, 0, 0))
    return pl.pallas_call(
        functools.partial(_mlstm_sample_kernel, seq_len),
        grid=(n // CHUNK,),
        in_specs=[
            tok(ML_WIDTH), tok(ML_WIDTH), tok(ML_WIDTH), tok(ML_WIDTH), tok(GATE_LANES),
            pl.BlockSpec((GATE_ROWS, CHUNK), lambda i: (0, i)),
            pl.BlockSpec((CHUNK, HEADS), lambda i: (i, 0)),
            c_spec, n_spec,
            _const_spec((1, ML_WIDTH)),
            _const_spec((CHUNK, CHUNK)), _const_spec((CHUNK, CHUNK)), _const_spec((CHUNK, CHUNK)),
            _const_spec((CHUNK, n_seq)), _const_spec((n_seq, CHUNK)), _const_spec((n_seq, CHUNK)),
        ],
        out_specs=(tok(ML_WIDTH), c_spec, n_spec, n_spec),
        out_shape=(
            jax.ShapeDtypeStruct((n, ML_WIDTH), BF16),
            jax.ShapeDtypeStruct((n_batch, HEADS, HEAD_DIM, HEAD_DIM), F32),
            jax.ShapeDtypeStruct((n_batch, HEADS, HEAD_DIM), F32),
            jax.ShapeDtypeStruct((n_batch, HEADS, HEAD_DIM), F32),
        ),
        compiler_params=pltpu.CompilerParams(dimension_semantics=("arbitrary",),
                                             vmem_limit_bytes=VMEM_LIMIT),
        name="mlstm_sample",
    )(q, k, v, og, gcol, grow, mrep, c0, n0, mln, tri, triu, sel_last, expand, pick_last, seq_sum)


def _back_math(final_norm, yml_transposed, x_ref, ygm_ref, yml_ref, pe_ref, wout_ref, nffn_ref,
               wup_ref, wdown_ref, nple_ref, wpg_ref, wpp_ref, nfin_ref, out_ref):
    ml_dot = _dot_tn if yml_transposed else _dot
    h = x_ref[...] + _dot(ygm_ref[...], wout_ref[:GM_WIDTH]) + ml_dot(yml_ref[...], wout_ref[GM_WIDTH:])
    a = _rms(h, nffn_ref[...]).astype(BF16)
    ff = D_FF // FF_SPLIT

    def mlp_part(c):
        f = _dot(a, wup_ref[:, c * ff:(c + 1) * ff])
        f = jnp.square(jnp.maximum(f, 0.0)).astype(BF16)
        return _dot(f, wdown_ref[c * ff:(c + 1) * ff, :])

    mlp = mlp_part(0)
    for c in range(1, FF_SPLIT):
        mlp = mlp + mlp_part(c)
    h = h + mlp
    gate = jax.nn.sigmoid(_dot(_rms(h, nple_ref[...]).astype(BF16), wpg_ref[...]))
    h = h + gate * _dot(pe_ref[...].astype(BF16), wpp_ref[...])
    if final_norm:
        h = _rms(h, nfin_ref[...])
    out_ref[...] = h


def _back_kernel(final_norm, yml_transposed, *refs):
    _back_math(final_norm, yml_transposed, *refs)


def _back(x2d, ygm, yml, pe2d, wout, nffn, wup, wdown, nple, wpg, wpp, nfin, final_norm, tb):
    n = x2d.shape[0]
    yml_transposed = yml.shape[0] != n
    tok = lambda w: pl.BlockSpec((tb, w), lambda i: (i, 0))
    yml_spec = pl.BlockSpec((ML_WIDTH, tb), lambda i: (0, i)) if yml_transposed else tok(ML_WIDTH)
    return pl.pallas_call(
        functools.partial(_back_kernel, final_norm, yml_transposed),
        grid=(n // tb,),
        in_specs=[
            tok(D_MODEL), tok(GM_WIDTH), yml_spec, tok(PLE_DIM),
            _const_spec((GM_WIDTH + ML_WIDTH, D_MODEL)),
            _const_spec((1, D_MODEL)),
            _const_spec((D_MODEL, D_FF)), _const_spec((D_FF, D_MODEL)),
            _const_spec((1, D_MODEL)),
            _const_spec((D_MODEL, D_MODEL)), _const_spec((PLE_DIM, D_MODEL)),
            _const_spec((1, D_MODEL)),
        ],
        out_specs=tok(D_MODEL),
        out_shape=jax.ShapeDtypeStruct((n, D_MODEL), F32),
        compiler_params=pltpu.CompilerParams(dimension_semantics=("arbitrary",),
                                             vmem_limit_bytes=VMEM_LIMIT),
        name="back",
    )(x2d, ygm, yml, pe2d, wout, nffn, wup, wdown, nple, wpg, wpp, nfin)


def _const(x):
    return jnp.asarray(np.asarray(x, np.float32), BF16)


def _block_tri(block):
    r = np.arange(CHUNK)[:, None]
    c = np.arange(CHUNK)[None, :]
    return (r // block == c // block) & (c <= r)


def _token_block(n):
    return 512 if n % 512 == 0 else CHUNK


def kernel(x_prompt, x_sample, p_prompt, p_sample, state_C, state_n, state_m, norm_mix, w_in, gm_ln_g,
           gm_ln_b, gm_ws, gm_bs, ml_b_i, ml_b_f, ml_norm, w_out, norm_ffn, w_up, w_down, norm_ple,
           w_ple_gate, w_ple_proj, norm_final):
    depth = w_in.shape[0]
    batch, seq, _ = x_prompt.shape
    dec_batch, dec_seq, _ = x_sample.shape
    n_p, n_s = batch * seq, dec_batch * dec_seq
    n_seq = CHUNK // dec_seq
    assert seq % CHUNK == 0 and CHUNK % dec_seq == 0 and n_s % CHUNK == 0
    tb_p, tb_s = _token_block(seq), _token_block(n_s)

    hp = x_prompt.reshape(n_p, D_MODEL)
    hs = x_sample.reshape(n_s, D_MODEL)

    tri_p, tri_s = _block_tri(CHUNK), _block_tri(dec_seq)
    triu_p, eye = _const(tri_p.T), _const(np.eye(CHUNK))
    r = np.arange(CHUNK)
    i = np.arange(n_seq)
    sel_last = _const(r[None, :] == (r[:, None] // dec_seq) * dec_seq + dec_seq - 1)
    expand = _const(r[:, None] // dec_seq == i[None, :])
    pick_last = _const(r[None, :] == i[:, None] * dec_seq + dec_seq - 1)
    seq_sum = _const(r[None, :] // dec_seq == i[:, None])

    outs = {k: [] for k in ("Cp", "Np", "Mp", "Vp", "Cs", "Ns", "Ms", "Vs")}
    for l in range(depth):
        row = lambda a: a[l].reshape(1, -1).astype(F32)
        w_t = w_in[l].T.astype(BF16)
        wg_t = w_t[MAIN_COLS:]
        b_i, b_f = ml_b_i[l].astype(F32), ml_b_f[l].astype(F32)
        row_pad = lambda a: jnp.pad(a, ((0, GATE_ROWS - HEADS), (0, 0)))
        wgr = jnp.concatenate([row_pad(wg_t[:HEADS]), row_pad(wg_t[HEADS:])], axis=0)
        bgr = jnp.broadcast_to(jnp.concatenate([row_pad(b_i[:, None]), row_pad(b_f[:, None])], axis=0),
                               (2 * GATE_ROWS, tb_p))
        gain_t = jnp.broadcast_to(ml_norm[l].astype(F32)[:, None], (ML_WIDTH, tb_p))
        wgate = jnp.pad(wg_t.T, ((0, 0), (0, GATE_LANES - 2 * HEADS)))
        wgate_t = jnp.pad(wg_t, ((0, GATE_ROWS - 2 * HEADS), (0, 0)))
        gbias = jnp.concatenate([b_i, b_f])
        bcol = jnp.pad(gbias, (0, GATE_LANES - 2 * HEADS)).reshape(1, GATE_LANES)
        brow = jnp.broadcast_to(jnp.pad(gbias, (0, GATE_ROWS - 2 * HEADS)).reshape(GATE_ROWS, 1),
                                (GATE_ROWS, tb_s))
        ws = gm_ws[l]
        mixw_p = (ws[:, :CHUNK, :CHUNK] * tri_p.astype(np.float32)).astype(BF16)
        mixb_p = jnp.broadcast_to(gm_bs[l][:, :CHUNK, None], (HEADS, CHUNK, CHUNK)).astype(F32)
        mixw_s = (jnp.tile(ws[:, :dec_seq, :dec_seq], (1, n_seq, n_seq)) * tri_s.astype(np.float32)).astype(BF16)
        mixb_s = jnp.broadcast_to(jnp.tile(gm_bs[l][:, :dec_seq], (1, n_seq))[:, :, None],
                                  (HEADS, CHUNK, CHUNK)).astype(F32)
        last = l == depth - 1
        nfin = norm_final.reshape(1, D_MODEL).astype(F32)

        back_f32 = [w[l].astype(F32) for w in (w_out, w_up, w_down, w_ple_gate, w_ple_proj)]
        ygm, yml_t, vlast, cn_t, m8, wout, wup, wdown, wpg, wpp = _mixer_prompt(
            hp, row(norm_mix), w_t, wgr, bgr, row(gm_ln_g), row(gm_ln_b), mixw_p, mixb_p,
            gain_t, triu_p, eye, back_f32, batch, tb_p)

        def back(x2d, ygm, yml, pe, tb):
            return _back(x2d, ygm, yml, pe, wout, row(norm_ffn), wup, wdown, row(norm_ple),
                         wpg, wpp, nfin, last, tb)

        hp = back(hp, ygm, yml_t, p_prompt[l].reshape(n_p, PLE_DIM), 2 * tb_p)
        outs["Cp"].append(jnp.swapaxes(cn_t[:, :, :HEAD_DIM, :], -1, -2))
        outs["Np"].append(cn_t[:, :, HEAD_DIM, :])
        outs["Mp"].append(m8[:, :, 0])
        outs["Vp"].append(vlast)

        ygm, q, k, v, og, gcol, grow, vgn = _front(
            hs, row(norm_mix), w_t, wgate, wgate_t, bcol, brow, row(gm_ln_g), row(gm_ln_b), mixw_s, mixb_s,
            tb_s)
        mrep = jnp.repeat(state_m[l].astype(F32), dec_seq, axis=0)
        yml, c_new, n_new, m_new = _mlstm_sample(
            q, k, v, og, gcol, grow, mrep, state_C[l].astype(F32), state_n[l].astype(F32), row(ml_norm),
            _const(tri_s), _const(tri_s.T), sel_last, expand, pick_last, seq_sum, dec_seq)
        hs = back(hs, ygm, yml, p_sample[l].reshape(n_s, PLE_DIM), tb_s)
        outs["Cs"].append(c_new)
        outs["Ns"].append(n_new)
        outs["Ms"].append(m_new[..., 0])
        outs["Vs"].append(vgn.reshape(dec_batch, dec_seq, GM_WIDTH))

    st = lambda k: outs[k][0][None] if depth == 1 else jnp.stack(outs[k])
    return (hp.reshape(batch, seq, D_MODEL), hs.reshape(dec_batch, dec_seq, D_MODEL),
            st("Cp"), st("Np"), st("Mp"), st("Vp"), st("Cs"), st("Ns"), st("Ms"), st("Vs"))
```

```python
import functools

import jax
import numpy as np
import jax.numpy as jnp
from jax import lax
from jax.experimental import pallas as pl
from jax.experimental.pallas import tpu as pltpu

F32 = jnp.float32
BF16 = jnp.bfloat16

D_MODEL = 1024
GM_WIDTH = 512
ML_WIDTH = 512
HEADS = 4
HEAD_DIM = 128
D_FF = 4096
PLE_DIM = 256
EPS = 1e-6
CHUNK = 128
MAIN_COLS = 2 * GM_WIDTH + 4 * ML_WIDTH
Q_LO = 2 * GM_WIDTH
K_LO = Q_LO + ML_WIDTH
V_LO = K_LO + ML_WIDTH
O_LO = V_LO + ML_WIDTH
IN_COLS = MAIN_COLS + 2 * HEADS
GATE_LANES = 128
GATE_ROWS = 16
STATE_ROWS = HEAD_DIM + 16
FF_SPLIT = 4
BF16_ROWS = 16
VMEM_LIMIT = 60 * 1024 * 1024


def _dot(a, b):
    return jnp.dot(a, b, preferred_element_type=F32)


def _dot_nt(a, b):
    return lax.dot_general(a, b, (((1,), (1,)), ((), ())), preferred_element_type=F32)


def _dot_tn(a, b):
    return lax.dot_general(a, b, (((0,), (0,)), ((), ())), preferred_element_type=F32)


def _split3(x):
    x1 = x.astype(BF16)
    r = x - x1.astype(F32)
    x2 = r.astype(BF16)
    r = r - x2.astype(F32)
    return x1, x2, r.astype(BF16)


def _sel_dot(sel, x):
    p1, p2, p3 = _split3(x)
    return _dot(sel, p1) + _dot(sel, p2) + _dot(sel, p3)


def _dot_sel(x, sel):
    p1, p2, p3 = _split3(x)
    return _dot(p1, sel) + _dot(p2, sel) + _dot(p3, sel)


def _sel_dot_nt(sel, x):
    p1, p2, p3 = _split3(x)
    return _dot_nt(sel, p1) + _dot_nt(sel, p2) + _dot_nt(sel, p3)


def _rms(x, g):
    return x * lax.rsqrt(jnp.mean(x * x, axis=-1, keepdims=True) + EPS) * g


def _log_sigmoid(x):
    return -(jnp.maximum(-x, 0.0) + jnp.log1p(jnp.exp(-jnp.abs(x))))


def _const_spec(shape):
    nd = len(shape)
    return pl.BlockSpec(shape, lambda *_: (0,) * nd, pipeline_mode=pl.Buffered(1))


def _gmlp_pieces(x_ref, nmix_ref, w_t_ref, lng_ref, lnb_ref, mixw_ref, mixb_ref, ygm_ref, vgn_ref, st):
    tb = x_ref.shape[0]

    def norm():
        st["a"] = _rms(x_ref[...], nmix_ref[...]).astype(BF16)

    def u_part():
        st["u"] = jax.nn.gelu(_dot_nt(st["a"], w_t_ref[:GM_WIDTH]))

    def v_part():
        vg = jax.nn.gelu(_dot_nt(st["a"], w_t_ref[GM_WIDTH:2 * GM_WIDTH]))
        mu = jnp.mean(vg, axis=-1, keepdims=True)
        var = jnp.mean(jnp.square(vg - mu), axis=-1, keepdims=True)
        vgn = (vg - mu) * lax.rsqrt(var + EPS) * lng_ref[...] + lnb_ref[...]
        vgn_ref[...] = vgn[tb - vgn_ref.shape[0]:, :]
        st["vgb"] = vgn.astype(BF16)

    def mix(c):
        rows = slice(c * CHUNK, (c + 1) * CHUNK)
        for h in range(HEADS):
            cols = slice(h * HEAD_DIM, (h + 1) * HEAD_DIM)
            s = _dot(mixw_ref[h], st["vgb"][rows, cols]) + mixb_ref[h]
            ygm_ref[rows, cols] = (st["u"][rows, cols] * s).astype(BF16)

    return [norm, u_part, v_part] + [functools.partial(mix, c) for c in range(tb // CHUNK)]


def _mixer_prompt_kernel(steps_per_seq, n_cast, x_ref, nmix_ref, w_t_ref, wgr_ref, bgr_ref,
                         lng_ref, lnb_ref, mixw_ref, mixb_ref, gain_t_ref, triu_ref, eye_ref, *rest):
    cast_in, rest = rest[:n_cast], rest[n_cast:]
    ygm_ref, yml_t_ref, vlast_ref, cn_out_ref, m_out_ref = rest[:5]
    cast_out, (cn_scr, m_scr) = rest[5:5 + n_cast], rest[5 + n_cast:]
    for src, dst in zip(cast_in, cast_out):
        dst[...] = src[...].astype(BF16)

    tb = x_ref.shape[0]
    n_chunks = tb // CHUNK
    blocks = [slice(c * CHUNK, (c + 1) * CHUNK) for c in range(n_chunks)]
    heads = [slice(h * HEAD_DIM, (h + 1) * HEAD_DIM) for h in range(HEADS)]
    unit_ids = [(c, h) for c in range(n_chunks) for h in range(HEADS)]
    new_seq = pl.program_id(0) % steps_per_seq == 0
    triu = triu_ref[...]
    mask_t = triu.astype(F32) > 0.0
    eye = eye_ref[...]
    lane_g = lax.broadcasted_iota(jnp.int32, (GATE_ROWS, tb), 1) % CHUNK
    sub_g = lax.broadcasted_iota(jnp.int32, (GATE_ROWS, CHUNK), 0)
    ones_row = (sub_g == 0).astype(BF16)
    last = slice(CHUNK - 1, CHUNK)

    st = {}
    norm, u_part, v_part, *mix = _gmlp_pieces(x_ref, nmix_ref, w_t_ref, lng_ref, lnb_ref, mixw_ref, mixb_ref,
                                              ygm_ref, vlast_ref, st)
    norm()
    a = st["a"]

    zr = _dot_nt(wgr_ref[...], a) + bgr_ref[...]
    lf_r = _log_sigmoid(zr[GATE_ROWS:])
    b_r = jnp.concatenate([_dot_sel(lf_r[:, t], triu) for t in blocks], axis=1)
    r_r = zr[:GATE_ROWS] - b_r
    k = (_dot_nt(a, w_t_ref[K_LO:V_LO]) * (HEAD_DIM ** -0.5)).astype(BF16)
    p_r = r_r
    shift = 1
    while shift < CHUNK:
        p_r = jnp.maximum(p_r, jnp.where(lane_g >= shift, pltpu.roll(p_r, shift, axis=1), -jnp.inf))
        shift *= 2
    r_c = [_sel_dot_nt(eye, r_r[:, t]) for t in blocks]
    e_end = [jnp.exp(r_r[:, t] - p_r[:, t][:, last]) for t in blocks]
    q_t = _dot_nt(w_t_ref[Q_LO:K_LO], a).astype(BF16)
    v_t = _dot_nt(w_t_ref[V_LO:O_LO], a)
    og_t = jax.nn.sigmoid(_dot_nt(w_t_ref[O_LO:MAIN_COLS], a)) * gain_t_ref[...]

    zero = jnp.zeros((HEAD_DIM, HEAD_DIM), BF16)

    def block_diag(x, y):
        return jnp.concatenate([jnp.concatenate([x, zero], axis=1), jnp.concatenate([zero, y], axis=1)], axis=0)

    def pair_dot(lhs, rhs):
        out = _dot(jnp.concatenate(lhs, axis=1), block_diag(*rhs))
        return out[:, :HEAD_DIM], out[:, HEAD_DIM:]

    pair_ids = [(c, h) for c in range(n_chunks) for h in range(0, HEADS, 2)]
    kq, e_intra, x1, upd = {}, {}, {}, {}
    for c, h in pair_ids:
        t = blocks[c]
        kq[c, h], kq[c, h + 1] = pair_dot([k[t, heads[h]], k[t, heads[h + 1]]],
                                          [q_t[heads[h], t], q_t[heads[h + 1], t]])
        for g in (h, h + 1):
            e_intra[c, g] = jnp.where(mask_t, jnp.exp(r_c[c][:, g:g + 1] - p_r[g:g + 1, t]), 0.0)
    u_part()
    for c, h in pair_ids:
        t = blocks[c]
        s0, vext, vw = [], [], []
        for g in (h, h + 1):
            s0.append((kq[c, g] * e_intra[c, g]).astype(BF16))
            vext.append(jnp.concatenate([v_t[heads[g], t].astype(BF16), ones_row], axis=0))
            e_row = e_end[c][g:g + 1, :]
            vw.append(jnp.concatenate([(v_t[heads[g], t] * e_row).astype(BF16),
                                       jnp.where(sub_g == 0, e_row, 0.0).astype(BF16)], axis=0))
        x1[c, h], x1[c, h + 1] = pair_dot(vext, s0)
        upd[c, h], upd[c, h + 1] = pair_dot(vw, [k[t, heads[h]], k[t, heads[h + 1]]])
    v_part()

    cn_in, cm, w_inter = {}, {}, {}
    for h in range(HEADS):
        cn = jnp.where(new_seq, 0.0, cn_scr[h])
        m_prev = jnp.where(new_seq, 0.0, m_scr[h:h + 1, 0:1])
        for c in range(n_chunks):
            prow = p_r[h:h + 1, blocks[c]]
            cn_in[c, h] = cn
            cm[c, h] = jnp.maximum(m_prev, prow)
            w_inter[c, h] = jnp.exp(m_prev - cm[c, h])
            cm_last = cm[c, h][:, last]
            cn = jnp.exp(m_prev - cm_last) * cn + jnp.exp(prow[:, last] - cm_last) * upd[c, h]
            m_prev = b_r[h:h + 1, blocks[c]][:, last] + cm_last
        cn_scr[h] = cn
        m_scr[h:h + 1, :] = jnp.broadcast_to(m_prev, (1, HEAD_DIM))
    x2 = {}

    def state_matmuls(c):
        t = blocks[c]
        for h in range(0, HEADS, 2):
            x2[c, h], x2[c, h + 1] = pair_dot([cn_in[c, h].astype(BF16), cn_in[c, h + 1].astype(BF16)],
                                              [q_t[heads[h], t], q_t[heads[h + 1], t]])

    def head_outputs(c):
        t = blocks[c]
        for h, hd in enumerate(heads):
            brow, prow = b_r[h:h + 1, t], p_r[h:h + 1, t]
            nd = x1[c, h] * jnp.exp(prow - cm[c, h]) + x2[c, h] * w_inter[c, h]
            num_t, den = nd[:HEAD_DIM], nd[HEAD_DIM:HEAD_DIM + 1]
            inv = 1.0 / jnp.maximum(jnp.abs(den), jnp.exp(-(brow + cm[c, h])))
            ssq = jnp.sum(num_t * num_t, axis=0, keepdims=True)
            scale = inv * lax.rsqrt(ssq * (inv * inv) * (1.0 / HEAD_DIM) + EPS)
            yml_t_ref[hd, t] = (num_t * scale * og_t[hd, t]).astype(BF16)

    for c in range(n_chunks):
        state_matmuls(c)
    for c in range(n_chunks):
        head_outputs(c)
        mix[c]()

    @pl.when(pl.program_id(0) % steps_per_seq == steps_per_seq - 1)
    def _():
        cn_out_ref[...] = cn_scr[...]
        m_out_ref[...] = m_scr[...]


def _mixer_prompt(x2d, nmix, w_t, wgr, bgr, lng, lnb, mixw, mixb, gain_t, triu, eye, cast, batch, tb):
    n = x2d.shape[0]
    n_steps = n // tb
    steps_per_seq = n_steps // batch

    def window(w):
        rows = max(BF16_ROWS, w.shape[0] // n_steps)
        return pl.BlockSpec((rows, w.shape[1]), lambda i: (jnp.minimum(i, w.shape[0] // rows - 1), 0))

    tok = lambda w: pl.BlockSpec((tb, w), lambda i: (i, 0))
    per_seq = lambda *shape: pl.BlockSpec((None,) + shape, lambda i: (i // steps_per_seq,) + (0,) * len(shape))
    return pl.pallas_call(
        functools.partial(_mixer_prompt_kernel, steps_per_seq, len(cast)),
        grid=(n_steps,),
        in_specs=[
            tok(D_MODEL),
            _const_spec((1, D_MODEL)),
            _const_spec((IN_COLS, D_MODEL)),
            _const_spec((2 * GATE_ROWS, D_MODEL)),
            _const_spec((2 * GATE_ROWS, tb)),
            _const_spec((1, GM_WIDTH)),
            _const_spec((1, GM_WIDTH)),
            _const_spec((HEADS, CHUNK, CHUNK)),
            _const_spec((HEADS, CHUNK, CHUNK)),
            _const_spec((ML_WIDTH, tb)),
            _const_spec((CHUNK, CHUNK)),
            _const_spec((CHUNK, CHUNK)),
        ] + [window(w) for w in cast],
        out_specs=(tok(GM_WIDTH), pl.BlockSpec((ML_WIDTH, tb), lambda i: (0, i)), per_seq(CHUNK, GM_WIDTH),
                   per_seq(HEADS, STATE_ROWS, HEAD_DIM), per_seq(HEADS, HEAD_DIM)) + tuple(window(w) for w in cast),
        out_shape=(
            jax.ShapeDtypeStruct((n, GM_WIDTH), BF16),
            jax.ShapeDtypeStruct((ML_WIDTH, n), BF16),
            jax.ShapeDtypeStruct((batch, CHUNK, GM_WIDTH), F32),
            jax.ShapeDtypeStruct((batch, HEADS, STATE_ROWS, HEAD_DIM), F32),
            jax.ShapeDtypeStruct((batch, HEADS, HEAD_DIM), F32),
        ) + tuple(jax.ShapeDtypeStruct(w.shape, BF16) for w in cast),
        scratch_shapes=[pltpu.VMEM((HEADS, STATE_ROWS, HEAD_DIM), F32), pltpu.VMEM((HEADS, HEAD_DIM), F32)],
        compiler_params=pltpu.CompilerParams(dimension_semantics=("arbitrary",),
                                             vmem_limit_bytes=VMEM_LIMIT),
        name="mixer_prompt",
    )(x2d, nmix, w_t, wgr, bgr, lng, lnb, mixw, mixb, gain_t, triu, eye, *cast)


def _front_kernel(x_ref, nmix_ref, w_t_ref, wgate_ref, wgate_t_ref, bcol_ref, brow_ref,
                  lng_ref, lnb_ref, mixw_ref, mixb_ref,
                  ygm_ref, q_ref, k_ref, v_ref, og_ref, gcol_ref, grow_ref, vgn_ref):
    st = {}
    for piece in _gmlp_pieces(x_ref, nmix_ref, w_t_ref, lng_ref, lnb_ref, mixw_ref, mixb_ref,
                              ygm_ref, vgn_ref, st):
        piece()
    a = st["a"]

    def proj(lo):
        return _dot_nt(a, w_t_ref[lo:lo + ML_WIDTH])

    q_ref[...] = proj(Q_LO).astype(BF16)
    k_ref[...] = (proj(K_LO) * (HEAD_DIM ** -0.5)).astype(BF16)
    v_ref[...] = proj(V_LO).astype(BF16)
    og_ref[...] = jax.nn.sigmoid(proj(O_LO))
    zc = _dot(a, wgate_ref[...]) + bcol_ref[...]
    lane = lax.broadcasted_iota(jnp.int32, zc.shape, 1)
    gcol_ref[...] = jnp.where(lane >= HEADS, _log_sigmoid(zc), zc)
    zr = _dot_nt(wgate_t_ref[...], a) + brow_ref[...]
    sub = lax.broadcasted_iota(jnp.int32, zr.shape, 0)
    grow_ref[...] = jnp.where(sub >= HEADS, _log_sigmoid(zr), zr)


def _front(x2d, nmix, w_t, wgate, wgate_t, bcol, brow, lng, lnb, mixw, mixb, tb):
    n = x2d.shape[0]
    tok = lambda w: pl.BlockSpec((tb, w), lambda i: (i, 0))
    out_shape = (
        jax.ShapeDtypeStruct((n, GM_WIDTH), BF16),
        jax.ShapeDtypeStruct((n, ML_WIDTH), BF16),
        jax.ShapeDtypeStruct((n, ML_WIDTH), BF16),
        jax.ShapeDtypeStruct((n, ML_WIDTH), BF16),
        jax.ShapeDtypeStruct((n, ML_WIDTH), F32),
        jax.ShapeDtypeStruct((n, GATE_LANES), F32),
        jax.ShapeDtypeStruct((GATE_ROWS, n), F32),
        jax.ShapeDtypeStruct((n, GM_WIDTH), F32),
    )
    return pl.pallas_call(
        _front_kernel,
        grid=(n // tb,),
        in_specs=[
            tok(D_MODEL),
            _const_spec((1, D_MODEL)),
            _const_spec((IN_COLS, D_MODEL)),
            _const_spec((D_MODEL, GATE_LANES)),
            _const_spec((GATE_ROWS, D_MODEL)),
            _const_spec((1, GATE_LANES)),
            _const_spec((GATE_ROWS, tb)),
            _const_spec((1, GM_WIDTH)),
            _const_spec((1, GM_WIDTH)),
            _const_spec((HEADS, CHUNK, CHUNK)),
            _const_spec((HEADS, CHUNK, CHUNK)),
        ],
        out_specs=(tok(GM_WIDTH), tok(ML_WIDTH), tok(ML_WIDTH), tok(ML_WIDTH), tok(ML_WIDTH),
                   tok(GATE_LANES), pl.BlockSpec((GATE_ROWS, tb), lambda i: (0, i)), tok(GM_WIDTH)),
        out_shape=out_shape,
        compiler_params=pltpu.CompilerParams(dimension_semantics=("arbitrary",),
                                             vmem_limit_bytes=VMEM_LIMIT),
        name="front",
    )(x2d, nmix, w_t, wgate, wgate_t, bcol, brow, lng, lnb, mixw, mixb)


def _intra(q, ks, igcol, bcol, igrow, brow, mprev, mask):
    d = bcol + (igrow - brow)
    g = bcol + mprev
    m_t = jnp.maximum(g, jnp.max(jnp.where(mask, d, -jnp.inf), axis=-1, keepdims=True))
    w_intra = jnp.where(mask, jnp.exp(d - m_t), 0.0)
    w_inter = jnp.exp(g - m_t)
    s = _dot_nt(q, ks) * w_intra
    return s, w_inter, m_t, g


def _head_out(num, den, m_t, gain, og):
    hh = num / jnp.maximum(jnp.abs(den), jnp.exp(-m_t))
    return (og * _rms(hh, gain)).astype(BF16)


def _mlstm_sample_kernel(seq_len, q_ref, k_ref, v_ref, og_ref, gcol_ref, grow_ref, mrep_ref, c_ref, n_ref,
                         mln_ref, tri_ref, triu_ref, sel_last_ref, expand_ref, pick_last_ref, seq_sum_ref,
                         yml_ref, c_out_ref, n_out_ref, m_out_ref):
    n_seq = CHUNK // seq_len
    tri = tri_ref[...]
    mask = tri.astype(F32) > 0.0
    gcol = gcol_ref[...]
    grow = grow_ref[...]
    bcol_all = _sel_dot(tri, gcol)
    brow_all = _dot_sel(grow, triu_ref[...])
    sel_last = sel_last_ref[...]
    expand = expand_ref[...]
    pick_last = pick_last_ref[...]
    seq_sum = seq_sum_ref[...]
    row = lax.broadcasted_iota(jnp.int32, (CHUNK, HEAD_DIM), 0)
    lane = lax.broadcasted_iota(jnp.int32, (CHUNK, HEAD_DIM), 1)
    seq_rows = [(row >= i * seq_len) & (row < (i + 1) * seq_len) for i in range(n_seq)]
    head_cols = [slice(h * HEAD_DIM, (h + 1) * HEAD_DIM) for h in range(HEADS)]
    every_head = range(HEADS)
    q = [q_ref[:, cols] for cols in head_cols]
    ks = [k_ref[:, cols] for cols in head_cols]
    v = [v_ref[:, cols] for cols in head_cols]
    zero = jnp.zeros_like(q[0])
    igcol = [gcol[:, h:h + 1] for h in every_head]
    bcol = [bcol_all[:, HEADS + h:HEADS + h + 1] for h in every_head]
    intra = [_intra(q[h], ks[h], igcol[h], bcol[h], grow[h:h + 1, :], brow_all[HEADS + h:HEADS + h + 1, :],
                    mrep_ref[:, h:h + 1], mask) for h in every_head]
    s, w_inter, m_t, g = zip(*intra)
    c_all = [c_ref[:, h] for h in every_head]
    n_all = [n_ref[:, h] for h in every_head]
    qc = [_dot(jnp.concatenate([jnp.where(m, q[h], zero) for m in seq_rows], axis=1),
               c_all[h].reshape(n_seq * HEAD_DIM, HEAD_DIM).astype(BF16)) for h in every_head]
    n_rows = [_sel_dot(expand, n_all[h]) for h in every_head]
    ends = [_sel_dot(sel_last, jnp.where(lane == 0, m_t[h], jnp.where(lane == 1, g[h],
                                                                        jnp.where(lane == 2, bcol[h], 0.0))))
            for h in every_head]
    sv = [_dot(s[h].astype(BF16), v[h]) for h in every_head]
    kw, dec, upd, dec_seq, n_inc, m_seq = [], [], [], [], [], []
    for h in every_head:
        m_new, g_last, b_last = ends[h][:, 0:1], ends[h][:, 1:2], ends[h][:, 2:3]
        w_end = jnp.exp(b_last - bcol[h] + igcol[h] - m_new)
        dec.append(jnp.exp(g_last - m_new))
        kw.append(ks[h].astype(F32) * w_end)
    for h in every_head:
        v_exp = jnp.concatenate([jnp.where(m, v[h], zero) for m in seq_rows], axis=1)
        upd.append(_dot(kw[h].T.astype(BF16), v_exp))
        dec_seq.append(_sel_dot(pick_last, jnp.broadcast_to(dec[h], (CHUNK, HEAD_DIM))))
        n_inc.append(_sel_dot(seq_sum, kw[h]))
        m_seq.append(_sel_dot(pick_last, jnp.broadcast_to(m_t[h], (CHUNK, HEAD_DIM))))
    for h, cols in enumerate(head_cols):
        qn = jnp.sum(q[h].astype(F32) * n_rows[h], axis=-1, keepdims=True)
        num = sv[h] + w_inter[h] * qc[h]
        den = jnp.sum(s[h], axis=-1, keepdims=True) + w_inter[h] * qn
        yml_ref[:, cols] = _head_out(num, den, m_t[h], mln_ref[:, cols], og_ref[:, cols])
        for i in range(n_seq):
            c_out_ref[i, h] = (dec_seq[h][i:i + 1, 0:1] * c_all[h][i]
                               + upd[h][:, i * HEAD_DIM:(i + 1) * HEAD_DIM])
        n_out_ref[:, h] = dec_seq[h] * n_all[h] + n_inc[h]
        m_out_ref[:, h] = m_seq[h]


def _mlstm_sample(q, k, v, og, gcol, grow, mrep, c0, n0, mln, tri, triu, sel_last, expand, pick_last,
                  seq_sum, seq_len):
    n = q.shape[0]
    n_seq = CHUNK // seq_len
    n_batch = n // seq_len
    tok = lambda w: pl.BlockSpec((CHUNK, w), lambda i: (i, 0))
    c_spec = pl.BlockSpec((n_seq, HEADS, HEAD_DIM, HEAD_DIM), lambda i: (i, 0, 0, 0))
    n_spec = pl.BlockSpec((n_seq, HEADS, HEAD_DIM), lambda i: (i, 0, 0))
    return pl.pallas_call(
        functools.partial(_mlstm_sample_kernel, seq_len),
        grid=(n // CHUNK,),
        in_specs=[
            tok(ML_WIDTH), tok(ML_WIDTH), tok(ML_WIDTH), tok(ML_WIDTH), tok(GATE_LANES),
            pl.BlockSpec((GATE_ROWS, CHUNK), lambda i: (0, i)),
            pl.BlockSpec((CHUNK, HEADS), lambda i: (i, 0)),
            c_spec, n_spec,
            _const_spec((1, ML_WIDTH)),
            _const_spec((CHUNK, CHUNK)), _const_spec((CHUNK, CHUNK)), _const_spec((CHUNK, CHUNK)),
            _const_spec((CHUNK, n_seq)), _const_spec((n_seq, CHUNK)), _const_spec((n_seq, CHUNK)),
        ],
        out_specs=(tok(ML_WIDTH), c_spec, n_spec, n_spec),
        out_shape=(
            jax.ShapeDtypeStruct((n, ML_WIDTH), BF16),
            jax.ShapeDtypeStruct((n_batch, HEADS, HEAD_DIM, HEAD_DIM), F32),
            jax.ShapeDtypeStruct((n_batch, HEADS, HEAD_DIM), F32),
            jax.ShapeDtypeStruct((n_batch, HEADS, HEAD_DIM), F32),
        ),
        compiler_params=pltpu.CompilerParams(dimension_semantics=("arbitrary",),
                                             vmem_limit_bytes=VMEM_LIMIT),
        name="mlstm_sample",
    )(q, k, v, og, gcol, grow, mrep, c0, n0, mln, tri, triu, sel_last, expand, pick_last, seq_sum)


def _back_math(final_norm, yml_transposed, x_ref, ygm_ref, yml_ref, pe_ref, wout_ref, nffn_ref,
               wup_ref, wdown_ref, nple_ref, wpg_ref, wpp_ref, nfin_ref, out_ref):
    ml_dot = _dot_tn if yml_transposed else _dot
    h = x_ref[...] + _dot(ygm_ref[...], wout_ref[:GM_WIDTH]) + ml_dot(yml_ref[...], wout_ref[GM_WIDTH:])
    a = _rms(h, nffn_ref[...]).astype(BF16)
    ff = D_FF // FF_SPLIT

    def mlp_part(c):
        f = _dot(a, wup_ref[:, c * ff:(c + 1) * ff])
        f = jnp.square(jnp.maximum(f, 0.0)).astype(BF16)
        return _dot(f, wdown_ref[c * ff:(c + 1) * ff, :])

    mlp = mlp_part(0)
    for c in range(1, FF_SPLIT):
        mlp = mlp + mlp_part(c)
    h = h + mlp
    gate = jax.nn.sigmoid(_dot(_rms(h, nple_ref[...]).astype(BF16), wpg_ref[...]))
    h = h + gate * _dot(pe_ref[...].astype(BF16), wpp_ref[...])
    if final_norm:
        h = _rms(h, nfin_ref[...])
    out_ref[...] = h


def _back_kernel(final_norm, yml_transposed, *refs):
    _back_math(final_norm, yml_transposed, *refs)


def _back(x2d, ygm, yml, pe2d, wout, nffn, wup, wdown, nple, wpg, wpp, nfin, final_norm, tb):
    n = x2d.shape[0]
    yml_transposed = yml.shape[0] != n
    tok = lambda w: pl.BlockSpec((tb, w), lambda i: (i, 0))
    yml_spec = pl.BlockSpec((ML_WIDTH, tb), lambda i: (0, i)) if yml_transposed else tok(ML_WIDTH)
    return pl.pallas_call(
        functools.partial(_back_kernel, final_norm, yml_transposed),
        grid=(n // tb,),
        in_specs=[
            tok(D_MODEL), tok(GM_WIDTH), yml_spec, tok(PLE_DIM),
            _const_spec((GM_WIDTH + ML_WIDTH, D_MODEL)),
            _const_spec((1, D_MODEL)),
            _const_spec((D_MODEL, D_FF)), _const_spec((D_FF, D_MODEL)),
            _const_spec((1, D_MODEL)),
            _const_spec((D_MODEL, D_MODEL)), _const_spec((PLE_DIM, D_MODEL)),
            _const_spec((1, D_MODEL)),
        ],
        out_specs=tok(D_MODEL),
        out_shape=jax.ShapeDtypeStruct((n, D_MODEL), F32),
        compiler_params=pltpu.CompilerParams(dimension_semantics=("arbitrary",),
                                             vmem_limit_bytes=VMEM_LIMIT),
        name="back",
    )(x2d, ygm, yml, pe2d, wout, nffn, wup, wdown, nple, wpg, wpp, nfin)


def _const(x):
    return jnp.asarray(np.asarray(x, np.float32), BF16)


def _block_tri(block):
    r = np.arange(CHUNK)[:, None]
    c = np.arange(CHUNK)[None, :]
    return (r // block == c // block) & (c <= r)


def _token_block(n):
    return 512 if n % 512 == 0 else CHUNK


def kernel(x_prompt, x_sample, p_prompt, p_sample, state_C, state_n, state_m, norm_mix, w_in, gm_ln_g,
           gm_ln_b, gm_ws, gm_bs, ml_b_i, ml_b_f, ml_norm, w_out, norm_ffn, w_up, w_down, norm_ple,
           w_ple_gate, w_ple_proj, norm_final):
    depth = w_in.shape[0]
    batch, seq, _ = x_prompt.shape
    dec_batch, dec_seq, _ = x_sample.shape
    n_p, n_s = batch * seq, dec_batch * dec_seq
    n_seq = CHUNK // dec_seq
    assert seq % CHUNK == 0 and CHUNK % dec_seq == 0 and n_s % CHUNK == 0
    tb_p, tb_s = _token_block(seq), _token_block(n_s)

    hp = x_prompt.reshape(n_p, D_MODEL)
    hs = x_sample.reshape(n_s, D_MODEL)

    tri_p, tri_s = _block_tri(CHUNK), _block_tri(dec_seq)
    triu_p, eye = _const(tri_p.T), _const(np.eye(CHUNK))
    r = np.arange(CHUNK)
    i = np.arange(n_seq)
    sel_last = _const(r[None, :] == (r[:, None] // dec_seq) * dec_seq + dec_seq - 1)
    expand = _const(r[:, None] // dec_seq == i[None, :])
    pick_last = _const(r[None, :] == i[:, None] * dec_seq + dec_seq - 1)
    seq_sum = _const(r[None, :] // dec_seq == i[:, None])

    outs = {k: [] for k in ("Cp", "Np", "Mp", "Vp", "Cs", "Ns", "Ms", "Vs")}
    for l in range(depth):
        row = lambda a: a[l].reshape(1, -1).astype(F32)
        w_t = w_in[l].T.astype(BF16)
        wg_t = w_t[MAIN_COLS:]
        b_i, b_f = ml_b_i[l].astype(F32), ml_b_f[l].astype(F32)
        row_pad = lambda a: jnp.pad(a, ((0, GATE_ROWS - HEADS), (0, 0)))
        wgr = jnp.concatenate([row_pad(wg_t[:HEADS]), row_pad(wg_t[HEADS:])], axis=0)
        bgr = jnp.broadcast_to(jnp.concatenate([row_pad(b_i[:, None]), row_pad(b_f[:, None])], axis=0),
                               (2 * GATE_ROWS, tb_p))
        gain_t = jnp.broadcast_to(ml_norm[l].astype(F32)[:, None], (ML_WIDTH, tb_p))
        wgate = jnp.pad(wg_t.T, ((0, 0), (0, GATE_LANES - 2 * HEADS)))
        wgate_t = jnp.pad(wg_t, ((0, GATE_ROWS - 2 * HEADS), (0, 0)))
        gbias = jnp.concatenate([b_i, b_f])
        bcol = jnp.pad(gbias, (0, GATE_LANES - 2 * HEADS)).reshape(1, GATE_LANES)
        brow = jnp.broadcast_to(jnp.pad(gbias, (0, GATE_ROWS - 2 * HEADS)).reshape(GATE_ROWS, 1),
                                (GATE_ROWS, tb_s))
        ws = gm_ws[l]
        mixw_p = (ws[:, :CHUNK, :CHUNK] * tri_p.astype(np.float32)).astype(BF16)
        mixb_p = jnp.broadcast_to(gm_bs[l][:, :CHUNK, None], (HEADS, CHUNK, CHUNK)).astype(F32)
        mixw_s = (jnp.tile(ws[:, :dec_seq, :dec_seq], (1, n_seq, n_seq)) * tri_s.astype(np.float32)).astype(BF16)
        mixb_s = jnp.broadcast_to(jnp.tile(gm_bs[l][:, :dec_seq], (1, n_seq))[:, :, None],
                                  (HEADS, CHUNK, CHUNK)).astype(F32)
        last = l == depth - 1
        nfin = norm_final.reshape(1, D_MODEL).astype(F32)

        back_f32 = [w[l].astype(F32) for w in (w_out, w_up, w_down, w_ple_gate, w_ple_proj)]
        ygm, yml_t, vlast, cn_t, m8, wout, wup, wdown, wpg, wpp = _mixer_prompt(
            hp, row(norm_mix), w_t, wgr, bgr, row(gm_ln_g), row(gm_ln_b), mixw_p, mixb_p,
            gain_t, triu_p, eye, back_f32, batch, tb_p)

        def back(x2d, ygm, yml, pe, tb):
            return _back(x2d, ygm, yml, pe, wout, row(norm_ffn), wup, wdown, row(norm_ple),
                         wpg, wpp, nfin, last, tb)

        hp = back(hp, ygm, yml_t, p_prompt[l].reshape(n_p, PLE_DIM), 2 * tb_p)
        outs["Cp"].append(jnp.swapaxes(cn_t[:, :, :HEAD_DIM, :], -1, -2))
        outs["Np"].append(cn_t[:, :, HEAD_DIM, :])
        outs["Mp"].append(m8[:, :, 0])
        outs["Vp"].append(vlast)

        ygm, q, k, v, og, gcol, grow, vgn = _front(
            hs, row(norm_mix), w_t, wgate, wgate_t, bcol, brow, row(gm_ln_g), row(gm_ln_b), mixw_s, mixb_s,
            tb_s)
        mrep = jnp.repeat(state_m[l].astype(F32), dec_seq, axis=0)
        yml, c_new, n_new, m_new = _mlstm_sample(
            q, k, v, og, gcol, grow, mrep, state_C[l].astype(F32), state_n[l].astype(F32), row(ml_norm),
            _const(tri_s), _const(tri_s.T), sel_last, expand, pick_last, seq_sum, dec_seq)
        hs = back(hs, ygm, yml, p_sample[l].reshape(n_s, PLE_DIM), tb_s)
        outs["Cs"].append(c_new)
        outs["Ns"].append(n_new)
        outs["Ms"].append(m_new[..., 0])
        outs["Vs"].append(vgn.reshape(dec_batch, dec_seq, GM_WIDTH))

    st = lambda k: outs[k][0][None] if depth == 1 else jnp.stack(outs[k])
    return (hp.reshape(batch, seq, D_MODEL), hs.reshape(dec_batch, dec_seq, D_MODEL),
            st("Cp"), st("Np"), st("Mp"), st("Vp"), st("Cs"), st("Ns"), st("Ms"), st("Vs"))
```

```python
import functools

import jax
import numpy as np
import jax.numpy as jnp
from jax import lax
from jax.experimental import pallas as pl
from jax.experimental.pallas import tpu as pltpu

F32 = jnp.float32
BF16 = jnp.bfloat16

D_MODEL = 1024
GM_WIDTH = 512
ML_WIDTH = 512
HEADS = 4
HEAD_DIM = 128
D_FF = 4096
PLE_DIM = 256
EPS = 1e-6
CHUNK = 128
MAIN_COLS = 2 * GM_WIDTH + 4 * ML_WIDTH
Q_LO = 2 * GM_WIDTH
K_LO = Q_LO + ML_WIDTH
V_LO = K_LO + ML_WIDTH
O_LO = V_LO + ML_WIDTH
IN_COLS = MAIN_COLS + 2 * HEADS
GATE_LANES = 128
GATE_ROWS = 16
STATE_ROWS = HEAD_DIM + 16
FF_SPLIT = 4
BF16_ROWS = 16
VMEM_LIMIT = 60 * 1024 * 1024


def _dot(a, b):
    return jnp.dot(a, b, preferred_element_type=F32)


def _dot_nt(a, b):
    return lax.dot_general(a, b, (((1,), (1,)), ((), ())), preferred_element_type=F32)


def _dot_tn(a, b):
    return lax.dot_general(a, b, (((0,), (0,)), ((), ())), preferred_element_type=F32)


def _split3(x):
    x1 = x.astype(BF16)
    r = x - x1.astype(F32)
    x2 = r.astype(BF16)
    r = r - x2.astype(F32)
    return x1, x2, r.astype(BF16)


def _sel_dot(sel, x):
    p1, p2, p3 = _split3(x)
    return _dot(sel, p1) + _dot(sel, p2) + _dot(sel, p3)


def _dot_sel(x, sel):
    p1, p2, p3 = _split3(x)
    return _dot(p1, sel) + _dot(p2, sel) + _dot(p3, sel)


def _sel_dot_nt(sel, x):
    p1, p2, p3 = _split3(x)
    return _dot_nt(sel, p1) + _dot_nt(sel, p2) + _dot_nt(sel, p3)


def _rms(x, g):
    return x * lax.rsqrt(jnp.mean(x * x, axis=-1, keepdims=True) + EPS) * g


def _log_sigmoid(x):
    return -(jnp.maximum(-x, 0.0) + jnp.log1p(jnp.exp(-jnp.abs(x))))


def _const_spec(shape):
    nd = len(shape)
    return pl.BlockSpec(shape, lambda *_: (0,) * nd, pipeline_mode=pl.Buffered(1))


def _gmlp_pieces(x_ref, nmix_ref, w_t_ref, lng_ref, lnb_ref, mixw_ref, mixb_ref, ygm_ref, vgn_ref, st):
    tb = x_ref.shape[0]

    def norm():
        st["a"] = _rms(x_ref[...], nmix_ref[...]).astype(BF16)

    def u_mm():
        st["u"] = _dot_nt(st["a"], w_t_ref[:GM_WIDTH])

    def u_act():
        st["u"] = jax.nn.gelu(st["u"])

    def v_mm():
        st["vg"] = _dot_nt(st["a"], w_t_ref[GM_WIDTH:2 * GM_WIDTH])

    def v_act():
        vg = jax.nn.gelu(st.pop("vg"))
        mu = jnp.mean(vg, axis=-1, keepdims=True)
        var = jnp.mean(jnp.square(vg - mu), axis=-1, keepdims=True)
        vgn = (vg - mu) * lax.rsqrt(var + EPS) * lng_ref[...] + lnb_ref[...]
        vgn_ref[...] = vgn[tb - vgn_ref.shape[0]:, :]
        st["vgb"] = vgn.astype(BF16)

    def mix(c):
        rows = slice(c * CHUNK, (c + 1) * CHUNK)
        for h in range(HEADS):
            cols = slice(h * HEAD_DIM, (h + 1) * HEAD_DIM)
            s = _dot(mixw_ref[h], st["vgb"][rows, cols]) + mixb_ref[h]
            ygm_ref[rows, cols] = (st["u"][rows, cols] * s).astype(BF16)

    return [norm, u_mm, u_act, v_mm, v_act] + [functools.partial(mix, c) for c in range(tb // CHUNK)]


def _mixer_prompt_kernel(steps_per_seq, n_cast, x_ref, nmix_ref, w_t_ref, wgr_ref, bgr_ref,
                         lng_ref, lnb_ref, mixw_ref, mixb_ref, gain_t_ref, triu_ref, eye_ref, *rest):
    cast_in, rest = rest[:n_cast], rest[n_cast:]
    ygm_ref, yml_t_ref, vlast_ref, cn_out_ref, m_out_ref = rest[:5]
    cast_out, (cn_scr, m_scr) = rest[5:5 + n_cast], rest[5 + n_cast:]
    for src, dst in zip(cast_in, cast_out):
        dst[...] = src[...].astype(BF16)

    tb = x_ref.shape[0]
    n_chunks = tb // CHUNK
    blocks = [slice(c * CHUNK, (c + 1) * CHUNK) for c in range(n_chunks)]
    heads = [slice(h * HEAD_DIM, (h + 1) * HEAD_DIM) for h in range(HEADS)]
    unit_ids = [(c, h) for c in range(n_chunks) for h in range(HEADS)]
    new_seq = pl.program_id(0) % steps_per_seq == 0
    triu = triu_ref[...]
    mask_t = triu.astype(F32) > 0.0
    eye = eye_ref[...]
    lane_g = lax.broadcasted_iota(jnp.int32, (GATE_ROWS, tb), 1) % CHUNK
    sub_g = lax.broadcasted_iota(jnp.int32, (GATE_ROWS, CHUNK), 0)
    ones_row = (sub_g == 0).astype(BF16)
    last = slice(CHUNK - 1, CHUNK)

    st = {}
    norm, u_mm, u_act, v_mm, v_act, *mix = _gmlp_pieces(x_ref, nmix_ref, w_t_ref, lng_ref, lnb_ref, mixw_ref,
                                                        mixb_ref, ygm_ref, vlast_ref, st)
    zero = jnp.zeros((HEAD_DIM, HEAD_DIM), BF16)

    def block_diag(x, y):
        return jnp.concatenate([jnp.concatenate([x, zero], axis=1), jnp.concatenate([zero, y], axis=1)], axis=0)

    def pair_dot(lhs, rhs):
        out = _dot(jnp.concatenate(lhs, axis=1), block_diag(*rhs))
        return out[:, :HEAD_DIM], out[:, HEAD_DIM:]

    norm()
    a = st["a"]
    k = (_dot_nt(a, w_t_ref[K_LO:V_LO]) * (HEAD_DIM ** -0.5)).astype(BF16)
    q_t = _dot_nt(w_t_ref[Q_LO:K_LO], a).astype(BF16)
    zr = _dot_nt(wgr_ref[...], a) + bgr_ref[...]
    v_t = _dot_nt(w_t_ref[V_LO:O_LO], a)
    lf_r = _log_sigmoid(zr[GATE_ROWS:])
    b_r = jnp.concatenate([_dot_sel(lf_r[:, t], triu) for t in blocks], axis=1)
    u_mm()
    r_r = zr[:GATE_ROWS] - b_r
    p_r = r_r
    shift = 1
    while shift < CHUNK:
        p_r = jnp.maximum(p_r, jnp.where(lane_g >= shift, pltpu.roll(p_r, shift, axis=1), -jnp.inf))
        shift *= 2
    e_end = [jnp.exp(r_r[:, t] - p_r[:, t][:, last]) for t in blocks]
    r_c = [_sel_dot_nt(eye, r_r[:, t]) for t in blocks]
    v_mm()
    u_act()
    og_t = jax.nn.sigmoid(_dot_nt(w_t_ref[O_LO:MAIN_COLS], a)) * gain_t_ref[...]

    pair_ids = [(c, h) for c in range(n_chunks) for h in range(0, HEADS, 2)]
    kq, e_intra, x1, upd = {}, {}, {}, {}
    for c, h in unit_ids:
        e_intra[c, h] = jnp.where(mask_t, jnp.exp(r_c[c][:, h:h + 1] - p_r[h:h + 1, blocks[c]]), 0.0)
    for c, h in pair_ids:
        t = blocks[c]
        kq[c, h], kq[c, h + 1] = pair_dot([k[t, heads[h]], k[t, heads[h + 1]]],
                                          [q_t[heads[h], t], q_t[heads[h + 1], t]])
    v_act()
    for piece in mix:
        piece()
    for c, h in pair_ids:
        t = blocks[c]
        s0, vext, vw = [], [], []
        for g in (h, h + 1):
            s0.append((kq[c, g] * e_intra[c, g]).astype(BF16))
            vext.append(jnp.concatenate([v_t[heads[g], t].astype(BF16), ones_row], axis=0))
            e_row = e_end[c][g:g + 1, :]
            vw.append(jnp.concatenate([(v_t[heads[g], t] * e_row).astype(BF16),
                                       jnp.where(sub_g == 0, e_row, 0.0).astype(BF16)], axis=0))
        x1[c, h], x1[c, h + 1] = pair_dot(vext, s0)
        upd[c, h], upd[c, h + 1] = pair_dot(vw, [k[t, heads[h]], k[t, heads[h + 1]]])

    cn_in, cm, w_inter = {}, {}, {}
    for h in range(HEADS):
        cn = jnp.where(new_seq, 0.0, cn_scr[h])
        m_prev = jnp.where(new_seq, 0.0, m_scr[h:h + 1, 0:1])
        for c in range(n_chunks):
            prow = p_r[h:h + 1, blocks[c]]
            cn_in[c, h] = cn
            cm[c, h] = jnp.maximum(m_prev, prow)
            w_inter[c, h] = jnp.exp(m_prev - cm[c, h])
            cm_last = cm[c, h][:, last]
            cn = jnp.exp(m_prev - cm_last) * cn + jnp.exp(prow[:, last] - cm_last) * upd[c, h]
            m_prev = b_r[h:h + 1, blocks[c]][:, last] + cm_last
        cn_scr[h] = cn
        m_scr[h:h + 1, :] = jnp.broadcast_to(m_prev, (1, HEAD_DIM))
    x2 = {}

    def state_matmuls(c):
        t = blocks[c]
        for h in range(0, HEADS, 2):
            x2[c, h], x2[c, h + 1] = pair_dot([cn_in[c, h].astype(BF16), cn_in[c, h + 1].astype(BF16)],
                                              [q_t[heads[h], t], q_t[heads[h + 1], t]])

    def head_outputs(c):
        t = blocks[c]
        for h, hd in enumerate(heads):
            brow, prow = b_r[h:h + 1, t], p_r[h:h + 1, t]
            nd = x1[c, h] * jnp.exp(prow - cm[c, h]) + x2[c, h] * w_inter[c, h]
            num_t, den = nd[:HEAD_DIM], nd[HEAD_DIM:HEAD_DIM + 1]
            inv = 1.0 / jnp.maximum(jnp.abs(den), jnp.exp(-(brow + cm[c, h])))
            ssq = jnp.sum(num_t * num_t, axis=0, keepdims=True)
            scale = inv * lax.rsqrt(ssq * (inv * inv) * (1.0 / HEAD_DIM) + EPS)
            yml_t_ref[hd, t] = (num_t * scale * og_t[hd, t]).astype(BF16)

    for c in range(n_chunks):
        state_matmuls(c)
    for c in range(n_chunks):
        head_outputs(c)

    @pl.when(pl.program_id(0) % steps_per_seq == steps_per_seq - 1)
    def _():
        cn_out_ref[...] = cn_scr[...]
        m_out_ref[...] = m_scr[...]


def _mixer_prompt(x2d, nmix, w_t, wgr, bgr, lng, lnb, mixw, mixb, gain_t, triu, eye, cast, batch, tb):
    n = x2d.shape[0]
    n_steps = n // tb
    steps_per_seq = n_steps // batch

    def window(w):
        rows = max(BF16_ROWS, w.shape[0] // n_steps)
        return pl.BlockSpec((rows, w.shape[1]), lambda i: (jnp.minimum(i, w.shape[0] // rows - 1), 0))

    tok = lambda w: pl.BlockSpec((tb, w), lambda i: (i, 0))
    per_seq = lambda *shape: pl.BlockSpec((None,) + shape, lambda i: (i // steps_per_seq,) + (0,) * len(shape))
    return pl.pallas_call(
        functools.partial(_mixer_prompt_kernel, steps_per_seq, len(cast)),
        grid=(n_steps,),
        in_specs=[
            tok(D_MODEL),
            _const_spec((1, D_MODEL)),
            _const_spec((IN_COLS, D_MODEL)),
            _const_spec((2 * GATE_ROWS, D_MODEL)),
            _const_spec((2 * GATE_ROWS, tb)),
            _const_spec((1, GM_WIDTH)),
            _const_spec((1, GM_WIDTH)),
            _const_spec((HEADS, CHUNK, CHUNK)),
            _const_spec((HEADS, CHUNK, CHUNK)),
            _const_spec((ML_WIDTH, tb)),
            _const_spec((CHUNK, CHUNK)),
            _const_spec((CHUNK, CHUNK)),
        ] + [window(w) for w in cast],
        out_specs=(tok(GM_WIDTH), pl.BlockSpec((ML_WIDTH, tb), lambda i: (0, i)), per_seq(CHUNK, GM_WIDTH),
                   per_seq(HEADS, STATE_ROWS, HEAD_DIM), per_seq(HEADS, HEAD_DIM)) + tuple(window(w) for w in cast),
        out_shape=(
            jax.ShapeDtypeStruct((n, GM_WIDTH), BF16),
            jax.ShapeDtypeStruct((ML_WIDTH, n), BF16),
            jax.ShapeDtypeStruct((batch, CHUNK, GM_WIDTH), F32),
            jax.ShapeDtypeStruct((batch, HEADS, STATE_ROWS, HEAD_DIM), F32),
            jax.ShapeDtypeStruct((batch, HEADS, HEAD_DIM), F32),
        ) + tuple(jax.ShapeDtypeStruct(w.shape, BF16) for w in cast),
        scratch_shapes=[pltpu.VMEM((HEADS, STATE_ROWS, HEAD_DIM), F32), pltpu.VMEM((HEADS, HEAD_DIM), F32)],
        compiler_params=pltpu.CompilerParams(dimension_semantics=("arbitrary",),
                                             vmem_limit_bytes=VMEM_LIMIT),
        name="mixer_prompt",
    )(x2d, nmix, w_t, wgr, bgr, lng, lnb, mixw, mixb, gain_t, triu, eye, *cast)


def _front_kernel(x_ref, nmix_ref, w_t_ref, wgate_ref, wgate_t_ref, bcol_ref, brow_ref,
                  lng_ref, lnb_ref, mixw_ref, mixb_ref,
                  ygm_ref, q_ref, k_ref, v_ref, og_ref, gcol_ref, grow_ref, vgn_ref):
    st = {}
    for piece in _gmlp_pieces(x_ref, nmix_ref, w_t_ref, lng_ref, lnb_ref, mixw_ref, mixb_ref,
                              ygm_ref, vgn_ref, st):
        piece()
    a = st["a"]

    def proj(lo):
        return _dot_nt(a, w_t_ref[lo:lo + ML_WIDTH])

    q_ref[...] = proj(Q_LO).astype(BF16)
    k_ref[...] = (proj(K_LO) * (HEAD_DIM ** -0.5)).astype(BF16)
    v_ref[...] = proj(V_LO).astype(BF16)
    og_ref[...] = jax.nn.sigmoid(proj(O_LO))
    zc = _dot(a, wgate_ref[...]) + bcol_ref[...]
    lane = lax.broadcasted_iota(jnp.int32, zc.shape, 1)
    gcol_ref[...] = jnp.where(lane >= HEADS, _log_sigmoid(zc), zc)
    zr = _dot_nt(wgate_t_ref[...], a) + brow_ref[...]
    sub = lax.broadcasted_iota(jnp.int32, zr.shape, 0)
    grow_ref[...] = jnp.where(sub >= HEADS, _log_sigmoid(zr), zr)


def _front(x2d, nmix, w_t, wgate, wgate_t, bcol, brow, lng, lnb, mixw, mixb, tb):
    n = x2d.shape[0]
    tok = lambda w: pl.BlockSpec((tb, w), lambda i: (i, 0))
    out_shape = (
        jax.ShapeDtypeStruct((n, GM_WIDTH), BF16),
        jax.ShapeDtypeStruct((n, ML_WIDTH), BF16),
        jax.ShapeDtypeStruct((n, ML_WIDTH), BF16),
        jax.ShapeDtypeStruct((n, ML_WIDTH), BF16),
        jax.ShapeDtypeStruct((n, ML_WIDTH), F32),
        jax.ShapeDtypeStruct((n, GATE_LANES), F32),
        jax.ShapeDtypeStruct((GATE_ROWS, n), F32),
        jax.ShapeDtypeStruct((n, GM_WIDTH), F32),
    )
    return pl.pallas_call(
        _front_kernel,
        grid=(n // tb,),
        in_specs=[
            tok(D_MODEL),
            _const_spec((1, D_MODEL)),
            _const_spec((IN_COLS, D_MODEL)),
            _const_spec((D_MODEL, GATE_LANES)),
            _const_spec((GATE_ROWS, D_MODEL)),
            _const_spec((1, GATE_LANES)),
            _const_spec((GATE_ROWS, tb)),
            _const_spec((1, GM_WIDTH)),
            _const_spec((1, GM_WIDTH)),
            _const_spec((HEADS, CHUNK, CHUNK)),
            _const_spec((HEADS, CHUNK, CHUNK)),
        ],
        out_specs=(tok(GM_WIDTH), tok(ML_WIDTH), tok(ML_WIDTH), tok(ML_WIDTH), tok(ML_WIDTH),
                   tok(GATE_LANES), pl.BlockSpec((GATE_ROWS, tb), lambda i: (0, i)), tok(GM_WIDTH)),
        out_shape=out_shape,
        compiler_params=pltpu.CompilerParams(dimension_semantics=("arbitrary",),
                                             vmem_limit_bytes=VMEM_LIMIT),
        name="front",
    )(x2d, nmix, w_t, wgate, wgate_t, bcol, brow, lng, lnb, mixw, mixb)


def _intra(q, ks, igcol, bcol, igrow, brow, mprev, mask):
    d = bcol + (igrow - brow)
    g = bcol + mprev
    m_t = jnp.maximum(g, jnp.max(jnp.where(mask, d, -jnp.inf), axis=-1, keepdims=True))
    w_intra = jnp.where(mask, jnp.exp(d - m_t), 0.0)
    w_inter = jnp.exp(g - m_t)
    s = _dot_nt(q, ks) * w_intra
    return s, w_inter, m_t, g


def _head_out(num, den, m_t, gain, og):
    hh = num / jnp.maximum(jnp.abs(den), jnp.exp(-m_t))
    return (og * _rms(hh, gain)).astype(BF16)


def _mlstm_sample_kernel(seq_len, q_ref, k_ref, v_ref, og_ref, gcol_ref, grow_ref, mrep_ref, c_ref, n_ref,
                         mln_ref, tri_ref, triu_ref, sel_last_ref, expand_ref, pick_last_ref, seq_sum_ref,
                         yml_ref, c_out_ref, n_out_ref, m_out_ref):
    n_seq = CHUNK // seq_len
    tri = tri_ref[...]
    mask = tri.astype(F32) > 0.0
    gcol = gcol_ref[...]
    grow = grow_ref[...]
    bcol_all = _sel_dot(tri, gcol)
    brow_all = _dot_sel(grow, triu_ref[...])
    sel_last = sel_last_ref[...]
    expand = expand_ref[...]
    pick_last = pick_last_ref[...]
    seq_sum = seq_sum_ref[...]
    row = lax.broadcasted_iota(jnp.int32, (CHUNK, HEAD_DIM), 0)
    lane = lax.broadcasted_iota(jnp.int32, (CHUNK, HEAD_DIM), 1)
    seq_rows = [(row >= i * seq_len) & (row < (i + 1) * seq_len) for i in range(n_seq)]
    head_cols = [slice(h * HEAD_DIM, (h + 1) * HEAD_DIM) for h in range(HEADS)]
    every_head = range(HEADS)
    q = [q_ref[:, cols] for cols in head_cols]
    ks = [k_ref[:, cols] for cols in head_cols]
    v = [v_ref[:, cols] for cols in head_cols]
    zero = jnp.zeros_like(q[0])
    igcol = [gcol[:, h:h + 1] for h in every_head]
    bcol = [bcol_all[:, HEADS + h:HEADS + h + 1] for h in every_head]
    intra = [_intra(q[h], ks[h], igcol[h], bcol[h], grow[h:h + 1, :], brow_all[HEADS + h:HEADS + h + 1, :],
                    mrep_ref[:, h:h + 1], mask) for h in every_head]
    s, w_inter, m_t, g = zip(*intra)
    c_all = [c_ref[:, h] for h in every_head]
    n_all = [n_ref[:, h] for h in every_head]
    qc = [_dot(jnp.concatenate([jnp.where(m, q[h], zero) for m in seq_rows], axis=1),
               c_all[h].reshape(n_seq * HEAD_DIM, HEAD_DIM).astype(BF16)) for h in every_head]
    n_rows = [_sel_dot(expand, n_all[h]) for h in every_head]
    ends = [_sel_dot(sel_last, jnp.where(lane == 0, m_t[h], jnp.where(lane == 1, g[h],
                                                                        jnp.where(lane == 2, bcol[h], 0.0))))
            for h in every_head]
    sv = [_dot(s[h].astype(BF16), v[h]) for h in every_head]
    kw, dec, upd, dec_seq, n_inc, m_seq = [], [], [], [], [], []
    for h in every_head:
        m_new, g_last, b_last = ends[h][:, 0:1], ends[h][:, 1:2], ends[h][:, 2:3]
        w_end = jnp.exp(b_last - bcol[h] + igcol[h] - m_new)
        dec.append(jnp.exp(g_last - m_new))
        kw.append(ks[h].astype(F32) * w_end)
    for h in every_head:
        v_exp = jnp.concatenate([jnp.where(m, v[h], zero) for m in seq_rows], axis=1)
        upd.append(_dot(kw[h].T.astype(BF16), v_exp))
        dec_seq.append(_sel_dot(pick_last, jnp.broadcast_to(dec[h], (CHUNK, HEAD_DIM))))
        n_inc.append(_sel_dot(seq_sum, kw[h]))
        m_seq.append(_sel_dot(pick_last, jnp.broadcast_to(m_t[h], (CHUNK, HEAD_DIM))))
    for h, cols in enumerate(head_cols):
        qn = jnp.sum(q[h].astype(F32) * n_rows[h], axis=-1, keepdims=True)
        num = sv[h] + w_inter[h] * qc[h]
        den = jnp.sum(s[h], axis=-1, keepdims=True) + w_inter[h] * qn
        yml_ref[:, cols] = _head_out(num, den, m_t[h], mln_ref[:, cols], og_ref[:, cols])
        for i in range(n_seq):
            c_out_ref[i, h] = (dec_seq[h][i:i + 1, 0:1] * c_all[h][i]
                               + upd[h][:, i * HEAD_DIM:(i + 1) * HEAD_DIM])
        n_out_ref[:, h] = dec_seq[h] * n_all[h] + n_inc[h]
        m_out_ref[:, h] = m_seq[h]


def _mlstm_sample(q, k, v, og, gcol, grow, mrep, c0, n0, mln, tri, triu, sel_last, expand, pick_last,
                  seq_sum, seq_len):
    n = q.shape[0]
    n_seq = CHUNK // seq_len
    n_batch = n // seq_len
    tok = lambda w: pl.BlockSpec((CHUNK, w), lambda i: (i, 0))
    c_spec = pl.BlockSpec((n_seq, HEADS, HEAD_DIM, HEAD_DIM), lambda i: (i, 0, 0, 0))
    n_spec = pl.BlockSpec((n_seq, HEADS, HEAD_DIM), lambda i: (i, 0, 0))
    return pl.pallas_call(
        functools.partial(_mlstm_sample_kernel, seq_len),
        grid=(n // CHUNK,),
        in_specs=[
            tok(ML_WIDTH), tok(ML_WIDTH), tok(ML_WIDTH), tok(ML_WIDTH), tok(GATE_LANES),
            pl.BlockSpec((GATE_ROWS, CHUNK), lambda i: (0, i)),
            pl.BlockSpec((CHUNK, HEADS), lambda i: (i, 0)),
            c_spec, n_spec,
            _const_spec((1, ML_WIDTH)),
            _const_spec((CHUNK, CHUNK)), _const_spec((CHUNK, CHUNK)), _const_spec((CHUNK, CHUNK)),
            _const_spec((CHUNK, n_seq)), _const_spec((n_seq, CHUNK)), _const_spec((n_seq, CHUNK)),
        ],
        out_specs=(tok(ML_WIDTH), c_spec, n_spec, n_spec),
        out_shape=(
            jax.ShapeDtypeStruct((n, ML_WIDTH), BF16),
            jax.ShapeDtypeStruct((n_batch, HEADS, HEAD_DIM, HEAD_DIM), F32),
            jax.ShapeDtypeStruct((n_batch, HEADS, HEAD_DIM), F32),
            jax.ShapeDtypeStruct((n_batch, HEADS, HEAD_DIM), F32),
        ),
        compiler_params=pltpu.CompilerParams(dimension_semantics=("arbitrary",),
                                             vmem_limit_bytes=VMEM_LIMIT),
        name="mlstm_sample",
    )(q, k, v, og, gcol, grow, mrep, c0, n0, mln, tri, triu, sel_last, expand, pick_last, seq_sum)


def _back_math(final_norm, yml_transposed, x_ref, ygm_ref, yml_ref, pe_ref, wout_ref, nffn_ref,
               wup_ref, wdown_ref, nple_ref, wpg_ref, wpp_ref, nfin_ref, out_ref):
    tb = x_ref.shape[0]
    halves = (slice(0, tb // 2), slice(tb // 2, tb))
    ff = D_FF // FF_SPLIT
    st = [{} for _ in halves]

    def proj(j):
        r = halves[j]
        ml = (_dot_tn(yml_ref[:, r], wout_ref[GM_WIDTH:]) if yml_transposed
              else _dot(yml_ref[r, :], wout_ref[GM_WIDTH:]))
        st[j]["h"] = x_ref[r, :] + _dot(ygm_ref[r, :], wout_ref[:GM_WIDTH]) + ml

    def norm(j):
        st[j]["a"] = _rms(st[j]["h"], nffn_ref[...]).astype(BF16)

    def mlp(j, c):
        f = _dot(st[j]["a"], wup_ref[:, c * ff:(c + 1) * ff])
        f = jnp.square(jnp.maximum(f, 0.0)).astype(BF16)
        part = _dot(f, wdown_ref[c * ff:(c + 1) * ff, :])
        st[j]["mlp"] = part if c == 0 else st[j]["mlp"] + part

    def embed(j):
        h = st[j]["h"] + st[j]["mlp"]
        st[j]["h"] = h
        st[j]["gate"] = _dot(_rms(h, nple_ref[...]).astype(BF16), wpg_ref[...])
        st[j]["pe"] = _dot(pe_ref[halves[j], :].astype(BF16), wpp_ref[...])

    def finish(j):
        h = st[j]["h"] + jax.nn.sigmoid(st[j]["gate"]) * st[j]["pe"]
        if final_norm:
            h = _rms(h, nfin_ref[...])
        out_ref[halves[j], :] = h

    proj(0)
    proj(1)
    norm(0)
    mlp(0, 0)
    norm(1)
    mlp(1, 0)
    for c in range(1, FF_SPLIT):
        mlp(0, c)
        mlp(1, c)
    embed(0)
    embed(1)
    finish(0)
    finish(1)


def _back_kernel(final_norm, yml_transposed, *refs):
    _back_math(final_norm, yml_transposed, *refs)


def _back(x2d, ygm, yml, pe2d, wout, nffn, wup, wdown, nple, wpg, wpp, nfin, final_norm, tb):
    n = x2d.shape[0]
    yml_transposed = yml.shape[0] != n
    tok = lambda w: pl.BlockSpec((tb, w), lambda i: (i, 0))
    yml_spec = pl.BlockSpec((ML_WIDTH, tb), lambda i: (0, i)) if yml_transposed else tok(ML_WIDTH)
    return pl.pallas_call(
        functools.partial(_back_kernel, final_norm, yml_transposed),
        grid=(n // tb,),
        in_specs=[
            tok(D_MODEL), tok(GM_WIDTH), yml_spec, tok(PLE_DIM),
            _const_spec((GM_WIDTH + ML_WIDTH, D_MODEL)),
            _const_spec((1, D_MODEL)),
            _const_spec((D_MODEL, D_FF)), _const_spec((D_FF, D_MODEL)),
            _const_spec((1, D_MODEL)),
            _const_spec((D_MODEL, D_MODEL)), _const_spec((PLE_DIM, D_MODEL)),
            _const_spec((1, D_MODEL)),
        ],
        out_specs=tok(D_MODEL),
        out_shape=jax.ShapeDtypeStruct((n, D_MODEL), F32),
        compiler_params=pltpu.CompilerParams(dimension_semantics=("arbitrary",),
                                             vmem_limit_bytes=VMEM_LIMIT),
        name="back",
    )(x2d, ygm, yml, pe2d, wout, nffn, wup, wdown, nple, wpg, wpp, nfin)


def _const(x):
    return jnp.asarray(np.asarray(x, np.float32), BF16)


def _block_tri(block):
    r = np.arange(CHUNK)[:, None]
    c = np.arange(CHUNK)[None, :]
    return (r // block == c // block) & (c <= r)


def _token_block(n):
    return 512 if n % 512 == 0 else CHUNK


def kernel(x_prompt, x_sample, p_prompt, p_sample, state_C, state_n, state_m, norm_mix, w_in, gm_ln_g,
           gm_ln_b, gm_ws, gm_bs, ml_b_i, ml_b_f, ml_norm, w_out, norm_ffn, w_up, w_down, norm_ple,
           w_ple_gate, w_ple_proj, norm_final):
    depth = w_in.shape[0]
    batch, seq, _ = x_prompt.shape
    dec_batch, dec_seq, _ = x_sample.shape
    n_p, n_s = batch * seq, dec_batch * dec_seq
    n_seq = CHUNK // dec_seq
    assert seq % CHUNK == 0 and CHUNK % dec_seq == 0 and n_s % CHUNK == 0
    tb_p, tb_s = _token_block(seq), _token_block(n_s)

    hp = x_prompt.reshape(n_p, D_MODEL)
    hs = x_sample.reshape(n_s, D_MODEL)

    tri_p, tri_s = _block_tri(CHUNK), _block_tri(dec_seq)
    triu_p, eye = _const(tri_p.T), _const(np.eye(CHUNK))
    r = np.arange(CHUNK)
    i = np.arange(n_seq)
    sel_last = _const(r[None, :] == (r[:, None] // dec_seq) * dec_seq + dec_seq - 1)
    expand = _const(r[:, None] // dec_seq == i[None, :])
    pick_last = _const(r[None, :] == i[:, None] * dec_seq + dec_seq - 1)
    seq_sum = _const(r[None, :] // dec_seq == i[:, None])

    outs = {k: [] for k in ("Cp", "Np", "Mp", "Vp", "Cs", "Ns", "Ms", "Vs")}
    for l in range(depth):
        row = lambda a: a[l].reshape(1, -1).astype(F32)
        w_t = w_in[l].T.astype(BF16)
        wg_t = w_t[MAIN_COLS:]
        b_i, b_f = ml_b_i[l].astype(F32), ml_b_f[l].astype(F32)
        row_pad = lambda a: jnp.pad(a, ((0, GATE_ROWS - HEADS), (0, 0)))
        wgr = jnp.concatenate([row_pad(wg_t[:HEADS]), row_pad(wg_t[HEADS:])], axis=0)
        bgr = jnp.broadcast_to(jnp.concatenate([row_pad(b_i[:, None]), row_pad(b_f[:, None])], axis=0),
                               (2 * GATE_ROWS, tb_p))
        gain_t = jnp.broadcast_to(ml_norm[l].astype(F32)[:, None], (ML_WIDTH, tb_p))
        wgate = jnp.pad(wg_t.T, ((0, 0), (0, GATE_LANES - 2 * HEADS)))
        wgate_t = jnp.pad(wg_t, ((0, GATE_ROWS - 2 * HEADS), (0, 0)))
        gbias = jnp.concatenate([b_i, b_f])
        bcol = jnp.pad(gbias, (0, GATE_LANES - 2 * HEADS)).reshape(1, GATE_LANES)
        brow = jnp.broadcast_to(jnp.pad(gbias, (0, GATE_ROWS - 2 * HEADS)).reshape(GATE_ROWS, 1),
                                (GATE_ROWS, tb_s))
        ws = gm_ws[l]
        mixw_p = (ws[:, :CHUNK, :CHUNK] * tri_p.astype(np.float32)).astype(BF16)
        mixb_p = jnp.broadcast_to(gm_bs[l][:, :CHUNK, None], (HEADS, CHUNK, CHUNK)).astype(F32)
        mixw_s = (jnp.tile(ws[:, :dec_seq, :dec_seq], (1, n_seq, n_seq)) * tri_s.astype(np.float32)).astype(BF16)
        mixb_s = jnp.broadcast_to(jnp.tile(gm_bs[l][:, :dec_seq], (1, n_seq))[:, :, None],
                                  (HEADS, CHUNK, CHUNK)).astype(F32)
        last = l == depth - 1
        nfin = norm_final.reshape(1, D_MODEL).astype(F32)

        back_f32 = [w[l].astype(F32) for w in (w_out, w_up, w_down, w_ple_gate, w_ple_proj)]
        ygm, yml_t, vlast, cn_t, m8, wout, wup, wdown, wpg, wpp = _mixer_prompt(
            hp, row(norm_mix), w_t, wgr, bgr, row(gm_ln_g), row(gm_ln_b), mixw_p, mixb_p,
            gain_t, triu_p, eye, back_f32, batch, tb_p)

        def back(x2d, ygm, yml, pe, tb):
            return _back(x2d, ygm, yml, pe, wout, row(norm_ffn), wup, wdown, row(norm_ple),
                         wpg, wpp, nfin, last, tb)

        hp = back(hp, ygm, yml_t, p_prompt[l].reshape(n_p, PLE_DIM), 2 * tb_p)
        outs["Cp"].append(jnp.swapaxes(cn_t[:, :, :HEAD_DIM, :], -1, -2))
        outs["Np"].append(cn_t[:, :, HEAD_DIM, :])
        outs["Mp"].append(m8[:, :, 0])
        outs["Vp"].append(vlast)

        ygm, q, k, v, og, gcol, grow, vgn = _front(
            hs, row(norm_mix), w_t, wgate, wgate_t, bcol, brow, row(gm_ln_g), row(gm_ln_b), mixw_s, mixb_s,
            tb_s)
        mrep = jnp.repeat(state_m[l].astype(F32), dec_seq, axis=0)
        yml, c_new, n_new, m_new = _mlstm_sample(
            q, k, v, og, gcol, grow, mrep, state_C[l].astype(F32), state_n[l].astype(F32), row(ml_norm),
            _const(tri_s), _const(tri_s.T), sel_last, expand, pick_last, seq_sum, dec_seq)
        hs = back(hs, ygm, yml, p_sample[l].reshape(n_s, PLE_DIM), tb_s)
        outs["Cs"].append(c_new)
        outs["Ns"].append(n_new)
        outs["Ms"].append(m_new[..., 0])
        outs["Vs"].append(vgn.reshape(dec_batch, dec_seq, GM_WIDTH))

    st = lambda k: outs[k][0][None] if depth == 1 else jnp.stack(outs[k])
    return (hp.reshape(batch, seq, D_MODEL), hs.reshape(dec_batch, dec_seq, D_MODEL),
            st("Cp"), st("Np"), st("Mp"), st("Vp"), st("Cs"), st("Ns"), st("Ms"), st("Vs"))
```

```python
import functools

import jax
import numpy as np
import jax.numpy as jnp
from jax import lax
from jax.experimental import pallas as pl
from jax.experimental.pallas import tpu as pltpu

F32 = jnp.float32
BF16 = jnp.bfloat16

D_MODEL = 1024
GM_WIDTH = 512
ML_WIDTH = 512
HEADS = 4
HEAD_DIM = 128
D_FF = 4096
PLE_DIM = 256
EPS = 1e-6
CHUNK = 128
MAIN_COLS = 2 * GM_WIDTH + 4 * ML_WIDTH
Q_LO = 2 * GM_WIDTH
K_LO = Q_LO + ML_WIDTH
V_LO = K_LO + ML_WIDTH
O_LO = V_LO + ML_WIDTH
IN_COLS = MAIN_COLS + 2 * HEADS
GATE_LANES = 128
GATE_ROWS = 16
STATE_ROWS = HEAD_DIM + 16
FF_SPLIT = 4
BF16_ROWS = 16
VMEM_LIMIT = 60 * 1024 * 1024


def _dot(a, b):
    return jnp.dot(a, b, preferred_element_type=F32)


def _dot_nt(a, b):
    return lax.dot_general(a, b, (((1,), (1,)), ((), ())), preferred_element_type=F32)


def _dot_tn(a, b):
    return lax.dot_general(a, b, (((0,), (0,)), ((), ())), preferred_element_type=F32)


def _split3(x):
    x1 = x.astype(BF16)
    r = x - x1.astype(F32)
    x2 = r.astype(BF16)
    r = r - x2.astype(F32)
    return x1, x2, r.astype(BF16)


def _sel_dot(sel, x):
    p1, p2, p3 = _split3(x)
    return _dot(sel, p1) + _dot(sel, p2) + _dot(sel, p3)


def _dot_sel(x, sel):
    p1, p2, p3 = _split3(x)
    return _dot(p1, sel) + _dot(p2, sel) + _dot(p3, sel)


def _sel_dot_nt(sel, x):
    p1, p2, p3 = _split3(x)
    return _dot_nt(sel, p1) + _dot_nt(sel, p2) + _dot_nt(sel, p3)


def _rms(x, g):
    return x * lax.rsqrt(jnp.mean(x * x, axis=-1, keepdims=True) + EPS) * g


def _log_sigmoid(x):
    return -(jnp.maximum(-x, 0.0) + jnp.log1p(jnp.exp(-jnp.abs(x))))


def _const_spec(shape):
    nd = len(shape)
    return pl.BlockSpec(shape, lambda *_: (0,) * nd, pipeline_mode=pl.Buffered(1))


def _gmlp_pieces(x_ref, nmix_ref, w_t_ref, lng_ref, lnb_ref, mixw_ref, mixb_ref, ygm_ref, vgn_ref, st):
    tb = x_ref.shape[0]

    def norm():
        st["a"] = _rms(x_ref[...], nmix_ref[...]).astype(BF16)

    def u_mm():
        st["u"] = _dot_nt(st["a"], w_t_ref[:GM_WIDTH])

    def u_act():
        st["u"] = jax.nn.gelu(st["u"])

    def v_mm():
        st["vg"] = _dot_nt(st["a"], w_t_ref[GM_WIDTH:2 * GM_WIDTH])

    def v_act():
        vg = jax.nn.gelu(st.pop("vg"))
        mu = jnp.mean(vg, axis=-1, keepdims=True)
        var = jnp.mean(jnp.square(vg - mu), axis=-1, keepdims=True)
        vgn = (vg - mu) * lax.rsqrt(var + EPS) * lng_ref[...] + lnb_ref[...]
        vgn_ref[...] = vgn[tb - vgn_ref.shape[0]:, :]
        st["vgb"] = vgn.astype(BF16)

    def mix(c):
        rows = slice(c * CHUNK, (c + 1) * CHUNK)
        for h in range(HEADS):
            cols = slice(h * HEAD_DIM, (h + 1) * HEAD_DIM)
            s = _dot(mixw_ref[h], st["vgb"][rows, cols]) + mixb_ref[h]
            ygm_ref[rows, cols] = (st["u"][rows, cols] * s).astype(BF16)

    return [norm, u_mm, u_act, v_mm, v_act] + [functools.partial(mix, c) for c in range(tb // CHUNK)]


def _mixer_prompt_kernel(steps_per_seq, n_cast, x_ref, nmix_ref, w_t_ref, wgr_ref, bgr_ref,
                         lng_ref, lnb_ref, mixw_ref, mixb_ref, gain_t_ref, triu_ref, eye_ref, *rest):
    cast_in, rest = rest[:n_cast], rest[n_cast:]
    ygm_ref, yml_t_ref, vlast_ref, cn_out_ref, m_out_ref = rest[:5]
    cast_out, (cn_scr, m_scr) = rest[5:5 + n_cast], rest[5 + n_cast:]
    for src, dst in zip(cast_in, cast_out):
        dst[...] = src[...].astype(BF16)

    tb = x_ref.shape[0]
    n_chunks = tb // CHUNK
    blocks = [slice(c * CHUNK, (c + 1) * CHUNK) for c in range(n_chunks)]
    heads = [slice(h * HEAD_DIM, (h + 1) * HEAD_DIM) for h in range(HEADS)]
    unit_ids = [(c, h) for c in range(n_chunks) for h in range(HEADS)]
    new_seq = pl.program_id(0) % steps_per_seq == 0
    triu = triu_ref[...]
    mask_t = triu.astype(F32) > 0.0
    eye = eye_ref[...]
    lane_g = lax.broadcasted_iota(jnp.int32, (GATE_ROWS, tb), 1) % CHUNK
    sub_g = lax.broadcasted_iota(jnp.int32, (GATE_ROWS, CHUNK), 0)
    ones_row = (sub_g == 0).astype(BF16)
    last = slice(CHUNK - 1, CHUNK)

    st = {}
    norm, u_mm, u_act, v_mm, v_act, *mix = _gmlp_pieces(x_ref, nmix_ref, w_t_ref, lng_ref, lnb_ref, mixw_ref,
                                                        mixb_ref, ygm_ref, vlast_ref, st)
    zero = jnp.zeros((HEAD_DIM, HEAD_DIM), BF16)

    def block_diag(x, y):
        return jnp.concatenate([jnp.concatenate([x, zero], axis=1), jnp.concatenate([zero, y], axis=1)], axis=0)

    def pair_dot(lhs, rhs):
        out = _dot(jnp.concatenate(lhs, axis=1), block_diag(*rhs))
        return out[:, :HEAD_DIM], out[:, HEAD_DIM:]

    norm()
    a = st["a"]
    k = (_dot_nt(a, w_t_ref[K_LO:V_LO]) * (HEAD_DIM ** -0.5)).astype(BF16)
    q_t = _dot_nt(w_t_ref[Q_LO:K_LO], a).astype(BF16)
    zr = _dot_nt(wgr_ref[...], a) + bgr_ref[...]
    v_t = _dot_nt(w_t_ref[V_LO:O_LO], a)
    lf_r = _log_sigmoid(zr[GATE_ROWS:])
    b_r = jnp.concatenate([_dot_sel(lf_r[:, t], triu) for t in blocks], axis=1)
    u_mm()
    r_r = zr[:GATE_ROWS] - b_r
    p_r = r_r
    shift = 1
    while shift < CHUNK:
        p_r = jnp.maximum(p_r, jnp.where(lane_g >= shift, pltpu.roll(p_r, shift, axis=1), -jnp.inf))
        shift *= 2
    e_end = [jnp.exp(r_r[:, t] - p_r[:, t][:, last]) for t in blocks]
    r_c = [_sel_dot_nt(eye, r_r[:, t]) for t in blocks]
    v_mm()
    u_act()
    og_t = jax.nn.sigmoid(_dot_nt(w_t_ref[O_LO:MAIN_COLS], a)) * gain_t_ref[...]

    pair_ids = [(c, h) for c in range(n_chunks) for h in range(0, HEADS, 2)]
    kq, e_intra, x1, upd = {}, {}, {}, {}
    for c, h in unit_ids:
        e_intra[c, h] = jnp.where(mask_t, jnp.exp(r_c[c][:, h:h + 1] - p_r[h:h + 1, blocks[c]]), 0.0)
    for c, h in pair_ids:
        t = blocks[c]
        kq[c, h], kq[c, h + 1] = pair_dot([k[t, heads[h]], k[t, heads[h + 1]]],
                                          [q_t[heads[h], t], q_t[heads[h + 1], t]])
    v_act()
    for piece in mix:
        piece()
    for c, h in pair_ids:
        t = blocks[c]
        s0, vext, vw = [], [], []
        for g in (h, h + 1):
            s0.append((kq[c, g] * e_intra[c, g]).astype(BF16))
            vext.append(jnp.concatenate([v_t[heads[g], t].astype(BF16), ones_row], axis=0))
            e_row = e_end[c][g:g + 1, :]
            vw.append(jnp.concatenate([(v_t[heads[g], t] * e_row).astype(BF16),
                                       jnp.where(sub_g == 0, e_row, 0.0).astype(BF16)], axis=0))
        x1[c, h], x1[c, h + 1] = pair_dot(vext, s0)
        upd[c, h], upd[c, h + 1] = pair_dot(vw, [k[t, heads[h]], k[t, heads[h + 1]]])

    cn_in, cm, w_inter = {}, {}, {}
    for h in range(HEADS):
        cn = jnp.where(new_seq, 0.0, cn_scr[h])
        m_prev = jnp.where(new_seq, 0.0, m_scr[h:h + 1, 0:1])
        for c in range(n_chunks):
            prow = p_r[h:h + 1, blocks[c]]
            cn_in[c, h] = cn
            cm[c, h] = jnp.maximum(m_prev, prow)
            w_inter[c, h] = jnp.exp(m_prev - cm[c, h])
            cm_last = cm[c, h][:, last]
            cn = jnp.exp(m_prev - cm_last) * cn + jnp.exp(prow[:, last] - cm_last) * upd[c, h]
            m_prev = b_r[h:h + 1, blocks[c]][:, last] + cm_last
        cn_scr[h] = cn
        m_scr[h:h + 1, :] = jnp.broadcast_to(m_prev, (1, HEAD_DIM))
    x2 = {}

    def state_matmuls(c):
        t = blocks[c]
        for h in range(0, HEADS, 2):
            x2[c, h], x2[c, h + 1] = pair_dot([cn_in[c, h].astype(BF16), cn_in[c, h + 1].astype(BF16)],
                                              [q_t[heads[h], t], q_t[heads[h + 1], t]])

    def head_outputs(c):
        t = blocks[c]
        for h, hd in enumerate(heads):
            brow, prow = b_r[h:h + 1, t], p_r[h:h + 1, t]
            nd = x1[c, h] * jnp.exp(prow - cm[c, h]) + x2[c, h] * w_inter[c, h]
            num_t, den = nd[:HEAD_DIM], nd[HEAD_DIM:HEAD_DIM + 1]
            inv = 1.0 / jnp.maximum(jnp.abs(den), jnp.exp(-(brow + cm[c, h])))
            ssq = jnp.sum(num_t * num_t, axis=0, keepdims=True)
            scale = inv * lax.rsqrt(ssq * (inv * inv) * (1.0 / HEAD_DIM) + EPS)
            yml_t_ref[hd, t] = (num_t * scale * og_t[hd, t]).astype(BF16)

    for c in range(n_chunks):
        state_matmuls(c)
    for c in range(n_chunks):
        head_outputs(c)

    @pl.when(pl.program_id(0) % steps_per_seq == steps_per_seq - 1)
    def _():
        cn_out_ref[...] = cn_scr[...]
        m_out_ref[...] = m_scr[...]


def _mixer_prompt(x2d, nmix, w_t, wgr, bgr, lng, lnb, mixw, mixb, gain_t, triu, eye, cast, batch, tb):
    n = x2d.shape[0]
    n_steps = n // tb
    steps_per_seq = n_steps // batch

    def window(w):
        rows = max(BF16_ROWS, w.shape[0] // n_steps)
        return pl.BlockSpec((rows, w.shape[1]), lambda i: (jnp.minimum(i, w.shape[0] // rows - 1), 0))

    tok = lambda w: pl.BlockSpec((tb, w), lambda i: (i, 0))
    per_seq = lambda *shape: pl.BlockSpec((None,) + shape, lambda i: (i // steps_per_seq,) + (0,) * len(shape))
    return pl.pallas_call(
        functools.partial(_mixer_prompt_kernel, steps_per_seq, len(cast)),
        grid=(n_steps,),
        in_specs=[
            tok(D_MODEL),
            _const_spec((1, D_MODEL)),
            _const_spec((IN_COLS, D_MODEL)),
            _const_spec((2 * GATE_ROWS, D_MODEL)),
            _const_spec((2 * GATE_ROWS, tb)),
            _const_spec((1, GM_WIDTH)),
            _const_spec((1, GM_WIDTH)),
            _const_spec((HEADS, CHUNK, CHUNK)),
            _const_spec((HEADS, CHUNK, CHUNK)),
            _const_spec((ML_WIDTH, tb)),
            _const_spec((CHUNK, CHUNK)),
            _const_spec((CHUNK, CHUNK)),
        ] + [window(w) for w in cast],
        out_specs=(tok(GM_WIDTH), pl.BlockSpec((ML_WIDTH, tb), lambda i: (0, i)), per_seq(CHUNK, GM_WIDTH),
                   per_seq(HEADS, STATE_ROWS, HEAD_DIM), per_seq(HEADS, HEAD_DIM)) + tuple(window(w) for w in cast),
        out_shape=(
            jax.ShapeDtypeStruct((n, GM_WIDTH), BF16),
            jax.ShapeDtypeStruct((ML_WIDTH, n), BF16),
            jax.ShapeDtypeStruct((batch, CHUNK, GM_WIDTH), F32),
            jax.ShapeDtypeStruct((batch, HEADS, STATE_ROWS, HEAD_DIM), F32),
            jax.ShapeDtypeStruct((batch, HEADS, HEAD_DIM), F32),
        ) + tuple(jax.ShapeDtypeStruct(w.shape, BF16) for w in cast),
        scratch_shapes=[pltpu.VMEM((HEADS, STATE_ROWS, HEAD_DIM), F32), pltpu.VMEM((HEADS, HEAD_DIM), F32)],
        compiler_params=pltpu.CompilerParams(dimension_semantics=("arbitrary",),
                                             vmem_limit_bytes=VMEM_LIMIT),
        name="mixer_prompt",
    )(x2d, nmix, w_t, wgr, bgr, lng, lnb, mixw, mixb, gain_t, triu, eye, *cast)


def _front_kernel(x_ref, nmix_ref, w_t_ref, wgate_ref, wgate_t_ref, bcol_ref, brow_ref,
                  lng_ref, lnb_ref, mixw_ref, mixb_ref,
                  ygm_ref, q_ref, k_ref, v_ref, og_ref, gcol_ref, grow_ref, vgn_ref):
    st = {}
    for piece in _gmlp_pieces(x_ref, nmix_ref, w_t_ref, lng_ref, lnb_ref, mixw_ref, mixb_ref,
                              ygm_ref, vgn_ref, st):
        piece()
    a = st["a"]

    def proj(lo):
        return _dot_nt(a, w_t_ref[lo:lo + ML_WIDTH])

    q_ref[...] = proj(Q_LO).astype(BF16)
    k_ref[...] = (proj(K_LO) * (HEAD_DIM ** -0.5)).astype(BF16)
    v_ref[...] = proj(V_LO).astype(BF16)
    og_ref[...] = jax.nn.sigmoid(proj(O_LO))
    zc = _dot(a, wgate_ref[...]) + bcol_ref[...]
    lane = lax.broadcasted_iota(jnp.int32, zc.shape, 1)
    gcol_ref[...] = jnp.where(lane >= HEADS, _log_sigmoid(zc), zc)
    zr = _dot_nt(wgate_t_ref[...], a) + brow_ref[...]
    sub = lax.broadcasted_iota(jnp.int32, zr.shape, 0)
    grow_ref[...] = jnp.where(sub >= HEADS, _log_sigmoid(zr), zr)


def _front(x2d, nmix, w_t, wgate, wgate_t, bcol, brow, lng, lnb, mixw, mixb, tb):
    n = x2d.shape[0]
    tok = lambda w: pl.BlockSpec((tb, w), lambda i: (i, 0))
    out_shape = (
        jax.ShapeDtypeStruct((n, GM_WIDTH), BF16),
        jax.ShapeDtypeStruct((n, ML_WIDTH), BF16),
        jax.ShapeDtypeStruct((n, ML_WIDTH), BF16),
        jax.ShapeDtypeStruct((n, ML_WIDTH), BF16),
        jax.ShapeDtypeStruct((n, ML_WIDTH), F32),
        jax.ShapeDtypeStruct((n, GATE_LANES), F32),
        jax.ShapeDtypeStruct((GATE_ROWS, n), F32),
        jax.ShapeDtypeStruct((n, GM_WIDTH), F32),
    )
    return pl.pallas_call(
        _front_kernel,
        grid=(n // tb,),
        in_specs=[
            tok(D_MODEL),
            _const_spec((1, D_MODEL)),
            _const_spec((IN_COLS, D_MODEL)),
            _const_spec((D_MODEL, GATE_LANES)),
            _const_spec((GATE_ROWS, D_MODEL)),
            _const_spec((1, GATE_LANES)),
            _const_spec((GATE_ROWS, tb)),
            _const_spec((1, GM_WIDTH)),
            _const_spec((1, GM_WIDTH)),
            _const_spec((HEADS, CHUNK, CHUNK)),
            _const_spec((HEADS, CHUNK, CHUNK)),
        ],
        out_specs=(tok(GM_WIDTH), tok(ML_WIDTH), tok(ML_WIDTH), tok(ML_WIDTH), tok(ML_WIDTH),
                   tok(GATE_LANES), pl.BlockSpec((GATE_ROWS, tb), lambda i: (0, i)), tok(GM_WIDTH)),
        out_shape=out_shape,
        compiler_params=pltpu.CompilerParams(dimension_semantics=("arbitrary",),
                                             vmem_limit_bytes=VMEM_LIMIT),
        name="front",
    )(x2d, nmix, w_t, wgate, wgate_t, bcol, brow, lng, lnb, mixw, mixb)


def _intra(q, ks, igcol, bcol, igrow, brow, mprev, mask):
    d = bcol + (igrow - brow)
    g = bcol + mprev
    m_t = jnp.maximum(g, jnp.max(jnp.where(mask, d, -jnp.inf), axis=-1, keepdims=True))
    w_intra = jnp.where(mask, jnp.exp(d - m_t), 0.0)
    w_inter = jnp.exp(g - m_t)
    s = _dot_nt(q, ks) * w_intra
    return s, w_inter, m_t, g


def _head_out(num, den, m_t, gain, og):
    hh = num / jnp.maximum(jnp.abs(den), jnp.exp(-m_t))
    return (og * _rms(hh, gain)).astype(BF16)


def _mlstm_sample_kernel(seq_len, q_ref, k_ref, v_ref, og_ref, gcol_ref, grow_ref, mrep_ref, c_ref, n_ref,
                         mln_ref, tri_ref, triu_ref, sel_last_ref, expand_ref, pick_last_ref, seq_sum_ref,
                         yml_ref, c_out_ref, n_out_ref, m_out_ref):
    n_seq = CHUNK // seq_len
    tri = tri_ref[...]
    mask = tri.astype(F32) > 0.0
    gcol = gcol_ref[...]
    grow = grow_ref[...]
    bcol_all = _sel_dot(tri, gcol)
    brow_all = _dot_sel(grow, triu_ref[...])
    sel_last = sel_last_ref[...]
    expand = expand_ref[...]
    pick_last = pick_last_ref[...]
    seq_sum = seq_sum_ref[...]
    row = lax.broadcasted_iota(jnp.int32, (CHUNK, HEAD_DIM), 0)
    lane = lax.broadcasted_iota(jnp.int32, (CHUNK, HEAD_DIM), 1)
    seq_rows = [(row >= i * seq_len) & (row < (i + 1) * seq_len) for i in range(n_seq)]
    head_cols = [slice(h * HEAD_DIM, (h + 1) * HEAD_DIM) for h in range(HEADS)]
    every_head = range(HEADS)
    q = [q_ref[:, cols] for cols in head_cols]
    ks = [k_ref[:, cols] for cols in head_cols]
    v = [v_ref[:, cols] for cols in head_cols]
    zero = jnp.zeros_like(q[0])
    igcol = [gcol[:, h:h + 1] for h in every_head]
    bcol = [bcol_all[:, HEADS + h:HEADS + h + 1] for h in every_head]
    intra = [_intra(q[h], ks[h], igcol[h], bcol[h], grow[h:h + 1, :], brow_all[HEADS + h:HEADS + h + 1, :],
                    mrep_ref[:, h:h + 1], mask) for h in every_head]
    s, w_inter, m_t, g = zip(*intra)
    c_all = [c_ref[:, h] for h in every_head]
    n_all = [n_ref[:, h] for h in every_head]
    qc = [_dot(jnp.concatenate([jnp.where(m, q[h], zero) for m in seq_rows], axis=1),
               c_all[h].reshape(n_seq * HEAD_DIM, HEAD_DIM).astype(BF16)) for h in every_head]
    n_rows = [_sel_dot(expand, n_all[h]) for h in every_head]
    ends = [_sel_dot(sel_last, jnp.where(lane == 0, m_t[h], jnp.where(lane == 1, g[h],
                                                                        jnp.where(lane == 2, bcol[h], 0.0))))
            for h in every_head]
    sv = [_dot(s[h].astype(BF16), v[h]) for h in every_head]
    kw, dec, upd, dec_seq, n_inc, m_seq = [], [], [], [], [], []
    for h in every_head:
        m_new, g_last, b_last = ends[h][:, 0:1], ends[h][:, 1:2], ends[h][:, 2:3]
        w_end = jnp.exp(b_last - bcol[h] + igcol[h] - m_new)
        dec.append(jnp.exp(g_last - m_new))
        kw.append(ks[h].astype(F32) * w_end)
    for h in every_head:
        v_exp = jnp.concatenate([jnp.where(m, v[h], zero) for m in seq_rows], axis=1)
        upd.append(_dot(kw[h].T.astype(BF16), v_exp))
        dec_seq.append(_sel_dot(pick_last, jnp.broadcast_to(dec[h], (CHUNK, HEAD_DIM))))
        n_inc.append(_sel_dot(seq_sum, kw[h]))
        m_seq.append(_sel_dot(pick_last, jnp.broadcast_to(m_t[h], (CHUNK, HEAD_DIM))))
    for h, cols in enumerate(head_cols):
        qn = jnp.sum(q[h].astype(F32) * n_rows[h], axis=-1, keepdims=True)
        num = sv[h] + w_inter[h] * qc[h]
        den = jnp.sum(s[h], axis=-1, keepdims=True) + w_inter[h] * qn
        yml_ref[:, cols] = _head_out(num, den, m_t[h], mln_ref[:, cols], og_ref[:, cols])
        for i in range(n_seq):
            c_out_ref[i, h] = (dec_seq[h][i:i + 1, 0:1] * c_all[h][i]
                               + upd[h][:, i * HEAD_DIM:(i + 1) * HEAD_DIM])
        n_out_ref[:, h] = dec_seq[h] * n_all[h] + n_inc[h]
        m_out_ref[:, h] = m_seq[h]


def _mlstm_sample(q, k, v, og, gcol, grow, mrep, c0, n0, mln, tri, triu, sel_last, expand, pick_last,
                  seq_sum, seq_len):
    n = q.shape[0]
    n_seq = CHUNK // seq_len
    n_batch = n // seq_len
    tok = lambda w: pl.BlockSpec((CHUNK, w), lambda i: (i, 0))
    c_spec = pl.BlockSpec((n_seq, HEADS, HEAD_DIM, HEAD_DIM), lambda i: (i, 0, 0, 0))
    n_spec = pl.BlockSpec((n_seq, HEADS, HEAD_DIM), lambda i: (i, 0, 0))
    return pl.pallas_call(
        functools.partial(_mlstm_sample_kernel, seq_len),
        grid=(n // CHUNK,),
        in_specs=[
            tok(ML_WIDTH), tok(ML_WIDTH), tok(ML_WIDTH), tok(ML_WIDTH), tok(GATE_LANES),
            pl.BlockSpec((GATE_ROWS, CHUNK), lambda i: (0, i)),
            pl.BlockSpec((CHUNK, HEADS), lambda i: (i, 0)),
            c_spec, n_spec,
            _const_spec((1, ML_WIDTH)),
            _const_spec((CHUNK, CHUNK)), _const_spec((CHUNK, CHUNK)), _const_spec((CHUNK, CHUNK)),
            _const_spec((CHUNK, n_seq)), _const_spec((n_seq, CHUNK)), _const_spec((n_seq, CHUNK)),
        ],
        out_specs=(tok(ML_WIDTH), c_spec, n_spec, n_spec),
        out_shape=(
            jax.ShapeDtypeStruct((n, ML_WIDTH), BF16),
            jax.ShapeDtypeStruct((n_batch, HEADS, HEAD_DIM, HEAD_DIM), F32),
            jax.ShapeDtypeStruct((n_batch, HEADS, HEAD_DIM), F32),
            jax.ShapeDtypeStruct((n_batch, HEADS, HEAD_DIM), F32),
        ),
        compiler_params=pltpu.CompilerParams(dimension_semantics=("arbitrary",),
                                             vmem_limit_bytes=VMEM_LIMIT),
        name="mlstm_sample",
    )(q, k, v, og, gcol, grow, mrep, c0, n0, mln, tri, triu, sel_last, expand, pick_last, seq_sum)


def _back_math(final_norm, yml_transposed, x_ref, ygm_ref, yml_ref, pe_ref, wout_ref, nffn_ref,
               wup_ref, wdown_ref, nple_ref, wpg_ref, wpp_ref, nfin_ref, out_ref):
    ml_dot = _dot_tn if yml_transposed else _dot
    h = x_ref[...] + _dot(ygm_ref[...], wout_ref[:GM_WIDTH]) + ml_dot(yml_ref[...], wout_ref[GM_WIDTH:])
    a = _rms(h, nffn_ref[...]).astype(BF16)
    ff = D_FF // FF_SPLIT

    def mlp_part(c):
        f = _dot(a, wup_ref[:, c * ff:(c + 1) * ff])
        f = jnp.square(jnp.maximum(f, 0.0)).astype(BF16)
        return _dot(f, wdown_ref[c * ff:(c + 1) * ff, :])

    mlp = mlp_part(0)
    for c in range(1, FF_SPLIT):
        mlp = mlp + mlp_part(c)
    h = h + mlp
    gate = jax.nn.sigmoid(_dot(_rms(h, nple_ref[...]).astype(BF16), wpg_ref[...]))
    h = h + gate * _dot(pe_ref[...].astype(BF16), wpp_ref[...])
    if final_norm:
        h = _rms(h, nfin_ref[...])
    out_ref[...] = h


def _back_kernel(final_norm, yml_transposed, *refs):
    _back_math(final_norm, yml_transposed, *refs)


def _back(x2d, ygm, yml, pe2d, wout, nffn, wup, wdown, nple, wpg, wpp, nfin, final_norm, tb):
    n = x2d.shape[0]
    yml_transposed = yml.shape[0] != n
    tok = lambda w: pl.BlockSpec((tb, w), lambda i: (i, 0))
    yml_spec = pl.BlockSpec((ML_WIDTH, tb), lambda i: (0, i)) if yml_transposed else tok(ML_WIDTH)
    return pl.pallas_call(
        functools.partial(_back_kernel, final_norm, yml_transposed),
        grid=(n // tb,),
        in_specs=[
            tok(D_MODEL), tok(GM_WIDTH), yml_spec, tok(PLE_DIM),
            _const_spec((GM_WIDTH + ML_WIDTH, D_MODEL)),
            _const_spec((1, D_MODEL)),
            _const_spec((D_MODEL, D_FF)), _const_spec((D_FF, D_MODEL)),
            _const_spec((1, D_MODEL)),
            _const_spec((D_MODEL, D_MODEL)), _const_spec((PLE_DIM, D_MODEL)),
            _const_spec((1, D_MODEL)),
        ],
        out_specs=tok(D_MODEL),
        out_shape=jax.ShapeDtypeStruct((n, D_MODEL), F32),
        compiler_params=pltpu.CompilerParams(dimension_semantics=("arbitrary",),
                                             vmem_limit_bytes=VMEM_LIMIT),
        name="back",
    )(x2d, ygm, yml, pe2d, wout, nffn, wup, wdown, nple, wpg, wpp, nfin)


def _const(x):
    return jnp.asarray(np.asarray(x, np.float32), BF16)


def _block_tri(block):
    r = np.arange(CHUNK)[:, None]
    c = np.arange(CHUNK)[None, :]
    return (r // block == c // block) & (c <= r)


def _token_block(n):
    return 512 if n % 512 == 0 else CHUNK


def kernel(x_prompt, x_sample, p_prompt, p_sample, state_C, state_n, state_m, norm_mix, w_in, gm_ln_g,
           gm_ln_b, gm_ws, gm_bs, ml_b_i, ml_b_f, ml_norm, w_out, norm_ffn, w_up, w_down, norm_ple,
           w_ple_gate, w_ple_proj, norm_final):
    depth = w_in.shape[0]
    batch, seq, _ = x_prompt.shape
    dec_batch, dec_seq, _ = x_sample.shape
    n_p, n_s = batch * seq, dec_batch * dec_seq
    n_seq = CHUNK // dec_seq
    assert seq % CHUNK == 0 and CHUNK % dec_seq == 0 and n_s % CHUNK == 0
    tb_p, tb_s = 2 * _token_block(seq), _token_block(n_s)

    hp = x_prompt.reshape(n_p, D_MODEL)
    hs = x_sample.reshape(n_s, D_MODEL)

    tri_p, tri_s = _block_tri(CHUNK), _block_tri(dec_seq)
    triu_p, eye = _const(tri_p.T), _const(np.eye(CHUNK))
    r = np.arange(CHUNK)
    i = np.arange(n_seq)
    sel_last = _const(r[None, :] == (r[:, None] // dec_seq) * dec_seq + dec_seq - 1)
    expand = _const(r[:, None] // dec_seq == i[None, :])
    pick_last = _const(r[None, :] == i[:, None] * dec_seq + dec_seq - 1)
    seq_sum = _const(r[None, :] // dec_seq == i[:, None])

    outs = {k: [] for k in ("Cp", "Np", "Mp", "Vp", "Cs", "Ns", "Ms", "Vs")}
    for l in range(depth):
        row = lambda a: a[l].reshape(1, -1).astype(F32)
        w_t = w_in[l].T.astype(BF16)
        wg_t = w_t[MAIN_COLS:]
        b_i, b_f = ml_b_i[l].astype(F32), ml_b_f[l].astype(F32)
        row_pad = lambda a: jnp.pad(a, ((0, GATE_ROWS - HEADS), (0, 0)))
        wgr = jnp.concatenate([row_pad(wg_t[:HEADS]), row_pad(wg_t[HEADS:])], axis=0)
        bgr = jnp.broadcast_to(jnp.concatenate([row_pad(b_i[:, None]), row_pad(b_f[:, None])], axis=0),
                               (2 * GATE_ROWS, tb_p))
        gain_t = jnp.broadcast_to(ml_norm[l].astype(F32)[:, None], (ML_WIDTH, tb_p))
        wgate = jnp.pad(wg_t.T, ((0, 0), (0, GATE_LANES - 2 * HEADS)))
        wgate_t = jnp.pad(wg_t, ((0, GATE_ROWS - 2 * HEADS), (0, 0)))
        gbias = jnp.concatenate([b_i, b_f])
        bcol = jnp.pad(gbias, (0, GATE_LANES - 2 * HEADS)).reshape(1, GATE_LANES)
        brow = jnp.broadcast_to(jnp.pad(gbias, (0, GATE_ROWS - 2 * HEADS)).reshape(GATE_ROWS, 1),
                                (GATE_ROWS, tb_s))
        ws = gm_ws[l]
        mixw_p = (ws[:, :CHUNK, :CHUNK] * tri_p.astype(np.float32)).astype(BF16)
        mixb_p = jnp.broadcast_to(gm_bs[l][:, :CHUNK, None], (HEADS, CHUNK, CHUNK)).astype(F32)
        mixw_s = (jnp.tile(ws[:, :dec_seq, :dec_seq], (1, n_seq, n_seq)) * tri_s.astype(np.float32)).astype(BF16)
        mixb_s = jnp.broadcast_to(jnp.tile(gm_bs[l][:, :dec_seq], (1, n_seq))[:, :, None],
                                  (HEADS, CHUNK, CHUNK)).astype(F32)
        last = l == depth - 1
        nfin = norm_final.reshape(1, D_MODEL).astype(F32)

        back_f32 = [w[l].astype(F32) for w in (w_out, w_up, w_down, w_ple_gate, w_ple_proj)]
        ygm, yml_t, vlast, cn_t, m8, wout, wup, wdown, wpg, wpp = _mixer_prompt(
            hp, row(norm_mix), w_t, wgr, bgr, row(gm_ln_g), row(gm_ln_b), mixw_p, mixb_p,
            gain_t, triu_p, eye, back_f32, batch, tb_p)

        def back(x2d, ygm, yml, pe, tb):
            return _back(x2d, ygm, yml, pe, wout, row(norm_ffn), wup, wdown, row(norm_ple),
                         wpg, wpp, nfin, last, tb)

        hp = back(hp, ygm, yml_t, p_prompt[l].reshape(n_p, PLE_DIM), tb_p)
        outs["Cp"].append(jnp.swapaxes(cn_t[:, :, :HEAD_DIM, :], -1, -2))
        outs["Np"].append(cn_t[:, :, HEAD_DIM, :])
        outs["Mp"].append(m8[:, :, 0])
        outs["Vp"].append(vlast)

        ygm, q, k, v, og, gcol, grow, vgn = _front(
            hs, row(norm_mix), w_t, wgate, wgate_t, bcol, brow, row(gm_ln_g), row(gm_ln_b), mixw_s, mixb_s,
            tb_s)
        mrep = jnp.repeat(state_m[l].astype(F32), dec_seq, axis=0)
        yml, c_new, n_new, m_new = _mlstm_sample(
            q, k, v, og, gcol, grow, mrep, state_C[l].astype(F32), state_n[l].astype(F32), row(ml_norm),
            _const(tri_s), _const(tri_s.T), sel_last, expand, pick_last, seq_sum, dec_seq)
        hs = back(hs, ygm, yml, p_sample[l].reshape(n_s, PLE_DIM), tb_s)
        outs["Cs"].append(c_new)
        outs["Ns"].append(n_new)
        outs["Ms"].append(m_new[..., 0])
        outs["Vs"].append(vgn.reshape(dec_batch, dec_seq, GM_WIDTH))

    st = lambda k: outs[k][0][None] if depth == 1 else jnp.stack(outs[k])
    return (hp.reshape(batch, seq, D_MODEL), hs.reshape(dec_batch, dec_seq, D_MODEL),
            st("Cp"), st("Np"), st("Mp"), st("Vp"), st("Cs"), st("Ns"), st("Ms"), st("Vs"))
```

```python
import functools

import jax
import numpy as np
import jax.numpy as jnp
from jax import lax
from jax.experimental import pallas as pl
from jax.experimental.pallas import tpu as pltpu

F32 = jnp.float32
BF16 = jnp.bfloat16

D_MODEL = 1024
GM_WIDTH = 512
ML_WIDTH = 512
HEADS = 4
HEAD_DIM = 128
D_FF = 4096
PLE_DIM = 256
EPS = 1e-6
CHUNK = 128
MAIN_COLS = 2 * GM_WIDTH + 4 * ML_WIDTH
K_ROW = 2 * GM_WIDTH
Q_ROW = K_ROW + ML_WIDTH
V_ROW = Q_ROW + ML_WIDTH
O_ROW = V_ROW + ML_WIDTH
G_ROW = O_ROW + ML_WIDTH
W_ROWS = G_ROW + 2 * 16
GATE_LANES = 128
GATE_ROWS = 16
STATE_ROWS = HEAD_DIM + 16
FF_SPLIT = 4
BF16_ROWS = 16
VMEM_LIMIT = 60 * 1024 * 1024


def _dot(a, b):
    return jnp.dot(a, b, preferred_element_type=F32)


def _dot_nt(a, b):
    return lax.dot_general(a, b, (((1,), (1,)), ((), ())), preferred_element_type=F32)


def _dot_tn(a, b):
    return lax.dot_general(a, b, (((0,), (0,)), ((), ())), preferred_element_type=F32)


def _split3(x):
    x1 = x.astype(BF16)
    r = x - x1.astype(F32)
    x2 = r.astype(BF16)
    r = r - x2.astype(F32)
    return x1, x2, r.astype(BF16)


def _sel_dot(sel, x):
    p1, p2, p3 = _split3(x)
    return _dot(sel, p1) + _dot(sel, p2) + _dot(sel, p3)


def _dot_sel(x, sel):
    p1, p2, p3 = _split3(x)
    return _dot(p1, sel) + _dot(p2, sel) + _dot(p3, sel)


def _sel_dot_nt(sel, x):
    p1, p2, p3 = _split3(x)
    return _dot_nt(sel, p1) + _dot_nt(sel, p2) + _dot_nt(sel, p3)


def _rms(x, g):
    return x * lax.rsqrt(jnp.mean(x * x, axis=-1, keepdims=True) + EPS) * g


def _log_sigmoid(x):
    return -(jnp.maximum(-x, 0.0) + jnp.log1p(jnp.exp(-jnp.abs(x))))


def _const_spec(shape):
    nd = len(shape)
    return pl.BlockSpec(shape, lambda *_: (0,) * nd, pipeline_mode=pl.Buffered(1))


def _gmlp_pieces(x_ref, nmix_ref, w_t_ref, lng_ref, lnb_ref, mixw_ref, mixb_ref, ygm_ref, vgn_ref, st):
    tb = x_ref.shape[0]

    def norm():
        st["a"] = _rms(x_ref[...], nmix_ref[...]).astype(BF16)

    def u_mm():
        st["u"] = _dot_nt(st["a"], w_t_ref[:GM_WIDTH])

    def u_act():
        st["u"] = jax.nn.gelu(st["u"])

    def v_mm():
        st["vg"] = _dot_nt(st["a"], w_t_ref[GM_WIDTH:2 * GM_WIDTH])

    def v_act():
        vg = jax.nn.gelu(st.pop("vg"))
        mu = jnp.mean(vg, axis=-1, keepdims=True)
        var = jnp.mean(jnp.square(vg - mu), axis=-1, keepdims=True)
        vgn = (vg - mu) * lax.rsqrt(var + EPS) * lng_ref[...] + lnb_ref[...]
        vgn_ref[...] = vgn[tb - vgn_ref.shape[0]:, :]
        st["vgb"] = vgn.astype(BF16)

    def mix(c):
        rows = slice(c * CHUNK, (c + 1) * CHUNK)
        for h in range(HEADS):
            cols = slice(h * HEAD_DIM, (h + 1) * HEAD_DIM)
            s = _dot(mixw_ref[h], st["vgb"][rows, cols]) + mixb_ref[h]
            ygm_ref[rows, cols] = (st["u"][rows, cols] * s).astype(BF16)

    return [norm, u_mm, u_act, v_mm, v_act] + [functools.partial(mix, c) for c in range(tb // CHUNK)]


def _mixer_prompt_kernel(steps_per_seq, n_cast, x_ref, nmix_ref, w_t_ref, bgr_ref,
                         lng_ref, lnb_ref, mixw_ref, mixb_ref, gain_t_ref, triu_ref, eye_ref, *rest):
    cast_in, rest = rest[:n_cast], rest[n_cast:]
    ygm_ref, yml_t_ref, vlast_ref, cn_out_ref, m_out_ref = rest[:5]
    cast_out, (cn_scr, m_scr) = rest[5:5 + n_cast], rest[5 + n_cast:]
    for src, dst in zip(cast_in, cast_out):
        dst[...] = src[...].astype(BF16)

    tb = x_ref.shape[0]
    n_chunks = tb // CHUNK
    blocks = [slice(c * CHUNK, (c + 1) * CHUNK) for c in range(n_chunks)]
    heads = [slice(h * HEAD_DIM, (h + 1) * HEAD_DIM) for h in range(HEADS)]
    unit_ids = [(c, h) for c in range(n_chunks) for h in range(HEADS)]
    new_seq = pl.program_id(0) % steps_per_seq == 0
    triu = triu_ref[...]
    mask_t = triu.astype(F32) > 0.0
    eye = eye_ref[...]
    lane_g = lax.broadcasted_iota(jnp.int32, (GATE_ROWS, tb), 1) % CHUNK
    sub_g = lax.broadcasted_iota(jnp.int32, (GATE_ROWS, CHUNK), 0)
    ones_row = (sub_g == 0).astype(BF16)
    last = slice(CHUNK - 1, CHUNK)

    st = {}
    norm, u_mm, u_act, v_mm, v_act, *mix = _gmlp_pieces(x_ref, nmix_ref, w_t_ref, lng_ref, lnb_ref, mixw_ref,
                                                        mixb_ref, ygm_ref, vlast_ref, st)
    zero = jnp.zeros((HEAD_DIM, HEAD_DIM), BF16)

    def block_diag(x, y):
        return jnp.concatenate([jnp.concatenate([x, zero], axis=1), jnp.concatenate([zero, y], axis=1)], axis=0)

    def pair_dot(lhs, rhs):
        out = _dot(jnp.concatenate(lhs, axis=1), block_diag(*rhs))
        return out[:, :HEAD_DIM], out[:, HEAD_DIM:]

    norm()
    a = st["a"]
    u_mm()
    v_mm()
    k = (_dot_nt(a, w_t_ref[K_ROW:Q_ROW]) * (HEAD_DIM ** -0.5)).astype(BF16)
    on_lanes = _dot_nt(w_t_ref[Q_ROW:W_ROWS], a)
    zr = on_lanes[G_ROW - Q_ROW:] + bgr_ref[...]
    lf_r = _log_sigmoid(zr[GATE_ROWS:])
    b_r = jnp.concatenate([_dot_sel(lf_r[:, t], triu) for t in blocks], axis=1)
    q_t = on_lanes[:ML_WIDTH].astype(BF16)
    r_r = zr[:GATE_ROWS] - b_r
    p_r = r_r
    shift = 1
    while shift < CHUNK:
        p_r = jnp.maximum(p_r, jnp.where(lane_g >= shift, pltpu.roll(p_r, shift, axis=1), -jnp.inf))
        shift *= 2
    e_end = [jnp.exp(r_r[:, t] - p_r[:, t][:, last]) for t in blocks]
    u_act()
    v_t = on_lanes[V_ROW - Q_ROW:O_ROW - Q_ROW]
    r_c = [_sel_dot_nt(eye, r_r[:, t]) for t in blocks]
    og_t = jax.nn.sigmoid(on_lanes[O_ROW - Q_ROW:G_ROW - Q_ROW]) * gain_t_ref[...]
    v_act()

    pair_ids = [(c, h) for c in range(n_chunks) for h in range(0, HEADS, 2)]
    kq, e_intra, x1, upd = {}, {}, {}, {}
    for c, h in unit_ids:
        e_intra[c, h] = jnp.where(mask_t, jnp.exp(r_c[c][:, h:h + 1] - p_r[h:h + 1, blocks[c]]), 0.0)
    for piece in mix:
        piece()
    for c, h in pair_ids:
        t = blocks[c]
        kq[c, h], kq[c, h + 1] = pair_dot([k[t, heads[h]], k[t, heads[h + 1]]],
                                          [q_t[heads[h], t], q_t[heads[h + 1], t]])
    for c, h in pair_ids:
        t = blocks[c]
        s0, vext, vw = [], [], []
        for g in (h, h + 1):
            s0.append((kq[c, g] * e_intra[c, g]).astype(BF16))
            vext.append(jnp.concatenate([v_t[heads[g], t].astype(BF16), ones_row], axis=0))
            e_row = e_end[c][g:g + 1, :]
            vw.append(jnp.concatenate([(v_t[heads[g], t] * e_row).astype(BF16),
                                       jnp.where(sub_g == 0, e_row, 0.0).astype(BF16)], axis=0))
        x1[c, h], x1[c, h + 1] = pair_dot(vext, s0)
        upd[c, h], upd[c, h + 1] = pair_dot(vw, [k[t, heads[h]], k[t, heads[h + 1]]])

    cn_in, cm, w_inter = {}, {}, {}
    for h in range(HEADS):
        cn = jnp.where(new_seq, 0.0, cn_scr[h])
        m_prev = jnp.where(new_seq, 0.0, m_scr[h:h + 1, 0:1])
        for c in range(n_chunks):
            prow = p_r[h:h + 1, blocks[c]]
            cn_in[c, h] = cn
            cm[c, h] = jnp.maximum(m_prev, prow)
            w_inter[c, h] = jnp.exp(m_prev - cm[c, h])
            cm_last = cm[c, h][:, last]
            cn = jnp.exp(m_prev - cm_last) * cn + jnp.exp(prow[:, last] - cm_last) * upd[c, h]
            m_prev = b_r[h:h + 1, blocks[c]][:, last] + cm_last
        cn_scr[h] = cn
        m_scr[h:h + 1, :] = jnp.broadcast_to(m_prev, (1, HEAD_DIM))
    x2 = {}

    def state_matmuls(c):
        t = blocks[c]
        for h in range(0, HEADS, 2):
            x2[c, h], x2[c, h + 1] = pair_dot([cn_in[c, h].astype(BF16), cn_in[c, h + 1].astype(BF16)],
                                              [q_t[heads[h], t], q_t[heads[h + 1], t]])

    def head_outputs(c):
        t = blocks[c]
        for h, hd in enumerate(heads):
            brow, prow = b_r[h:h + 1, t], p_r[h:h + 1, t]
            nd = x1[c, h] * jnp.exp(prow - cm[c, h]) + x2[c, h] * w_inter[c, h]
            num_t, den = nd[:HEAD_DIM], nd[HEAD_DIM:HEAD_DIM + 1]
            inv = 1.0 / jnp.maximum(jnp.abs(den), jnp.exp(-(brow + cm[c, h])))
            ssq = jnp.sum(num_t * num_t, axis=0, keepdims=True)
            scale = inv * lax.rsqrt(ssq * (inv * inv) * (1.0 / HEAD_DIM) + EPS)
            yml_t_ref[hd, t] = (num_t * scale * og_t[hd, t]).astype(BF16)

    for c in range(n_chunks):
        state_matmuls(c)
    for c in range(n_chunks):
        head_outputs(c)

    @pl.when(pl.program_id(0) % steps_per_seq == steps_per_seq - 1)
    def _():
        cn_out_ref[...] = cn_scr[...]
        m_out_ref[...] = m_scr[...]


def _mixer_prompt(x2d, nmix, w_t, bgr, lng, lnb, mixw, mixb, gain_t, triu, eye, cast, batch, tb):
    n = x2d.shape[0]
    n_steps = n // tb
    steps_per_seq = n_steps // batch

    def window(w):
        rows = max(BF16_ROWS, w.shape[0] // n_steps)
        return pl.BlockSpec((rows, w.shape[1]), lambda i: (jnp.minimum(i, w.shape[0] // rows - 1), 0))

    tok = lambda w: pl.BlockSpec((tb, w), lambda i: (i, 0))
    per_seq = lambda *shape: pl.BlockSpec((None,) + shape, lambda i: (i // steps_per_seq,) + (0,) * len(shape))
    return pl.pallas_call(
        functools.partial(_mixer_prompt_kernel, steps_per_seq, len(cast)),
        grid=(n_steps,),
        in_specs=[
            tok(D_MODEL),
            _const_spec((1, D_MODEL)),
            _const_spec((W_ROWS, D_MODEL)),
            _const_spec((2 * GATE_ROWS, tb)),
            _const_spec((1, GM_WIDTH)),
            _const_spec((1, GM_WIDTH)),
            _const_spec((HEADS, CHUNK, CHUNK)),
            _const_spec((HEADS, CHUNK, CHUNK)),
            _const_spec((ML_WIDTH, tb)),
            _const_spec((CHUNK, CHUNK)),
            _const_spec((CHUNK, CHUNK)),
        ] + [window(w) for w in cast],
        out_specs=(tok(GM_WIDTH), pl.BlockSpec((ML_WIDTH, tb), lambda i: (0, i)), per_seq(CHUNK, GM_WIDTH),
                   per_seq(HEADS, STATE_ROWS, HEAD_DIM), per_seq(HEADS, HEAD_DIM)) + tuple(window(w) for w in cast),
        out_shape=(
            jax.ShapeDtypeStruct((n, GM_WIDTH), BF16),
            jax.ShapeDtypeStruct((ML_WIDTH, n), BF16),
            jax.ShapeDtypeStruct((batch, CHUNK, GM_WIDTH), F32),
            jax.ShapeDtypeStruct((batch, HEADS, STATE_ROWS, HEAD_DIM), F32),
            jax.ShapeDtypeStruct((batch, HEADS, HEAD_DIM), F32),
        ) + tuple(jax.ShapeDtypeStruct(w.shape, BF16) for w in cast),
        scratch_shapes=[pltpu.VMEM((HEADS, STATE_ROWS, HEAD_DIM), F32), pltpu.VMEM((HEADS, HEAD_DIM), F32)],
        compiler_params=pltpu.CompilerParams(dimension_semantics=("arbitrary",),
                                             vmem_limit_bytes=VMEM_LIMIT),
        name="mixer_prompt",
    )(x2d, nmix, w_t, bgr, lng, lnb, mixw, mixb, gain_t, triu, eye, *cast)


def _front_kernel(x_ref, nmix_ref, w_t_ref, wgate_ref, wgate_t_ref, bcol_ref, brow_ref,
                  lng_ref, lnb_ref, mixw_ref, mixb_ref,
                  ygm_ref, q_ref, k_ref, v_ref, og_ref, gcol_ref, grow_ref, vgn_ref):
    st = {}
    for piece in _gmlp_pieces(x_ref, nmix_ref, w_t_ref, lng_ref, lnb_ref, mixw_ref, mixb_ref,
                              ygm_ref, vgn_ref, st):
        piece()
    a = st["a"]

    def proj(lo):
        return _dot_nt(a, w_t_ref[lo:lo + ML_WIDTH])

    q_ref[...] = proj(Q_ROW).astype(BF16)
    k_ref[...] = (proj(K_ROW) * (HEAD_DIM ** -0.5)).astype(BF16)
    v_ref[...] = proj(V_ROW).astype(BF16)
    og_ref[...] = jax.nn.sigmoid(proj(O_ROW))
    zc = _dot(a, wgate_ref[...]) + bcol_ref[...]
    lane = lax.broadcasted_iota(jnp.int32, zc.shape, 1)
    gcol_ref[...] = jnp.where(lane >= HEADS, _log_sigmoid(zc), zc)
    zr = _dot_nt(wgate_t_ref[...], a) + brow_ref[...]
    sub = lax.broadcasted_iota(jnp.int32, zr.shape, 0)
    grow_ref[...] = jnp.where(sub >= HEADS, _log_sigmoid(zr), zr)


def _front(x2d, nmix, w_t, wgate, wgate_t, bcol, brow, lng, lnb, mixw, mixb, tb):
    n = x2d.shape[0]
    tok = lambda w: pl.BlockSpec((tb, w), lambda i: (i, 0))
    out_shape = (
        jax.ShapeDtypeStruct((n, GM_WIDTH), BF16),
        jax.ShapeDtypeStruct((n, ML_WIDTH), BF16),
        jax.ShapeDtypeStruct((n, ML_WIDTH), BF16),
        jax.ShapeDtypeStruct((n, ML_WIDTH), BF16),
        jax.ShapeDtypeStruct((n, ML_WIDTH), F32),
        jax.ShapeDtypeStruct((n, GATE_LANES), F32),
        jax.ShapeDtypeStruct((GATE_ROWS, n), F32),
        jax.ShapeDtypeStruct((n, GM_WIDTH), F32),
    )
    return pl.pallas_call(
        _front_kernel,
        grid=(n // tb,),
        in_specs=[
            tok(D_MODEL),
            _const_spec((1, D_MODEL)),
            _const_spec((W_ROWS, D_MODEL)),
            _const_spec((D_MODEL, GATE_LANES)),
            _const_spec((GATE_ROWS, D_MODEL)),
            _const_spec((1, GATE_LANES)),
            _const_spec((GATE_ROWS, tb)),
            _const_spec((1, GM_WIDTH)),
            _const_spec((1, GM_WIDTH)),
            _const_spec((HEADS, CHUNK, CHUNK)),
            _const_spec((HEADS, CHUNK, CHUNK)),
        ],
        out_specs=(tok(GM_WIDTH), tok(ML_WIDTH), tok(ML_WIDTH), tok(ML_WIDTH), tok(ML_WIDTH),
                   tok(GATE_LANES), pl.BlockSpec((GATE_ROWS, tb), lambda i: (0, i)), tok(GM_WIDTH)),
        out_shape=out_shape,
        compiler_params=pltpu.CompilerParams(dimension_semantics=("arbitrary",),
                                             vmem_limit_bytes=VMEM_LIMIT),
        name="front",
    )(x2d, nmix, w_t, wgate, wgate_t, bcol, brow, lng, lnb, mixw, mixb)


def _intra(q, ks, igcol, bcol, igrow, brow, mprev, mask):
    d = bcol + (igrow - brow)
    g = bcol + mprev
    m_t = jnp.maximum(g, jnp.max(jnp.where(mask, d, -jnp.inf), axis=-1, keepdims=True))
    w_intra = jnp.where(mask, jnp.exp(d - m_t), 0.0)
    w_inter = jnp.exp(g - m_t)
    s = _dot_nt(q, ks) * w_intra
    return s, w_inter, m_t, g


def _head_out(num, den, m_t, gain, og):
    hh = num / jnp.maximum(jnp.abs(den), jnp.exp(-m_t))
    return (og * _rms(hh, gain)).astype(BF16)


def _mlstm_sample_kernel(seq_len, q_ref, k_ref, v_ref, og_ref, gcol_ref, grow_ref, mrep_ref, c_ref, n_ref,
                         mln_ref, tri_ref, triu_ref, sel_last_ref, expand_ref, pick_last_ref, seq_sum_ref,
                         yml_ref, c_out_ref, n_out_ref, m_out_ref):
    n_seq = CHUNK // seq_len
    tri = tri_ref[...]
    mask = tri.astype(F32) > 0.0
    gcol = gcol_ref[...]
    grow = grow_ref[...]
    bcol_all = _sel_dot(tri, gcol)
    brow_all = _dot_sel(grow, triu_ref[...])
    sel_last = sel_last_ref[...]
    expand = expand_ref[...]
    pick_last = pick_last_ref[...]
    seq_sum = seq_sum_ref[...]
    row = lax.broadcasted_iota(jnp.int32, (CHUNK, HEAD_DIM), 0)
    lane = lax.broadcasted_iota(jnp.int32, (CHUNK, HEAD_DIM), 1)
    seq_rows = [(row >= i * seq_len) & (row < (i + 1) * seq_len) for i in range(n_seq)]
    head_cols = [slice(h * HEAD_DIM, (h + 1) * HEAD_DIM) for h in range(HEADS)]
    every_head = range(HEADS)
    q = [q_ref[:, cols] for cols in head_cols]
    ks = [k_ref[:, cols] for cols in head_cols]
    v = [v_ref[:, cols] for cols in head_cols]
    zero = jnp.zeros_like(q[0])
    igcol = [gcol[:, h:h + 1] for h in every_head]
    bcol = [bcol_all[:, HEADS + h:HEADS + h + 1] for h in every_head]
    intra = [_intra(q[h], ks[h], igcol[h], bcol[h], grow[h:h + 1, :], brow_all[HEADS + h:HEADS + h + 1, :],
                    mrep_ref[:, h:h + 1], mask) for h in every_head]
    s, w_inter, m_t, g = zip(*intra)
    c_all = [c_ref[:, h] for h in every_head]
    n_all = [n_ref[:, h] for h in every_head]
    qc = [_dot(jnp.concatenate([jnp.where(m, q[h], zero) for m in seq_rows], axis=1),
               c_all[h].reshape(n_seq * HEAD_DIM, HEAD_DIM).astype(BF16)) for h in every_head]
    n_rows = [_sel_dot(expand, n_all[h]) for h in every_head]
    ends = [_sel_dot(sel_last, jnp.where(lane == 0, m_t[h], jnp.where(lane == 1, g[h],
                                                                        jnp.where(lane == 2, bcol[h], 0.0))))
            for h in every_head]
    sv = [_dot(s[h].astype(BF16), v[h]) for h in every_head]
    kw, dec, upd, dec_seq, n_inc, m_seq = [], [], [], [], [], []
    for h in every_head:
        m_new, g_last, b_last = ends[h][:, 0:1], ends[h][:, 1:2], ends[h][:, 2:3]
        w_end = jnp.exp(b_last - bcol[h] + igcol[h] - m_new)
        dec.append(jnp.exp(g_last - m_new))
        kw.append(ks[h].astype(F32) * w_end)
    for h in every_head:
        v_exp = jnp.concatenate([jnp.where(m, v[h], zero) for m in seq_rows], axis=1)
        upd.append(_dot(kw[h].T.astype(BF16), v_exp))
        dec_seq.append(_sel_dot(pick_last, jnp.broadcast_to(dec[h], (CHUNK, HEAD_DIM))))
        n_inc.append(_sel_dot(seq_sum, kw[h]))
        m_seq.append(_sel_dot(pick_last, jnp.broadcast_to(m_t[h], (CHUNK, HEAD_DIM))))
    for h, cols in enumerate(head_cols):
        qn = jnp.sum(q[h].astype(F32) * n_rows[h], axis=-1, keepdims=True)
        num = sv[h] + w_inter[h] * qc[h]
        den = jnp.sum(s[h], axis=-1, keepdims=True) + w_inter[h] * qn
        yml_ref[:, cols] = _head_out(num, den, m_t[h], mln_ref[:, cols], og_ref[:, cols])
        for i in range(n_seq):
            c_out_ref[i, h] = (dec_seq[h][i:i + 1, 0:1] * c_all[h][i]
                               + upd[h][:, i * HEAD_DIM:(i + 1) * HEAD_DIM])
        n_out_ref[:, h] = dec_seq[h] * n_all[h] + n_inc[h]
        m_out_ref[:, h] = m_seq[h]


def _mlstm_sample(q, k, v, og, gcol, grow, mrep, c0, n0, mln, tri, triu, sel_last, expand, pick_last,
                  seq_sum, seq_len):
    n = q.shape[0]
    n_seq = CHUNK // seq_len
    n_batch = n // seq_len
    tok = lambda w: pl.BlockSpec((CHUNK, w), lambda i: (i, 0))
    c_spec = pl.BlockSpec((n_seq, HEADS, HEAD_DIM, HEAD_DIM), lambda i: (i, 0, 0, 0))
    n_spec = pl.BlockSpec((n_seq, HEADS, HEAD_DIM), lambda i: (i, 0, 0))
    return pl.pallas_call(
        functools.partial(_mlstm_sample_kernel, seq_len),
        grid=(n // CHUNK,),
        in_specs=[
            tok(ML_WIDTH), tok(ML_WIDTH), tok(ML_WIDTH), tok(ML_WIDTH), tok(GATE_LANES),
            pl.BlockSpec((GATE_ROWS, CHUNK), lambda i: (0, i)),
            pl.BlockSpec((CHUNK, HEADS), lambda i: (i, 0)),
            c_spec, n_spec,
            _const_spec((1, ML_WIDTH)),
            _const_spec((CHUNK, CHUNK)), _const_spec((CHUNK, CHUNK)), _const_spec((CHUNK, CHUNK)),
            _const_spec((CHUNK, n_seq)), _const_spec((n_seq, CHUNK)), _const_spec((n_seq, CHUNK)),
        ],
        out_specs=(tok(ML_WIDTH), c_spec, n_spec, n_spec),
        out_shape=(
            jax.ShapeDtypeStruct((n, ML_WIDTH), BF16),
            jax.ShapeDtypeStruct((n_batch, HEADS, HEAD_DIM, HEAD_DIM), F32),
            jax.ShapeDtypeStruct((n_batch, HEADS, HEAD_DIM), F32),
            jax.ShapeDtypeStruct((n_batch, HEADS, HEAD_DIM), F32),
        ),
        compiler_params=pltpu.CompilerParams(dimension_semantics=("arbitrary",),
                                             vmem_limit_bytes=VMEM_LIMIT),
        name="mlstm_sample",
    )(q, k, v, og, gcol, grow, mrep, c0, n0, mln, tri, triu, sel_last, expand, pick_last, seq_sum)


def _back_math(final_norm, yml_transposed, x_ref, ygm_ref, yml_ref, pe_ref, wout_ref, nffn_ref,
               wup_ref, wdown_ref, nple_ref, wpg_ref, wpp_ref, nfin_ref, out_ref):
    ml_dot = _dot_tn if yml_transposed else _dot
    h = x_ref[...] + _dot(ygm_ref[...], wout_ref[:GM_WIDTH]) + ml_dot(yml_ref[...], wout_ref[GM_WIDTH:])
    a = _rms(h, nffn_ref[...]).astype(BF16)
    ff = D_FF // FF_SPLIT

    def mlp_part(c):
        f = _dot(a, wup_ref[:, c * ff:(c + 1) * ff])
        f = jnp.square(jnp.maximum(f, 0.0)).astype(BF16)
        return _dot(f, wdown_ref[c * ff:(c + 1) * ff, :])

    mlp = mlp_part(0)
    for c in range(1, FF_SPLIT):
        mlp = mlp + mlp_part(c)
    h = h + mlp
    gate = jax.nn.sigmoid(_dot(_rms(h, nple_ref[...]).astype(BF16), wpg_ref[...]))
    h = h + gate * _dot(pe_ref[...].astype(BF16), wpp_ref[...])
    if final_norm:
        h = _rms(h, nfin_ref[...])
    out_ref[...] = h


def _back_kernel(final_norm, yml_transposed, *refs):
    _back_math(final_norm, yml_transposed, *refs)


def _back(x2d, ygm, yml, pe2d, wout, nffn, wup, wdown, nple, wpg, wpp, nfin, final_norm, tb):
    n = x2d.shape[0]
    yml_transposed = yml.shape[0] != n
    tok = lambda w: pl.BlockSpec((tb, w), lambda i: (i, 0))
    yml_spec = pl.BlockSpec((ML_WIDTH, tb), lambda i: (0, i)) if yml_transposed else tok(ML_WIDTH)
    return pl.pallas_call(
        functools.partial(_back_kernel, final_norm, yml_transposed),
        grid=(n // tb,),
        in_specs=[
            tok(D_MODEL), tok(GM_WIDTH), yml_spec, tok(PLE_DIM),
            _const_spec((GM_WIDTH + ML_WIDTH, D_MODEL)),
            _const_spec((1, D_MODEL)),
            _const_spec((D_MODEL, D_FF)), _const_spec((D_FF, D_MODEL)),
            _const_spec((1, D_MODEL)),
            _const_spec((D_MODEL, D_MODEL)), _const_spec((PLE_DIM, D_MODEL)),
            _const_spec((1, D_MODEL)),
        ],
        out_specs=tok(D_MODEL),
        out_shape=jax.ShapeDtypeStruct((n, D_MODEL), F32),
        compiler_params=pltpu.CompilerParams(dimension_semantics=("arbitrary",),
                                             vmem_limit_bytes=VMEM_LIMIT),
        name="back",
    )(x2d, ygm, yml, pe2d, wout, nffn, wup, wdown, nple, wpg, wpp, nfin)


def _const(x):
    return jnp.asarray(np.asarray(x, np.float32), BF16)


def _block_tri(block):
    r = np.arange(CHUNK)[:, None]
    c = np.arange(CHUNK)[None, :]
    return (r // block == c // block) & (c <= r)


def _token_block(n):
    return 512 if n % 512 == 0 else CHUNK


def kernel(x_prompt, x_sample, p_prompt, p_sample, state_C, state_n, state_m, norm_mix, w_in, gm_ln_g,
           gm_ln_b, gm_ws, gm_bs, ml_b_i, ml_b_f, ml_norm, w_out, norm_ffn, w_up, w_down, norm_ple,
           w_ple_gate, w_ple_proj, norm_final):
    depth = w_in.shape[0]
    batch, seq, _ = x_prompt.shape
    dec_batch, dec_seq, _ = x_sample.shape
    n_p, n_s = batch * seq, dec_batch * dec_seq
    n_seq = CHUNK // dec_seq
    assert seq % CHUNK == 0 and CHUNK % dec_seq == 0 and n_s % CHUNK == 0
    tb_p, tb_s = 2 * _token_block(seq), _token_block(n_s)

    hp = x_prompt.reshape(n_p, D_MODEL)
    hs = x_sample.reshape(n_s, D_MODEL)

    tri_p, tri_s = _block_tri(CHUNK), _block_tri(dec_seq)
    triu_p, eye = _const(tri_p.T), _const(np.eye(CHUNK))
    r = np.arange(CHUNK)
    i = np.arange(n_seq)
    sel_last = _const(r[None, :] == (r[:, None] // dec_seq) * dec_seq + dec_seq - 1)
    expand = _const(r[:, None] // dec_seq == i[None, :])
    pick_last = _const(r[None, :] == i[:, None] * dec_seq + dec_seq - 1)
    seq_sum = _const(r[None, :] // dec_seq == i[:, None])

    outs = {k: [] for k in ("Cp", "Np", "Mp", "Vp", "Cs", "Ns", "Ms", "Vs")}
    for l in range(depth):
        row = lambda a: a[l].reshape(1, -1).astype(F32)
        q_lo, k_lo, v_lo = 2 * GM_WIDTH, 2 * GM_WIDTH + ML_WIDTH, 2 * GM_WIDTH + 2 * ML_WIDTH
        w_in_t = jnp.concatenate([w_in[l][:, :q_lo], w_in[l][:, k_lo:v_lo], w_in[l][:, q_lo:k_lo],
                                  w_in[l][:, v_lo:]], axis=1).T.astype(BF16)
        wg_t = w_in_t[MAIN_COLS:]
        b_i, b_f = ml_b_i[l].astype(F32), ml_b_f[l].astype(F32)
        row_pad = lambda a: jnp.pad(a, ((0, GATE_ROWS - HEADS), (0, 0)))
        w_t = jnp.concatenate([w_in_t[:MAIN_COLS], row_pad(wg_t[:HEADS]), row_pad(wg_t[HEADS:])], axis=0)
        bgr = jnp.broadcast_to(jnp.concatenate([row_pad(b_i[:, None]), row_pad(b_f[:, None])], axis=0),
                               (2 * GATE_ROWS, tb_p))
        gain_t = jnp.broadcast_to(ml_norm[l].astype(F32)[:, None], (ML_WIDTH, tb_p))
        wgate = jnp.pad(wg_t.T, ((0, 0), (0, GATE_LANES - 2 * HEADS)))
        wgate_t = jnp.pad(wg_t, ((0, GATE_ROWS - 2 * HEADS), (0, 0)))
        gbias = jnp.concatenate([b_i, b_f])
        bcol = jnp.pad(gbias, (0, GATE_LANES - 2 * HEADS)).reshape(1, GATE_LANES)
        brow = jnp.broadcast_to(jnp.pad(gbias, (0, GATE_ROWS - 2 * HEADS)).reshape(GATE_ROWS, 1),
                                (GATE_ROWS, tb_s))
        ws = gm_ws[l]
        mixw_p = (ws[:, :CHUNK, :CHUNK] * tri_p.astype(np.float32)).astype(BF16)
        mixb_p = jnp.broadcast_to(gm_bs[l][:, :CHUNK, None], (HEADS, CHUNK, CHUNK)).astype(F32)
        mixw_s = (jnp.tile(ws[:, :dec_seq, :dec_seq], (1, n_seq, n_seq)) * tri_s.astype(np.float32)).astype(BF16)
        mixb_s = jnp.broadcast_to(jnp.tile(gm_bs[l][:, :dec_seq], (1, n_seq))[:, :, None],
                                  (HEADS, CHUNK, CHUNK)).astype(F32)
        last = l == depth - 1
        nfin = norm_final.reshape(1, D_MODEL).astype(F32)

        back_f32 = [w[l].astype(F32) for w in (w_out, w_up, w_down, w_ple_gate, w_ple_proj)]
        ygm, yml_t, vlast, cn_t, m8, wout, wup, wdown, wpg, wpp = _mixer_prompt(
            hp, row(norm_mix), w_t, bgr, row(gm_ln_g), row(gm_ln_b), mixw_p, mixb_p,
            gain_t, triu_p, eye, back_f32, batch, tb_p)

        def back(x2d, ygm, yml, pe, tb):
            return _back(x2d, ygm, yml, pe, wout, row(norm_ffn), wup, wdown, row(norm_ple),
                         wpg, wpp, nfin, last, tb)

        hp = back(hp, ygm, yml_t, p_prompt[l].reshape(n_p, PLE_DIM), tb_p)
        outs["Cp"].append(jnp.swapaxes(cn_t[:, :, :HEAD_DIM, :], -1, -2))
        outs["Np"].append(cn_t[:, :, HEAD_DIM, :])
        outs["Mp"].append(m8[:, :, 0])
        outs["Vp"].append(vlast)

        ygm, q, k, v, og, gcol, grow, vgn = _front(
            hs, row(norm_mix), w_t, wgate, wgate_t, bcol, brow, row(gm_ln_g), row(gm_ln_b), mixw_s, mixb_s,
            tb_s)
        mrep = jnp.repeat(state_m[l].astype(F32), dec_seq, axis=0)
        yml, c_new, n_new, m_new = _mlstm_sample(
            q, k, v, og, gcol, grow, mrep, state_C[l].astype(F32), state_n[l].astype(F32), row(ml_norm),
            _const(tri_s), _const(tri_s.T), sel_last, expand, pick_last, seq_sum, dec_seq)
        hs = back(hs, ygm, yml, p_sample[l].reshape(n_s, PLE_DIM), tb_s)
        outs["Cs"].append(c_new)
        outs["Ns"].append(n_new)
        outs["Ms"].append(m_new[..., 0])
        outs["Vs"].append(vgn.reshape(dec_batch, dec_seq, GM_WIDTH))

    st = lambda k: outs[k][0][None] if depth == 1 else jnp.stack(outs[k])
    return (hp.reshape(batch, seq, D_MODEL), hs.reshape(dec_batch, dec_seq, D_MODEL),
            st("Cp"), st("Np"), st("Mp"), st("Vp"), st("Cs"), st("Ns"), st("Ms"), st("Vs"))
```

```python
import functools

import jax
import numpy as np
import jax.numpy as jnp
from jax import lax
from jax.experimental import pallas as pl
from jax.experimental.pallas import tpu as pltpu

F32 = jnp.float32
BF16 = jnp.bfloat16

D_MODEL = 1024
GM_WIDTH = 512
ML_WIDTH = 512
HEADS = 4
HEAD_DIM = 128
D_FF = 4096
PLE_DIM = 256
EPS = 1e-6
CHUNK = 128
MAIN_COLS = 2 * GM_WIDTH + 4 * ML_WIDTH
Q_ROW = 2 * GM_WIDTH
K_ROW = Q_ROW + ML_WIDTH
V_ROW = K_ROW + ML_WIDTH
O_ROW = V_ROW + ML_WIDTH
W_ROWS = MAIN_COLS + 2 * HEADS
LANES_V = ML_WIDTH
LANES_O = 2 * ML_WIDTH
LANES_G = 3 * ML_WIDTH
LANES_ROWS = LANES_G + 2 * 16
GATE_LANES = 128
GATE_ROWS = 16
STATE_ROWS = HEAD_DIM + 16
FF_SPLIT = 4
BF16_ROWS = 16
VMEM_LIMIT = 60 * 1024 * 1024


def _dot(a, b):
    return jnp.dot(a, b, preferred_element_type=F32)


def _dot_nt(a, b):
    return lax.dot_general(a, b, (((1,), (1,)), ((), ())), preferred_element_type=F32)


def _dot_tn(a, b):
    return lax.dot_general(a, b, (((0,), (0,)), ((), ())), preferred_element_type=F32)


def _split3(x):
    x1 = x.astype(BF16)
    r = x - x1.astype(F32)
    x2 = r.astype(BF16)
    r = r - x2.astype(F32)
    return x1, x2, r.astype(BF16)


def _sel_dot(sel, x):
    p1, p2, p3 = _split3(x)
    return _dot(sel, p1) + _dot(sel, p2) + _dot(sel, p3)


def _dot_sel(x, sel):
    p1, p2, p3 = _split3(x)
    return _dot(p1, sel) + _dot(p2, sel) + _dot(p3, sel)


def _sel_dot_nt(sel, x):
    p1, p2, p3 = _split3(x)
    return _dot_nt(sel, p1) + _dot_nt(sel, p2) + _dot_nt(sel, p3)


def _rms(x, g):
    return x * lax.rsqrt(jnp.mean(x * x, axis=-1, keepdims=True) + EPS) * g


def _log_sigmoid(x):
    return -(jnp.maximum(-x, 0.0) + jnp.log1p(jnp.exp(-jnp.abs(x))))


def _const_spec(shape):
    nd = len(shape)
    return pl.BlockSpec(shape, lambda *_: (0,) * nd, pipeline_mode=pl.Buffered(1))


def _gmlp_pieces(x_ref, nmix_ref, w_t_ref, lng_ref, lnb_ref, mixw_ref, mixb_ref, ygm_ref, vgn_ref, st):
    tb = x_ref.shape[0]

    def norm():
        st["a"] = _rms(x_ref[...], nmix_ref[...]).astype(BF16)

    def u_mm():
        st["u"] = _dot_nt(st["a"], w_t_ref[:GM_WIDTH])

    def u_act():
        st["u"] = jax.nn.gelu(st["u"])

    def v_mm():
        st["vg"] = _dot_nt(st["a"], w_t_ref[GM_WIDTH:2 * GM_WIDTH])

    def v_act():
        vg = jax.nn.gelu(st.pop("vg"))
        mu = jnp.mean(vg, axis=-1, keepdims=True)
        var = jnp.mean(jnp.square(vg - mu), axis=-1, keepdims=True)
        vgn = (vg - mu) * lax.rsqrt(var + EPS) * lng_ref[...] + lnb_ref[...]
        vgn_ref[...] = vgn[tb - vgn_ref.shape[0]:, :]
        st["vgb"] = vgn.astype(BF16)

    def mix(c):
        rows = slice(c * CHUNK, (c + 1) * CHUNK)
        for h in range(HEADS):
            cols = slice(h * HEAD_DIM, (h + 1) * HEAD_DIM)
            s = _dot(mixw_ref[h], st["vgb"][rows, cols]) + mixb_ref[h]
            ygm_ref[rows, cols] = (st["u"][rows, cols] * s).astype(BF16)

    return [norm, u_mm, u_act, v_mm, v_act] + [functools.partial(mix, c) for c in range(tb // CHUNK)]


def _mixer_prompt_kernel(steps_per_seq, n_cast, x_ref, nmix_ref, w_t_ref, wgr_ref, bgr_ref,
                         lng_ref, lnb_ref, mixw_ref, mixb_ref, gain_t_ref, triu_ref, eye_ref, *rest):
    cast_in, rest = rest[:n_cast], rest[n_cast:]
    ygm_ref, yml_t_ref, vlast_ref, cn_out_ref, m_out_ref = rest[:5]
    cast_out, (cn_scr, m_scr, w_lanes_scr) = rest[5:5 + n_cast], rest[5 + n_cast:]

    @pl.when(pl.program_id(0) == 0)
    def _():
        w_lanes_scr[:LANES_V] = w_t_ref[Q_ROW:K_ROW]
        w_lanes_scr[LANES_V:LANES_G] = w_t_ref[V_ROW:MAIN_COLS]
        w_lanes_scr[LANES_G:] = wgr_ref[...]

    for src, dst in zip(cast_in, cast_out):
        dst[...] = src[...].astype(BF16)

    tb = x_ref.shape[0]
    n_chunks = tb // CHUNK
    blocks = [slice(c * CHUNK, (c + 1) * CHUNK) for c in range(n_chunks)]
    heads = [slice(h * HEAD_DIM, (h + 1) * HEAD_DIM) for h in range(HEADS)]
    unit_ids = [(c, h) for c in range(n_chunks) for h in range(HEADS)]
    new_seq = pl.program_id(0) % steps_per_seq == 0
    triu = triu_ref[...]
    mask_t = triu.astype(F32) > 0.0
    eye = eye_ref[...]
    lane_g = lax.broadcasted_iota(jnp.int32, (GATE_ROWS, tb), 1) % CHUNK
    sub_g = lax.broadcasted_iota(jnp.int32, (GATE_ROWS, CHUNK), 0)
    ones_row = (sub_g == 0).astype(BF16)
    last = slice(CHUNK - 1, CHUNK)

    st = {}
    norm, u_mm, u_act, v_mm, v_act, *mix = _gmlp_pieces(x_ref, nmix_ref, w_t_ref, lng_ref, lnb_ref, mixw_ref,
                                                        mixb_ref, ygm_ref, vlast_ref, st)
    zero = jnp.zeros((HEAD_DIM, HEAD_DIM), BF16)

    def block_diag(x, y):
        return jnp.concatenate([jnp.concatenate([x, zero], axis=1), jnp.concatenate([zero, y], axis=1)], axis=0)

    def pair_dot(lhs, rhs):
        out = _dot(jnp.concatenate(lhs, axis=1), block_diag(*rhs))
        return out[:, :HEAD_DIM], out[:, HEAD_DIM:]

    norm()
    a = st["a"]
    u_mm()
    v_mm()
    k = (_dot_nt(a, w_t_ref[K_ROW:V_ROW]) * (HEAD_DIM ** -0.5)).astype(BF16)
    on_lanes = _dot_nt(w_lanes_scr[...], a)
    zr = on_lanes[LANES_G:] + bgr_ref[...]
    lf_r = _log_sigmoid(zr[GATE_ROWS:])
    b_r = jnp.concatenate([_dot_sel(lf_r[:, t], triu) for t in blocks], axis=1)
    q_t = on_lanes[:ML_WIDTH].astype(BF16)
    r_r = zr[:GATE_ROWS] - b_r
    p_r = r_r
    shift = 1
    while shift < CHUNK:
        p_r = jnp.maximum(p_r, jnp.where(lane_g >= shift, pltpu.roll(p_r, shift, axis=1), -jnp.inf))
        shift *= 2
    e_end = [jnp.exp(r_r[:, t] - p_r[:, t][:, last]) for t in blocks]
    u_act()
    v_t = on_lanes[LANES_V:LANES_O]
    r_c = [_sel_dot_nt(eye, r_r[:, t]) for t in blocks]
    og_t = jax.nn.sigmoid(on_lanes[LANES_O:LANES_G]) * gain_t_ref[...]
    v_act()

    pair_ids = [(c, h) for c in range(n_chunks) for h in range(0, HEADS, 2)]
    kq, e_intra, x1, upd = {}, {}, {}, {}
    for c, h in unit_ids:
        e_intra[c, h] = jnp.where(mask_t, jnp.exp(r_c[c][:, h:h + 1] - p_r[h:h + 1, blocks[c]]), 0.0)
    for piece in mix:
        piece()
    for c, h in pair_ids:
        t = blocks[c]
        kq[c, h], kq[c, h + 1] = pair_dot([k[t, heads[h]], k[t, heads[h + 1]]],
                                          [q_t[heads[h], t], q_t[heads[h + 1], t]])
    for c, h in pair_ids:
        t = blocks[c]
        s0, vext, vw = [], [], []
        for g in (h, h + 1):
            s0.append((kq[c, g] * e_intra[c, g]).astype(BF16))
            vext.append(jnp.concatenate([v_t[heads[g], t].astype(BF16), ones_row], axis=0))
            e_row = e_end[c][g:g + 1, :]
            vw.append(jnp.concatenate([(v_t[heads[g], t] * e_row).astype(BF16),
                                       jnp.where(sub_g == 0, e_row, 0.0).astype(BF16)], axis=0))
        x1[c, h], x1[c, h + 1] = pair_dot(vext, s0)
        upd[c, h], upd[c, h + 1] = pair_dot(vw, [k[t, heads[h]], k[t, heads[h + 1]]])

    cn_in, cm, w_inter = {}, {}, {}
    for h in range(HEADS):
        cn = jnp.where(new_seq, 0.0, cn_scr[h])
        m_prev = jnp.where(new_seq, 0.0, m_scr[h:h + 1, 0:1])
        for c in range(n_chunks):
            prow = p_r[h:h + 1, blocks[c]]
            cn_in[c, h] = cn
            cm[c, h] = jnp.maximum(m_prev, prow)
            w_inter[c, h] = jnp.exp(m_prev - cm[c, h])
            cm_last = cm[c, h][:, last]
            cn = jnp.exp(m_prev - cm_last) * cn + jnp.exp(prow[:, last] - cm_last) * upd[c, h]
            m_prev = b_r[h:h + 1, blocks[c]][:, last] + cm_last
        cn_scr[h] = cn
        m_scr[h:h + 1, :] = jnp.broadcast_to(m_prev, (1, HEAD_DIM))
    x2 = {}

    def state_matmuls(c):
        t = blocks[c]
        for h in range(0, HEADS, 2):
            x2[c, h], x2[c, h + 1] = pair_dot([cn_in[c, h].astype(BF16), cn_in[c, h + 1].astype(BF16)],
                                              [q_t[heads[h], t], q_t[heads[h + 1], t]])

    def head_outputs(c):
        t = blocks[c]
        for h, hd in enumerate(heads):
            brow, prow = b_r[h:h + 1, t], p_r[h:h + 1, t]
            nd = x1[c, h] * jnp.exp(prow - cm[c, h]) + x2[c, h] * w_inter[c, h]
            num_t, den = nd[:HEAD_DIM], nd[HEAD_DIM:HEAD_DIM + 1]
            inv = 1.0 / jnp.maximum(jnp.abs(den), jnp.exp(-(brow + cm[c, h])))
            ssq = jnp.sum(num_t * num_t, axis=0, keepdims=True)
            scale = inv * lax.rsqrt(ssq * (inv * inv) * (1.0 / HEAD_DIM) + EPS)
            yml_t_ref[hd, t] = (num_t * scale * og_t[hd, t]).astype(BF16)

    for c in range(n_chunks):
        state_matmuls(c)
    for c in range(n_chunks):
        head_outputs(c)

    @pl.when(pl.program_id(0) % steps_per_seq == steps_per_seq - 1)
    def _():
        cn_out_ref[...] = cn_scr[...]
        m_out_ref[...] = m_scr[...]


def _mixer_prompt(x2d, nmix, w_t, wgr, bgr, lng, lnb, mixw, mixb, gain_t, triu, eye, cast, batch, tb):
    n = x2d.shape[0]
    n_steps = n // tb
    steps_per_seq = n_steps // batch

    def window(w):
        rows = max(BF16_ROWS, w.shape[0] // n_steps)
        return pl.BlockSpec((rows, w.shape[1]), lambda i: (jnp.minimum(i, w.shape[0] // rows - 1), 0))

    tok = lambda w: pl.BlockSpec((tb, w), lambda i: (i, 0))
    per_seq = lambda *shape: pl.BlockSpec((None,) + shape, lambda i: (i // steps_per_seq,) + (0,) * len(shape))
    return pl.pallas_call(
        functools.partial(_mixer_prompt_kernel, steps_per_seq, len(cast)),
        grid=(n_steps,),
        in_specs=[
            tok(D_MODEL),
            _const_spec((1, D_MODEL)),
            _const_spec((W_ROWS, D_MODEL)),
            _const_spec((2 * GATE_ROWS, D_MODEL)),
            _const_spec((2 * GATE_ROWS, tb)),
            _const_spec((1, GM_WIDTH)),
            _const_spec((1, GM_WIDTH)),
            _const_spec((HEADS, CHUNK, CHUNK)),
            _const_spec((HEADS, CHUNK, CHUNK)),
            _const_spec((ML_WIDTH, tb)),
            _const_spec((CHUNK, CHUNK)),
            _const_spec((CHUNK, CHUNK)),
        ] + [window(w) for w in cast],
        out_specs=(tok(GM_WIDTH), pl.BlockSpec((ML_WIDTH, tb), lambda i: (0, i)), per_seq(CHUNK, GM_WIDTH),
                   per_seq(HEADS, STATE_ROWS, HEAD_DIM), per_seq(HEADS, HEAD_DIM)) + tuple(window(w) for w in cast),
        out_shape=(
            jax.ShapeDtypeStruct((n, GM_WIDTH), BF16),
            jax.ShapeDtypeStruct((ML_WIDTH, n), BF16),
            jax.ShapeDtypeStruct((batch, CHUNK, GM_WIDTH), F32),
            jax.ShapeDtypeStruct((batch, HEADS, STATE_ROWS, HEAD_DIM), F32),
            jax.ShapeDtypeStruct((batch, HEADS, HEAD_DIM), F32),
        ) + tuple(jax.ShapeDtypeStruct(w.shape, BF16) for w in cast),
        scratch_shapes=[pltpu.VMEM((HEADS, STATE_ROWS, HEAD_DIM), F32), pltpu.VMEM((HEADS, HEAD_DIM), F32),
                        pltpu.VMEM((LANES_ROWS, D_MODEL), BF16)],
        compiler_params=pltpu.CompilerParams(dimension_semantics=("arbitrary",),
                                             vmem_limit_bytes=VMEM_LIMIT),
        name="mixer_prompt",
    )(x2d, nmix, w_t, wgr, bgr, lng, lnb, mixw, mixb, gain_t, triu, eye, *cast)


def _front_kernel(x_ref, nmix_ref, w_t_ref, wgate_ref, wgate_t_ref, bcol_ref, brow_ref,
                  lng_ref, lnb_ref, mixw_ref, mixb_ref,
                  ygm_ref, q_ref, k_ref, v_ref, og_ref, gcol_ref, grow_ref, vgn_ref):
    st = {}
    for piece in _gmlp_pieces(x_ref, nmix_ref, w_t_ref, lng_ref, lnb_ref, mixw_ref, mixb_ref,
                              ygm_ref, vgn_ref, st):
        piece()
    a = st["a"]

    def proj(lo):
        return _dot_nt(a, w_t_ref[lo:lo + ML_WIDTH])

    q_ref[...] = proj(Q_ROW).astype(BF16)
    k_ref[...] = (proj(K_ROW) * (HEAD_DIM ** -0.5)).astype(BF16)
    v_ref[...] = proj(V_ROW).astype(BF16)
    og_ref[...] = jax.nn.sigmoid(proj(O_ROW))
    zc = _dot(a, wgate_ref[...]) + bcol_ref[...]
    lane = lax.broadcasted_iota(jnp.int32, zc.shape, 1)
    gcol_ref[...] = jnp.where(lane >= HEADS, _log_sigmoid(zc), zc)
    zr = _dot_nt(wgate_t_ref[...], a) + brow_ref[...]
    sub = lax.broadcasted_iota(jnp.int32, zr.shape, 0)
    grow_ref[...] = jnp.where(sub >= HEADS, _log_sigmoid(zr), zr)


def _front(x2d, nmix, w_t, wgate, wgate_t, bcol, brow, lng, lnb, mixw, mixb, tb):
    n = x2d.shape[0]
    tok = lambda w: pl.BlockSpec((tb, w), lambda i: (i, 0))
    out_shape = (
        jax.ShapeDtypeStruct((n, GM_WIDTH), BF16),
        jax.ShapeDtypeStruct((n, ML_WIDTH), BF16),
        jax.ShapeDtypeStruct((n, ML_WIDTH), BF16),
        jax.ShapeDtypeStruct((n, ML_WIDTH), BF16),
        jax.ShapeDtypeStruct((n, ML_WIDTH), F32),
        jax.ShapeDtypeStruct((n, GATE_LANES), F32),
        jax.ShapeDtypeStruct((GATE_ROWS, n), F32),
        jax.ShapeDtypeStruct((n, GM_WIDTH), F32),
    )
    return pl.pallas_call(
        _front_kernel,
        grid=(n // tb,),
        in_specs=[
            tok(D_MODEL),
            _const_spec((1, D_MODEL)),
            _const_spec((W_ROWS, D_MODEL)),
            _const_spec((D_MODEL, GATE_LANES)),
            _const_spec((GATE_ROWS, D_MODEL)),
            _const_spec((1, GATE_LANES)),
            _const_spec((GATE_ROWS, tb)),
            _const_spec((1, GM_WIDTH)),
            _const_spec((1, GM_WIDTH)),
            _const_spec((HEADS, CHUNK, CHUNK)),
            _const_spec((HEADS, CHUNK, CHUNK)),
        ],
        out_specs=(tok(GM_WIDTH), tok(ML_WIDTH), tok(ML_WIDTH), tok(ML_WIDTH), tok(ML_WIDTH),
                   tok(GATE_LANES), pl.BlockSpec((GATE_ROWS, tb), lambda i: (0, i)), tok(GM_WIDTH)),
        out_shape=out_shape,
        compiler_params=pltpu.CompilerParams(dimension_semantics=("arbitrary",),
                                             vmem_limit_bytes=VMEM_LIMIT),
        name="front",
    )(x2d, nmix, w_t, wgate, wgate_t, bcol, brow, lng, lnb, mixw, mixb)


def _intra(q, ks, igcol, bcol, igrow, brow, mprev, mask):
    d = bcol + (igrow - brow)
    g = bcol + mprev
    m_t = jnp.maximum(g, jnp.max(jnp.where(mask, d, -jnp.inf), axis=-1, keepdims=True))
    w_intra = jnp.where(mask, jnp.exp(d - m_t), 0.0)
    w_inter = jnp.exp(g - m_t)
    s = _dot_nt(q, ks) * w_intra
    return s, w_inter, m_t, g


def _head_out(num, den, m_t, gain, og):
    hh = num / jnp.maximum(jnp.abs(den), jnp.exp(-m_t))
    return (og * _rms(hh, gain)).astype(BF16)


def _mlstm_sample_kernel(seq_len, q_ref, k_ref, v_ref, og_ref, gcol_ref, grow_ref, mrep_ref, c_ref, n_ref,
                         mln_ref, tri_ref, triu_ref, sel_last_ref, expand_ref, pick_last_ref, seq_sum_ref,
                         yml_ref, c_out_ref, n_out_ref, m_out_ref):
    n_seq = CHUNK // seq_len
    tri = tri_ref[...]
    mask = tri.astype(F32) > 0.0
    gcol = gcol_ref[...]
    grow = grow_ref[...]
    bcol_all = _sel_dot(tri, gcol)
    brow_all = _dot_sel(grow, triu_ref[...])
    sel_last = sel_last_ref[...]
    expand = expand_ref[...]
    pick_last = pick_last_ref[...]
    seq_sum = seq_sum_ref[...]
    row = lax.broadcasted_iota(jnp.int32, (CHUNK, HEAD_DIM), 0)
    lane = lax.broadcasted_iota(jnp.int32, (CHUNK, HEAD_DIM), 1)
    seq_rows = [(row >= i * seq_len) & (row < (i + 1) * seq_len) for i in range(n_seq)]
    head_cols = [slice(h * HEAD_DIM, (h + 1) * HEAD_DIM) for h in range(HEADS)]
    every_head = range(HEADS)
    q = [q_ref[:, cols] for cols in head_cols]
    ks = [k_ref[:, cols] for cols in head_cols]
    v = [v_ref[:, cols] for cols in head_cols]
    zero = jnp.zeros_like(q[0])
    igcol = [gcol[:, h:h + 1] for h in every_head]
    bcol = [bcol_all[:, HEADS + h:HEADS + h + 1] for h in every_head]
    intra = [_intra(q[h], ks[h], igcol[h], bcol[h], grow[h:h + 1, :], brow_all[HEADS + h:HEADS + h + 1, :],
                    mrep_ref[:, h:h + 1], mask) for h in every_head]
    s, w_inter, m_t, g = zip(*intra)
    c_all = [c_ref[:, h] for h in every_head]
    n_all = [n_ref[:, h] for h in every_head]
    qc = [_dot(jnp.concatenate([jnp.where(m, q[h], zero) for m in seq_rows], axis=1),
               c_all[h].reshape(n_seq * HEAD_DIM, HEAD_DIM).astype(BF16)) for h in every_head]
    n_rows = [_sel_dot(expand, n_all[h]) for h in every_head]
    ends = [_sel_dot(sel_last, jnp.where(lane == 0, m_t[h], jnp.where(lane == 1, g[h],
                                                                        jnp.where(lane == 2, bcol[h], 0.0))))
            for h in every_head]
    sv = [_dot(s[h].astype(BF16), v[h]) for h in every_head]
    kw, dec, upd, dec_seq, n_inc, m_seq = [], [], [], [], [], []
    for h in every_head:
        m_new, g_last, b_last = ends[h][:, 0:1], ends[h][:, 1:2], ends[h][:, 2:3]
        w_end = jnp.exp(b_last - bcol[h] + igcol[h] - m_new)
        dec.append(jnp.exp(g_last - m_new))
        kw.append(ks[h].astype(F32) * w_end)
    for h in every_head:
        v_exp = jnp.concatenate([jnp.where(m, v[h], zero) for m in seq_rows], axis=1)
        upd.append(_dot(kw[h].T.astype(BF16), v_exp))
        dec_seq.append(_sel_dot(pick_last, jnp.broadcast_to(dec[h], (CHUNK, HEAD_DIM))))
        n_inc.append(_sel_dot(seq_sum, kw[h]))
        m_seq.append(_sel_dot(pick_last, jnp.broadcast_to(m_t[h], (CHUNK, HEAD_DIM))))
    for h, cols in enumerate(head_cols):
        qn = jnp.sum(q[h].astype(F32) * n_rows[h], axis=-1, keepdims=True)
        num = sv[h] + w_inter[h] * qc[h]
        den = jnp.sum(s[h], axis=-1, keepdims=True) + w_inter[h] * qn
        yml_ref[:, cols] = _head_out(num, den, m_t[h], mln_ref[:, cols], og_ref[:, cols])
        for i in range(n_seq):
            c_out_ref[i, h] = (dec_seq[h][i:i + 1, 0:1] * c_all[h][i]
                               + upd[h][:, i * HEAD_DIM:(i + 1) * HEAD_DIM])
        n_out_ref[:, h] = dec_seq[h] * n_all[h] + n_inc[h]
        m_out_ref[:, h] = m_seq[h]


def _mlstm_sample(q, k, v, og, gcol, grow, mrep, c0, n0, mln, tri, triu, sel_last, expand, pick_last,
                  seq_sum, seq_len):
    n = q.shape[0]
    n_seq = CHUNK // seq_len
    n_batch = n // seq_len
    tok = lambda w: pl.BlockSpec((CHUNK, w), lambda i: (i, 0))
    c_spec = pl.BlockSpec((n_seq, HEADS, HEAD_DIM, HEAD_DIM), lambda i: (i, 0, 0, 0))
    n_spec = pl.BlockSpec((n_seq, HEADS, HEAD_DIM), lambda i: (i, 0, 0))
    return pl.pallas_call(
        functools.partial(_mlstm_sample_kernel, seq_len),
        grid=(n // CHUNK,),
        in_specs=[
            tok(ML_WIDTH), tok(ML_WIDTH), tok(ML_WIDTH), tok(ML_WIDTH), tok(GATE_LANES),
            pl.BlockSpec((GATE_ROWS, CHUNK), lambda i: (0, i)),
            pl.BlockSpec((CHUNK, HEADS), lambda i: (i, 0)),
            c_spec, n_spec,
            _const_spec((1, ML_WIDTH)),
            _const_spec((CHUNK, CHUNK)), _const_spec((CHUNK, CHUNK)), _const_spec((CHUNK, CHUNK)),
            _const_spec((CHUNK, n_seq)), _const_spec((n_seq, CHUNK)), _const_spec((n_seq, CHUNK)),
        ],
        out_specs=(tok(ML_WIDTH), c_spec, n_spec, n_spec),
        out_shape=(
            jax.ShapeDtypeStruct((n, ML_WIDTH), BF16),
            jax.ShapeDtypeStruct((n_batch, HEADS, HEAD_DIM, HEAD_DIM), F32),
            jax.ShapeDtypeStruct((n_batch, HEADS, HEAD_DIM), F32),
            jax.ShapeDtypeStruct((n_batch, HEADS, HEAD_DIM), F32),
        ),
        compiler_params=pltpu.CompilerParams(dimension_semantics=("arbitrary",),
                                             vmem_limit_bytes=VMEM_LIMIT),
        name="mlstm_sample",
    )(q, k, v, og, gcol, grow, mrep, c0, n0, mln, tri, triu, sel_last, expand, pick_last, seq_sum)


def _back_math(final_norm, yml_transposed, x_ref, ygm_ref, yml_ref, pe_ref, wout_ref, nffn_ref,
               wup_ref, wdown_ref, nple_ref, wpg_ref, wpp_ref, nfin_ref, out_ref):
    ml_dot = _dot_tn if yml_transposed else _dot
    h = x_ref[...] + _dot(ygm_ref[...], wout_ref[:GM_WIDTH]) + ml_dot(yml_ref[...], wout_ref[GM_WIDTH:])
    a = _rms(h, nffn_ref[...]).astype(BF16)
    ff = D_FF // FF_SPLIT

    def mlp_part(c):
        f = _dot(a, wup_ref[:, c * ff:(c + 1) * ff])
        f = jnp.square(jnp.maximum(f, 0.0)).astype(BF16)
        return _dot(f, wdown_ref[c * ff:(c + 1) * ff, :])

    mlp = mlp_part(0)
    for c in range(1, FF_SPLIT):
        mlp = mlp + mlp_part(c)
    h = h + mlp
    gate = jax.nn.sigmoid(_dot(_rms(h, nple_ref[...]).astype(BF16), wpg_ref[...]))
    h = h + gate * _dot(pe_ref[...].astype(BF16), wpp_ref[...])
    if final_norm:
        h = _rms(h, nfin_ref[...])
    out_ref[...] = h


def _back_kernel(final_norm, yml_transposed, *refs):
    _back_math(final_norm, yml_transposed, *refs)


def _back(x2d, ygm, yml, pe2d, wout, nffn, wup, wdown, nple, wpg, wpp, nfin, final_norm, tb):
    n = x2d.shape[0]
    yml_transposed = yml.shape[0] != n
    tok = lambda w: pl.BlockSpec((tb, w), lambda i: (i, 0))
    yml_spec = pl.BlockSpec((ML_WIDTH, tb), lambda i: (0, i)) if yml_transposed else tok(ML_WIDTH)
    return pl.pallas_call(
        functools.partial(_back_kernel, final_norm, yml_transposed),
        grid=(n // tb,),
        in_specs=[
            tok(D_MODEL), tok(GM_WIDTH), yml_spec, tok(PLE_DIM),
            _const_spec((GM_WIDTH + ML_WIDTH, D_MODEL)),
            _const_spec((1, D_MODEL)),
            _const_spec((D_MODEL, D_FF)), _const_spec((D_FF, D_MODEL)),
            _const_spec((1, D_MODEL)),
            _const_spec((D_MODEL, D_MODEL)), _const_spec((PLE_DIM, D_MODEL)),
            _const_spec((1, D_MODEL)),
        ],
        out_specs=tok(D_MODEL),
        out_shape=jax.ShapeDtypeStruct((n, D_MODEL), F32),
        compiler_params=pltpu.CompilerParams(dimension_semantics=("arbitrary",),
                                             vmem_limit_bytes=VMEM_LIMIT),
        name="back",
    )(x2d, ygm, yml, pe2d, wout, nffn, wup, wdown, nple, wpg, wpp, nfin)


def _const(x):
    return jnp.asarray(np.asarray(x, np.float32), BF16)


def _block_tri(block):
    r = np.arange(CHUNK)[:, None]
    c = np.arange(CHUNK)[None, :]
    return (r // block == c // block) & (c <= r)


def _token_block(n):
    return 512 if n % 512 == 0 else CHUNK


def kernel(x_prompt, x_sample, p_prompt, p_sample, state_C, state_n, state_m, norm_mix, w_in, gm_ln_g,
           gm_ln_b, gm_ws, gm_bs, ml_b_i, ml_b_f, ml_norm, w_out, norm_ffn, w_up, w_down, norm_ple,
           w_ple_gate, w_ple_proj, norm_final):
    depth = w_in.shape[0]
    batch, seq, _ = x_prompt.shape
    dec_batch, dec_seq, _ = x_sample.shape
    n_p, n_s = batch * seq, dec_batch * dec_seq
    n_seq = CHUNK // dec_seq
    assert seq % CHUNK == 0 and CHUNK % dec_seq == 0 and n_s % CHUNK == 0
    tb_p, tb_s = 2 * _token_block(seq), _token_block(n_s)

    hp = x_prompt.reshape(n_p, D_MODEL)
    hs = x_sample.reshape(n_s, D_MODEL)

    tri_p, tri_s = _block_tri(CHUNK), _block_tri(dec_seq)
    triu_p, eye = _const(tri_p.T), _const(np.eye(CHUNK))
    r = np.arange(CHUNK)
    i = np.arange(n_seq)
    sel_last = _const(r[None, :] == (r[:, None] // dec_seq) * dec_seq + dec_seq - 1)
    expand = _const(r[:, None] // dec_seq == i[None, :])
    pick_last = _const(r[None, :] == i[:, None] * dec_seq + dec_seq - 1)
    seq_sum = _const(r[None, :] // dec_seq == i[:, None])

    outs = {k: [] for k in ("Cp", "Np", "Mp", "Vp", "Cs", "Ns", "Ms", "Vs")}
    for l in range(depth):
        row = lambda a: a[l].reshape(1, -1).astype(F32)
        w_t = w_in[l].T.astype(BF16)
        wg_t = w_t[MAIN_COLS:]
        b_i, b_f = ml_b_i[l].astype(F32), ml_b_f[l].astype(F32)
        row_pad = lambda a: jnp.pad(a, ((0, GATE_ROWS - HEADS), (0, 0)))
        wgr = jnp.concatenate([row_pad(wg_t[:HEADS]), row_pad(wg_t[HEADS:])], axis=0)
        bgr = jnp.broadcast_to(jnp.concatenate([row_pad(b_i[:, None]), row_pad(b_f[:, None])], axis=0),
                               (2 * GATE_ROWS, tb_p))
        gain_t = jnp.broadcast_to(ml_norm[l].astype(F32)[:, None], (ML_WIDTH, tb_p))
        wgate = jnp.pad(wg_t.T, ((0, 0), (0, GATE_LANES - 2 * HEADS)))
        wgate_t = jnp.pad(wg_t, ((0, GATE_ROWS - 2 * HEADS), (0, 0)))
        gbias = jnp.concatenate([b_i, b_f])
        bcol = jnp.pad(gbias, (0, GATE_LANES - 2 * HEADS)).reshape(1, GATE_LANES)
        brow = jnp.broadcast_to(jnp.pad(gbias, (0, GATE_ROWS - 2 * HEADS)).reshape(GATE_ROWS, 1),
                                (GATE_ROWS, tb_s))
        ws = gm_ws[l]
        mixw_p = (ws[:, :CHUNK, :CHUNK] * tri_p.astype(np.float32)).astype(BF16)
        mixb_p = jnp.broadcast_to(gm_bs[l][:, :CHUNK, None], (HEADS, CHUNK, CHUNK)).astype(F32)
        mixw_s = (jnp.tile(ws[:, :dec_seq, :dec_seq], (1, n_seq, n_seq)) * tri_s.astype(np.float32)).astype(BF16)
        mixb_s = jnp.broadcast_to(jnp.tile(gm_bs[l][:, :dec_seq], (1, n_seq))[:, :, None],
                                  (HEADS, CHUNK, CHUNK)).astype(F32)
        last = l == depth - 1
        nfin = norm_final.reshape(1, D_MODEL).astype(F32)

        back_f32 = [w[l].astype(F32) for w in (w_out, w_up, w_down, w_ple_gate, w_ple_proj)]
        ygm, yml_t, vlast, cn_t, m8, wout, wup, wdown, wpg, wpp = _mixer_prompt(
            hp, row(norm_mix), w_t, wgr, bgr, row(gm_ln_g), row(gm_ln_b), mixw_p, mixb_p,
            gain_t, triu_p, eye, back_f32, batch, tb_p)

        def back(x2d, ygm, yml, pe, tb):
            return _back(x2d, ygm, yml, pe, wout, row(norm_ffn), wup, wdown, row(norm_ple),
                         wpg, wpp, nfin, last, tb)

        hp = back(hp, ygm, yml_t, p_prompt[l].reshape(n_p, PLE_DIM), tb_p)
        outs["Cp"].append(jnp.swapaxes(cn_t[:, :, :HEAD_DIM, :], -1, -2))
        outs["Np"].append(cn_t[:, :, HEAD_DIM, :])
        outs["Mp"].append(m8[:, :, 0])
        outs["Vp"].append(vlast)

        ygm, q, k, v, og, gcol, grow, vgn = _front(
            hs, row(norm_mix), w_t, wgate, wgate_t, bcol, brow, row(gm_ln_g), row(gm_ln_b), mixw_s, mixb_s,
            tb_s)
        mrep = jnp.repeat(state_m[l].astype(F32), dec_seq, axis=0)
        yml, c_new, n_new, m_new = _mlstm_sample(
            q, k, v, og, gcol, grow, mrep, state_C[l].astype(F32), state_n[l].astype(F32), row(ml_norm),
            _const(tri_s), _const(tri_s.T), sel_last, expand, pick_last, seq_sum, dec_seq)
        hs = back(hs, ygm, yml, p_sample[l].reshape(n_s, PLE_DIM), tb_s)
        outs["Cs"].append(c_new)
        outs["Ns"].append(n_new)
        outs["Ms"].append(m_new[..., 0])
        outs["Vs"].append(vgn.reshape(dec_batch, dec_seq, GM_WIDTH))

    st = lambda k: outs[k][0][None] if depth == 1 else jnp.stack(outs[k])
    return (hp.reshape(batch, seq, D_MODEL), hs.reshape(dec_batch, dec_seq, D_MODEL),
            st("Cp"), st("Np"), st("Mp"), st("Vp"), st("Cs"), st("Ns"), st("Ms"), st("Vs"))
```

```python
import functools

import jax
import numpy as np
import jax.numpy as jnp
from jax import lax
from jax.experimental import pallas as pl
from jax.experimental.pallas import tpu as pltpu

F32 = jnp.float32
BF16 = jnp.bfloat16

D_MODEL = 1024
GM_WIDTH = 512
ML_WIDTH = 512
HEADS = 4
HEAD_DIM = 128
D_FF = 4096
PLE_DIM = 256
EPS = 1e-6
CHUNK = 128
MAIN_COLS = 2 * GM_WIDTH + 4 * ML_WIDTH
Q_ROW = 2 * GM_WIDTH
K_ROW = Q_ROW + ML_WIDTH
V_ROW = K_ROW + ML_WIDTH
O_ROW = V_ROW + ML_WIDTH
W_ROWS = MAIN_COLS + 2 * HEADS
LANES_V = ML_WIDTH
LANES_O = 2 * ML_WIDTH
LANES_G = 3 * ML_WIDTH
LANES_ROWS = LANES_G + 2 * 16
GATE_LANES = 128
GATE_ROWS = 16
STATE_ROWS = HEAD_DIM + 16
FF_SPLIT = 4
BF16_ROWS = 16
VMEM_LIMIT = 60 * 1024 * 1024


def _dot(a, b):
    return jnp.dot(a, b, preferred_element_type=F32)


def _dot_nt(a, b):
    return lax.dot_general(a, b, (((1,), (1,)), ((), ())), preferred_element_type=F32)


def _dot_tn(a, b):
    return lax.dot_general(a, b, (((0,), (0,)), ((), ())), preferred_element_type=F32)


def _split3(x):
    x1 = x.astype(BF16)
    r = x - x1.astype(F32)
    x2 = r.astype(BF16)
    r = r - x2.astype(F32)
    return x1, x2, r.astype(BF16)


def _sel_dot(sel, x):
    p1, p2, p3 = _split3(x)
    return _dot(sel, p1) + _dot(sel, p2) + _dot(sel, p3)


def _dot_sel(x, sel):
    p1, p2, p3 = _split3(x)
    return _dot(p1, sel) + _dot(p2, sel) + _dot(p3, sel)


def _sel_dot_nt(sel, x):
    p1, p2, p3 = _split3(x)
    return _dot_nt(sel, p1) + _dot_nt(sel, p2) + _dot_nt(sel, p3)


def _rms(x, g):
    return x * lax.rsqrt(jnp.mean(x * x, axis=-1, keepdims=True) + EPS) * g


def _log_sigmoid(x):
    return -(jnp.maximum(-x, 0.0) + jnp.log1p(jnp.exp(-jnp.abs(x))))


def _const_spec(shape):
    nd = len(shape)
    return pl.BlockSpec(shape, lambda *_: (0,) * nd, pipeline_mode=pl.Buffered(1))


def _gmlp_pieces(x_ref, nmix_ref, w_t_ref, lng_ref, lnb_ref, mixw_ref, mixb_ref, ygm_ref, vgn_ref, st):
    tb = x_ref.shape[0]

    def norm():
        st["a"] = _rms(x_ref[...], nmix_ref[...]).astype(BF16)

    def u_mm():
        st["u"] = _dot_nt(st["a"], w_t_ref[:GM_WIDTH])

    def u_act():
        st["u"] = jax.nn.gelu(st["u"])

    def v_mm():
        st["vg"] = _dot_nt(st["a"], w_t_ref[GM_WIDTH:2 * GM_WIDTH])

    def v_act():
        vg = jax.nn.gelu(st.pop("vg"))
        mu = jnp.mean(vg, axis=-1, keepdims=True)
        var = jnp.mean(jnp.square(vg - mu), axis=-1, keepdims=True)
        vgn = (vg - mu) * lax.rsqrt(var + EPS) * lng_ref[...] + lnb_ref[...]
        vgn_ref[...] = vgn[tb - vgn_ref.shape[0]:, :]
        st["vgb"] = vgn.astype(BF16)

    def mix(c):
        rows = slice(c * CHUNK, (c + 1) * CHUNK)
        for h in range(HEADS):
            cols = slice(h * HEAD_DIM, (h + 1) * HEAD_DIM)
            s = _dot(mixw_ref[h], st["vgb"][rows, cols]) + mixb_ref[h]
            ygm_ref[rows, cols] = (st["u"][rows, cols] * s).astype(BF16)

    return [norm, u_mm, u_act, v_mm, v_act] + [functools.partial(mix, c) for c in range(tb // CHUNK)]


def _mixer_prompt_kernel(steps_per_seq, n_cast, x_ref, nmix_ref, w_t_ref, wgr_ref, bgr_ref,
                         lng_ref, lnb_ref, mixw_ref, mixb_ref, gain_t_ref, triu_ref, eye_ref, *rest):
    cast_in, rest = rest[:n_cast], rest[n_cast:]
    ygm_ref, yml_t_ref, vlast_ref, cn_out_ref, m_out_ref = rest[:5]
    cast_out, (cn_scr, m_scr, w_lanes_scr) = rest[5:5 + n_cast], rest[5 + n_cast:]

    @pl.when(pl.program_id(0) == 0)
    def _():
        w_lanes_scr[:LANES_V] = w_t_ref[Q_ROW:K_ROW]
        w_lanes_scr[LANES_V:LANES_G] = w_t_ref[V_ROW:MAIN_COLS]
        w_lanes_scr[LANES_G:] = wgr_ref[...]

    for src, dst in zip(cast_in, cast_out):
        dst[...] = src[...].astype(BF16)

    tb = x_ref.shape[0]
    n_chunks = tb // CHUNK
    blocks = [slice(c * CHUNK, (c + 1) * CHUNK) for c in range(n_chunks)]
    heads = [slice(h * HEAD_DIM, (h + 1) * HEAD_DIM) for h in range(HEADS)]
    unit_ids = [(c, h) for c in range(n_chunks) for h in range(HEADS)]
    new_seq = pl.program_id(0) % steps_per_seq == 0
    triu = triu_ref[...]
    mask_t = triu.astype(F32) > 0.0
    eye = eye_ref[...]
    lane_g = lax.broadcasted_iota(jnp.int32, (GATE_ROWS, tb), 1) % CHUNK
    sub_g = lax.broadcasted_iota(jnp.int32, (GATE_ROWS, CHUNK), 0)
    ones_row = (sub_g == 0).astype(BF16)
    last = slice(CHUNK - 1, CHUNK)

    st = {}
    norm, u_mm, u_act, v_mm, v_act, *mix = _gmlp_pieces(x_ref, nmix_ref, w_t_ref, lng_ref, lnb_ref, mixw_ref,
                                                        mixb_ref, ygm_ref, vlast_ref, st)
    zero = jnp.zeros((HEAD_DIM, HEAD_DIM), BF16)

    def block_diag(x, y):
        return jnp.concatenate([jnp.concatenate([x, zero], axis=1), jnp.concatenate([zero, y], axis=1)], axis=0)

    def pair_dot(lhs, rhs):
        out = _dot(jnp.concatenate(lhs, axis=1), block_diag(*rhs))
        return out[:, :HEAD_DIM], out[:, HEAD_DIM:]

    norm()
    a = st["a"]
    on_lanes = _dot_nt(w_lanes_scr[...], a)
    u_mm()
    v_mm()
    k = (_dot_nt(a, w_t_ref[K_ROW:V_ROW]) * (HEAD_DIM ** -0.5)).astype(BF16)
    zr = on_lanes[LANES_G:] + bgr_ref[...]
    lf_r = _log_sigmoid(zr[GATE_ROWS:])
    b_r = jnp.concatenate([_dot_sel(lf_r[:, t], triu) for t in blocks], axis=1)
    q_t = on_lanes[:ML_WIDTH].astype(BF16)
    r_r = zr[:GATE_ROWS] - b_r
    p_r = r_r
    shift = 1
    while shift < CHUNK:
        p_r = jnp.maximum(p_r, jnp.where(lane_g >= shift, pltpu.roll(p_r, shift, axis=1), -jnp.inf))
        shift *= 2
    e_end = [jnp.exp(r_r[:, t] - p_r[:, t][:, last]) for t in blocks]
    u_act()
    v_t = on_lanes[LANES_V:LANES_O]
    r_c = [_sel_dot_nt(eye, r_r[:, t]) for t in blocks]
    og_t = jax.nn.sigmoid(on_lanes[LANES_O:LANES_G]) * gain_t_ref[...]
    v_act()

    pair_ids = [(c, h) for c in range(n_chunks) for h in range(0, HEADS, 2)]
    kq, e_intra, x1, upd = {}, {}, {}, {}
    for c, h in unit_ids:
        e_intra[c, h] = jnp.where(mask_t, jnp.exp(r_c[c][:, h:h + 1] - p_r[h:h + 1, blocks[c]]), 0.0)
    for piece in mix:
        piece()
    for c, h in pair_ids:
        t = blocks[c]
        kq[c, h], kq[c, h + 1] = pair_dot([k[t, heads[h]], k[t, heads[h + 1]]],
                                          [q_t[heads[h], t], q_t[heads[h + 1], t]])
    for c, h in pair_ids:
        t = blocks[c]
        s0, vext, vw = [], [], []
        for g in (h, h + 1):
            s0.append((kq[c, g] * e_intra[c, g]).astype(BF16))
            vext.append(jnp.concatenate([v_t[heads[g], t].astype(BF16), ones_row], axis=0))
            e_row = e_end[c][g:g + 1, :]
            vw.append(jnp.concatenate([(v_t[heads[g], t] * e_row).astype(BF16),
                                       jnp.where(sub_g == 0, e_row, 0.0).astype(BF16)], axis=0))
        x1[c, h], x1[c, h + 1] = pair_dot(vext, s0)
        upd[c, h], upd[c, h + 1] = pair_dot(vw, [k[t, heads[h]], k[t, heads[h + 1]]])

    cn_in, cm, w_inter = {}, {}, {}
    for h in range(HEADS):
        cn = jnp.where(new_seq, 0.0, cn_scr[h])
        m_prev = jnp.where(new_seq, 0.0, m_scr[h:h + 1, 0:1])
        for c in range(n_chunks):
            prow = p_r[h:h + 1, blocks[c]]
            cn_in[c, h] = cn
            cm[c, h] = jnp.maximum(m_prev, prow)
            w_inter[c, h] = jnp.exp(m_prev - cm[c, h])
            cm_last = cm[c, h][:, last]
            cn = jnp.exp(m_prev - cm_last) * cn + jnp.exp(prow[:, last] - cm_last) * upd[c, h]
            m_prev = b_r[h:h + 1, blocks[c]][:, last] + cm_last
        cn_scr[h] = cn
        m_scr[h:h + 1, :] = jnp.broadcast_to(m_prev, (1, HEAD_DIM))
    x2 = {}

    def state_matmuls(c):
        t = blocks[c]
        for h in range(0, HEADS, 2):
            x2[c, h], x2[c, h + 1] = pair_dot([cn_in[c, h].astype(BF16), cn_in[c, h + 1].astype(BF16)],
                                              [q_t[heads[h], t], q_t[heads[h + 1], t]])

    def head_outputs(c):
        t = blocks[c]
        for h, hd in enumerate(heads):
            brow, prow = b_r[h:h + 1, t], p_r[h:h + 1, t]
            nd = x1[c, h] * jnp.exp(prow - cm[c, h]) + x2[c, h] * w_inter[c, h]
            num_t, den = nd[:HEAD_DIM], nd[HEAD_DIM:HEAD_DIM + 1]
            inv = 1.0 / jnp.maximum(jnp.abs(den), jnp.exp(-(brow + cm[c, h])))
            ssq = jnp.sum(num_t * num_t, axis=0, keepdims=True)
            scale = inv * lax.rsqrt(ssq * (inv * inv) * (1.0 / HEAD_DIM) + EPS)
            yml_t_ref[hd, t] = (num_t * scale * og_t[hd, t]).astype(BF16)

    for c in range(n_chunks):
        state_matmuls(c)
    for c in range(n_chunks):
        head_outputs(c)

    @pl.when(pl.program_id(0) % steps_per_seq == steps_per_seq - 1)
    def _():
        cn_out_ref[...] = cn_scr[...]
        m_out_ref[...] = m_scr[...]


def _mixer_prompt(x2d, nmix, w_t, wgr, bgr, lng, lnb, mixw, mixb, gain_t, triu, eye, cast, batch, tb):
    n = x2d.shape[0]
    n_steps = n // tb
    steps_per_seq = n_steps // batch

    def window(w):
        rows = max(BF16_ROWS, w.shape[0] // n_steps)
        return pl.BlockSpec((rows, w.shape[1]), lambda i: (jnp.minimum(i, w.shape[0] // rows - 1), 0))

    tok = lambda w: pl.BlockSpec((tb, w), lambda i: (i, 0))
    per_seq = lambda *shape: pl.BlockSpec((None,) + shape, lambda i: (i // steps_per_seq,) + (0,) * len(shape))
    return pl.pallas_call(
        functools.partial(_mixer_prompt_kernel, steps_per_seq, len(cast)),
        grid=(n_steps,),
        in_specs=[
            tok(D_MODEL),
            _const_spec((1, D_MODEL)),
            _const_spec((W_ROWS, D_MODEL)),
            _const_spec((2 * GATE_ROWS, D_MODEL)),
            _const_spec((2 * GATE_ROWS, tb)),
            _const_spec((1, GM_WIDTH)),
            _const_spec((1, GM_WIDTH)),
            _const_spec((HEADS, CHUNK, CHUNK)),
            _const_spec((HEADS, CHUNK, CHUNK)),
            _const_spec((ML_WIDTH, tb)),
            _const_spec((CHUNK, CHUNK)),
            _const_spec((CHUNK, CHUNK)),
        ] + [window(w) for w in cast],
        out_specs=(tok(GM_WIDTH), pl.BlockSpec((ML_WIDTH, tb), lambda i: (0, i)), per_seq(CHUNK, GM_WIDTH),
                   per_seq(HEADS, STATE_ROWS, HEAD_DIM), per_seq(HEADS, HEAD_DIM)) + tuple(window(w) for w in cast),
        out_shape=(
            jax.ShapeDtypeStruct((n, GM_WIDTH), BF16),
            jax.ShapeDtypeStruct((ML_WIDTH, n), BF16),
            jax.ShapeDtypeStruct((batch, CHUNK, GM_WIDTH), F32),
            jax.ShapeDtypeStruct((batch, HEADS, STATE_ROWS, HEAD_DIM), F32),
            jax.ShapeDtypeStruct((batch, HEADS, HEAD_DIM), F32),
        ) + tuple(jax.ShapeDtypeStruct(w.shape, BF16) for w in cast),
        scratch_shapes=[pltpu.VMEM((HEADS, STATE_ROWS, HEAD_DIM), F32), pltpu.VMEM((HEADS, HEAD_DIM), F32),
                        pltpu.VMEM((LANES_ROWS, D_MODEL), BF16)],
        compiler_params=pltpu.CompilerParams(dimension_semantics=("arbitrary",),
                                             vmem_limit_bytes=VMEM_LIMIT),
        name="mixer_prompt",
    )(x2d, nmix, w_t, wgr, bgr, lng, lnb, mixw, mixb, gain_t, triu, eye, *cast)


def _front_kernel(x_ref, nmix_ref, w_t_ref, wgate_ref, wgate_t_ref, bcol_ref, brow_ref,
                  lng_ref, lnb_ref, mixw_ref, mixb_ref,
                  ygm_ref, q_ref, k_ref, v_ref, og_ref, gcol_ref, grow_ref, vgn_ref):
    st = {}
    for piece in _gmlp_pieces(x_ref, nmix_ref, w_t_ref, lng_ref, lnb_ref, mixw_ref, mixb_ref,
                              ygm_ref, vgn_ref, st):
        piece()
    a = st["a"]

    def proj(lo):
        return _dot_nt(a, w_t_ref[lo:lo + ML_WIDTH])

    q_ref[...] = proj(Q_ROW).astype(BF16)
    k_ref[...] = (proj(K_ROW) * (HEAD_DIM ** -0.5)).astype(BF16)
    v_ref[...] = proj(V_ROW).astype(BF16)
    og_ref[...] = jax.nn.sigmoid(proj(O_ROW))
    zc = _dot(a, wgate_ref[...]) + bcol_ref[...]
    lane = lax.broadcasted_iota(jnp.int32, zc.shape, 1)
    gcol_ref[...] = jnp.where(lane >= HEADS, _log_sigmoid(zc), zc)
    zr = _dot_nt(wgate_t_ref[...], a) + brow_ref[...]
    sub = lax.broadcasted_iota(jnp.int32, zr.shape, 0)
    grow_ref[...] = jnp.where(sub >= HEADS, _log_sigmoid(zr), zr)


def _front(x2d, nmix, w_t, wgate, wgate_t, bcol, brow, lng, lnb, mixw, mixb, tb):
    n = x2d.shape[0]
    tok = lambda w: pl.BlockSpec((tb, w), lambda i: (i, 0))
    out_shape = (
        jax.ShapeDtypeStruct((n, GM_WIDTH), BF16),
        jax.ShapeDtypeStruct((n, ML_WIDTH), BF16),
        jax.ShapeDtypeStruct((n, ML_WIDTH), BF16),
        jax.ShapeDtypeStruct((n, ML_WIDTH), BF16),
        jax.ShapeDtypeStruct((n, ML_WIDTH), F32),
        jax.ShapeDtypeStruct((n, GATE_LANES), F32),
        jax.ShapeDtypeStruct((GATE_ROWS, n), F32),
        jax.ShapeDtypeStruct((n, GM_WIDTH), F32),
    )
    return pl.pallas_call(
        _front_kernel,
        grid=(n // tb,),
        in_specs=[
            tok(D_MODEL),
            _const_spec((1, D_MODEL)),
            _const_spec((W_ROWS, D_MODEL)),
            _const_spec((D_MODEL, GATE_LANES)),
            _const_spec((GATE_ROWS, D_MODEL)),
            _const_spec((1, GATE_LANES)),
            _const_spec((GATE_ROWS, tb)),
            _const_spec((1, GM_WIDTH)),
            _const_spec((1, GM_WIDTH)),
            _const_spec((HEADS, CHUNK, CHUNK)),
            _const_spec((HEADS, CHUNK, CHUNK)),
        ],
        out_specs=(tok(GM_WIDTH), tok(ML_WIDTH), tok(ML_WIDTH), tok(ML_WIDTH), tok(ML_WIDTH),
                   tok(GATE_LANES), pl.BlockSpec((GATE_ROWS, tb), lambda i: (0, i)), tok(GM_WIDTH)),
        out_shape=out_shape,
        compiler_params=pltpu.CompilerParams(dimension_semantics=("arbitrary",),
                                             vmem_limit_bytes=VMEM_LIMIT),
        name="front",
    )(x2d, nmix, w_t, wgate, wgate_t, bcol, brow, lng, lnb, mixw, mixb)


def _intra(q, ks, igcol, bcol, igrow, brow, mprev, mask):
    d = bcol + (igrow - brow)
    g = bcol + mprev
    m_t = jnp.maximum(g, jnp.max(jnp.where(mask, d, -jnp.inf), axis=-1, keepdims=True))
    w_intra = jnp.where(mask, jnp.exp(d - m_t), 0.0)
    w_inter = jnp.exp(g - m_t)
    s = _dot_nt(q, ks) * w_intra
    return s, w_inter, m_t, g


def _head_out(num, den, m_t, gain, og):
    hh = num / jnp.maximum(jnp.abs(den), jnp.exp(-m_t))
    return (og * _rms(hh, gain)).astype(BF16)


def _mlstm_sample_kernel(seq_len, q_ref, k_ref, v_ref, og_ref, gcol_ref, grow_ref, mrep_ref, c_ref, n_ref,
                         mln_ref, tri_ref, triu_ref, sel_last_ref, expand_ref, pick_last_ref, seq_sum_ref,
                         yml_ref, c_out_ref, n_out_ref, m_out_ref):
    n_seq = CHUNK // seq_len
    tri = tri_ref[...]
    mask = tri.astype(F32) > 0.0
    gcol = gcol_ref[...]
    grow = grow_ref[...]
    bcol_all = _sel_dot(tri, gcol)
    brow_all = _dot_sel(grow, triu_ref[...])
    sel_last = sel_last_ref[...]
    expand = expand_ref[...]
    pick_last = pick_last_ref[...]
    seq_sum = seq_sum_ref[...]
    row = lax.broadcasted_iota(jnp.int32, (CHUNK, HEAD_DIM), 0)
    lane = lax.broadcasted_iota(jnp.int32, (CHUNK, HEAD_DIM), 1)
    seq_rows = [(row >= i * seq_len) & (row < (i + 1) * seq_len) for i in range(n_seq)]
    head_cols = [slice(h * HEAD_DIM, (h + 1) * HEAD_DIM) for h in range(HEADS)]
    every_head = range(HEADS)
    q = [q_ref[:, cols] for cols in head_cols]
    ks = [k_ref[:, cols] for cols in head_cols]
    v = [v_ref[:, cols] for cols in head_cols]
    zero = jnp.zeros_like(q[0])
    igcol = [gcol[:, h:h + 1] for h in every_head]
    bcol = [bcol_all[:, HEADS + h:HEADS + h + 1] for h in every_head]
    intra = [_intra(q[h], ks[h], igcol[h], bcol[h], grow[h:h + 1, :], brow_all[HEADS + h:HEADS + h + 1, :],
                    mrep_ref[:, h:h + 1], mask) for h in every_head]
    s, w_inter, m_t, g = zip(*intra)
    c_all = [c_ref[:, h] for h in every_head]
    n_all = [n_ref[:, h] for h in every_head]
    qc = [_dot(jnp.concatenate([jnp.where(m, q[h], zero) for m in seq_rows], axis=1),
               c_all[h].reshape(n_seq * HEAD_DIM, HEAD_DIM).astype(BF16)) for h in every_head]
    n_rows = [_sel_dot(expand, n_all[h]) for h in every_head]
    ends = [_sel_dot(sel_last, jnp.where(lane == 0, m_t[h], jnp.where(lane == 1, g[h],
                                                                        jnp.where(lane == 2, bcol[h], 0.0))))
            for h in every_head]
    sv = [_dot(s[h].astype(BF16), v[h]) for h in every_head]
    kw, dec, upd, dec_seq, n_inc, m_seq = [], [], [], [], [], []
    for h in every_head:
        m_new, g_last, b_last = ends[h][:, 0:1], ends[h][:, 1:2], ends[h][:, 2:3]
        w_end = jnp.exp(b_last - bcol[h] + igcol[h] - m_new)
        dec.append(jnp.exp(g_last - m_new))
        kw.append(ks[h].astype(F32) * w_end)
    for h in every_head:
        v_exp = jnp.concatenate([jnp.where(m, v[h], zero) for m in seq_rows], axis=1)
        upd.append(_dot(kw[h].T.astype(BF16), v_exp))
        dec_seq.append(_sel_dot(pick_last, jnp.broadcast_to(dec[h], (CHUNK, HEAD_DIM))))
        n_inc.append(_sel_dot(seq_sum, kw[h]))
        m_seq.append(_sel_dot(pick_last, jnp.broadcast_to(m_t[h], (CHUNK, HEAD_DIM))))
    for h, cols in enumerate(head_cols):
        qn = jnp.sum(q[h].astype(F32) * n_rows[h], axis=-1, keepdims=True)
        num = sv[h] + w_inter[h] * qc[h]
        den = jnp.sum(s[h], axis=-1, keepdims=True) + w_inter[h] * qn
        yml_ref[:, cols] = _head_out(num, den, m_t[h], mln_ref[:, cols], og_ref[:, cols])
        for i in range(n_seq):
            c_out_ref[i, h] = (dec_seq[h][i:i + 1, 0:1] * c_all[h][i]
                               + upd[h][:, i * HEAD_DIM:(i + 1) * HEAD_DIM])
        n_out_ref[:, h] = dec_seq[h] * n_all[h] + n_inc[h]
        m_out_ref[:, h] = m_seq[h]


def _mlstm_sample(q, k, v, og, gcol, grow, mrep, c0, n0, mln, tri, triu, sel_last, expand, pick_last,
                  seq_sum, seq_len):
    n = q.shape[0]
    n_seq = CHUNK // seq_len
    n_batch = n // seq_len
    tok = lambda w: pl.BlockSpec((CHUNK, w), lambda i: (i, 0))
    c_spec = pl.BlockSpec((n_seq, HEADS, HEAD_DIM, HEAD_DIM), lambda i: (i, 0, 0, 0))
    n_spec = pl.BlockSpec((n_seq, HEADS, HEAD_DIM), lambda i: (i, 0, 0))
    return pl.pallas_call(
        functools.partial(_mlstm_sample_kernel, seq_len),
        grid=(n // CHUNK,),
        in_specs=[
            tok(ML_WIDTH), tok(ML_WIDTH), tok(ML_WIDTH), tok(ML_WIDTH), tok(GATE_LANES),
            pl.BlockSpec((GATE_ROWS, CHUNK), lambda i: (0, i)),
            pl.BlockSpec((CHUNK, HEADS), lambda i: (i, 0)),
            c_spec, n_spec,
            _const_spec((1, ML_WIDTH)),
            _const_spec((CHUNK, CHUNK)), _const_spec((CHUNK, CHUNK)), _const_spec((CHUNK, CHUNK)),
            _const_spec((CHUNK, n_seq)), _const_spec((n_seq, CHUNK)), _const_spec((n_seq, CHUNK)),
        ],
        out_specs=(tok(ML_WIDTH), c_spec, n_spec, n_spec),
        out_shape=(
            jax.ShapeDtypeStruct((n, ML_WIDTH), BF16),
            jax.ShapeDtypeStruct((n_batch, HEADS, HEAD_DIM, HEAD_DIM), F32),
            jax.ShapeDtypeStruct((n_batch, HEADS, HEAD_DIM), F32),
            jax.ShapeDtypeStruct((n_batch, HEADS, HEAD_DIM), F32),
        ),
        compiler_params=pltpu.CompilerParams(dimension_semantics=("arbitrary",),
                                             vmem_limit_bytes=VMEM_LIMIT),
        name="mlstm_sample",
    )(q, k, v, og, gcol, grow, mrep, c0, n0, mln, tri, triu, sel_last, expand, pick_last, seq_sum)


def _back_math(final_norm, yml_transposed, x_ref, ygm_ref, yml_ref, pe_ref, wout_ref, nffn_ref,
               wup_ref, wdown_ref, nple_ref, wpg_ref, wpp_ref, nfin_ref, out_ref):
    ml_dot = _dot_tn if yml_transposed else _dot
    h = x_ref[...] + _dot(ygm_ref[...], wout_ref[:GM_WIDTH]) + ml_dot(yml_ref[...], wout_ref[GM_WIDTH:])
    a = _rms(h, nffn_ref[...]).astype(BF16)
    ff = D_FF // FF_SPLIT

    def mlp_part(c):
        f = _dot(a, wup_ref[:, c * ff:(c + 1) * ff])
        f = jnp.square(jnp.maximum(f, 0.0)).astype(BF16)
        return _dot(f, wdown_ref[c * ff:(c + 1) * ff, :])

    mlp = mlp_part(0)
    for c in range(1, FF_SPLIT):
        mlp = mlp + mlp_part(c)
    h = h + mlp
    gate = jax.nn.sigmoid(_dot(_rms(h, nple_ref[...]).astype(BF16), wpg_ref[...]))
    h = h + gate * _dot(pe_ref[...].astype(BF16), wpp_ref[...])
    if final_norm:
        h = _rms(h, nfin_ref[...])
    out_ref[...] = h


def _back_kernel(final_norm, yml_transposed, *refs):
    _back_math(final_norm, yml_transposed, *refs)


def _back(x2d, ygm, yml, pe2d, wout, nffn, wup, wdown, nple, wpg, wpp, nfin, final_norm, tb):
    n = x2d.shape[0]
    yml_transposed = yml.shape[0] != n
    tok = lambda w: pl.BlockSpec((tb, w), lambda i: (i, 0))
    yml_spec = pl.BlockSpec((ML_WIDTH, tb), lambda i: (0, i)) if yml_transposed else tok(ML_WIDTH)
    return pl.pallas_call(
        functools.partial(_back_kernel, final_norm, yml_transposed),
        grid=(n // tb,),
        in_specs=[
            tok(D_MODEL), tok(GM_WIDTH), yml_spec, tok(PLE_DIM),
            _const_spec((GM_WIDTH + ML_WIDTH, D_MODEL)),
            _const_spec((1, D_MODEL)),
            _const_spec((D_MODEL, D_FF)), _const_spec((D_FF, D_MODEL)),
            _const_spec((1, D_MODEL)),
            _const_spec((D_MODEL, D_MODEL)), _const_spec((PLE_DIM, D_MODEL)),
            _const_spec((1, D_MODEL)),
        ],
        out_specs=tok(D_MODEL),
        out_shape=jax.ShapeDtypeStruct((n, D_MODEL), F32),
        compiler_params=pltpu.CompilerParams(dimension_semantics=("arbitrary",),
                                             vmem_limit_bytes=VMEM_LIMIT),
        name="back",
    )(x2d, ygm, yml, pe2d, wout, nffn, wup, wdown, nple, wpg, wpp, nfin)


def _const(x):
    return jnp.asarray(np.asarray(x, np.float32), BF16)


def _block_tri(block):
    r = np.arange(CHUNK)[:, None]
    c = np.arange(CHUNK)[None, :]
    return (r // block == c // block) & (c <= r)


def _token_block(n):
    return 512 if n % 512 == 0 else CHUNK


def kernel(x_prompt, x_sample, p_prompt, p_sample, state_C, state_n, state_m, norm_mix, w_in, gm_ln_g,
           gm_ln_b, gm_ws, gm_bs, ml_b_i, ml_b_f, ml_norm, w_out, norm_ffn, w_up, w_down, norm_ple,
           w_ple_gate, w_ple_proj, norm_final):
    depth = w_in.shape[0]
    batch, seq, _ = x_prompt.shape
    dec_batch, dec_seq, _ = x_sample.shape
    n_p, n_s = batch * seq, dec_batch * dec_seq
    n_seq = CHUNK // dec_seq
    assert seq % CHUNK == 0 and CHUNK % dec_seq == 0 and n_s % CHUNK == 0
    tb_p, tb_s = 2 * _token_block(seq), _token_block(n_s)

    hp = x_prompt.reshape(n_p, D_MODEL)
    hs = x_sample.reshape(n_s, D_MODEL)

    tri_p, tri_s = _block_tri(CHUNK), _block_tri(dec_seq)
    triu_p, eye = _const(tri_p.T), _const(np.eye(CHUNK))
    r = np.arange(CHUNK)
    i = np.arange(n_seq)
    sel_last = _const(r[None, :] == (r[:, None] // dec_seq) * dec_seq + dec_seq - 1)
    expand = _const(r[:, None] // dec_seq == i[None, :])
    pick_last = _const(r[None, :] == i[:, None] * dec_seq + dec_seq - 1)
    seq_sum = _const(r[None, :] // dec_seq == i[:, None])

    outs = {k: [] for k in ("Cp", "Np", "Mp", "Vp", "Cs", "Ns", "Ms", "Vs")}
    for l in range(depth):
        row = lambda a: a[l].reshape(1, -1).astype(F32)
        w_t = w_in[l].T.astype(BF16)
        wg_t = w_t[MAIN_COLS:]
        b_i, b_f = ml_b_i[l].astype(F32), ml_b_f[l].astype(F32)
        row_pad = lambda a: jnp.pad(a, ((0, GATE_ROWS - HEADS), (0, 0)))
        wgr = jnp.concatenate([row_pad(wg_t[:HEADS]), row_pad(wg_t[HEADS:])], axis=0)
        bgr = jnp.broadcast_to(jnp.concatenate([row_pad(b_i[:, None]), row_pad(b_f[:, None])], axis=0),
                               (2 * GATE_ROWS, tb_p))
        gain_t = jnp.broadcast_to(ml_norm[l].astype(F32)[:, None], (ML_WIDTH, tb_p))
        wgate = jnp.pad(wg_t.T, ((0, 0), (0, GATE_LANES - 2 * HEADS)))
        wgate_t = jnp.pad(wg_t, ((0, GATE_ROWS - 2 * HEADS), (0, 0)))
        gbias = jnp.concatenate([b_i, b_f])
        bcol = jnp.pad(gbias, (0, GATE_LANES - 2 * HEADS)).reshape(1, GATE_LANES)
        brow = jnp.broadcast_to(jnp.pad(gbias, (0, GATE_ROWS - 2 * HEADS)).reshape(GATE_ROWS, 1),
                                (GATE_ROWS, tb_s))
        ws = gm_ws[l]
        mixw_p = (ws[:, :CHUNK, :CHUNK] * tri_p.astype(np.float32)).astype(BF16)
        mixb_p = jnp.broadcast_to(gm_bs[l][:, :CHUNK, None], (HEADS, CHUNK, CHUNK)).astype(F32)
        mixw_s = (jnp.tile(ws[:, :dec_seq, :dec_seq], (1, n_seq, n_seq)) * tri_s.astype(np.float32)).astype(BF16)
        mixb_s = jnp.broadcast_to(jnp.tile(gm_bs[l][:, :dec_seq], (1, n_seq))[:, :, None],
                                  (HEADS, CHUNK, CHUNK)).astype(F32)
        last = l == depth - 1
        nfin = norm_final.reshape(1, D_MODEL).astype(F32)

        back_f32 = [w[l].astype(F32) for w in (w_out, w_up, w_down, w_ple_gate, w_ple_proj)]
        ygm, yml_t, vlast, cn_t, m8, wout, wup, wdown, wpg, wpp = _mixer_prompt(
            hp, row(norm_mix), w_t, wgr, bgr, row(gm_ln_g), row(gm_ln_b), mixw_p, mixb_p,
            gain_t, triu_p, eye, back_f32, batch, tb_p)

        def back(x2d, ygm, yml, pe, tb):
            return _back(x2d, ygm, yml, pe, wout, row(norm_ffn), wup, wdown, row(norm_ple),
                         wpg, wpp, nfin, last, tb)

        hp = back(hp, ygm, yml_t, p_prompt[l].reshape(n_p, PLE_DIM), tb_p)
        outs["Cp"].append(jnp.swapaxes(cn_t[:, :, :HEAD_DIM, :], -1, -2))
        outs["Np"].append(cn_t[:, :, HEAD_DIM, :])
        outs["Mp"].append(m8[:, :, 0])
        outs["Vp"].append(vlast)

        ygm, q, k, v, og, gcol, grow, vgn = _front(
            hs, row(norm_mix), w_t, wgate, wgate_t, bcol, brow, row(gm_ln_g), row(gm_ln_b), mixw_s, mixb_s,
            tb_s)
        mrep = jnp.repeat(state_m[l].astype(F32), dec_seq, axis=0)
        yml, c_new, n_new, m_new = _mlstm_sample(
            q, k, v, og, gcol, grow, mrep, state_C[l].astype(F32), state_n[l].astype(F32), row(ml_norm),
            _const(tri_s), _const(tri_s.T), sel_last, expand, pick_last, seq_sum, dec_seq)
        hs = back(hs, ygm, yml, p_sample[l].reshape(n_s, PLE_DIM), tb_s)
        outs["Cs"].append(c_new)
        outs["Ns"].append(n_new)
        outs["Ms"].append(m_new[..., 0])
        outs["Vs"].append(vgn.reshape(dec_batch, dec_seq, GM_WIDTH))

    st = lambda k: outs[k][0][None] if depth == 1 else jnp.stack(outs[k])
    return (hp.reshape(batch, seq, D_MODEL), hs.reshape(dec_batch, dec_seq, D_MODEL),
            st("Cp"), st("Np"), st("Mp"), st("Vp"), st("Cs"), st("Ns"), st("Ms"), st("Vs"))
```

```python
import functools

import jax
import numpy as np
import jax.numpy as jnp
from jax import lax
from jax.experimental import pallas as pl
from jax.experimental.pallas import tpu as pltpu

F32 = jnp.float32
BF16 = jnp.bfloat16

D_MODEL = 1024
GM_WIDTH = 512
ML_WIDTH = 512
HEADS = 4
HEAD_DIM = 128
D_FF = 4096
PLE_DIM = 256
EPS = 1e-6
CHUNK = 128
MAIN_COLS = 2 * GM_WIDTH + 4 * ML_WIDTH
Q_ROW = 2 * GM_WIDTH
K_ROW = Q_ROW + ML_WIDTH
V_ROW = K_ROW + ML_WIDTH
O_ROW = V_ROW + ML_WIDTH
W_ROWS = MAIN_COLS + 2 * HEADS
LANES_V = ML_WIDTH
LANES_O = 2 * ML_WIDTH
LANES_G = 3 * ML_WIDTH
LANES_ROWS = LANES_G + 2 * 16
GATE_LANES = 128
GATE_ROWS = 16
STATE_ROWS = HEAD_DIM + 16
FF_SPLIT = 4
BF16_ROWS = 16
VMEM_LIMIT = 60 * 1024 * 1024


def _dot(a, b):
    return jnp.dot(a, b, preferred_element_type=F32)


def _dot_nt(a, b):
    return lax.dot_general(a, b, (((1,), (1,)), ((), ())), preferred_element_type=F32)


def _dot_tn(a, b):
    return lax.dot_general(a, b, (((0,), (0,)), ((), ())), preferred_element_type=F32)


def _split3(x):
    x1 = x.astype(BF16)
    r = x - x1.astype(F32)
    x2 = r.astype(BF16)
    r = r - x2.astype(F32)
    return x1, x2, r.astype(BF16)


def _sel_dot(sel, x):
    p1, p2, p3 = _split3(x)
    return _dot(sel, p1) + _dot(sel, p2) + _dot(sel, p3)


def _dot_sel(x, sel):
    p1, p2, p3 = _split3(x)
    return _dot(p1, sel) + _dot(p2, sel) + _dot(p3, sel)


def _sel_dot_nt(sel, x):
    p1, p2, p3 = _split3(x)
    return _dot_nt(sel, p1) + _dot_nt(sel, p2) + _dot_nt(sel, p3)


def _rms(x, g):
    return x * lax.rsqrt(jnp.mean(x * x, axis=-1, keepdims=True) + EPS) * g


def _log_sigmoid(x):
    return -(jnp.maximum(-x, 0.0) + jnp.log1p(jnp.exp(-jnp.abs(x))))


def _const_spec(shape):
    nd = len(shape)
    return pl.BlockSpec(shape, lambda *_: (0,) * nd, pipeline_mode=pl.Buffered(1))


def _gmlp_pieces(x_ref, nmix_ref, w_t_ref, lng_ref, lnb_ref, mixw_ref, mixb_ref, ygm_ref, vgn_ref, st):
    tb = x_ref.shape[0]

    def norm():
        st["a"] = _rms(x_ref[...], nmix_ref[...]).astype(BF16)

    def u_mm():
        st["u"] = _dot_nt(st["a"], w_t_ref[:GM_WIDTH])

    def u_act():
        st["u"] = jax.nn.gelu(st["u"])

    def v_mm():
        st["vg"] = _dot_nt(st["a"], w_t_ref[GM_WIDTH:2 * GM_WIDTH])

    def v_act():
        vg = jax.nn.gelu(st.pop("vg"))
        mu = jnp.mean(vg, axis=-1, keepdims=True)
        var = jnp.mean(jnp.square(vg - mu), axis=-1, keepdims=True)
        vgn = (vg - mu) * lax.rsqrt(var + EPS) * lng_ref[...] + lnb_ref[...]
        vgn_ref[...] = vgn[tb - vgn_ref.shape[0]:, :]
        st["vgb"] = vgn.astype(BF16)

    def mix(c):
        rows = slice(c * CHUNK, (c + 1) * CHUNK)
        for h in range(HEADS):
            cols = slice(h * HEAD_DIM, (h + 1) * HEAD_DIM)
            s = _dot(mixw_ref[h], st["vgb"][rows, cols]) + mixb_ref[h]
            ygm_ref[rows, cols] = (st["u"][rows, cols] * s).astype(BF16)

    return [norm, u_mm, u_act, v_mm, v_act] + [functools.partial(mix, c) for c in range(tb // CHUNK)]


def _mixer_prompt_kernel(steps_per_seq, n_cast, x_ref, nmix_ref, w_t_ref, wgr_ref, bgr_ref,
                         lng_ref, lnb_ref, mixw_ref, mixb_ref, gain_t_ref, triu_ref, eye_ref, *rest):
    cast_in, rest = rest[:n_cast], rest[n_cast:]
    ygm_ref, yml_t_ref, vlast_ref, cn_out_ref, m_out_ref = rest[:5]
    cast_out, (cn_scr, m_scr, w_lanes_scr) = rest[5:5 + n_cast], rest[5 + n_cast:]

    @pl.when(pl.program_id(0) == 0)
    def _():
        w_lanes_scr[:LANES_V] = w_t_ref[Q_ROW:K_ROW]
        w_lanes_scr[LANES_V:LANES_G] = w_t_ref[V_ROW:MAIN_COLS]
        w_lanes_scr[LANES_G:] = wgr_ref[...]

    for src, dst in zip(cast_in, cast_out):
        dst[...] = src[...].astype(BF16)

    tb = x_ref.shape[0]
    n_chunks = tb // CHUNK
    blocks = [slice(c * CHUNK, (c + 1) * CHUNK) for c in range(n_chunks)]
    heads = [slice(h * HEAD_DIM, (h + 1) * HEAD_DIM) for h in range(HEADS)]
    unit_ids = [(c, h) for c in range(n_chunks) for h in range(HEADS)]
    new_seq = pl.program_id(0) % steps_per_seq == 0
    triu = triu_ref[...]
    mask_t = triu.astype(F32) > 0.0
    eye = eye_ref[...]
    lane_g = lax.broadcasted_iota(jnp.int32, (GATE_ROWS, tb), 1) % CHUNK
    sub_g = lax.broadcasted_iota(jnp.int32, (GATE_ROWS, CHUNK), 0)
    ones_row = (sub_g == 0).astype(BF16)
    last = slice(CHUNK - 1, CHUNK)

    st = {}
    norm, u_mm, u_act, v_mm, v_act, *mix = _gmlp_pieces(x_ref, nmix_ref, w_t_ref, lng_ref, lnb_ref, mixw_ref,
                                                        mixb_ref, ygm_ref, vlast_ref, st)
    zero = jnp.zeros((HEAD_DIM, HEAD_DIM), BF16)

    def block_diag(x, y):
        return jnp.concatenate([jnp.concatenate([x, zero], axis=1), jnp.concatenate([zero, y], axis=1)], axis=0)

    def pair_dot(lhs, rhs):
        out = _dot(jnp.concatenate(lhs, axis=1), block_diag(*rhs))
        return out[:, :HEAD_DIM], out[:, HEAD_DIM:]

    norm()
    a = st["a"]
    on_lanes = _dot_nt(w_lanes_scr[...], a)
    u_mm()
    v_mm()
    k = (_dot_nt(a, w_t_ref[K_ROW:V_ROW]) * (HEAD_DIM ** -0.5)).astype(BF16)
    lanes = lambda ref: jnp.concatenate([ref[...]] * n_chunks, axis=1)
    zr = on_lanes[LANES_G:] + lanes(bgr_ref)
    lf_r = _log_sigmoid(zr[GATE_ROWS:])
    b_r = jnp.concatenate([_dot_sel(lf_r[:, t], triu) for t in blocks], axis=1)
    q_t = on_lanes[:ML_WIDTH].astype(BF16)
    r_r = zr[:GATE_ROWS] - b_r
    p_r = r_r
    shift = 1
    while shift < CHUNK:
        p_r = jnp.maximum(p_r, jnp.where(lane_g >= shift, pltpu.roll(p_r, shift, axis=1), -jnp.inf))
        shift *= 2
    e_end = [jnp.exp(r_r[:, t] - p_r[:, t][:, last]) for t in blocks]
    u_act()
    v_t = on_lanes[LANES_V:LANES_O]
    r_c = [_sel_dot_nt(eye, r_r[:, t]) for t in blocks]
    og_t = jax.nn.sigmoid(on_lanes[LANES_O:LANES_G]) * lanes(gain_t_ref)
    v_act()

    pair_ids = [(c, h) for c in range(n_chunks) for h in range(0, HEADS, 2)]
    kq, e_intra, x1, upd = {}, {}, {}, {}
    for c, h in unit_ids:
        e_intra[c, h] = jnp.where(mask_t, jnp.exp(r_c[c][:, h:h + 1] - p_r[h:h + 1, blocks[c]]), 0.0)
    for piece in mix:
        piece()
    for c, h in pair_ids:
        t = blocks[c]
        kq[c, h], kq[c, h + 1] = pair_dot([k[t, heads[h]], k[t, heads[h + 1]]],
                                          [q_t[heads[h], t], q_t[heads[h + 1], t]])
    for c, h in pair_ids:
        t = blocks[c]
        s0, vext, vw = [], [], []
        for g in (h, h + 1):
            s0.append((kq[c, g] * e_intra[c, g]).astype(BF16))
            vext.append(jnp.concatenate([v_t[heads[g], t].astype(BF16), ones_row], axis=0))
            e_row = e_end[c][g:g + 1, :]
            vw.append(jnp.concatenate([(v_t[heads[g], t] * e_row).astype(BF16),
                                       jnp.where(sub_g == 0, e_row, 0.0).astype(BF16)], axis=0))
        x1[c, h], x1[c, h + 1] = pair_dot(vext, s0)
        upd[c, h], upd[c, h + 1] = pair_dot(vw, [k[t, heads[h]], k[t, heads[h + 1]]])

    cn_in, cm, w_inter = {}, {}, {}
    for h in range(HEADS):
        cn = jnp.where(new_seq, 0.0, cn_scr[h])
        m_prev = jnp.where(new_seq, 0.0, m_scr[h:h + 1, 0:1])
        for c in range(n_chunks):
            prow = p_r[h:h + 1, blocks[c]]
            cn_in[c, h] = cn
            cm[c, h] = jnp.maximum(m_prev, prow)
            w_inter[c, h] = jnp.exp(m_prev - cm[c, h])
            cm_last = cm[c, h][:, last]
            cn = jnp.exp(m_prev - cm_last) * cn + jnp.exp(prow[:, last] - cm_last) * upd[c, h]
            m_prev = b_r[h:h + 1, blocks[c]][:, last] + cm_last
        cn_scr[h] = cn
        m_scr[h:h + 1, :] = jnp.broadcast_to(m_prev, (1, HEAD_DIM))
    x2 = {}

    def state_matmuls(c):
        t = blocks[c]
        for h in range(0, HEADS, 2):
            x2[c, h], x2[c, h + 1] = pair_dot([cn_in[c, h].astype(BF16), cn_in[c, h + 1].astype(BF16)],
                                              [q_t[heads[h], t], q_t[heads[h + 1], t]])

    def head_outputs(c):
        t = blocks[c]
        for h, hd in enumerate(heads):
            brow, prow = b_r[h:h + 1, t], p_r[h:h + 1, t]
            nd = x1[c, h] * jnp.exp(prow - cm[c, h]) + x2[c, h] * w_inter[c, h]
            num_t, den = nd[:HEAD_DIM], nd[HEAD_DIM:HEAD_DIM + 1]
            inv = 1.0 / jnp.maximum(jnp.abs(den), jnp.exp(-(brow + cm[c, h])))
            ssq = jnp.sum(num_t * num_t, axis=0, keepdims=True)
            scale = inv * lax.rsqrt(ssq * (inv * inv) * (1.0 / HEAD_DIM) + EPS)
            yml_t_ref[hd, t] = (num_t * scale * og_t[hd, t]).astype(BF16)

    for c in range(n_chunks):
        state_matmuls(c)
    for c in range(n_chunks):
        head_outputs(c)

    @pl.when(pl.program_id(0) % steps_per_seq == steps_per_seq - 1)
    def _():
        cn_out_ref[...] = cn_scr[...]
        m_out_ref[...] = m_scr[...]


def _mixer_prompt(x2d, nmix, w_t, wgr, bgr, lng, lnb, mixw, mixb, gain_t, triu, eye, cast, batch, tb):
    n = x2d.shape[0]
    n_steps = n // tb
    steps_per_seq = n_steps // batch

    def window(w):
        rows = max(BF16_ROWS, w.shape[0] // n_steps)
        return pl.BlockSpec((rows, w.shape[1]), lambda i: (jnp.minimum(i, w.shape[0] // rows - 1), 0))

    tok = lambda w: pl.BlockSpec((tb, w), lambda i: (i, 0))
    per_seq = lambda *shape: pl.BlockSpec((None,) + shape, lambda i: (i // steps_per_seq,) + (0,) * len(shape))
    return pl.pallas_call(
        functools.partial(_mixer_prompt_kernel, steps_per_seq, len(cast)),
        grid=(n_steps,),
        in_specs=[
            tok(D_MODEL),
            _const_spec((1, D_MODEL)),
            _const_spec((W_ROWS, D_MODEL)),
            _const_spec((2 * GATE_ROWS, D_MODEL)),
            _const_spec((2 * GATE_ROWS, CHUNK)),
            _const_spec((1, GM_WIDTH)),
            _const_spec((1, GM_WIDTH)),
            _const_spec((HEADS, CHUNK, CHUNK)),
            _const_spec((HEADS, CHUNK, CHUNK)),
            _const_spec((ML_WIDTH, CHUNK)),
            _const_spec((CHUNK, CHUNK)),
            _const_spec((CHUNK, CHUNK)),
        ] + [window(w) for w in cast],
        out_specs=(tok(GM_WIDTH), pl.BlockSpec((ML_WIDTH, tb), lambda i: (0, i)), per_seq(CHUNK, GM_WIDTH),
                   per_seq(HEADS, STATE_ROWS, HEAD_DIM), per_seq(HEADS, HEAD_DIM)) + tuple(window(w) for w in cast),
        out_shape=(
            jax.ShapeDtypeStruct((n, GM_WIDTH), BF16),
            jax.ShapeDtypeStruct((ML_WIDTH, n), BF16),
            jax.ShapeDtypeStruct((batch, CHUNK, GM_WIDTH), F32),
            jax.ShapeDtypeStruct((batch, HEADS, STATE_ROWS, HEAD_DIM), F32),
            jax.ShapeDtypeStruct((batch, HEADS, HEAD_DIM), F32),
        ) + tuple(jax.ShapeDtypeStruct(w.shape, BF16) for w in cast),
        scratch_shapes=[pltpu.VMEM((HEADS, STATE_ROWS, HEAD_DIM), F32), pltpu.VMEM((HEADS, HEAD_DIM), F32),
                        pltpu.VMEM((LANES_ROWS, D_MODEL), BF16)],
        compiler_params=pltpu.CompilerParams(dimension_semantics=("arbitrary",),
                                             vmem_limit_bytes=VMEM_LIMIT),
        name="mixer_prompt",
    )(x2d, nmix, w_t, wgr, bgr, lng, lnb, mixw, mixb, gain_t, triu, eye, *cast)


def _front_kernel(x_ref, nmix_ref, w_t_ref, wgate_ref, wgate_t_ref, bcol_ref, brow_ref,
                  lng_ref, lnb_ref, mixw_ref, mixb_ref,
                  ygm_ref, q_ref, k_ref, v_ref, og_ref, gcol_ref, grow_ref, vgn_ref):
    st = {}
    for piece in _gmlp_pieces(x_ref, nmix_ref, w_t_ref, lng_ref, lnb_ref, mixw_ref, mixb_ref,
                              ygm_ref, vgn_ref, st):
        piece()
    a = st["a"]

    def proj(lo):
        return _dot_nt(a, w_t_ref[lo:lo + ML_WIDTH])

    q_ref[...] = proj(Q_ROW).astype(BF16)
    k_ref[...] = (proj(K_ROW) * (HEAD_DIM ** -0.5)).astype(BF16)
    v_ref[...] = proj(V_ROW).astype(BF16)
    og_ref[...] = jax.nn.sigmoid(proj(O_ROW))
    zc = _dot(a, wgate_ref[...]) + bcol_ref[...]
    lane = lax.broadcasted_iota(jnp.int32, zc.shape, 1)
    gcol_ref[...] = jnp.where(lane >= HEADS, _log_sigmoid(zc), zc)
    zr = _dot_nt(wgate_t_ref[...], a) + brow_ref[...]
    sub = lax.broadcasted_iota(jnp.int32, zr.shape, 0)
    grow_ref[...] = jnp.where(sub >= HEADS, _log_sigmoid(zr), zr)


def _front(x2d, nmix, w_t, wgate, wgate_t, bcol, brow, lng, lnb, mixw, mixb, tb):
    n = x2d.shape[0]
    tok = lambda w: pl.BlockSpec((tb, w), lambda i: (i, 0))
    out_shape = (
        jax.ShapeDtypeStruct((n, GM_WIDTH), BF16),
        jax.ShapeDtypeStruct((n, ML_WIDTH), BF16),
        jax.ShapeDtypeStruct((n, ML_WIDTH), BF16),
        jax.ShapeDtypeStruct((n, ML_WIDTH), BF16),
        jax.ShapeDtypeStruct((n, ML_WIDTH), F32),
        jax.ShapeDtypeStruct((n, GATE_LANES), F32),
        jax.ShapeDtypeStruct((GATE_ROWS, n), F32),
        jax.ShapeDtypeStruct((n, GM_WIDTH), F32),
    )
    return pl.pallas_call(
        _front_kernel,
        grid=(n // tb,),
        in_specs=[
            tok(D_MODEL),
            _const_spec((1, D_MODEL)),
            _const_spec((W_ROWS, D_MODEL)),
            _const_spec((D_MODEL, GATE_LANES)),
            _const_spec((GATE_ROWS, D_MODEL)),
            _const_spec((1, GATE_LANES)),
            _const_spec((GATE_ROWS, tb)),
            _const_spec((1, GM_WIDTH)),
            _const_spec((1, GM_WIDTH)),
            _const_spec((HEADS, CHUNK, CHUNK)),
            _const_spec((HEADS, CHUNK, CHUNK)),
        ],
        out_specs=(tok(GM_WIDTH), tok(ML_WIDTH), tok(ML_WIDTH), tok(ML_WIDTH), tok(ML_WIDTH),
                   tok(GATE_LANES), pl.BlockSpec((GATE_ROWS, tb), lambda i: (0, i)), tok(GM_WIDTH)),
        out_shape=out_shape,
        compiler_params=pltpu.CompilerParams(dimension_semantics=("arbitrary",),
                                             vmem_limit_bytes=VMEM_LIMIT),
        name="front",
    )(x2d, nmix, w_t, wgate, wgate_t, bcol, brow, lng, lnb, mixw, mixb)


def _intra(q, ks, igcol, bcol, igrow, brow, mprev, mask):
    d = bcol + (igrow - brow)
    g = bcol + mprev
    m_t = jnp.maximum(g, jnp.max(jnp.where(mask, d, -jnp.inf), axis=-1, keepdims=True))
    w_intra = jnp.where(mask, jnp.exp(d - m_t), 0.0)
    w_inter = jnp.exp(g - m_t)
    s = _dot_nt(q, ks) * w_intra
    return s, w_inter, m_t, g


def _head_out(num, den, m_t, gain, og):
    hh = num / jnp.maximum(jnp.abs(den), jnp.exp(-m_t))
    return (og * _rms(hh, gain)).astype(BF16)


def _mlstm_sample_kernel(seq_len, q_ref, k_ref, v_ref, og_ref, gcol_ref, grow_ref, mrep_ref, c_ref, n_ref,
                         mln_ref, tri_ref, triu_ref, sel_last_ref, expand_ref, pick_last_ref, seq_sum_ref,
                         yml_ref, c_out_ref, n_out_ref, m_out_ref):
    n_seq = CHUNK // seq_len
    tri = tri_ref[...]
    mask = tri.astype(F32) > 0.0
    gcol = gcol_ref[...]
    grow = grow_ref[...]
    bcol_all = _sel_dot(tri, gcol)
    brow_all = _dot_sel(grow, triu_ref[...])
    sel_last = sel_last_ref[...]
    expand = expand_ref[...]
    pick_last = pick_last_ref[...]
    seq_sum = seq_sum_ref[...]
    row = lax.broadcasted_iota(jnp.int32, (CHUNK, HEAD_DIM), 0)
    lane = lax.broadcasted_iota(jnp.int32, (CHUNK, HEAD_DIM), 1)
    seq_rows = [(row >= i * seq_len) & (row < (i + 1) * seq_len) for i in range(n_seq)]
    head_cols = [slice(h * HEAD_DIM, (h + 1) * HEAD_DIM) for h in range(HEADS)]
    every_head = range(HEADS)
    q = [q_ref[:, cols] for cols in head_cols]
    ks = [k_ref[:, cols] for cols in head_cols]
    v = [v_ref[:, cols] for cols in head_cols]
    zero = jnp.zeros_like(q[0])
    igcol = [gcol[:, h:h + 1] for h in every_head]
    bcol = [bcol_all[:, HEADS + h:HEADS + h + 1] for h in every_head]
    intra = [_intra(q[h], ks[h], igcol[h], bcol[h], grow[h:h + 1, :], brow_all[HEADS + h:HEADS + h + 1, :],
                    mrep_ref[:, h:h + 1], mask) for h in every_head]
    s, w_inter, m_t, g = zip(*intra)
    c_all = [c_ref[:, h] for h in every_head]
    n_all = [n_ref[:, h] for h in every_head]
    qc = [_dot(jnp.concatenate([jnp.where(m, q[h], zero) for m in seq_rows], axis=1),
               c_all[h].reshape(n_seq * HEAD_DIM, HEAD_DIM).astype(BF16)) for h in every_head]
    n_rows = [_sel_dot(expand, n_all[h]) for h in every_head]
    ends = [_sel_dot(sel_last, jnp.where(lane == 0, m_t[h], jnp.where(lane == 1, g[h],
                                                                        jnp.where(lane == 2, bcol[h], 0.0))))
            for h in every_head]
    sv = [_dot(s[h].astype(BF16), v[h]) for h in every_head]
    kw, dec, upd, dec_seq, n_inc, m_seq = [], [], [], [], [], []
    for h in every_head:
        m_new, g_last, b_last = ends[h][:, 0:1], ends[h][:, 1:2], ends[h][:, 2:3]
        w_end = jnp.exp(b_last - bcol[h] + igcol[h] - m_new)
        dec.append(jnp.exp(g_last - m_new))
        kw.append(ks[h].astype(F32) * w_end)
    for h in every_head:
        v_exp = jnp.concatenate([jnp.where(m, v[h], zero) for m in seq_rows], axis=1)
        upd.append(_dot(kw[h].T.astype(BF16), v_exp))
        dec_seq.append(_sel_dot(pick_last, jnp.broadcast_to(dec[h], (CHUNK, HEAD_DIM))))
        n_inc.append(_sel_dot(seq_sum, kw[h]))
        m_seq.append(_sel_dot(pick_last, jnp.broadcast_to(m_t[h], (CHUNK, HEAD_DIM))))
    for h, cols in enumerate(head_cols):
        qn = jnp.sum(q[h].astype(F32) * n_rows[h], axis=-1, keepdims=True)
        num = sv[h] + w_inter[h] * qc[h]
        den = jnp.sum(s[h], axis=-1, keepdims=True) + w_inter[h] * qn
        yml_ref[:, cols] = _head_out(num, den, m_t[h], mln_ref[:, cols], og_ref[:, cols])
        for i in range(n_seq):
            c_out_ref[i, h] = (dec_seq[h][i:i + 1, 0:1] * c_all[h][i]
                               + upd[h][:, i * HEAD_DIM:(i + 1) * HEAD_DIM])
        n_out_ref[:, h] = dec_seq[h] * n_all[h] + n_inc[h]
        m_out_ref[:, h] = m_seq[h]


def _mlstm_sample(q, k, v, og, gcol, grow, mrep, c0, n0, mln, tri, triu, sel_last, expand, pick_last,
                  seq_sum, seq_len):
    n = q.shape[0]
    n_seq = CHUNK // seq_len
    n_batch = n // seq_len
    tok = lambda w: pl.BlockSpec((CHUNK, w), lambda i: (i, 0))
    c_spec = pl.BlockSpec((n_seq, HEADS, HEAD_DIM, HEAD_DIM), lambda i: (i, 0, 0, 0))
    n_spec = pl.BlockSpec((n_seq, HEADS, HEAD_DIM), lambda i: (i, 0, 0))
    return pl.pallas_call(
        functools.partial(_mlstm_sample_kernel, seq_len),
        grid=(n // CHUNK,),
        in_specs=[
            tok(ML_WIDTH), tok(ML_WIDTH), tok(ML_WIDTH), tok(ML_WIDTH), tok(GATE_LANES),
            pl.BlockSpec((GATE_ROWS, CHUNK), lambda i: (0, i)),
            pl.BlockSpec((CHUNK, HEADS), lambda i: (i, 0)),
            c_spec, n_spec,
            _const_spec((1, ML_WIDTH)),
            _const_spec((CHUNK, CHUNK)), _const_spec((CHUNK, CHUNK)), _const_spec((CHUNK, CHUNK)),
            _const_spec((CHUNK, n_seq)), _const_spec((n_seq, CHUNK)), _const_spec((n_seq, CHUNK)),
        ],
        out_specs=(tok(ML_WIDTH), c_spec, n_spec, n_spec),
        out_shape=(
            jax.ShapeDtypeStruct((n, ML_WIDTH), BF16),
            jax.ShapeDtypeStruct((n_batch, HEADS, HEAD_DIM, HEAD_DIM), F32),
            jax.ShapeDtypeStruct((n_batch, HEADS, HEAD_DIM), F32),
            jax.ShapeDtypeStruct((n_batch, HEADS, HEAD_DIM), F32),
        ),
        compiler_params=pltpu.CompilerParams(dimension_semantics=("arbitrary",),
                                             vmem_limit_bytes=VMEM_LIMIT),
        name="mlstm_sample",
    )(q, k, v, og, gcol, grow, mrep, c0, n0, mln, tri, triu, sel_last, expand, pick_last, seq_sum)


def _back_math(final_norm, yml_transposed, x_ref, ygm_ref, yml_ref, pe_ref, wout_ref, nffn_ref,
               wup_ref, wdown_ref, nple_ref, wpg_ref, wpp_ref, nfin_ref, out_ref):
    ml_dot = _dot_tn if yml_transposed else _dot
    h = x_ref[...] + _dot(ygm_ref[...], wout_ref[:GM_WIDTH]) + ml_dot(yml_ref[...], wout_ref[GM_WIDTH:])
    a = _rms(h, nffn_ref[...]).astype(BF16)
    ff = D_FF // FF_SPLIT

    def mlp_part(c):
        f = _dot(a, wup_ref[:, c * ff:(c + 1) * ff])
        f = jnp.square(jnp.maximum(f, 0.0)).astype(BF16)
        return _dot(f, wdown_ref[c * ff:(c + 1) * ff, :])

    mlp = mlp_part(0)
    for c in range(1, FF_SPLIT):
        mlp = mlp + mlp_part(c)
    h = h + mlp
    tb = x_ref.shape[0]
    halves = (slice(0, tb // 2), slice(tb // 2, tb))
    emb = _dot(pe_ref[...].astype(BF16), wpp_ref[...])
    gates = [_dot(_rms(h[r], nple_ref[...]).astype(BF16), wpg_ref[...]) for r in halves]
    for r, g in zip(halves, gates):
        y = h[r] + jax.nn.sigmoid(g) * emb[r]
        if final_norm:
            y = _rms(y, nfin_ref[...])
        out_ref[r, :] = y


def _back_kernel(final_norm, yml_transposed, *refs):
    _back_math(final_norm, yml_transposed, *refs)


def _back(x2d, ygm, yml, pe2d, wout, nffn, wup, wdown, nple, wpg, wpp, nfin, final_norm, tb):
    n = x2d.shape[0]
    yml_transposed = yml.shape[0] != n
    tok = lambda w: pl.BlockSpec((tb, w), lambda i: (i, 0))
    yml_spec = pl.BlockSpec((ML_WIDTH, tb), lambda i: (0, i)) if yml_transposed else tok(ML_WIDTH)
    return pl.pallas_call(
        functools.partial(_back_kernel, final_norm, yml_transposed),
        grid=(n // tb,),
        in_specs=[
            tok(D_MODEL), tok(GM_WIDTH), yml_spec, tok(PLE_DIM),
            _const_spec((GM_WIDTH + ML_WIDTH, D_MODEL)),
            _const_spec((1, D_MODEL)),
            _const_spec((D_MODEL, D_FF)), _const_spec((D_FF, D_MODEL)),
            _const_spec((1, D_MODEL)),
            _const_spec((D_MODEL, D_MODEL)), _const_spec((PLE_DIM, D_MODEL)),
            _const_spec((1, D_MODEL)),
        ],
        out_specs=tok(D_MODEL),
        out_shape=jax.ShapeDtypeStruct((n, D_MODEL), F32),
        compiler_params=pltpu.CompilerParams(dimension_semantics=("arbitrary",),
                                             vmem_limit_bytes=VMEM_LIMIT),
        name="back",
    )(x2d, ygm, yml, pe2d, wout, nffn, wup, wdown, nple, wpg, wpp, nfin)


def _const(x):
    return jnp.asarray(np.asarray(x, np.float32), BF16)


def _block_tri(block):
    r = np.arange(CHUNK)[:, None]
    c = np.arange(CHUNK)[None, :]
    return (r // block == c // block) & (c <= r)


def _token_block(n):
    return 512 if n % 512 == 0 else CHUNK


def kernel(x_prompt, x_sample, p_prompt, p_sample, state_C, state_n, state_m, norm_mix, w_in, gm_ln_g,
           gm_ln_b, gm_ws, gm_bs, ml_b_i, ml_b_f, ml_norm, w_out, norm_ffn, w_up, w_down, norm_ple,
           w_ple_gate, w_ple_proj, norm_final):
    depth = w_in.shape[0]
    batch, seq, _ = x_prompt.shape
    dec_batch, dec_seq, _ = x_sample.shape
    n_p, n_s = batch * seq, dec_batch * dec_seq
    n_seq = CHUNK // dec_seq
    assert seq % CHUNK == 0 and CHUNK % dec_seq == 0 and n_s % CHUNK == 0
    tb_p, tb_s = 2 * _token_block(seq), _token_block(n_s)

    hp = x_prompt.reshape(n_p, D_MODEL)
    hs = x_sample.reshape(n_s, D_MODEL)

    tri_p, tri_s = _block_tri(CHUNK), _block_tri(dec_seq)
    triu_p, eye = _const(tri_p.T), _const(np.eye(CHUNK))
    r = np.arange(CHUNK)
    i = np.arange(n_seq)
    sel_last = _const(r[None, :] == (r[:, None] // dec_seq) * dec_seq + dec_seq - 1)
    expand = _const(r[:, None] // dec_seq == i[None, :])
    pick_last = _const(r[None, :] == i[:, None] * dec_seq + dec_seq - 1)
    seq_sum = _const(r[None, :] // dec_seq == i[:, None])

    outs = {k: [] for k in ("Cp", "Np", "Mp", "Vp", "Cs", "Ns", "Ms", "Vs")}
    for l in range(depth):
        row = lambda a: a[l].reshape(1, -1).astype(F32)
        w_t = w_in[l].T.astype(BF16)
        wg_t = w_t[MAIN_COLS:]
        b_i, b_f = ml_b_i[l].astype(F32), ml_b_f[l].astype(F32)
        row_pad = lambda a: jnp.pad(a, ((0, GATE_ROWS - HEADS), (0, 0)))
        wgr = jnp.concatenate([row_pad(wg_t[:HEADS]), row_pad(wg_t[HEADS:])], axis=0)
        bgr = jnp.broadcast_to(jnp.concatenate([row_pad(b_i[:, None]), row_pad(b_f[:, None])], axis=0),
                               (2 * GATE_ROWS, CHUNK))
        gain_t = jnp.broadcast_to(ml_norm[l].astype(F32)[:, None], (ML_WIDTH, CHUNK))
        wgate = jnp.pad(wg_t.T, ((0, 0), (0, GATE_LANES - 2 * HEADS)))
        wgate_t = jnp.pad(wg_t, ((0, GATE_ROWS - 2 * HEADS), (0, 0)))
        gbias = jnp.concatenate([b_i, b_f])
        bcol = jnp.pad(gbias, (0, GATE_LANES - 2 * HEADS)).reshape(1, GATE_LANES)
        brow = jnp.broadcast_to(jnp.pad(gbias, (0, GATE_ROWS - 2 * HEADS)).reshape(GATE_ROWS, 1),
                                (GATE_ROWS, tb_s))
        ws = gm_ws[l]
        mixw_p = (ws[:, :CHUNK, :CHUNK] * tri_p.astype(np.float32)).astype(BF16)
        mixb_p = jnp.broadcast_to(gm_bs[l][:, :CHUNK, None], (HEADS, CHUNK, CHUNK)).astype(F32)
        mixw_s = (jnp.tile(ws[:, :dec_seq, :dec_seq], (1, n_seq, n_seq)) * tri_s.astype(np.float32)).astype(BF16)
        mixb_s = jnp.broadcast_to(jnp.tile(gm_bs[l][:, :dec_seq], (1, n_seq))[:, :, None],
                                  (HEADS, CHUNK, CHUNK)).astype(F32)
        last = l == depth - 1
        nfin = norm_final.reshape(1, D_MODEL).astype(F32)

        back_f32 = [w[l].astype(F32) for w in (w_out, w_up, w_down, w_ple_gate, w_ple_proj)]
        ygm, yml_t, vlast, cn_t, m8, wout, wup, wdown, wpg, wpp = _mixer_prompt(
            hp, row(norm_mix), w_t, wgr, bgr, row(gm_ln_g), row(gm_ln_b), mixw_p, mixb_p,
            gain_t, triu_p, eye, back_f32, batch, tb_p)

        def back(x2d, ygm, yml, pe, tb):
            return _back(x2d, ygm, yml, pe, wout, row(norm_ffn), wup, wdown, row(norm_ple),
                         wpg, wpp, nfin, last, tb)

        hp = back(hp, ygm, yml_t, p_prompt[l].reshape(n_p, PLE_DIM), tb_p)
        outs["Cp"].append(jnp.swapaxes(cn_t[:, :, :HEAD_DIM, :], -1, -2))
        outs["Np"].append(cn_t[:, :, HEAD_DIM, :])
        outs["Mp"].append(m8[:, :, 0])
        outs["Vp"].append(vlast)

        ygm, q, k, v, og, gcol, grow, vgn = _front(
            hs, row(norm_mix), w_t, wgate, wgate_t, bcol, brow, row(gm_ln_g), row(gm_ln_b), mixw_s, mixb_s,
            tb_s)
        mrep = jnp.repeat(state_m[l].astype(F32), dec_seq, axis=0)
        yml, c_new, n_new, m_new = _mlstm_sample(
            q, k, v, og, gcol, grow, mrep, state_C[l].astype(F32), state_n[l].astype(F32), row(ml_norm),
            _const(tri_s), _const(tri_s.T), sel_last, expand, pick_last, seq_sum, dec_seq)
        hs = back(hs, ygm, yml, p_sample[l].reshape(n_s, PLE_DIM), tb_s)
        outs["Cs"].append(c_new)
        outs["Ns"].append(n_new)
        outs["Ms"].append(m_new[..., 0])
        outs["Vs"].append(vgn.reshape(dec_batch, dec_seq, GM_WIDTH))

    st = lambda k: outs[k][0][None] if depth == 1 else jnp.stack(outs[k])
    return (hp.reshape(batch, seq, D_MODEL), hs.reshape(dec_batch, dec_seq, D_MODEL),
            st("Cp"), st("Np"), st("Mp"), st("Vp"), st("Cs"), st("Ns"), st("Ms"), st("Vs"))
```

```python
import functools

import jax
import numpy as np
import jax.numpy as jnp
from jax import lax
from jax.experimental import pallas as pl
from jax.experimental.pallas import tpu as pltpu

F32 = jnp.float32
BF16 = jnp.bfloat16

D_MODEL = 1024
GM_WIDTH = 512
ML_WIDTH = 512
HEADS = 4
HEAD_DIM = 128
D_FF = 4096
PLE_DIM = 256
EPS = 1e-6
CHUNK = 128
MAIN_COLS = 2 * GM_WIDTH + 4 * ML_WIDTH
Q_ROW = 2 * GM_WIDTH
K_ROW = Q_ROW + ML_WIDTH
V_ROW = K_ROW + ML_WIDTH
O_ROW = V_ROW + ML_WIDTH
W_ROWS = MAIN_COLS + 2 * HEADS
LANES_V = ML_WIDTH
LANES_O = 2 * ML_WIDTH
LANES_G = 3 * ML_WIDTH
LANES_ROWS = LANES_G + 2 * 16
GATE_LANES = 128
GATE_ROWS = 16
STATE_ROWS = HEAD_DIM + 16
FF_SPLIT = 4
BF16_ROWS = 16
VMEM_LIMIT = 60 * 1024 * 1024


def _dot(a, b):
    return jnp.dot(a, b, preferred_element_type=F32)


def _dot_nt(a, b):
    return lax.dot_general(a, b, (((1,), (1,)), ((), ())), preferred_element_type=F32)


def _dot_tn(a, b):
    return lax.dot_general(a, b, (((0,), (0,)), ((), ())), preferred_element_type=F32)


def _split3(x):
    x1 = x.astype(BF16)
    r = x - x1.astype(F32)
    x2 = r.astype(BF16)
    r = r - x2.astype(F32)
    return x1, x2, r.astype(BF16)


def _sel_dot(sel, x):
    p1, p2, p3 = _split3(x)
    return _dot(sel, p1) + _dot(sel, p2) + _dot(sel, p3)


def _dot_sel(x, sel):
    p1, p2, p3 = _split3(x)
    return _dot(p1, sel) + _dot(p2, sel) + _dot(p3, sel)


def _rms(x, g):
    return x * lax.rsqrt(jnp.mean(x * x, axis=-1, keepdims=True) + EPS) * g


def _log_sigmoid(x):
    return -(jnp.maximum(-x, 0.0) + jnp.log1p(jnp.exp(-jnp.abs(x))))


def _const_spec(shape):
    nd = len(shape)
    return pl.BlockSpec(shape, lambda *_: (0,) * nd, pipeline_mode=pl.Buffered(1))


def _gmlp_pieces(x_ref, nmix_ref, w_t_ref, lng_ref, lnb_ref, mixw_ref, mixb_ref, ygm_ref, vgn_ref, st):
    tb = x_ref.shape[0]

    def norm():
        st["a"] = _rms(x_ref[...], nmix_ref[...]).astype(BF16)

    def u_mm():
        st["u"] = _dot_nt(st["a"], w_t_ref[:GM_WIDTH])

    def u_act():
        st["u"] = jax.nn.gelu(st["u"])

    def v_mm():
        st["vg"] = _dot_nt(st["a"], w_t_ref[GM_WIDTH:2 * GM_WIDTH])

    def v_act():
        vg = jax.nn.gelu(st.pop("vg"))
        mu = jnp.mean(vg, axis=-1, keepdims=True)
        var = jnp.mean(jnp.square(vg - mu), axis=-1, keepdims=True)
        vgn = (vg - mu) * lax.rsqrt(var + EPS) * lng_ref[...] + lnb_ref[...]
        vgn_ref[...] = vgn[tb - vgn_ref.shape[0]:, :]
        st["vgb"] = vgn.astype(BF16)

    def mix(c):
        rows = slice(c * CHUNK, (c + 1) * CHUNK)
        for h in range(HEADS):
            cols = slice(h * HEAD_DIM, (h + 1) * HEAD_DIM)
            s = _dot(mixw_ref[h], st["vgb"][rows, cols]) + mixb_ref[h]
            ygm_ref[rows, cols] = (st["u"][rows, cols] * s).astype(BF16)

    return [norm, u_mm, u_act, v_mm, v_act] + [functools.partial(mix, c) for c in range(tb // CHUNK)]


def _mixer_prompt_kernel(steps_per_seq, n_cast, x_ref, nmix_ref, w_t_ref, wgr_ref, bgr_ref,
                         lng_ref, lnb_ref, mixw_ref, mixb_ref, gain_t_ref, triu_ref, *rest):
    cast_in, rest = rest[:n_cast], rest[n_cast:]
    ygm_ref, yml_t_ref, vlast_ref, cn_out_ref, m_out_ref = rest[:5]
    cast_out, (cn_scr, m_scr, w_lanes_scr) = rest[5:5 + n_cast], rest[5 + n_cast:]

    @pl.when(pl.program_id(0) == 0)
    def _():
        w_lanes_scr[:LANES_V] = w_t_ref[Q_ROW:K_ROW]
        w_lanes_scr[LANES_V:LANES_G] = w_t_ref[V_ROW:MAIN_COLS]
        w_lanes_scr[LANES_G:] = wgr_ref[...]

    for src, dst in zip(cast_in, cast_out):
        dst[...] = src[...].astype(BF16)

    tb = x_ref.shape[0]
    n_chunks = tb // CHUNK
    blocks = [slice(c * CHUNK, (c + 1) * CHUNK) for c in range(n_chunks)]
    heads = [slice(h * HEAD_DIM, (h + 1) * HEAD_DIM) for h in range(HEADS)]
    unit_ids = [(c, h) for c in range(n_chunks) for h in range(HEADS)]
    new_seq = pl.program_id(0) % steps_per_seq == 0
    triu = triu_ref[...]
    mask_t = triu.astype(F32) > 0.0
    lane_g = lax.broadcasted_iota(jnp.int32, (GATE_ROWS, tb), 1) % CHUNK
    sub_g = lax.broadcasted_iota(jnp.int32, (GATE_ROWS, CHUNK), 0)
    ones_row = (sub_g == 0).astype(BF16)
    last = slice(CHUNK - 1, CHUNK)

    st = {}
    norm, u_mm, u_act, v_mm, v_act, *mix = _gmlp_pieces(x_ref, nmix_ref, w_t_ref, lng_ref, lnb_ref, mixw_ref,
                                                        mixb_ref, ygm_ref, vlast_ref, st)
    zero = jnp.zeros((HEAD_DIM, HEAD_DIM), BF16)

    def block_diag(x, y):
        return jnp.concatenate([jnp.concatenate([x, zero], axis=1), jnp.concatenate([zero, y], axis=1)], axis=0)

    def pair_dot(lhs, rhs):
        out = _dot(jnp.concatenate(lhs, axis=1), block_diag(*rhs))
        return out[:, :HEAD_DIM], out[:, HEAD_DIM:]

    norm()
    a = st["a"]
    on_lanes = _dot_nt(w_lanes_scr[...], a)
    u_mm()
    v_mm()
    k = (_dot_nt(a, w_t_ref[K_ROW:V_ROW]) * (HEAD_DIM ** -0.5)).astype(BF16)
    lanes = lambda ref: jnp.concatenate([ref[...]] * n_chunks, axis=1)
    zr = on_lanes[LANES_G:] + lanes(bgr_ref)
    lf_r = _log_sigmoid(zr[GATE_ROWS:])
    b_r = jnp.concatenate([_dot_sel(lf_r[:, t], triu) for t in blocks], axis=1)
    q_t = on_lanes[:ML_WIDTH].astype(BF16)
    r_r = zr[:GATE_ROWS] - b_r
    p_r = r_r
    shift = 1
    while shift < CHUNK:
        p_r = jnp.maximum(p_r, jnp.where(lane_g >= shift, pltpu.roll(p_r, shift, axis=1), -jnp.inf))
        shift *= 2
    e_end = [jnp.exp(r_r[:, t] - p_r[:, t][:, last]) for t in blocks]
    u_act()
    v_t = on_lanes[LANES_V:LANES_O]
    r_c = [r_r[:, t].T for t in blocks]
    og_t = jax.nn.sigmoid(on_lanes[LANES_O:LANES_G]) * lanes(gain_t_ref)
    v_act()

    pair_ids = [(c, h) for c in range(n_chunks) for h in range(0, HEADS, 2)]
    kq, e_intra, x1, upd = {}, {}, {}, {}
    for c, h in unit_ids:
        e_intra[c, h] = jnp.where(mask_t, jnp.exp(r_c[c][:, h:h + 1] - p_r[h:h + 1, blocks[c]]), 0.0)
    for piece in mix:
        piece()
    for c, h in pair_ids:
        t = blocks[c]
        kq[c, h], kq[c, h + 1] = pair_dot([k[t, heads[h]], k[t, heads[h + 1]]],
                                          [q_t[heads[h], t], q_t[heads[h + 1], t]])
    for c, h in pair_ids:
        t = blocks[c]
        s0, vext, vw = [], [], []
        for g in (h, h + 1):
            s0.append((kq[c, g] * e_intra[c, g]).astype(BF16))
            vext.append(jnp.concatenate([v_t[heads[g], t].astype(BF16), ones_row], axis=0))
            e_row = e_end[c][g:g + 1, :]
            vw.append(jnp.concatenate([(v_t[heads[g], t] * e_row).astype(BF16),
                                       jnp.where(sub_g == 0, e_row, 0.0).astype(BF16)], axis=0))
        x1[c, h], x1[c, h + 1] = pair_dot(vext, s0)
        upd[c, h], upd[c, h + 1] = pair_dot(vw, [k[t, heads[h]], k[t, heads[h + 1]]])

    cn_in, cm, w_inter = {}, {}, {}
    for h in range(HEADS):
        cn = jnp.where(new_seq, 0.0, cn_scr[h])
        m_prev = jnp.where(new_seq, 0.0, m_scr[h:h + 1, 0:1])
        for c in range(n_chunks):
            prow = p_r[h:h + 1, blocks[c]]
            cn_in[c, h] = cn
            cm[c, h] = jnp.maximum(m_prev, prow)
            w_inter[c, h] = jnp.exp(m_prev - cm[c, h])
            cm_last = cm[c, h][:, last]
            cn = jnp.exp(m_prev - cm_last) * cn + jnp.exp(prow[:, last] - cm_last) * upd[c, h]
            m_prev = b_r[h:h + 1, blocks[c]][:, last] + cm_last
        cn_scr[h] = cn
        m_scr[h:h + 1, :] = jnp.broadcast_to(m_prev, (1, HEAD_DIM))
    x2 = {}

    def state_matmuls(c):
        t = blocks[c]
        for h in range(0, HEADS, 2):
            x2[c, h], x2[c, h + 1] = pair_dot([cn_in[c, h].astype(BF16), cn_in[c, h + 1].astype(BF16)],
                                              [q_t[heads[h], t], q_t[heads[h + 1], t]])

    def head_outputs(c):
        t = blocks[c]
        for h, hd in enumerate(heads):
            brow, prow = b_r[h:h + 1, t], p_r[h:h + 1, t]
            nd = x1[c, h] * jnp.exp(prow - cm[c, h]) + x2[c, h] * w_inter[c, h]
            num_t, den = nd[:HEAD_DIM], nd[HEAD_DIM:HEAD_DIM + 1]
            inv = 1.0 / jnp.maximum(jnp.abs(den), jnp.exp(-(brow + cm[c, h])))
            ssq = jnp.sum(num_t * num_t, axis=0, keepdims=True)
            scale = inv * lax.rsqrt(ssq * (inv * inv) * (1.0 / HEAD_DIM) + EPS)
            yml_t_ref[hd, t] = (num_t * scale * og_t[hd, t]).astype(BF16)

    for c in range(n_chunks):
        state_matmuls(c)
    for c in range(n_chunks):
        head_outputs(c)

    @pl.when(pl.program_id(0) % steps_per_seq == steps_per_seq - 1)
    def _():
        cn_out_ref[...] = cn_scr[...]
        m_out_ref[...] = m_scr[...]


def _mixer_prompt(x2d, nmix, w_t, wgr, bgr, lng, lnb, mixw, mixb, gain_t, triu, cast, batch, tb):
    n = x2d.shape[0]
    n_steps = n // tb
    steps_per_seq = n_steps // batch

    def window(w):
        rows = max(BF16_ROWS, w.shape[0] // n_steps)
        return pl.BlockSpec((rows, w.shape[1]), lambda i: (jnp.minimum(i, w.shape[0] // rows - 1), 0))

    tok = lambda w: pl.BlockSpec((tb, w), lambda i: (i, 0))
    per_seq = lambda *shape: pl.BlockSpec((None,) + shape, lambda i: (i // steps_per_seq,) + (0,) * len(shape))
    return pl.pallas_call(
        functools.partial(_mixer_prompt_kernel, steps_per_seq, len(cast)),
        grid=(n_steps,),
        in_specs=[
            tok(D_MODEL),
            _const_spec((1, D_MODEL)),
            _const_spec((W_ROWS, D_MODEL)),
            _const_spec((2 * GATE_ROWS, D_MODEL)),
            _const_spec((2 * GATE_ROWS, CHUNK)),
            _const_spec((1, GM_WIDTH)),
            _const_spec((1, GM_WIDTH)),
            _const_spec((HEADS, CHUNK, CHUNK)),
            _const_spec((HEADS, CHUNK, CHUNK)),
            _const_spec((ML_WIDTH, CHUNK)),
            _const_spec((CHUNK, CHUNK)),
        ] + [window(w) for w in cast],
        out_specs=(tok(GM_WIDTH), pl.BlockSpec((ML_WIDTH, tb), lambda i: (0, i)), per_seq(CHUNK, GM_WIDTH),
                   per_seq(HEADS, STATE_ROWS, HEAD_DIM), per_seq(HEADS, HEAD_DIM)) + tuple(window(w) for w in cast),
        out_shape=(
            jax.ShapeDtypeStruct((n, GM_WIDTH), BF16),
            jax.ShapeDtypeStruct((ML_WIDTH, n), BF16),
            jax.ShapeDtypeStruct((batch, CHUNK, GM_WIDTH), F32),
            jax.ShapeDtypeStruct((batch, HEADS, STATE_ROWS, HEAD_DIM), F32),
            jax.ShapeDtypeStruct((batch, HEADS, HEAD_DIM), F32),
        ) + tuple(jax.ShapeDtypeStruct(w.shape, BF16) for w in cast),
        scratch_shapes=[pltpu.VMEM((HEADS, STATE_ROWS, HEAD_DIM), F32), pltpu.VMEM((HEADS, HEAD_DIM), F32),
                        pltpu.VMEM((LANES_ROWS, D_MODEL), BF16)],
        compiler_params=pltpu.CompilerParams(dimension_semantics=("arbitrary",),
                                             vmem_limit_bytes=VMEM_LIMIT),
        name="mixer_prompt",
    )(x2d, nmix, w_t, wgr, bgr, lng, lnb, mixw, mixb, gain_t, triu, *cast)


def _front_kernel(x_ref, nmix_ref, w_t_ref, wgate_ref, wgate_t_ref, bcol_ref, brow_ref,
                  lng_ref, lnb_ref, mixw_ref, mixb_ref,
                  ygm_ref, q_ref, k_ref, v_ref, og_ref, gcol_ref, grow_ref, vgn_ref):
    st = {}
    for piece in _gmlp_pieces(x_ref, nmix_ref, w_t_ref, lng_ref, lnb_ref, mixw_ref, mixb_ref,
                              ygm_ref, vgn_ref, st):
        piece()
    a = st["a"]

    def proj(lo):
        return _dot_nt(a, w_t_ref[lo:lo + ML_WIDTH])

    q_ref[...] = proj(Q_ROW).astype(BF16)
    k_ref[...] = (proj(K_ROW) * (HEAD_DIM ** -0.5)).astype(BF16)
    v_ref[...] = proj(V_ROW).astype(BF16)
    og_ref[...] = jax.nn.sigmoid(proj(O_ROW))
    zc = _dot(a, wgate_ref[...]) + bcol_ref[...]
    lane = lax.broadcasted_iota(jnp.int32, zc.shape, 1)
    gcol_ref[...] = jnp.where(lane >= HEADS, _log_sigmoid(zc), zc)
    zr = _dot_nt(wgate_t_ref[...], a) + brow_ref[...]
    sub = lax.broadcasted_iota(jnp.int32, zr.shape, 0)
    grow_ref[...] = jnp.where(sub >= HEADS, _log_sigmoid(zr), zr)


def _front(x2d, nmix, w_t, wgate, wgate_t, bcol, brow, lng, lnb, mixw, mixb, tb):
    n = x2d.shape[0]
    tok = lambda w: pl.BlockSpec((tb, w), lambda i: (i, 0))
    out_shape = (
        jax.ShapeDtypeStruct((n, GM_WIDTH), BF16),
        jax.ShapeDtypeStruct((n, ML_WIDTH), BF16),
        jax.ShapeDtypeStruct((n, ML_WIDTH), BF16),
        jax.ShapeDtypeStruct((n, ML_WIDTH), BF16),
        jax.ShapeDtypeStruct((n, ML_WIDTH), F32),
        jax.ShapeDtypeStruct((n, GATE_LANES), F32),
        jax.ShapeDtypeStruct((GATE_ROWS, n), F32),
        jax.ShapeDtypeStruct((n, GM_WIDTH), F32),
    )
    return pl.pallas_call(
        _front_kernel,
        grid=(n // tb,),
        in_specs=[
            tok(D_MODEL),
            _const_spec((1, D_MODEL)),
            _const_spec((W_ROWS, D_MODEL)),
            _const_spec((D_MODEL, GATE_LANES)),
            _const_spec((GATE_ROWS, D_MODEL)),
            _const_spec((1, GATE_LANES)),
            _const_spec((GATE_ROWS, tb)),
            _const_spec((1, GM_WIDTH)),
            _const_spec((1, GM_WIDTH)),
            _const_spec((HEADS, CHUNK, CHUNK)),
            _const_spec((HEADS, CHUNK, CHUNK)),
        ],
        out_specs=(tok(GM_WIDTH), tok(ML_WIDTH), tok(ML_WIDTH), tok(ML_WIDTH), tok(ML_WIDTH),
                   tok(GATE_LANES), pl.BlockSpec((GATE_ROWS, tb), lambda i: (0, i)), tok(GM_WIDTH)),
        out_shape=out_shape,
        compiler_params=pltpu.CompilerParams(dimension_semantics=("arbitrary",),
                                             vmem_limit_bytes=VMEM_LIMIT),
        name="front",
    )(x2d, nmix, w_t, wgate, wgate_t, bcol, brow, lng, lnb, mixw, mixb)


def _intra(q, ks, igcol, bcol, igrow, brow, mprev, mask):
    d = bcol + (igrow - brow)
    g = bcol + mprev
    m_t = jnp.maximum(g, jnp.max(jnp.where(mask, d, -jnp.inf), axis=-1, keepdims=True))
    w_intra = jnp.where(mask, jnp.exp(d - m_t), 0.0)
    w_inter = jnp.exp(g - m_t)
    s = _dot_nt(q, ks) * w_intra
    return s, w_inter, m_t, g


def _head_out(num, den, m_t, gain, og):
    hh = num / jnp.maximum(jnp.abs(den), jnp.exp(-m_t))
    return (og * _rms(hh, gain)).astype(BF16)


def _mlstm_sample_kernel(seq_len, q_ref, k_ref, v_ref, og_ref, gcol_ref, grow_ref, mrep_ref, c_ref, n_ref,
                         mln_ref, tri_ref, triu_ref, sel_last_ref, expand_ref, pick_last_ref, seq_sum_ref,
                         yml_ref, c_out_ref, n_out_ref, m_out_ref):
    n_seq = CHUNK // seq_len
    tri = tri_ref[...]
    mask = tri.astype(F32) > 0.0
    gcol = gcol_ref[...]
    grow = grow_ref[...]
    bcol_all = _sel_dot(tri, gcol)
    brow_all = _dot_sel(grow, triu_ref[...])
    sel_last = sel_last_ref[...]
    expand = expand_ref[...]
    pick_last = pick_last_ref[...]
    seq_sum = seq_sum_ref[...]
    row = lax.broadcasted_iota(jnp.int32, (CHUNK, HEAD_DIM), 0)
    lane = lax.broadcasted_iota(jnp.int32, (CHUNK, HEAD_DIM), 1)
    seq_rows = [(row >= i * seq_len) & (row < (i + 1) * seq_len) for i in range(n_seq)]
    head_cols = [slice(h * HEAD_DIM, (h + 1) * HEAD_DIM) for h in range(HEADS)]
    every_head = range(HEADS)
    q = [q_ref[:, cols] for cols in head_cols]
    ks = [k_ref[:, cols] for cols in head_cols]
    v = [v_ref[:, cols] for cols in head_cols]
    zero = jnp.zeros_like(q[0])
    igcol = [gcol[:, h:h + 1] for h in every_head]
    bcol = [bcol_all[:, HEADS + h:HEADS + h + 1] for h in every_head]
    intra = [_intra(q[h], ks[h], igcol[h], bcol[h], grow[h:h + 1, :], brow_all[HEADS + h:HEADS + h + 1, :],
                    mrep_ref[:, h:h + 1], mask) for h in every_head]
    s, w_inter, m_t, g = zip(*intra)
    c_all = [c_ref[:, h] for h in every_head]
    n_all = [n_ref[:, h] for h in every_head]
    qc = [_dot(jnp.concatenate([jnp.where(m, q[h], zero) for m in seq_rows], axis=1),
               c_all[h].reshape(n_seq * HEAD_DIM, HEAD_DIM).astype(BF16)) for h in every_head]
    n_rows = [_sel_dot(expand, n_all[h]) for h in every_head]
    ends = [_sel_dot(sel_last, jnp.where(lane == 0, m_t[h], jnp.where(lane == 1, g[h],
                                                                        jnp.where(lane == 2, bcol[h], 0.0))))
            for h in every_head]
    sv = [_dot(s[h].astype(BF16), v[h]) for h in every_head]
    kw, dec, upd, dec_seq, n_inc, m_seq = [], [], [], [], [], []
    for h in every_head:
        m_new, g_last, b_last = ends[h][:, 0:1], ends[h][:, 1:2], ends[h][:, 2:3]
        w_end = jnp.exp(b_last - bcol[h] + igcol[h] - m_new)
        dec.append(jnp.exp(g_last - m_new))
        kw.append(ks[h].astype(F32) * w_end)
    for h in every_head:
        v_exp = jnp.concatenate([jnp.where(m, v[h], zero) for m in seq_rows], axis=1)
        upd.append(_dot(kw[h].T.astype(BF16), v_exp))
        dec_seq.append(_sel_dot(pick_last, jnp.broadcast_to(dec[h], (CHUNK, HEAD_DIM))))
        n_inc.append(_sel_dot(seq_sum, kw[h]))
        m_seq.append(_sel_dot(pick_last, jnp.broadcast_to(m_t[h], (CHUNK, HEAD_DIM))))
    for h, cols in enumerate(head_cols):
        qn = jnp.sum(q[h].astype(F32) * n_rows[h], axis=-1, keepdims=True)
        num = sv[h] + w_inter[h] * qc[h]
        den = jnp.sum(s[h], axis=-1, keepdims=True) + w_inter[h] * qn
        yml_ref[:, cols] = _head_out(num, den, m_t[h], mln_ref[:, cols], og_ref[:, cols])
        for i in range(n_seq):
            c_out_ref[i, h] = (dec_seq[h][i:i + 1, 0:1] * c_all[h][i]
                               + upd[h][:, i * HEAD_DIM:(i + 1) * HEAD_DIM])
        n_out_ref[:, h] = dec_seq[h] * n_all[h] + n_inc[h]
        m_out_ref[:, h] = m_seq[h]


def _mlstm_sample(q, k, v, og, gcol, grow, mrep, c0, n0, mln, tri, triu, sel_last, expand, pick_last,
                  seq_sum, seq_len):
    n = q.shape[0]
    n_seq = CHUNK // seq_len
    n_batch = n // seq_len
    tok = lambda w: pl.BlockSpec((CHUNK, w), lambda i: (i, 0))
    c_spec = pl.BlockSpec((n_seq, HEADS, HEAD_DIM, HEAD_DIM), lambda i: (i, 0, 0, 0))
    n_spec = pl.BlockSpec((n_seq, HEADS, HEAD_DIM), lambda i: (i, 0, 0))
    return pl.pallas_call(
        functools.partial(_mlstm_sample_kernel, seq_len),
        grid=(n // CHUNK,),
        in_specs=[
            tok(ML_WIDTH), tok(ML_WIDTH), tok(ML_WIDTH), tok(ML_WIDTH), tok(GATE_LANES),
            pl.BlockSpec((GATE_ROWS, CHUNK), lambda i: (0, i)),
            pl.BlockSpec((CHUNK, HEADS), lambda i: (i, 0)),
            c_spec, n_spec,
            _const_spec((1, ML_WIDTH)),
            _const_spec((CHUNK, CHUNK)), _const_spec((CHUNK, CHUNK)), _const_spec((CHUNK, CHUNK)),
            _const_spec((CHUNK, n_seq)), _const_spec((n_seq, CHUNK)), _const_spec((n_seq, CHUNK)),
        ],
        out_specs=(tok(ML_WIDTH), c_spec, n_spec, n_spec),
        out_shape=(
            jax.ShapeDtypeStruct((n, ML_WIDTH), BF16),
            jax.ShapeDtypeStruct((n_batch, HEADS, HEAD_DIM, HEAD_DIM), F32),
            jax.ShapeDtypeStruct((n_batch, HEADS, HEAD_DIM), F32),
            jax.ShapeDtypeStruct((n_batch, HEADS, HEAD_DIM), F32),
        ),
        compiler_params=pltpu.CompilerParams(dimension_semantics=("arbitrary",),
                                             vmem_limit_bytes=VMEM_LIMIT),
        name="mlstm_sample",
    )(q, k, v, og, gcol, grow, mrep, c0, n0, mln, tri, triu, sel_last, expand, pick_last, seq_sum)


def _back_math(final_norm, yml_transposed, x_ref, ygm_ref, yml_ref, pe_ref, wout_ref, nffn_ref,
               wup_ref, wdown_ref, nple_ref, wpg_ref, wpp_ref, nfin_ref, out_ref):
    ml_dot = _dot_tn if yml_transposed else _dot
    h = x_ref[...] + _dot(ygm_ref[...], wout_ref[:GM_WIDTH]) + ml_dot(yml_ref[...], wout_ref[GM_WIDTH:])
    a = _rms(h, nffn_ref[...]).astype(BF16)
    ff = D_FF // FF_SPLIT

    def mlp_part(c):
        f = _dot(a, wup_ref[:, c * ff:(c + 1) * ff])
        f = jnp.square(jnp.maximum(f, 0.0)).astype(BF16)
        return _dot(f, wdown_ref[c * ff:(c + 1) * ff, :])

    mlp = mlp_part(0)
    for c in range(1, FF_SPLIT):
        mlp = mlp + mlp_part(c)
    h = h + mlp
    tb = x_ref.shape[0]
    halves = (slice(0, tb // 2), slice(tb // 2, tb))
    emb = _dot(pe_ref[...].astype(BF16), wpp_ref[...])
    gates = [_dot(_rms(h[r], nple_ref[...]).astype(BF16), wpg_ref[...]) for r in halves]
    for r, g in zip(halves, gates):
        y = h[r] + jax.nn.sigmoid(g) * emb[r]
        if final_norm:
            y = _rms(y, nfin_ref[...])
        out_ref[r, :] = y


def _back_kernel(final_norm, yml_transposed, *refs):
    _back_math(final_norm, yml_transposed, *refs)


def _back(x2d, ygm, yml, pe2d, wout, nffn, wup, wdown, nple, wpg, wpp, nfin, final_norm, tb):
    n = x2d.shape[0]
    yml_transposed = yml.shape[0] != n
    tok = lambda w: pl.BlockSpec((tb, w), lambda i: (i, 0))
    yml_spec = pl.BlockSpec((ML_WIDTH, tb), lambda i: (0, i)) if yml_transposed else tok(ML_WIDTH)
    return pl.pallas_call(
        functools.partial(_back_kernel, final_norm, yml_transposed),
        grid=(n // tb,),
        in_specs=[
            tok(D_MODEL), tok(GM_WIDTH), yml_spec, tok(PLE_DIM),
            _const_spec((GM_WIDTH + ML_WIDTH, D_MODEL)),
            _const_spec((1, D_MODEL)),
            _const_spec((D_MODEL, D_FF)), _const_spec((D_FF, D_MODEL)),
            _const_spec((1, D_MODEL)),
            _const_spec((D_MODEL, D_MODEL)), _const_spec((PLE_DIM, D_MODEL)),
            _const_spec((1, D_MODEL)),
        ],
        out_specs=tok(D_MODEL),
        out_shape=jax.ShapeDtypeStruct((n, D_MODEL), F32),
        compiler_params=pltpu.CompilerParams(dimension_semantics=("arbitrary",),
                                             vmem_limit_bytes=VMEM_LIMIT),
        name="back",
    )(x2d, ygm, yml, pe2d, wout, nffn, wup, wdown, nple, wpg, wpp, nfin)


def _const(x):
    return jnp.asarray(np.asarray(x, np.float32), BF16)


def _block_tri(block):
    r = np.arange(CHUNK)[:, None]
    c = np.arange(CHUNK)[None, :]
    return (r // block == c // block) & (c <= r)


def _token_block(n):
    return 512 if n % 512 == 0 else CHUNK


def kernel(x_prompt, x_sample, p_prompt, p_sample, state_C, state_n, state_m, norm_mix, w_in, gm_ln_g,
           gm_ln_b, gm_ws, gm_bs, ml_b_i, ml_b_f, ml_norm, w_out, norm_ffn, w_up, w_down, norm_ple,
           w_ple_gate, w_ple_proj, norm_final):
    depth = w_in.shape[0]
    batch, seq, _ = x_prompt.shape
    dec_batch, dec_seq, _ = x_sample.shape
    n_p, n_s = batch * seq, dec_batch * dec_seq
    n_seq = CHUNK // dec_seq
    assert seq % CHUNK == 0 and CHUNK % dec_seq == 0 and n_s % CHUNK == 0
    tb_p, tb_s = 2 * _token_block(seq), _token_block(n_s)

    hp = x_prompt.reshape(n_p, D_MODEL)
    hs = x_sample.reshape(n_s, D_MODEL)

    tri_p, tri_s = _block_tri(CHUNK), _block_tri(dec_seq)
    triu_p = _const(tri_p.T)
    r = np.arange(CHUNK)
    i = np.arange(n_seq)
    sel_last = _const(r[None, :] == (r[:, None] // dec_seq) * dec_seq + dec_seq - 1)
    expand = _const(r[:, None] // dec_seq == i[None, :])
    pick_last = _const(r[None, :] == i[:, None] * dec_seq + dec_seq - 1)
    seq_sum = _const(r[None, :] // dec_seq == i[:, None])

    outs = {k: [] for k in ("Cp", "Np", "Mp", "Vp", "Cs", "Ns", "Ms", "Vs")}
    for l in range(depth):
        row = lambda a: a[l].reshape(1, -1).astype(F32)
        w_t = w_in[l].T.astype(BF16)
        wg_t = w_t[MAIN_COLS:]
        b_i, b_f = ml_b_i[l].astype(F32), ml_b_f[l].astype(F32)
        row_pad = lambda a: jnp.pad(a, ((0, GATE_ROWS - HEADS), (0, 0)))
        wgr = jnp.concatenate([row_pad(wg_t[:HEADS]), row_pad(wg_t[HEADS:])], axis=0)
        bgr = jnp.broadcast_to(jnp.concatenate([row_pad(b_i[:, None]), row_pad(b_f[:, None])], axis=0),
                               (2 * GATE_ROWS, CHUNK))
        gain_t = jnp.broadcast_to(ml_norm[l].astype(F32)[:, None], (ML_WIDTH, CHUNK))
        wgate = jnp.pad(wg_t.T, ((0, 0), (0, GATE_LANES - 2 * HEADS)))
        wgate_t = jnp.pad(wg_t, ((0, GATE_ROWS - 2 * HEADS), (0, 0)))
        gbias = jnp.concatenate([b_i, b_f])
        bcol = jnp.pad(gbias, (0, GATE_LANES - 2 * HEADS)).reshape(1, GATE_LANES)
        brow = jnp.broadcast_to(jnp.pad(gbias, (0, GATE_ROWS - 2 * HEADS)).reshape(GATE_ROWS, 1),
                                (GATE_ROWS, tb_s))
        ws = gm_ws[l]
        mixw_p = (ws[:, :CHUNK, :CHUNK] * tri_p.astype(np.float32)).astype(BF16)
        mixb_p = jnp.broadcast_to(gm_bs[l][:, :CHUNK, None], (HEADS, CHUNK, CHUNK)).astype(F32)
        mixw_s = (jnp.tile(ws[:, :dec_seq, :dec_seq], (1, n_seq, n_seq)) * tri_s.astype(np.float32)).astype(BF16)
        mixb_s = jnp.broadcast_to(jnp.tile(gm_bs[l][:, :dec_seq], (1, n_seq))[:, :, None],
                                  (HEADS, CHUNK, CHUNK)).astype(F32)
        last = l == depth - 1
        nfin = norm_final.reshape(1, D_MODEL).astype(F32)

        back_f32 = [w[l].astype(F32) for w in (w_out, w_up, w_down, w_ple_gate, w_ple_proj)]
        ygm, yml_t, vlast, cn_t, m8, wout, wup, wdown, wpg, wpp = _mixer_prompt(
            hp, row(norm_mix), w_t, wgr, bgr, row(gm_ln_g), row(gm_ln_b), mixw_p, mixb_p,
            gain_t, triu_p, back_f32, batch, tb_p)

        def back(x2d, ygm, yml, pe, tb):
            return _back(x2d, ygm, yml, pe, wout, row(norm_ffn), wup, wdown, row(norm_ple),
                         wpg, wpp, nfin, last, tb)

        hp = back(hp, ygm, yml_t, p_prompt[l].reshape(n_p, PLE_DIM), tb_p)
        outs["Cp"].append(jnp.swapaxes(cn_t[:, :, :HEAD_DIM, :], -1, -2))
        outs["Np"].append(cn_t[:, :, HEAD_DIM, :])
        outs["Mp"].append(m8[:, :, 0])
        outs["Vp"].append(vlast)

        ygm, q, k, v, og, gcol, grow, vgn = _front(
            hs, row(norm_mix), w_t, wgate, wgate_t, bcol, brow, row(gm_ln_g), row(gm_ln_b), mixw_s, mixb_s,
            tb_s)
        mrep = jnp.repeat(state_m[l].astype(F32), dec_seq, axis=0)
        yml, c_new, n_new, m_new = _mlstm_sample(
            q, k, v, og, gcol, grow, mrep, state_C[l].astype(F32), state_n[l].astype(F32), row(ml_norm),
            _const(tri_s), _const(tri_s.T), sel_last, expand, pick_last, seq_sum, dec_seq)
        hs = back(hs, ygm, yml, p_sample[l].reshape(n_s, PLE_DIM), tb_s)
        outs["Cs"].append(c_new)
        outs["Ns"].append(n_new)
        outs["Ms"].append(m_new[..., 0])
        outs["Vs"].append(vgn.reshape(dec_batch, dec_seq, GM_WIDTH))

    st = lambda k: outs[k][0][None] if depth == 1 else jnp.stack(outs[k])
    return (hp.reshape(batch, seq, D_MODEL), hs.reshape(dec_batch, dec_seq, D_MODEL),
            st("Cp"), st("Np"), st("Mp"), st("Vp"), st("Cs"), st("Ns"), st("Ms"), st("Vs"))
```

```python
import functools

import jax
import numpy as np
import jax.numpy as jnp
from jax import lax
from jax.experimental import pallas as pl
from jax.experimental.pallas import tpu as pltpu

F32 = jnp.float32
BF16 = jnp.bfloat16

D_MODEL = 1024
GM_WIDTH = 512
ML_WIDTH = 512
HEADS = 4
HEAD_DIM = 128
D_FF = 4096
PLE_DIM = 256
EPS = 1e-6
CHUNK = 128
MAIN_COLS = 2 * GM_WIDTH + 4 * ML_WIDTH
Q_ROW = 2 * GM_WIDTH
K_ROW = Q_ROW + ML_WIDTH
V_ROW = K_ROW + ML_WIDTH
O_ROW = V_ROW + ML_WIDTH
W_ROWS = MAIN_COLS + 2 * HEADS
LANES_V = ML_WIDTH
LANES_O = 2 * ML_WIDTH
LANES_G = 3 * ML_WIDTH
LANES_ROWS = LANES_G + 2 * 16
GATE_LANES = 128
GATE_ROWS = 16
STATE_ROWS = HEAD_DIM + 16
FF_SPLIT = 4
BF16_ROWS = 16
VMEM_LIMIT = 60 * 1024 * 1024


def _dot(a, b):
    return jnp.dot(a, b, preferred_element_type=F32)


def _dot_nt(a, b):
    return lax.dot_general(a, b, (((1,), (1,)), ((), ())), preferred_element_type=F32)


def _dot_tn(a, b):
    return lax.dot_general(a, b, (((0,), (0,)), ((), ())), preferred_element_type=F32)


def _split3(x):
    x1 = x.astype(BF16)
    r = x - x1.astype(F32)
    x2 = r.astype(BF16)
    r = r - x2.astype(F32)
    return x1, x2, r.astype(BF16)


def _sel_dot(sel, x):
    p1, p2, p3 = _split3(x)
    return _dot(sel, p1) + _dot(sel, p2) + _dot(sel, p3)


def _dot_sel(x, sel):
    p1, p2, p3 = _split3(x)
    return _dot(p1, sel) + _dot(p2, sel) + _dot(p3, sel)


def _sel_dot_nt(sel, x):
    p1, p2, p3 = _split3(x)
    return _dot_nt(sel, p1) + _dot_nt(sel, p2) + _dot_nt(sel, p3)


def _rms(x, g):
    return x * lax.rsqrt(jnp.mean(x * x, axis=-1, keepdims=True) + EPS) * g


def _log_sigmoid(x):
    return -(jnp.maximum(-x, 0.0) + jnp.log1p(jnp.exp(-jnp.abs(x))))


def _const_spec(shape):
    nd = len(shape)
    return pl.BlockSpec(shape, lambda *_: (0,) * nd, pipeline_mode=pl.Buffered(1))


def _gmlp_pieces(x_ref, nmix_ref, w_t_ref, lng_ref, lnb_ref, mixw_ref, mixb_ref, ygm_ref, vgn_ref, st):
    tb = x_ref.shape[0]

    def norm():
        st["a"] = _rms(x_ref[...], nmix_ref[...]).astype(BF16)

    def u_mm():
        st["u"] = _dot_nt(st["a"], w_t_ref[:GM_WIDTH])

    def u_act():
        st["u"] = jax.nn.gelu(st["u"])

    def v_mm():
        st["vg"] = _dot_nt(st["a"], w_t_ref[GM_WIDTH:2 * GM_WIDTH])

    def v_act():
        vg = jax.nn.gelu(st.pop("vg"))
        mu = jnp.mean(vg, axis=-1, keepdims=True)
        var = jnp.mean(jnp.square(vg - mu), axis=-1, keepdims=True)
        vgn = (vg - mu) * lax.rsqrt(var + EPS) * lng_ref[...] + lnb_ref[...]
        vgn_ref[...] = vgn[tb - vgn_ref.shape[0]:, :]
        st["vgb"] = vgn.astype(BF16)

    def mix(c):
        rows = slice(c * CHUNK, (c + 1) * CHUNK)
        for h in range(HEADS):
            cols = slice(h * HEAD_DIM, (h + 1) * HEAD_DIM)
            s = _dot(mixw_ref[h], st["vgb"][rows, cols]) + mixb_ref[h]
            ygm_ref[rows, cols] = (st["u"][rows, cols] * s).astype(BF16)

    return [norm, u_mm, u_act, v_mm, v_act] + [functools.partial(mix, c) for c in range(tb // CHUNK)]


def _mixer_prompt_kernel(steps_per_seq, n_cast, x_ref, nmix_ref, w_t_ref, wgr_ref, bgr_ref,
                         lng_ref, lnb_ref, mixw_ref, mixb_ref, gain_t_ref, triu_ref, eye_ref, *rest):
    cast_in, rest = rest[:n_cast], rest[n_cast:]
    ygm_ref, yml_t_ref, vlast_ref, cn_out_ref, m_out_ref = rest[:5]
    cast_out, (cn_scr, m_scr, w_lanes_scr) = rest[5:5 + n_cast], rest[5 + n_cast:]

    @pl.when(pl.program_id(0) == 0)
    def _():
        w_lanes_scr[:LANES_V] = w_t_ref[Q_ROW:K_ROW]
        w_lanes_scr[LANES_V:LANES_G] = w_t_ref[V_ROW:MAIN_COLS]
        w_lanes_scr[LANES_G:] = wgr_ref[...]

    for src, dst in zip(cast_in, cast_out):
        dst[...] = src[...].astype(BF16)

    tb = x_ref.shape[0]
    n_chunks = tb // CHUNK
    blocks = [slice(c * CHUNK, (c + 1) * CHUNK) for c in range(n_chunks)]
    heads = [slice(h * HEAD_DIM, (h + 1) * HEAD_DIM) for h in range(HEADS)]
    unit_ids = [(c, h) for c in range(n_chunks) for h in range(HEADS)]
    new_seq = pl.program_id(0) % steps_per_seq == 0
    triu = triu_ref[...]
    mask_t = triu.astype(F32) > 0.0
    eye = eye_ref[...]
    lane_g = lax.broadcasted_iota(jnp.int32, (GATE_ROWS, tb), 1) % CHUNK
    sub_g = lax.broadcasted_iota(jnp.int32, (GATE_ROWS, CHUNK), 0)
    ones_row = (sub_g == 0).astype(BF16)
    last = slice(CHUNK - 1, CHUNK)

    st = {}
    norm, u_mm, u_act, v_mm, v_act, *mix = _gmlp_pieces(x_ref, nmix_ref, w_t_ref, lng_ref, lnb_ref, mixw_ref,
                                                        mixb_ref, ygm_ref, vlast_ref, st)
    zero = jnp.zeros((HEAD_DIM, HEAD_DIM), BF16)

    def block_diag(x, y):
        return jnp.concatenate([jnp.concatenate([x, zero], axis=1), jnp.concatenate([zero, y], axis=1)], axis=0)

    def pair_dot(lhs, rhs):
        out = _dot(jnp.concatenate(lhs, axis=1), block_diag(*rhs))
        return out[:, :HEAD_DIM], out[:, HEAD_DIM:]

    norm()
    a = st["a"]
    on_lanes = _dot_nt(w_lanes_scr[...], a)
    u_mm()
    v_mm()
    k = (_dot_nt(a, w_t_ref[K_ROW:V_ROW]) * (HEAD_DIM ** -0.5)).astype(BF16)
    lanes = lambda ref: jnp.concatenate([ref[...]] * n_chunks, axis=1)
    zr = on_lanes[LANES_G:] + lanes(bgr_ref)
    lf_r = _log_sigmoid(zr[GATE_ROWS:])
    b_r = jnp.concatenate([_dot_sel(lf_r[:, t], triu) for t in blocks], axis=1)
    q_t = on_lanes[:ML_WIDTH].astype(BF16)
    r_r = zr[:GATE_ROWS] - b_r
    p_r = r_r
    shift = 1
    while shift < CHUNK:
        p_r = jnp.maximum(p_r, jnp.where(lane_g >= shift, pltpu.roll(p_r, shift, axis=1), -jnp.inf))
        shift *= 2
    e_end = [jnp.exp(r_r[:, t] - p_r[:, t][:, last]) for t in blocks]
    u_act()
    v_t = on_lanes[LANES_V:LANES_O]
    r_c = [_sel_dot_nt(eye, r_r[:, t]) for t in blocks]
    og_t = jax.nn.sigmoid(on_lanes[LANES_O:LANES_G]) * lanes(gain_t_ref)
    v_act()

    pair_ids = [(c, h) for c in range(n_chunks) for h in range(0, HEADS, 2)]
    kq, e_intra, x1, upd = {}, {}, {}, {}
    for c, h in unit_ids:
        e_intra[c, h] = jnp.where(mask_t, jnp.exp(r_c[c][:, h:h + 1] - p_r[h:h + 1, blocks[c]]), 0.0)
    for piece in mix:
        piece()
    for c, h in pair_ids:
        t = blocks[c]
        kq[c, h], kq[c, h + 1] = pair_dot([k[t, heads[h]], k[t, heads[h + 1]]],
                                          [q_t[heads[h], t], q_t[heads[h + 1], t]])
    for c, h in pair_ids:
        t = blocks[c]
        s0, vext, vw = [], [], []
        for g in (h, h + 1):
            s0.append((kq[c, g] * e_intra[c, g]).astype(BF16))
            vext.append(jnp.concatenate([v_t[heads[g], t].astype(BF16), ones_row], axis=0))
            e_row = e_end[c][g:g + 1, :]
            vw.append(jnp.concatenate([(v_t[heads[g], t] * e_row).astype(BF16),
                                       jnp.where(sub_g == 0, e_row, 0.0).astype(BF16)], axis=0))
        x1[c, h], x1[c, h + 1] = pair_dot(vext, s0)
        upd[c, h], upd[c, h + 1] = pair_dot(vw, [k[t, heads[h]], k[t, heads[h + 1]]])

    cn_in, cm, w_inter = {}, {}, {}
    for h in range(HEADS):
        cn = jnp.where(new_seq, 0.0, cn_scr[h])
        m_prev = jnp.where(new_seq, 0.0, m_scr[h:h + 1, 0:1])
        for c in range(n_chunks):
            prow = p_r[h:h + 1, blocks[c]]
            cn_in[c, h] = cn
            cm[c, h] = jnp.maximum(m_prev, prow)
            w_inter[c, h] = jnp.exp(m_prev - cm[c, h])
            cm_last = cm[c, h][:, last]
            cn = jnp.exp(m_prev - cm_last) * cn + jnp.exp(prow[:, last] - cm_last) * upd[c, h]
            m_prev = b_r[h:h + 1, blocks[c]][:, last] + cm_last
        cn_scr[h] = cn
        m_scr[h:h + 1, :] = jnp.broadcast_to(m_prev, (1, HEAD_DIM))
    x2 = {}

    def state_matmuls(c):
        t = blocks[c]
        for h in range(0, HEADS, 2):
            x2[c, h], x2[c, h + 1] = pair_dot([cn_in[c, h].astype(BF16), cn_in[c, h + 1].astype(BF16)],
                                              [q_t[heads[h], t], q_t[heads[h + 1], t]])

    def head_outputs(c):
        t = blocks[c]
        for h, hd in enumerate(heads):
            brow, prow = b_r[h:h + 1, t], p_r[h:h + 1, t]
            nd = x1[c, h] * jnp.exp(prow - cm[c, h]) + x2[c, h] * w_inter[c, h]
            num_t, den = nd[:HEAD_DIM], nd[HEAD_DIM:HEAD_DIM + 1]
            inv = 1.0 / jnp.maximum(jnp.abs(den), jnp.exp(-(brow + cm[c, h])))
            ssq = jnp.sum(num_t * num_t, axis=0, keepdims=True)
            scale = inv * lax.rsqrt(ssq * (inv * inv) * (1.0 / HEAD_DIM) + EPS)
            yml_t_ref[hd, t] = (num_t * scale * og_t[hd, t]).astype(BF16)

    for c in range(n_chunks):
        state_matmuls(c)
    for c in range(n_chunks):
        head_outputs(c)

    @pl.when(pl.program_id(0) % steps_per_seq == steps_per_seq - 1)
    def _():
        cn_out_ref[...] = cn_scr[...]
        m_out_ref[...] = m_scr[...]


def _mixer_prompt(x2d, nmix, w_t, wgr, bgr, lng, lnb, mixw, mixb, gain_t, triu, eye, cast, batch, tb):
    n = x2d.shape[0]
    n_steps = n // tb
    steps_per_seq = n_steps // batch

    def window(w):
        rows = max(BF16_ROWS, w.shape[0] // n_steps)
        return pl.BlockSpec((rows, w.shape[1]), lambda i: (jnp.minimum(i, w.shape[0] // rows - 1), 0))

    tok = lambda w: pl.BlockSpec((tb, w), lambda i: (i, 0))
    per_seq = lambda *shape: pl.BlockSpec((None,) + shape, lambda i: (i // steps_per_seq,) + (0,) * len(shape))
    return pl.pallas_call(
        functools.partial(_mixer_prompt_kernel, steps_per_seq, len(cast)),
        grid=(n_steps,),
        in_specs=[
            tok(D_MODEL),
            _const_spec((1, D_MODEL)),
            _const_spec((W_ROWS, D_MODEL)),
            _const_spec((2 * GATE_ROWS, D_MODEL)),
            _const_spec((2 * GATE_ROWS, CHUNK)),
            _const_spec((1, GM_WIDTH)),
            _const_spec((1, GM_WIDTH)),
            _const_spec((HEADS, CHUNK, CHUNK)),
            _const_spec((HEADS, CHUNK, CHUNK)),
            _const_spec((ML_WIDTH, CHUNK)),
            _const_spec((CHUNK, CHUNK)),
            _const_spec((CHUNK, CHUNK)),
        ] + [window(w) for w in cast],
        out_specs=(tok(GM_WIDTH), pl.BlockSpec((ML_WIDTH, tb), lambda i: (0, i)), per_seq(CHUNK, GM_WIDTH),
                   per_seq(HEADS, STATE_ROWS, HEAD_DIM), per_seq(HEADS, HEAD_DIM)) + tuple(window(w) for w in cast),
        out_shape=(
            jax.ShapeDtypeStruct((n, GM_WIDTH), BF16),
            jax.ShapeDtypeStruct((ML_WIDTH, n), BF16),
            jax.ShapeDtypeStruct((batch, CHUNK, GM_WIDTH), F32),
            jax.ShapeDtypeStruct((batch, HEADS, STATE_ROWS, HEAD_DIM), F32),
            jax.ShapeDtypeStruct((batch, HEADS, HEAD_DIM), F32),
        ) + tuple(jax.ShapeDtypeStruct(w.shape, BF16) for w in cast),
        scratch_shapes=[pltpu.VMEM((HEADS, STATE_ROWS, HEAD_DIM), F32), pltpu.VMEM((HEADS, HEAD_DIM), F32),
                        pltpu.VMEM((LANES_ROWS, D_MODEL), BF16)],
        compiler_params=pltpu.CompilerParams(dimension_semantics=("arbitrary",),
                                             vmem_limit_bytes=VMEM_LIMIT),
        name="mixer_prompt",
    )(x2d, nmix, w_t, wgr, bgr, lng, lnb, mixw, mixb, gain_t, triu, eye, *cast)


def _front_kernel(x_ref, nmix_ref, w_t_ref, wgate_ref, wgate_t_ref, bcol_ref, brow_ref,
                  lng_ref, lnb_ref, mixw_ref, mixb_ref,
                  ygm_ref, q_ref, k_ref, v_ref, og_ref, gcol_ref, grow_ref, vgn_ref):
    st = {}
    for piece in _gmlp_pieces(x_ref, nmix_ref, w_t_ref, lng_ref, lnb_ref, mixw_ref, mixb_ref,
                              ygm_ref, vgn_ref, st):
        piece()
    a = st["a"]

    def proj(lo):
        return _dot_nt(a, w_t_ref[lo:lo + ML_WIDTH])

    q_ref[...] = proj(Q_ROW).astype(BF16)
    k_ref[...] = (proj(K_ROW) * (HEAD_DIM ** -0.5)).astype(BF16)
    v_ref[...] = proj(V_ROW).astype(BF16)
    og_ref[...] = jax.nn.sigmoid(proj(O_ROW))
    zc = _dot(a, wgate_ref[...]) + bcol_ref[...]
    lane = lax.broadcasted_iota(jnp.int32, zc.shape, 1)
    gcol_ref[...] = jnp.where(lane >= HEADS, _log_sigmoid(zc), zc)
    zr = _dot_nt(wgate_t_ref[...], a) + brow_ref[...]
    sub = lax.broadcasted_iota(jnp.int32, zr.shape, 0)
    grow_ref[...] = jnp.where(sub >= HEADS, _log_sigmoid(zr), zr)


def _front(x2d, nmix, w_t, wgate, wgate_t, bcol, brow, lng, lnb, mixw, mixb, tb):
    n = x2d.shape[0]
    tok = lambda w: pl.BlockSpec((tb, w), lambda i: (i, 0))
    out_shape = (
        jax.ShapeDtypeStruct((n, GM_WIDTH), BF16),
        jax.ShapeDtypeStruct((n, ML_WIDTH), BF16),
        jax.ShapeDtypeStruct((n, ML_WIDTH), BF16),
        jax.ShapeDtypeStruct((n, ML_WIDTH), BF16),
        jax.ShapeDtypeStruct((n, ML_WIDTH), F32),
        jax.ShapeDtypeStruct((n, GATE_LANES), F32),
        jax.ShapeDtypeStruct((GATE_ROWS, n), F32),
        jax.ShapeDtypeStruct((n, GM_WIDTH), F32),
    )
    return pl.pallas_call(
        _front_kernel,
        grid=(n // tb,),
        in_specs=[
            tok(D_MODEL),
            _const_spec((1, D_MODEL)),
            _const_spec((W_ROWS, D_MODEL)),
            _const_spec((D_MODEL, GATE_LANES)),
            _const_spec((GATE_ROWS, D_MODEL)),
            _const_spec((1, GATE_LANES)),
            _const_spec((GATE_ROWS, tb)),
            _const_spec((1, GM_WIDTH)),
            _const_spec((1, GM_WIDTH)),
            _const_spec((HEADS, CHUNK, CHUNK)),
            _const_spec((HEADS, CHUNK, CHUNK)),
        ],
        out_specs=(tok(GM_WIDTH), tok(ML_WIDTH), tok(ML_WIDTH), tok(ML_WIDTH), tok(ML_WIDTH),
                   tok(GATE_LANES), pl.BlockSpec((GATE_ROWS, tb), lambda i: (0, i)), tok(GM_WIDTH)),
        out_shape=out_shape,
        compiler_params=pltpu.CompilerParams(dimension_semantics=("arbitrary",),
                                             vmem_limit_bytes=VMEM_LIMIT),
        name="front",
    )(x2d, nmix, w_t, wgate, wgate_t, bcol, brow, lng, lnb, mixw, mixb)


def _intra(q, ks, igcol, bcol, igrow, brow, mprev, mask):
    d = bcol + (igrow - brow)
    g = bcol + mprev
    m_t = jnp.maximum(g, jnp.max(jnp.where(mask, d, -jnp.inf), axis=-1, keepdims=True))
    w_intra = jnp.where(mask, jnp.exp(d - m_t), 0.0)
    w_inter = jnp.exp(g - m_t)
    s = _dot_nt(q, ks) * w_intra
    return s, w_inter, m_t, g


def _head_out(num, den, m_t, gain, og):
    hh = num / jnp.maximum(jnp.abs(den), jnp.exp(-m_t))
    return (og * _rms(hh, gain)).astype(BF16)


def _mlstm_sample_kernel(seq_len, q_ref, k_ref, v_ref, og_ref, gcol_ref, grow_ref, mrep_ref, c_ref, n_ref,
                         mln_ref, tri_ref, triu_ref, sel_last_ref, expand_ref, pick_last_ref, seq_sum_ref,
                         yml_ref, c_out_ref, n_out_ref, m_out_ref):
    n_seq = CHUNK // seq_len
    tri = tri_ref[...]
    mask = tri.astype(F32) > 0.0
    gcol = gcol_ref[...]
    grow = grow_ref[...]
    bcol_all = _sel_dot(tri, gcol)
    brow_all = _dot_sel(grow, triu_ref[...])
    sel_last = sel_last_ref[...]
    expand = expand_ref[...]
    pick_last = pick_last_ref[...]
    seq_sum = seq_sum_ref[...]
    row = lax.broadcasted_iota(jnp.int32, (CHUNK, HEAD_DIM), 0)
    lane = lax.broadcasted_iota(jnp.int32, (CHUNK, HEAD_DIM), 1)
    seq_rows = [(row >= i * seq_len) & (row < (i + 1) * seq_len) for i in range(n_seq)]
    head_cols = [slice(h * HEAD_DIM, (h + 1) * HEAD_DIM) for h in range(HEADS)]
    every_head = range(HEADS)
    q = [q_ref[:, cols] for cols in head_cols]
    ks = [k_ref[:, cols] for cols in head_cols]
    v = [v_ref[:, cols] for cols in head_cols]
    zero = jnp.zeros_like(q[0])
    igcol = [gcol[:, h:h + 1] for h in every_head]
    bcol = [bcol_all[:, HEADS + h:HEADS + h + 1] for h in every_head]
    intra = [_intra(q[h], ks[h], igcol[h], bcol[h], grow[h:h + 1, :], brow_all[HEADS + h:HEADS + h + 1, :],
                    mrep_ref[:, h:h + 1], mask) for h in every_head]
    s, w_inter, m_t, g = zip(*intra)
    c_all = [c_ref[:, h] for h in every_head]
    n_all = [n_ref[:, h] for h in every_head]
    qc = [_dot(jnp.concatenate([jnp.where(m, q[h], zero) for m in seq_rows], axis=1),
               c_all[h].reshape(n_seq * HEAD_DIM, HEAD_DIM).astype(BF16)) for h in every_head]
    n_rows = [_sel_dot(expand, n_all[h]) for h in every_head]
    ends = [_sel_dot(sel_last, jnp.where(lane == 0, m_t[h], jnp.where(lane == 1, g[h],
                                                                        jnp.where(lane == 2, bcol[h], 0.0))))
            for h in every_head]
    sv = [_dot(s[h].astype(BF16), v[h]) for h in every_head]
    kw, dec, upd, dec_seq, n_inc, m_seq = [], [], [], [], [], []
    for h in every_head:
        m_new, g_last, b_last = ends[h][:, 0:1], ends[h][:, 1:2], ends[h][:, 2:3]
        w_end = jnp.exp(b_last - bcol[h] + igcol[h] - m_new)
        dec.append(jnp.exp(g_last - m_new))
        kw.append(ks[h].astype(F32) * w_end)
    for h in every_head:
        v_exp = jnp.concatenate([jnp.where(m, v[h], zero) for m in seq_rows], axis=1)
        upd.append(_dot(kw[h].T.astype(BF16), v_exp))
        dec_seq.append(_sel_dot(pick_last, jnp.broadcast_to(dec[h], (CHUNK, HEAD_DIM))))
        n_inc.append(_sel_dot(seq_sum, kw[h]))
        m_seq.append(_sel_dot(pick_last, jnp.broadcast_to(m_t[h], (CHUNK, HEAD_DIM))))
    for h, cols in enumerate(head_cols):
        qn = jnp.sum(q[h].astype(F32) * n_rows[h], axis=-1, keepdims=True)
        num = sv[h] + w_inter[h] * qc[h]
        den = jnp.sum(s[h], axis=-1, keepdims=True) + w_inter[h] * qn
        yml_ref[:, cols] = _head_out(num, den, m_t[h], mln_ref[:, cols], og_ref[:, cols])
        for i in range(n_seq):
            c_out_ref[i, h] = (dec_seq[h][i:i + 1, 0:1] * c_all[h][i]
                               + upd[h][:, i * HEAD_DIM:(i + 1) * HEAD_DIM])
        n_out_ref[:, h] = dec_seq[h] * n_all[h] + n_inc[h]
        m_out_ref[:, h] = m_seq[h]


def _mlstm_sample(q, k, v, og, gcol, grow, mrep, c0, n0, mln, tri, triu, sel_last, expand, pick_last,
                  seq_sum, seq_len):
    n = q.shape[0]
    n_seq = CHUNK // seq_len
    n_batch = n // seq_len
    tok = lambda w: pl.BlockSpec((CHUNK, w), lambda i: (i, 0))
    c_spec = pl.BlockSpec((n_seq, HEADS, HEAD_DIM, HEAD_DIM), lambda i: (i, 0, 0, 0))
    n_spec = pl.BlockSpec((n_seq, HEADS, HEAD_DIM), lambda i: (i, 0, 0))
    return pl.pallas_call(
        functools.partial(_mlstm_sample_kernel, seq_len),
        grid=(n // CHUNK,),
        in_specs=[
            tok(ML_WIDTH), tok(ML_WIDTH), tok(ML_WIDTH), tok(ML_WIDTH), tok(GATE_LANES),
            pl.BlockSpec((GATE_ROWS, CHUNK), lambda i: (0, i)),
            pl.BlockSpec((CHUNK, HEADS), lambda i: (i, 0)),
            c_spec, n_spec,
            _const_spec((1, ML_WIDTH)),
            _const_spec((CHUNK, CHUNK)), _const_spec((CHUNK, CHUNK)), _const_spec((CHUNK, CHUNK)),
            _const_spec((CHUNK, n_seq)), _const_spec((n_seq, CHUNK)), _const_spec((n_seq, CHUNK)),
        ],
        out_specs=(tok(ML_WIDTH), c_spec, n_spec, n_spec),
        out_shape=(
            jax.ShapeDtypeStruct((n, ML_WIDTH), BF16),
            jax.ShapeDtypeStruct((n_batch, HEADS, HEAD_DIM, HEAD_DIM), F32),
            jax.ShapeDtypeStruct((n_batch, HEADS, HEAD_DIM), F32),
            jax.ShapeDtypeStruct((n_batch, HEADS, HEAD_DIM), F32),
        ),
        compiler_params=pltpu.CompilerParams(dimension_semantics=("arbitrary",),
                                             vmem_limit_bytes=VMEM_LIMIT),
        name="mlstm_sample",
    )(q, k, v, og, gcol, grow, mrep, c0, n0, mln, tri, triu, sel_last, expand, pick_last, seq_sum)


def _back_math(final_norm, yml_transposed, x_ref, ygm_ref, yml_ref, pe_ref, wout_ref, nffn_ref,
               wup_ref, wdown_ref, nple_ref, wpg_ref, wpp_ref, nfin_ref, out_ref):
    ml_dot = _dot_tn if yml_transposed else _dot
    h = x_ref[...] + _dot(ygm_ref[...], wout_ref[:GM_WIDTH]) + ml_dot(yml_ref[...], wout_ref[GM_WIDTH:])
    a = _rms(h, nffn_ref[...]).astype(BF16)
    ff = D_FF // FF_SPLIT

    def hidden(c):
        f = _dot(a, wup_ref[:, c * ff:(c + 1) * ff])
        return jnp.square(jnp.maximum(f, 0.0)).astype(BF16)

    def down(f, c):
        return _dot(f, wdown_ref[c * ff:(c + 1) * ff, :])

    mlp = down(hidden(0), 0)
    for c in range(1, FF_SPLIT - 1):
        mlp = mlp + down(hidden(c), c)
    tb = x_ref.shape[0]
    pieces = (slice(0, tb // 2), slice(tb // 2, 3 * tb // 4), slice(3 * tb // 4, tb))
    f_last = hidden(FF_SPLIT - 1)
    h_new = []
    for r in (pieces[0], slice(pieces[0].stop, tb)):
        h_r = h[r] + (mlp[r] + down(f_last[r], FF_SPLIT - 1))
        h_new += [h_r[p.start - r.start:p.stop - r.start] for p in pieces if r.start <= p.start < r.stop]
    emb = _dot(pe_ref[...].astype(BF16), wpp_ref[...])
    gates = [_dot(_rms(h_p, nple_ref[...]).astype(BF16), wpg_ref[...]) for h_p in h_new]
    for r, h_p, g in zip(pieces, h_new, gates):
        y = h_p + jax.nn.sigmoid(g) * emb[r]
        if final_norm:
            y = _rms(y, nfin_ref[...])
        out_ref[r, :] = y


def _back_kernel(final_norm, yml_transposed, *refs):
    _back_math(final_norm, yml_transposed, *refs)


def _back(x2d, ygm, yml, pe2d, wout, nffn, wup, wdown, nple, wpg, wpp, nfin, final_norm, tb):
    n = x2d.shape[0]
    yml_transposed = yml.shape[0] != n
    tok = lambda w: pl.BlockSpec((tb, w), lambda i: (i, 0))
    yml_spec = pl.BlockSpec((ML_WIDTH, tb), lambda i: (0, i)) if yml_transposed else tok(ML_WIDTH)
    return pl.pallas_call(
        functools.partial(_back_kernel, final_norm, yml_transposed),
        grid=(n // tb,),
        in_specs=[
            tok(D_MODEL), tok(GM_WIDTH), yml_spec, tok(PLE_DIM),
            _const_spec((GM_WIDTH + ML_WIDTH, D_MODEL)),
            _const_spec((1, D_MODEL)),
            _const_spec((D_MODEL, D_FF)), _const_spec((D_FF, D_MODEL)),
            _const_spec((1, D_MODEL)),
            _const_spec((D_MODEL, D_MODEL)), _const_spec((PLE_DIM, D_MODEL)),
            _const_spec((1, D_MODEL)),
        ],
        out_specs=tok(D_MODEL),
        out_shape=jax.ShapeDtypeStruct((n, D_MODEL), F32),
        compiler_params=pltpu.CompilerParams(dimension_semantics=("arbitrary",),
                                             vmem_limit_bytes=VMEM_LIMIT),
        name="back",
    )(x2d, ygm, yml, pe2d, wout, nffn, wup, wdown, nple, wpg, wpp, nfin)


def _const(x):
    return jnp.asarray(np.asarray(x, np.float32), BF16)


def _block_tri(block):
    r = np.arange(CHUNK)[:, None]
    c = np.arange(CHUNK)[None, :]
    return (r // block == c // block) & (c <= r)


def _token_block(n):
    return 512 if n % 512 == 0 else CHUNK


def kernel(x_prompt, x_sample, p_prompt, p_sample, state_C, state_n, state_m, norm_mix, w_in, gm_ln_g,
           gm_ln_b, gm_ws, gm_bs, ml_b_i, ml_b_f, ml_norm, w_out, norm_ffn, w_up, w_down, norm_ple,
           w_ple_gate, w_ple_proj, norm_final):
    depth = w_in.shape[0]
    batch, seq, _ = x_prompt.shape
    dec_batch, dec_seq, _ = x_sample.shape
    n_p, n_s = batch * seq, dec_batch * dec_seq
    n_seq = CHUNK // dec_seq
    assert seq % CHUNK == 0 and CHUNK % dec_seq == 0 and n_s % CHUNK == 0
    tb_p, tb_s = 2 * _token_block(seq), _token_block(n_s)

    hp = x_prompt.reshape(n_p, D_MODEL)
    hs = x_sample.reshape(n_s, D_MODEL)

    tri_p, tri_s = _block_tri(CHUNK), _block_tri(dec_seq)
    triu_p, eye = _const(tri_p.T), _const(np.eye(CHUNK))
    r = np.arange(CHUNK)
    i = np.arange(n_seq)
    sel_last = _const(r[None, :] == (r[:, None] // dec_seq) * dec_seq + dec_seq - 1)
    expand = _const(r[:, None] // dec_seq == i[None, :])
    pick_last = _const(r[None, :] == i[:, None] * dec_seq + dec_seq - 1)
    seq_sum = _const(r[None, :] // dec_seq == i[:, None])

    outs = {k: [] for k in ("Cp", "Np", "Mp", "Vp", "Cs", "Ns", "Ms", "Vs")}
    for l in range(depth):
        row = lambda a: a[l].reshape(1, -1).astype(F32)
        w_t = w_in[l].T.astype(BF16)
        wg_t = w_t[MAIN_COLS:]
        b_i, b_f = ml_b_i[l].astype(F32), ml_b_f[l].astype(F32)
        row_pad = lambda a: jnp.pad(a, ((0, GATE_ROWS - HEADS), (0, 0)))
        wgr = jnp.concatenate([row_pad(wg_t[:HEADS]), row_pad(wg_t[HEADS:])], axis=0)
        bgr = jnp.broadcast_to(jnp.concatenate([row_pad(b_i[:, None]), row_pad(b_f[:, None])], axis=0),
                               (2 * GATE_ROWS, CHUNK))
        gain_t = jnp.broadcast_to(ml_norm[l].astype(F32)[:, None], (ML_WIDTH, CHUNK))
        wgate = jnp.pad(wg_t.T, ((0, 0), (0, GATE_LANES - 2 * HEADS)))
        wgate_t = jnp.pad(wg_t, ((0, GATE_ROWS - 2 * HEADS), (0, 0)))
        gbias = jnp.concatenate([b_i, b_f])
        bcol = jnp.pad(gbias, (0, GATE_LANES - 2 * HEADS)).reshape(1, GATE_LANES)
        brow = jnp.broadcast_to(jnp.pad(gbias, (0, GATE_ROWS - 2 * HEADS)).reshape(GATE_ROWS, 1),
                                (GATE_ROWS, tb_s))
        ws = gm_ws[l]
        mixw_p = (ws[:, :CHUNK, :CHUNK] * tri_p.astype(np.float32)).astype(BF16)
        mixb_p = jnp.broadcast_to(gm_bs[l][:, :CHUNK, None], (HEADS, CHUNK, CHUNK)).astype(F32)
        mixw_s = (jnp.tile(ws[:, :dec_seq, :dec_seq], (1, n_seq, n_seq)) * tri_s.astype(np.float32)).astype(BF16)
        mixb_s = jnp.broadcast_to(jnp.tile(gm_bs[l][:, :dec_seq], (1, n_seq))[:, :, None],
                                  (HEADS, CHUNK, CHUNK)).astype(F32)
        last = l == depth - 1
        nfin = norm_final.reshape(1, D_MODEL).astype(F32)

        back_f32 = [w[l].astype(F32) for w in (w_out, w_up, w_down, w_ple_gate, w_ple_proj)]
        ygm, yml_t, vlast, cn_t, m8, wout, wup, wdown, wpg, wpp = _mixer_prompt(
            hp, row(norm_mix), w_t, wgr, bgr, row(gm_ln_g), row(gm_ln_b), mixw_p, mixb_p,
            gain_t, triu_p, eye, back_f32, batch, tb_p)

        def back(x2d, ygm, yml, pe, tb):
            return _back(x2d, ygm, yml, pe, wout, row(norm_ffn), wup, wdown, row(norm_ple),
                         wpg, wpp, nfin, last, tb)

        hp = back(hp, ygm, yml_t, p_prompt[l].reshape(n_p, PLE_DIM), tb_p)
        outs["Cp"].append(jnp.swapaxes(cn_t[:, :, :HEAD_DIM, :], -1, -2))
        outs["Np"].append(cn_t[:, :, HEAD_DIM, :])
        outs["Mp"].append(m8[:, :, 0])
        outs["Vp"].append(vlast)

        ygm, q, k, v, og, gcol, grow, vgn = _front(
            hs, row(norm_mix), w_t, wgate, wgate_t, bcol, brow, row(gm_ln_g), row(gm_ln_b), mixw_s, mixb_s,
            tb_s)
        mrep = jnp.repeat(state_m[l].astype(F32), dec_seq, axis=0)
        yml, c_new, n_new, m_new = _mlstm_sample(
            q, k, v, og, gcol, grow, mrep, state_C[l].astype(F32), state_n[l].astype(F32), row(ml_norm),
            _const(tri_s), _const(tri_s.T), sel_last, expand, pick_last, seq_sum, dec_seq)
        hs = back(hs, ygm, yml, p_sample[l].reshape(n_s, PLE_DIM), tb_s)
        outs["Cs"].append(c_new)
        outs["Ns"].append(n_new)
        outs["Ms"].append(m_new[..., 0])
        outs["Vs"].append(vgn.reshape(dec_batch, dec_seq, GM_WIDTH))

    st = lambda k: outs[k][0][None] if depth == 1 else jnp.stack(outs[k])
    return (hp.reshape(batch, seq, D_MODEL), hs.reshape(dec_batch, dec_seq, D_MODEL),
            st("Cp"), st("Np"), st("Mp"), st("Vp"), st("Cs"), st("Ns"), st("Ms"), st("Vs"))
```

```python
import functools

import jax
import numpy as np
import jax.numpy as jnp
from jax import lax
from jax.experimental import pallas as pl
from jax.experimental.pallas import tpu as pltpu

F32 = jnp.float32
BF16 = jnp.bfloat16

D_MODEL = 1024
GM_WIDTH = 512
ML_WIDTH = 512
HEADS = 4
HEAD_DIM = 128
D_FF = 4096
PLE_DIM = 256
EPS = 1e-6
CHUNK = 128
MAIN_COLS = 2 * GM_WIDTH + 4 * ML_WIDTH
Q_ROW = 2 * GM_WIDTH
K_ROW = Q_ROW + ML_WIDTH
V_ROW = K_ROW + ML_WIDTH
O_ROW = V_ROW + ML_WIDTH
W_ROWS = MAIN_COLS + 2 * HEADS
LANES_V = ML_WIDTH
LANES_O = 2 * ML_WIDTH
LANES_G = 3 * ML_WIDTH
LANES_ROWS = LANES_G + 2 * 16
GATE_LANES = 128
GATE_ROWS = 16
STATE_ROWS = HEAD_DIM + 16
FF_SPLIT = 4
BF16_ROWS = 16
VMEM_LIMIT = 60 * 1024 * 1024


def _dot(a, b):
    return jnp.dot(a, b, preferred_element_type=F32)


def _dot_nt(a, b):
    return lax.dot_general(a, b, (((1,), (1,)), ((), ())), preferred_element_type=F32)


def _dot_tn(a, b):
    return lax.dot_general(a, b, (((0,), (0,)), ((), ())), preferred_element_type=F32)


def _split3(x):
    x1 = x.astype(BF16)
    r = x - x1.astype(F32)
    x2 = r.astype(BF16)
    r = r - x2.astype(F32)
    return x1, x2, r.astype(BF16)


def _sel_dot(sel, x):
    p1, p2, p3 = _split3(x)
    return _dot(sel, p1) + _dot(sel, p2) + _dot(sel, p3)


def _dot_sel(x, sel):
    p1, p2, p3 = _split3(x)
    return _dot(p1, sel) + _dot(p2, sel) + _dot(p3, sel)


def _sel_dot_nt(sel, x):
    p1, p2, p3 = _split3(x)
    return _dot_nt(sel, p1) + _dot_nt(sel, p2) + _dot_nt(sel, p3)


def _rms(x, g):
    return x * lax.rsqrt(jnp.mean(x * x, axis=-1, keepdims=True) + EPS) * g


def _log_sigmoid(x):
    return -(jnp.maximum(-x, 0.0) + jnp.log1p(jnp.exp(-jnp.abs(x))))


def _const_spec(shape):
    nd = len(shape)
    return pl.BlockSpec(shape, lambda *_: (0,) * nd, pipeline_mode=pl.Buffered(1))


def _gmlp_pieces(x_ref, nmix_ref, w_t_ref, lng_ref, lnb_ref, mixw_ref, mixb_ref, ygm_ref, vgn_ref, st):
    tb = x_ref.shape[0]

    def norm():
        st["a"] = _rms(x_ref[...], nmix_ref[...]).astype(BF16)

    def u_mm():
        st["u"] = _dot_nt(st["a"], w_t_ref[:GM_WIDTH])

    def u_act():
        st["u"] = jax.nn.gelu(st["u"])

    def v_mm():
        st["vg"] = _dot_nt(st["a"], w_t_ref[GM_WIDTH:2 * GM_WIDTH])

    def v_act():
        vg = jax.nn.gelu(st.pop("vg"))
        mu = jnp.mean(vg, axis=-1, keepdims=True)
        var = jnp.mean(jnp.square(vg - mu), axis=-1, keepdims=True)
        vgn = (vg - mu) * lax.rsqrt(var + EPS) * lng_ref[...] + lnb_ref[...]
        vgn_ref[...] = vgn[tb - vgn_ref.shape[0]:, :]
        st["vgb"] = vgn.astype(BF16)

    def mix(c):
        rows = slice(c * CHUNK, (c + 1) * CHUNK)
        for h in range(HEADS):
            cols = slice(h * HEAD_DIM, (h + 1) * HEAD_DIM)
            s = _dot(mixw_ref[h], st["vgb"][rows, cols]) + mixb_ref[h]
            ygm_ref[rows, cols] = (st["u"][rows, cols] * s).astype(BF16)

    return [norm, u_mm, u_act, v_mm, v_act] + [functools.partial(mix, c) for c in range(tb // CHUNK)]


def _mixer_prompt_kernel(steps_per_seq, n_cast, x_ref, nmix_ref, w_t_ref, wgr_ref, bgr_ref,
                         lng_ref, lnb_ref, mixw_ref, mixb_ref, gain_t_ref, triu_ref, eye_ref, *rest):
    cast_in, rest = rest[:n_cast], rest[n_cast:]
    ygm_ref, yml_t_ref, vlast_ref, cn_out_ref, m_out_ref = rest[:5]
    cast_out, (cn_scr, m_scr, w_lanes_scr) = rest[5:5 + n_cast], rest[5 + n_cast:]

    @pl.when(pl.program_id(0) == 0)
    def _():
        w_lanes_scr[:LANES_V] = w_t_ref[Q_ROW:K_ROW]
        w_lanes_scr[LANES_V:LANES_G] = w_t_ref[V_ROW:MAIN_COLS]
        w_lanes_scr[LANES_G:] = wgr_ref[...]

    tb = x_ref.shape[0]
    n_chunks = tb // CHUNK
    blocks = [slice(c * CHUNK, (c + 1) * CHUNK) for c in range(n_chunks)]
    heads = [slice(h * HEAD_DIM, (h + 1) * HEAD_DIM) for h in range(HEADS)]
    unit_ids = [(c, h) for c in range(n_chunks) for h in range(HEADS)]
    new_seq = pl.program_id(0) % steps_per_seq == 0
    triu = triu_ref[...]
    mask_t = triu.astype(F32) > 0.0
    eye = eye_ref[...]
    lane_g = lax.broadcasted_iota(jnp.int32, (GATE_ROWS, tb), 1) % CHUNK
    sub_g = lax.broadcasted_iota(jnp.int32, (GATE_ROWS, CHUNK), 0)
    ones_row = (sub_g == 0).astype(BF16)
    last = slice(CHUNK - 1, CHUNK)

    st = {}
    norm, u_mm, u_act, v_mm, v_act, *mix = _gmlp_pieces(x_ref, nmix_ref, w_t_ref, lng_ref, lnb_ref, mixw_ref,
                                                        mixb_ref, ygm_ref, vlast_ref, st)
    zero = jnp.zeros((HEAD_DIM, HEAD_DIM), BF16)

    def block_diag(x, y):
        return jnp.concatenate([jnp.concatenate([x, zero], axis=1), jnp.concatenate([zero, y], axis=1)], axis=0)

    def pair_dot(lhs, rhs):
        out = _dot(jnp.concatenate(lhs, axis=1), block_diag(*rhs))
        return out[:, :HEAD_DIM], out[:, HEAD_DIM:]

    norm()
    a = st["a"]
    u_mm()
    v_mm()
    on_lanes = _dot_nt(w_lanes_scr[...], a)
    u_act()
    v_act()
    k = (_dot_nt(a, w_t_ref[K_ROW:V_ROW]) * (HEAD_DIM ** -0.5)).astype(BF16)
    for src, dst in zip(cast_in, cast_out):
        dst[...] = src[...].astype(BF16)
    for piece in mix:
        piece()
    lanes = lambda ref: jnp.concatenate([ref[...]] * n_chunks, axis=1)
    zr = on_lanes[LANES_G:] + lanes(bgr_ref)
    lf_r = _log_sigmoid(zr[GATE_ROWS:])
    b_r = jnp.concatenate([_dot_sel(lf_r[:, t], triu) for t in blocks], axis=1)
    q_t = on_lanes[:ML_WIDTH].astype(BF16)
    r_r = zr[:GATE_ROWS] - b_r
    p_r = r_r
    shift = 1
    while shift < CHUNK:
        p_r = jnp.maximum(p_r, jnp.where(lane_g >= shift, pltpu.roll(p_r, shift, axis=1), -jnp.inf))
        shift *= 2
    e_end = [jnp.exp(r_r[:, t] - p_r[:, t][:, last]) for t in blocks]
    v_t = on_lanes[LANES_V:LANES_O]
    r_c = [_sel_dot_nt(eye, r_r[:, t]) for t in blocks]
    og_t = jax.nn.sigmoid(on_lanes[LANES_O:LANES_G]) * lanes(gain_t_ref)

    pair_ids = [(c, h) for c in range(n_chunks) for h in range(0, HEADS, 2)]
    kq, e_intra, x1, upd = {}, {}, {}, {}
    for c, h in unit_ids:
        e_intra[c, h] = jnp.where(mask_t, jnp.exp(r_c[c][:, h:h + 1] - p_r[h:h + 1, blocks[c]]), 0.0)
    for c, h in pair_ids:
        t = blocks[c]
        kq[c, h], kq[c, h + 1] = pair_dot([k[t, heads[h]], k[t, heads[h + 1]]],
                                          [q_t[heads[h], t], q_t[heads[h + 1], t]])
    for c, h in pair_ids:
        t = blocks[c]
        s0, vext, vw = [], [], []
        for g in (h, h + 1):
            s0.append((kq[c, g] * e_intra[c, g]).astype(BF16))
            vext.append(jnp.concatenate([v_t[heads[g], t].astype(BF16), ones_row], axis=0))
            e_row = e_end[c][g:g + 1, :]
            vw.append(jnp.concatenate([(v_t[heads[g], t] * e_row).astype(BF16),
                                       jnp.where(sub_g == 0, e_row, 0.0).astype(BF16)], axis=0))
        x1[c, h], x1[c, h + 1] = pair_dot(vext, s0)
        upd[c, h], upd[c, h + 1] = pair_dot(vw, [k[t, heads[h]], k[t, heads[h + 1]]])

    cn_in, cm, w_inter = {}, {}, {}
    for h in range(HEADS):
        cn = jnp.where(new_seq, 0.0, cn_scr[h])
        m_prev = jnp.where(new_seq, 0.0, m_scr[h:h + 1, 0:1])
        for c in range(n_chunks):
            prow = p_r[h:h + 1, blocks[c]]
            cn_in[c, h] = cn
            cm[c, h] = jnp.maximum(m_prev, prow)
            w_inter[c, h] = jnp.exp(m_prev - cm[c, h])
            cm_last = cm[c, h][:, last]
            cn = jnp.exp(m_prev - cm_last) * cn + jnp.exp(prow[:, last] - cm_last) * upd[c, h]
            m_prev = b_r[h:h + 1, blocks[c]][:, last] + cm_last
        cn_scr[h] = cn
        m_scr[h:h + 1, :] = jnp.broadcast_to(m_prev, (1, HEAD_DIM))
    x2 = {}

    def state_matmuls(c):
        t = blocks[c]
        for h in range(0, HEADS, 2):
            x2[c, h], x2[c, h + 1] = pair_dot([cn_in[c, h].astype(BF16), cn_in[c, h + 1].astype(BF16)],
                                              [q_t[heads[h], t], q_t[heads[h + 1], t]])

    def head_outputs(c):
        t = blocks[c]
        for h, hd in enumerate(heads):
            brow, prow = b_r[h:h + 1, t], p_r[h:h + 1, t]
            nd = x1[c, h] * jnp.exp(prow - cm[c, h]) + x2[c, h] * w_inter[c, h]
            num_t, den = nd[:HEAD_DIM], nd[HEAD_DIM:HEAD_DIM + 1]
            inv = 1.0 / jnp.maximum(jnp.abs(den), jnp.exp(-(brow + cm[c, h])))
            ssq = jnp.sum(num_t * num_t, axis=0, keepdims=True)
            scale = inv * lax.rsqrt(ssq * (inv * inv) * (1.0 / HEAD_DIM) + EPS)
            yml_t_ref[hd, t] = (num_t * scale * og_t[hd, t]).astype(BF16)

    for c in range(n_chunks):
        state_matmuls(c)
    for c in range(n_chunks):
        head_outputs(c)

    @pl.when(pl.program_id(0) % steps_per_seq == steps_per_seq - 1)
    def _():
        cn_out_ref[...] = cn_scr[...]
        m_out_ref[...] = m_scr[...]


def _mixer_prompt(x2d, nmix, w_t, wgr, bgr, lng, lnb, mixw, mixb, gain_t, triu, eye, cast, batch, tb):
    n = x2d.shape[0]
    n_steps = n // tb
    steps_per_seq = n_steps // batch

    def window(w):
        rows = max(BF16_ROWS, w.shape[0] // n_steps)
        return pl.BlockSpec((rows, w.shape[1]), lambda i: (jnp.minimum(i, w.shape[0] // rows - 1), 0))

    tok = lambda w: pl.BlockSpec((tb, w), lambda i: (i, 0))
    per_seq = lambda *shape: pl.BlockSpec((None,) + shape, lambda i: (i // steps_per_seq,) + (0,) * len(shape))
    return pl.pallas_call(
        functools.partial(_mixer_prompt_kernel, steps_per_seq, len(cast)),
        grid=(n_steps,),
        in_specs=[
            tok(D_MODEL),
            _const_spec((1, D_MODEL)),
            _const_spec((W_ROWS, D_MODEL)),
            _const_spec((2 * GATE_ROWS, D_MODEL)),
            _const_spec((2 * GATE_ROWS, CHUNK)),
            _const_spec((1, GM_WIDTH)),
            _const_spec((1, GM_WIDTH)),
            _const_spec((HEADS, CHUNK, CHUNK)),
            _const_spec((HEADS, CHUNK, CHUNK)),
            _const_spec((ML_WIDTH, CHUNK)),
            _const_spec((CHUNK, CHUNK)),
            _const_spec((CHUNK, CHUNK)),
        ] + [window(w) for w in cast],
        out_specs=(tok(GM_WIDTH), pl.BlockSpec((ML_WIDTH, tb), lambda i: (0, i)), per_seq(CHUNK, GM_WIDTH),
                   per_seq(HEADS, STATE_ROWS, HEAD_DIM), per_seq(HEADS, HEAD_DIM)) + tuple(window(w) for w in cast),
        out_shape=(
            jax.ShapeDtypeStruct((n, GM_WIDTH), BF16),
            jax.ShapeDtypeStruct((ML_WIDTH, n), BF16),
            jax.ShapeDtypeStruct((batch, CHUNK, GM_WIDTH), F32),
            jax.ShapeDtypeStruct((batch, HEADS, STATE_ROWS, HEAD_DIM), F32),
            jax.ShapeDtypeStruct((batch, HEADS, HEAD_DIM), F32),
        ) + tuple(jax.ShapeDtypeStruct(w.shape, BF16) for w in cast),
        scratch_shapes=[pltpu.VMEM((HEADS, STATE_ROWS, HEAD_DIM), F32), pltpu.VMEM((HEADS, HEAD_DIM), F32),
                        pltpu.VMEM((LANES_ROWS, D_MODEL), BF16)],
        compiler_params=pltpu.CompilerParams(dimension_semantics=("arbitrary",),
                                             vmem_limit_bytes=VMEM_LIMIT),
        name="mixer_prompt",
    )(x2d, nmix, w_t, wgr, bgr, lng, lnb, mixw, mixb, gain_t, triu, eye, *cast)


def _front_kernel(x_ref, nmix_ref, w_t_ref, wgate_ref, wgate_t_ref, bcol_ref, brow_ref,
                  lng_ref, lnb_ref, mixw_ref, mixb_ref,
                  ygm_ref, q_ref, k_ref, v_ref, og_ref, gcol_ref, grow_ref, vgn_ref):
    st = {}
    for piece in _gmlp_pieces(x_ref, nmix_ref, w_t_ref, lng_ref, lnb_ref, mixw_ref, mixb_ref,
                              ygm_ref, vgn_ref, st):
        piece()
    a = st["a"]

    def proj(lo):
        return _dot_nt(a, w_t_ref[lo:lo + ML_WIDTH])

    q_ref[...] = proj(Q_ROW).astype(BF16)
    k_ref[...] = (proj(K_ROW) * (HEAD_DIM ** -0.5)).astype(BF16)
    v_ref[...] = proj(V_ROW).astype(BF16)
    og_ref[...] = jax.nn.sigmoid(proj(O_ROW))
    zc = _dot(a, wgate_ref[...]) + bcol_ref[...]
    lane = lax.broadcasted_iota(jnp.int32, zc.shape, 1)
    gcol_ref[...] = jnp.where(lane >= HEADS, _log_sigmoid(zc), zc)
    zr = _dot_nt(wgate_t_ref[...], a) + brow_ref[...]
    sub = lax.broadcasted_iota(jnp.int32, zr.shape, 0)
    grow_ref[...] = jnp.where(sub >= HEADS, _log_sigmoid(zr), zr)


def _front(x2d, nmix, w_t, wgate, wgate_t, bcol, brow, lng, lnb, mixw, mixb, tb):
    n = x2d.shape[0]
    tok = lambda w: pl.BlockSpec((tb, w), lambda i: (i, 0))
    out_shape = (
        jax.ShapeDtypeStruct((n, GM_WIDTH), BF16),
        jax.ShapeDtypeStruct((n, ML_WIDTH), BF16),
        jax.ShapeDtypeStruct((n, ML_WIDTH), BF16),
        jax.ShapeDtypeStruct((n, ML_WIDTH), BF16),
        jax.ShapeDtypeStruct((n, ML_WIDTH), F32),
        jax.ShapeDtypeStruct((n, GATE_LANES), F32),
        jax.ShapeDtypeStruct((GATE_ROWS, n), F32),
        jax.ShapeDtypeStruct((n, GM_WIDTH), F32),
    )
    return pl.pallas_call(
        _front_kernel,
        grid=(n // tb,),
        in_specs=[
            tok(D_MODEL),
            _const_spec((1, D_MODEL)),
            _const_spec((W_ROWS, D_MODEL)),
            _const_spec((D_MODEL, GATE_LANES)),
            _const_spec((GATE_ROWS, D_MODEL)),
            _const_spec((1, GATE_LANES)),
            _const_spec((GATE_ROWS, tb)),
            _const_spec((1, GM_WIDTH)),
            _const_spec((1, GM_WIDTH)),
            _const_spec((HEADS, CHUNK, CHUNK)),
            _const_spec((HEADS, CHUNK, CHUNK)),
        ],
        out_specs=(tok(GM_WIDTH), tok(ML_WIDTH), tok(ML_WIDTH), tok(ML_WIDTH), tok(ML_WIDTH),
                   tok(GATE_LANES), pl.BlockSpec((GATE_ROWS, tb), lambda i: (0, i)), tok(GM_WIDTH)),
        out_shape=out_shape,
        compiler_params=pltpu.CompilerParams(dimension_semantics=("arbitrary",),
                                             vmem_limit_bytes=VMEM_LIMIT),
        name="front",
    )(x2d, nmix, w_t, wgate, wgate_t, bcol, brow, lng, lnb, mixw, mixb)


def _intra(q, ks, igcol, bcol, igrow, brow, mprev, mask):
    d = bcol + (igrow - brow)
    g = bcol + mprev
    m_t = jnp.maximum(g, jnp.max(jnp.where(mask, d, -jnp.inf), axis=-1, keepdims=True))
    w_intra = jnp.where(mask, jnp.exp(d - m_t), 0.0)
    w_inter = jnp.exp(g - m_t)
    s = _dot_nt(q, ks) * w_intra
    return s, w_inter, m_t, g


def _head_out(num, den, m_t, gain, og):
    hh = num / jnp.maximum(jnp.abs(den), jnp.exp(-m_t))
    return (og * _rms(hh, gain)).astype(BF16)


def _mlstm_sample_kernel(seq_len, q_ref, k_ref, v_ref, og_ref, gcol_ref, grow_ref, mrep_ref, c_ref, n_ref,
                         mln_ref, tri_ref, triu_ref, sel_last_ref, expand_ref, pick_last_ref, seq_sum_ref,
                         yml_ref, c_out_ref, n_out_ref, m_out_ref):
    n_seq = CHUNK // seq_len
    tri = tri_ref[...]
    mask = tri.astype(F32) > 0.0
    gcol = gcol_ref[...]
    grow = grow_ref[...]
    bcol_all = _sel_dot(tri, gcol)
    brow_all = _dot_sel(grow, triu_ref[...])
    sel_last = sel_last_ref[...]
    expand = expand_ref[...]
    pick_last = pick_last_ref[...]
    seq_sum = seq_sum_ref[...]
    row = lax.broadcasted_iota(jnp.int32, (CHUNK, HEAD_DIM), 0)
    lane = lax.broadcasted_iota(jnp.int32, (CHUNK, HEAD_DIM), 1)
    seq_rows = [(row >= i * seq_len) & (row < (i + 1) * seq_len) for i in range(n_seq)]
    head_cols = [slice(h * HEAD_DIM, (h + 1) * HEAD_DIM) for h in range(HEADS)]
    every_head = range(HEADS)
    q = [q_ref[:, cols] for cols in head_cols]
    ks = [k_ref[:, cols] for cols in head_cols]
    v = [v_ref[:, cols] for cols in head_cols]
    zero = jnp.zeros_like(q[0])
    igcol = [gcol[:, h:h + 1] for h in every_head]
    bcol = [bcol_all[:, HEADS + h:HEADS + h + 1] for h in every_head]
    intra = [_intra(q[h], ks[h], igcol[h], bcol[h], grow[h:h + 1, :], brow_all[HEADS + h:HEADS + h + 1, :],
                    mrep_ref[:, h:h + 1], mask) for h in every_head]
    s, w_inter, m_t, g = zip(*intra)
    c_all = [c_ref[:, h] for h in every_head]
    n_all = [n_ref[:, h] for h in every_head]
    qc = [_dot(jnp.concatenate([jnp.where(m, q[h], zero) for m in seq_rows], axis=1),
               c_all[h].reshape(n_seq * HEAD_DIM, HEAD_DIM).astype(BF16)) for h in every_head]
    n_rows = [_sel_dot(expand, n_all[h]) for h in every_head]
    ends = [_sel_dot(sel_last, jnp.where(lane == 0, m_t[h], jnp.where(lane == 1, g[h],
                                                                        jnp.where(lane == 2, bcol[h], 0.0))))
            for h in every_head]
    sv = [_dot(s[h].astype(BF16), v[h]) for h in every_head]
    kw, dec, upd, dec_seq, n_inc, m_seq = [], [], [], [], [], []
    for h in every_head:
        m_new, g_last, b_last = ends[h][:, 0:1], ends[h][:, 1:2], ends[h][:, 2:3]
        w_end = jnp.exp(b_last - bcol[h] + igcol[h] - m_new)
        dec.append(jnp.exp(g_last - m_new))
        kw.append(ks[h].astype(F32) * w_end)
    for h in every_head:
        v_exp = jnp.concatenate([jnp.where(m, v[h], zero) for m in seq_rows], axis=1)
        upd.append(_dot(kw[h].T.astype(BF16), v_exp))
        dec_seq.append(_sel_dot(pick_last, jnp.broadcast_to(dec[h], (CHUNK, HEAD_DIM))))
        n_inc.append(_sel_dot(seq_sum, kw[h]))
        m_seq.append(_sel_dot(pick_last, jnp.broadcast_to(m_t[h], (CHUNK, HEAD_DIM))))
    for h, cols in enumerate(head_cols):
        qn = jnp.sum(q[h].astype(F32) * n_rows[h], axis=-1, keepdims=True)
        num = sv[h] + w_inter[h] * qc[h]
        den = jnp.sum(s[h], axis=-1, keepdims=True) + w_inter[h] * qn
        yml_ref[:, cols] = _head_out(num, den, m_t[h], mln_ref[:, cols], og_ref[:, cols])
        for i in range(n_seq):
            c_out_ref[i, h] = (dec_seq[h][i:i + 1, 0:1] * c_all[h][i]
                               + upd[h][:, i * HEAD_DIM:(i + 1) * HEAD_DIM])
        n_out_ref[:, h] = dec_seq[h] * n_all[h] + n_inc[h]
        m_out_ref[:, h] = m_seq[h]


def _mlstm_sample(q, k, v, og, gcol, grow, mrep, c0, n0, mln, tri, triu, sel_last, expand, pick_last,
                  seq_sum, seq_len):
    n = q.shape[0]
    n_seq = CHUNK // seq_len
    n_batch = n // seq_len
    tok = lambda w: pl.BlockSpec((CHUNK, w), lambda i: (i, 0))
    c_spec = pl.BlockSpec((n_seq, HEADS, HEAD_DIM, HEAD_DIM), lambda i: (i, 0, 0, 0))
    n_spec = pl.BlockSpec((n_seq, HEADS, HEAD_DIM), lambda i: (i, 0, 0))
    return pl.pallas_call(
        functools.partial(_mlstm_sample_kernel, seq_len),
        grid=(n // CHUNK,),
        in_specs=[
            tok(ML_WIDTH), tok(ML_WIDTH), tok(ML_WIDTH), tok(ML_WIDTH), tok(GATE_LANES),
            pl.BlockSpec((GATE_ROWS, CHUNK), lambda i: (0, i)),
            pl.BlockSpec((CHUNK, HEADS), lambda i: (i, 0)),
            c_spec, n_spec,
            _const_spec((1, ML_WIDTH)),
            _const_spec((CHUNK, CHUNK)), _const_spec((CHUNK, CHUNK)), _const_spec((CHUNK, CHUNK)),
            _const_spec((CHUNK, n_seq)), _const_spec((n_seq, CHUNK)), _const_spec((n_seq, CHUNK)),
        ],
        out_specs=(tok(ML_WIDTH), c_spec, n_spec, n_spec),
        out_shape=(
            jax.ShapeDtypeStruct((n, ML_WIDTH), BF16),
            jax.ShapeDtypeStruct((n_batch, HEADS, HEAD_DIM, HEAD_DIM), F32),
            jax.ShapeDtypeStruct((n_batch, HEADS, HEAD_DIM), F32),
            jax.ShapeDtypeStruct((n_batch, HEADS, HEAD_DIM), F32),
        ),
        compiler_params=pltpu.CompilerParams(dimension_semantics=("arbitrary",),
                                             vmem_limit_bytes=VMEM_LIMIT),
        name="mlstm_sample",
    )(q, k, v, og, gcol, grow, mrep, c0, n0, mln, tri, triu, sel_last, expand, pick_last, seq_sum)


def _back_math(final_norm, yml_transposed, x_ref, ygm_ref, yml_ref, pe_ref, wout_ref, nffn_ref,
               wup_ref, wdown_ref, nple_ref, wpg_ref, wpp_ref, nfin_ref, out_ref):
    ml_dot = _dot_tn if yml_transposed else _dot
    h = x_ref[...] + _dot(ygm_ref[...], wout_ref[:GM_WIDTH]) + ml_dot(yml_ref[...], wout_ref[GM_WIDTH:])
    a = _rms(h, nffn_ref[...]).astype(BF16)
    ff = D_FF // FF_SPLIT

    def hidden(c):
        f = _dot(a, wup_ref[:, c * ff:(c + 1) * ff])
        return jnp.square(jnp.maximum(f, 0.0)).astype(BF16)

    def down(f, c):
        return _dot(f, wdown_ref[c * ff:(c + 1) * ff, :])

    mlp = down(hidden(0), 0)
    for c in range(1, FF_SPLIT - 1):
        mlp = mlp + down(hidden(c), c)
    tb = x_ref.shape[0]
    pieces = (slice(0, tb // 2), slice(tb // 2, 3 * tb // 4), slice(3 * tb // 4, tb))
    f_last = hidden(FF_SPLIT - 1)
    h_new = []
    for r in (pieces[0], slice(pieces[0].stop, tb)):
        h_r = h[r] + (mlp[r] + down(f_last[r], FF_SPLIT - 1))
        h_new += [h_r[p.start - r.start:p.stop - r.start] for p in pieces if r.start <= p.start < r.stop]
    emb = _dot(pe_ref[...].astype(BF16), wpp_ref[...])
    gates = [_dot(_rms(h_p, nple_ref[...]).astype(BF16), wpg_ref[...]) for h_p in h_new]
    for r, h_p, g in zip(pieces, h_new, gates):
        y = h_p + jax.nn.sigmoid(g) * emb[r]
        if final_norm:
            y = _rms(y, nfin_ref[...])
        out_ref[r, :] = y


def _back_kernel(final_norm, yml_transposed, *refs):
    _back_math(final_norm, yml_transposed, *refs)


def _back(x2d, ygm, yml, pe2d, wout, nffn, wup, wdown, nple, wpg, wpp, nfin, final_norm, tb):
    n = x2d.shape[0]
    yml_transposed = yml.shape[0] != n
    tok = lambda w: pl.BlockSpec((tb, w), lambda i: (i, 0))
    yml_spec = pl.BlockSpec((ML_WIDTH, tb), lambda i: (0, i)) if yml_transposed else tok(ML_WIDTH)
    return pl.pallas_call(
        functools.partial(_back_kernel, final_norm, yml_transposed),
        grid=(n // tb,),
        in_specs=[
            tok(D_MODEL), tok(GM_WIDTH), yml_spec, tok(PLE_DIM),
            _const_spec((GM_WIDTH + ML_WIDTH, D_MODEL)),
            _const_spec((1, D_MODEL)),
            _const_spec((D_MODEL, D_FF)), _const_spec((D_FF, D_MODEL)),
            _const_spec((1, D_MODEL)),
            _const_spec((D_MODEL, D_MODEL)), _const_spec((PLE_DIM, D_MODEL)),
            _const_spec((1, D_MODEL)),
        ],
        out_specs=tok(D_MODEL),
        out_shape=jax.ShapeDtypeStruct((n, D_MODEL), F32),
        compiler_params=pltpu.CompilerParams(dimension_semantics=("arbitrary",),
                                             vmem_limit_bytes=VMEM_LIMIT),
        name="back",
    )(x2d, ygm, yml, pe2d, wout, nffn, wup, wdown, nple, wpg, wpp, nfin)


def _const(x):
    return jnp.asarray(np.asarray(x, np.float32), BF16)


def _block_tri(block):
    r = np.arange(CHUNK)[:, None]
    c = np.arange(CHUNK)[None, :]
    return (r // block == c // block) & (c <= r)


def _token_block(n):
    return 512 if n % 512 == 0 else CHUNK


def kernel(x_prompt, x_sample, p_prompt, p_sample, state_C, state_n, state_m, norm_mix, w_in, gm_ln_g,
           gm_ln_b, gm_ws, gm_bs, ml_b_i, ml_b_f, ml_norm, w_out, norm_ffn, w_up, w_down, norm_ple,
           w_ple_gate, w_ple_proj, norm_final):
    depth = w_in.shape[0]
    batch, seq, _ = x_prompt.shape
    dec_batch, dec_seq, _ = x_sample.shape
    n_p, n_s = batch * seq, dec_batch * dec_seq
    n_seq = CHUNK // dec_seq
    assert seq % CHUNK == 0 and CHUNK % dec_seq == 0 and n_s % CHUNK == 0
    tb_p, tb_s = 2 * _token_block(seq), _token_block(n_s)

    hp = x_prompt.reshape(n_p, D_MODEL)
    hs = x_sample.reshape(n_s, D_MODEL)

    tri_p, tri_s = _block_tri(CHUNK), _block_tri(dec_seq)
    triu_p, eye = _const(tri_p.T), _const(np.eye(CHUNK))
    r = np.arange(CHUNK)
    i = np.arange(n_seq)
    sel_last = _const(r[None, :] == (r[:, None] // dec_seq) * dec_seq + dec_seq - 1)
    expand = _const(r[:, None] // dec_seq == i[None, :])
    pick_last = _const(r[None, :] == i[:, None] * dec_seq + dec_seq - 1)
    seq_sum = _const(r[None, :] // dec_seq == i[:, None])

    outs = {k: [] for k in ("Cp", "Np", "Mp", "Vp", "Cs", "Ns", "Ms", "Vs")}
    for l in range(depth):
        row = lambda a: a[l].reshape(1, -1).astype(F32)
        w_t = w_in[l].T.astype(BF16)
        wg_t = w_t[MAIN_COLS:]
        b_i, b_f = ml_b_i[l].astype(F32), ml_b_f[l].astype(F32)
        row_pad = lambda a: jnp.pad(a, ((0, GATE_ROWS - HEADS), (0, 0)))
        wgr = jnp.concatenate([row_pad(wg_t[:HEADS]), row_pad(wg_t[HEADS:])], axis=0)
        bgr = jnp.broadcast_to(jnp.concatenate([row_pad(b_i[:, None]), row_pad(b_f[:, None])], axis=0),
                               (2 * GATE_ROWS, CHUNK))
        gain_t = jnp.broadcast_to(ml_norm[l].astype(F32)[:, None], (ML_WIDTH, CHUNK))
        wgate = jnp.pad(wg_t.T, ((0, 0), (0, GATE_LANES - 2 * HEADS)))
        wgate_t = jnp.pad(wg_t, ((0, GATE_ROWS - 2 * HEADS), (0, 0)))
        gbias = jnp.concatenate([b_i, b_f])
        bcol = jnp.pad(gbias, (0, GATE_LANES - 2 * HEADS)).reshape(1, GATE_LANES)
        brow = jnp.broadcast_to(jnp.pad(gbias, (0, GATE_ROWS - 2 * HEADS)).reshape(GATE_ROWS, 1),
                                (GATE_ROWS, tb_s))
        ws = gm_ws[l]
        mixw_p = (ws[:, :CHUNK, :CHUNK] * tri_p.astype(np.float32)).astype(BF16)
        mixb_p = jnp.broadcast_to(gm_bs[l][:, :CHUNK, None], (HEADS, CHUNK, CHUNK)).astype(F32)
        mixw_s = (jnp.tile(ws[:, :dec_seq, :dec_seq], (1, n_seq, n_seq)) * tri_s.astype(np.float32)).astype(BF16)
        mixb_s = jnp.broadcast_to(jnp.tile(gm_bs[l][:, :dec_seq], (1, n_seq))[:, :, None],
                                  (HEADS, CHUNK, CHUNK)).astype(F32)
        last = l == depth - 1
        nfin = norm_final.reshape(1, D_MODEL).astype(F32)

        back_f32 = [w[l].astype(F32) for w in (w_out, w_up, w_down, w_ple_gate, w_ple_proj)]
        ygm, yml_t, vlast, cn_t, m8, wout, wup, wdown, wpg, wpp = _mixer_prompt(
            hp, row(norm_mix), w_t, wgr, bgr, row(gm_ln_g), row(gm_ln_b), mixw_p, mixb_p,
            gain_t, triu_p, eye, back_f32, batch, tb_p)

        def back(x2d, ygm, yml, pe, tb):
            return _back(x2d, ygm, yml, pe, wout, row(norm_ffn), wup, wdown, row(norm_ple),
                         wpg, wpp, nfin, last, tb)

        hp = back(hp, ygm, yml_t, p_prompt[l].reshape(n_p, PLE_DIM), tb_p)
        outs["Cp"].append(jnp.swapaxes(cn_t[:, :, :HEAD_DIM, :], -1, -2))
        outs["Np"].append(cn_t[:, :, HEAD_DIM, :])
        outs["Mp"].append(m8[:, :, 0])
        outs["Vp"].append(vlast)

        ygm, q, k, v, og, gcol, grow, vgn = _front(
            hs, row(norm_mix), w_t, wgate, wgate_t, bcol, brow, row(gm_ln_g), row(gm_ln_b), mixw_s, mixb_s,
            tb_s)
        mrep = jnp.repeat(state_m[l].astype(F32), dec_seq, axis=0)
        yml, c_new, n_new, m_new = _mlstm_sample(
            q, k, v, og, gcol, grow, mrep, state_C[l].astype(F32), state_n[l].astype(F32), row(ml_norm),
            _const(tri_s), _const(tri_s.T), sel_last, expand, pick_last, seq_sum, dec_seq)
        hs = back(hs, ygm, yml, p_sample[l].reshape(n_s, PLE_DIM), tb_s)
        outs["Cs"].append(c_new)
        outs["Ns"].append(n_new)
        outs["Ms"].append(m_new[..., 0])
        outs["Vs"].append(vgn.reshape(dec_batch, dec_seq, GM_WIDTH))

    st = lambda k: outs[k][0][None] if depth == 1 else jnp.stack(outs[k])
    return (hp.reshape(batch, seq, D_MODEL), hs.reshape(dec_batch, dec_seq, D_MODEL),
            st("Cp"), st("Np"), st("Mp"), st("Vp"), st("Cs"), st("Ns"), st("Ms"), st("Vs"))
```

```python
import functools

import jax
import numpy as np
import jax.numpy as jnp
from jax import lax
from jax.experimental import pallas as pl
from jax.experimental.pallas import tpu as pltpu

F32 = jnp.float32
BF16 = jnp.bfloat16

D_MODEL = 1024
GM_WIDTH = 512
ML_WIDTH = 512
HEADS = 4
HEAD_DIM = 128
D_FF = 4096
PLE_DIM = 256
EPS = 1e-6
CHUNK = 128
MAIN_COLS = 2 * GM_WIDTH + 4 * ML_WIDTH
Q_ROW = 2 * GM_WIDTH
K_ROW = Q_ROW + ML_WIDTH
V_ROW = K_ROW + ML_WIDTH
O_ROW = V_ROW + ML_WIDTH
W_ROWS = MAIN_COLS + 2 * HEADS
LANES_V = ML_WIDTH
LANES_O = 2 * ML_WIDTH
LANES_G = 3 * ML_WIDTH
LANES_ROWS = LANES_G + 2 * 16
GATE_LANES = 128
GATE_ROWS = 16
STATE_ROWS = HEAD_DIM + 16
FF_SPLIT = 4
BF16_ROWS = 16
VMEM_LIMIT = 60 * 1024 * 1024


def _dot(a, b):
    return jnp.dot(a, b, preferred_element_type=F32)


def _dot_nt(a, b):
    return lax.dot_general(a, b, (((1,), (1,)), ((), ())), preferred_element_type=F32)


def _dot_tn(a, b):
    return lax.dot_general(a, b, (((0,), (0,)), ((), ())), preferred_element_type=F32)


def _split3(x):
    x1 = x.astype(BF16)
    r = x - x1.astype(F32)
    x2 = r.astype(BF16)
    r = r - x2.astype(F32)
    return x1, x2, r.astype(BF16)


def _sel_dot(sel, x):
    p1, p2, p3 = _split3(x)
    return _dot(sel, p1) + _dot(sel, p2) + _dot(sel, p3)


def _dot_sel(x, sel):
    p1, p2, p3 = _split3(x)
    return _dot(p1, sel) + _dot(p2, sel) + _dot(p3, sel)


def _sel_dot_nt(sel, x):
    p1, p2, p3 = _split3(x)
    return _dot_nt(sel, p1) + _dot_nt(sel, p2) + _dot_nt(sel, p3)


def _rms(x, g):
    return x * lax.rsqrt(jnp.mean(x * x, axis=-1, keepdims=True) + EPS) * g


def _log_sigmoid(x):
    return -(jnp.maximum(-x, 0.0) + jnp.log1p(jnp.exp(-jnp.abs(x))))


def _const_spec(shape):
    nd = len(shape)
    return pl.BlockSpec(shape, lambda *_: (0,) * nd, pipeline_mode=pl.Buffered(1))


def _gmlp_pieces(x_ref, nmix_ref, w_t_ref, lng_ref, lnb_ref, mixw_ref, mixb_ref, ygm_ref, vgn_ref, st):
    tb = x_ref.shape[0]

    def norm():
        st["a"] = _rms(x_ref[...], nmix_ref[...]).astype(BF16)

    def u_mm():
        st["u"] = _dot_nt(st["a"], w_t_ref[:GM_WIDTH])

    def u_act():
        st["u"] = jax.nn.gelu(st["u"])

    def v_mm():
        st["vg"] = _dot_nt(st["a"], w_t_ref[GM_WIDTH:2 * GM_WIDTH])

    def v_act():
        vg = jax.nn.gelu(st.pop("vg"))
        mu = jnp.mean(vg, axis=-1, keepdims=True)
        var = jnp.mean(jnp.square(vg - mu), axis=-1, keepdims=True)
        vgn = (vg - mu) * lax.rsqrt(var + EPS) * lng_ref[...] + lnb_ref[...]
        vgn_ref[...] = vgn[tb - vgn_ref.shape[0]:, :]
        st["vgb"] = vgn.astype(BF16)

    def mix(c):
        rows = slice(c * CHUNK, (c + 1) * CHUNK)
        for h in range(HEADS):
            cols = slice(h * HEAD_DIM, (h + 1) * HEAD_DIM)
            s = _dot(mixw_ref[h], st["vgb"][rows, cols]) + mixb_ref[h]
            ygm_ref[rows, cols] = (st["u"][rows, cols] * s).astype(BF16)

    return [norm, u_mm, u_act, v_mm, v_act] + [functools.partial(mix, c) for c in range(tb // CHUNK)]


def _mixer_prompt_kernel(steps_per_seq, n_cast, x_ref, nmix_ref, w_t_ref, wgr_ref, bgr_ref,
                         lng_ref, lnb_ref, mixw_ref, mixb_ref, gain_t_ref, triu_ref, eye_ref, *rest):
    cast_in, rest = rest[:n_cast], rest[n_cast:]
    ygm_ref, yml_t_ref, vlast_ref, c_out_ref, n_out_ref, m_out_ref = rest[:6]
    cast_out, (cn_scr, m_scr, w_lanes_scr) = rest[6:6 + n_cast], rest[6 + n_cast:]

    @pl.when(pl.program_id(0) == 0)
    def _():
        w_lanes_scr[:LANES_V] = w_t_ref[Q_ROW:K_ROW]
        w_lanes_scr[LANES_V:LANES_G] = w_t_ref[V_ROW:MAIN_COLS]
        w_lanes_scr[LANES_G:] = wgr_ref[...]

    tb = x_ref.shape[0]
    n_chunks = tb // CHUNK
    blocks = [slice(c * CHUNK, (c + 1) * CHUNK) for c in range(n_chunks)]
    heads = [slice(h * HEAD_DIM, (h + 1) * HEAD_DIM) for h in range(HEADS)]
    unit_ids = [(c, h) for c in range(n_chunks) for h in range(HEADS)]
    new_seq = pl.program_id(0) % steps_per_seq == 0
    triu = triu_ref[...]
    mask_t = triu.astype(F32) > 0.0
    eye = eye_ref[...]
    lane_g = lax.broadcasted_iota(jnp.int32, (GATE_ROWS, tb), 1) % CHUNK
    sub_g = lax.broadcasted_iota(jnp.int32, (GATE_ROWS, CHUNK), 0)
    ones_row = (sub_g == 0).astype(BF16)
    last = slice(CHUNK - 1, CHUNK)

    st = {}
    norm, u_mm, u_act, v_mm, v_act, *mix = _gmlp_pieces(x_ref, nmix_ref, w_t_ref, lng_ref, lnb_ref, mixw_ref,
                                                        mixb_ref, ygm_ref, vlast_ref, st)
    zero = jnp.zeros((HEAD_DIM, HEAD_DIM), BF16)

    def block_diag(x, y):
        return jnp.concatenate([jnp.concatenate([x, zero], axis=1), jnp.concatenate([zero, y], axis=1)], axis=0)

    def pair_dot(lhs, rhs):
        out = _dot(jnp.concatenate(lhs, axis=1), block_diag(*rhs))
        return out[:, :HEAD_DIM], out[:, HEAD_DIM:]

    norm()
    a = st["a"]
    u_mm()
    v_mm()
    on_lanes = _dot_nt(w_lanes_scr[...], a)
    u_act()
    v_act()
    k = (_dot_nt(a, w_t_ref[K_ROW:V_ROW]) * (HEAD_DIM ** -0.5)).astype(BF16)
    for src, dst in zip(cast_in, cast_out):
        dst[...] = src[...].astype(BF16)
    for piece in mix:
        piece()
    lanes = lambda ref: jnp.concatenate([ref[...]] * n_chunks, axis=1)
    zr = on_lanes[LANES_G:] + lanes(bgr_ref)
    lf_r = _log_sigmoid(zr[GATE_ROWS:])
    b_r = jnp.concatenate([_dot_sel(lf_r[:, t], triu) for t in blocks], axis=1)
    q_t = on_lanes[:ML_WIDTH].astype(BF16)
    r_r = zr[:GATE_ROWS] - b_r
    p_r = r_r
    shift = 1
    while shift < CHUNK:
        p_r = jnp.maximum(p_r, jnp.where(lane_g >= shift, pltpu.roll(p_r, shift, axis=1), -jnp.inf))
        shift *= 2
    e_end = [jnp.exp(r_r[:, t] - p_r[:, t][:, last]) for t in blocks]
    v_t = on_lanes[LANES_V:LANES_O]
    r_c = [_sel_dot_nt(eye, r_r[:, t]) for t in blocks]
    og_t = jax.nn.sigmoid(on_lanes[LANES_O:LANES_G]) * lanes(gain_t_ref)

    pair_ids = [(c, h) for c in range(n_chunks) for h in range(0, HEADS, 2)]
    kq, e_intra, x1, upd = {}, {}, {}, {}
    for c, h in unit_ids:
        e_intra[c, h] = jnp.where(mask_t, jnp.exp(r_c[c][:, h:h + 1] - p_r[h:h + 1, blocks[c]]), 0.0)
    for c, h in pair_ids:
        t = blocks[c]
        kq[c, h], kq[c, h + 1] = pair_dot([k[t, heads[h]], k[t, heads[h + 1]]],
                                          [q_t[heads[h], t], q_t[heads[h + 1], t]])
    for c, h in pair_ids:
        t = blocks[c]
        s0, vext, vw = [], [], []
        for g in (h, h + 1):
            s0.append((kq[c, g] * e_intra[c, g]).astype(BF16))
            vext.append(jnp.concatenate([v_t[heads[g], t].astype(BF16), ones_row], axis=0))
            e_row = e_end[c][g:g + 1, :]
            vw.append(jnp.concatenate([(v_t[heads[g], t] * e_row).astype(BF16),
                                       jnp.where(sub_g == 0, e_row, 0.0).astype(BF16)], axis=0))
        x1[c, h], x1[c, h + 1] = pair_dot(vext, s0)
        upd[c, h], upd[c, h + 1] = pair_dot(vw, [k[t, heads[h]], k[t, heads[h + 1]]])

    cn_in, cm, w_inter = {}, {}, {}
    for h in range(HEADS):
        cn = jnp.where(new_seq, 0.0, cn_scr[h])
        m_prev = jnp.where(new_seq, 0.0, m_scr[h:h + 1, 0:1])
        for c in range(n_chunks):
            prow = p_r[h:h + 1, blocks[c]]
            cn_in[c, h] = cn
            cm[c, h] = jnp.maximum(m_prev, prow)
            w_inter[c, h] = jnp.exp(m_prev - cm[c, h])
            cm_last = cm[c, h][:, last]
            cn = jnp.exp(m_prev - cm_last) * cn + jnp.exp(prow[:, last] - cm_last) * upd[c, h]
            m_prev = b_r[h:h + 1, blocks[c]][:, last] + cm_last
        cn_scr[h] = cn
        m_scr[h:h + 1, :] = jnp.broadcast_to(m_prev, (1, HEAD_DIM))
    x2 = {}

    def state_matmuls(c):
        t = blocks[c]
        for h in range(0, HEADS, 2):
            x2[c, h], x2[c, h + 1] = pair_dot([cn_in[c, h].astype(BF16), cn_in[c, h + 1].astype(BF16)],
                                              [q_t[heads[h], t], q_t[heads[h + 1], t]])

    def head_outputs(c):
        t = blocks[c]
        for h, hd in enumerate(heads):
            brow, prow = b_r[h:h + 1, t], p_r[h:h + 1, t]
            nd = x1[c, h] * jnp.exp(prow - cm[c, h]) + x2[c, h] * w_inter[c, h]
            num_t, den = nd[:HEAD_DIM], nd[HEAD_DIM:HEAD_DIM + 1]
            inv = 1.0 / jnp.maximum(jnp.abs(den), jnp.exp(-(brow + cm[c, h])))
            ssq = jnp.sum(num_t * num_t, axis=0, keepdims=True)
            scale = inv * lax.rsqrt(ssq * (inv * inv) * (1.0 / HEAD_DIM) + EPS)
            yml_t_ref[hd, t] = (num_t * scale * og_t[hd, t]).astype(BF16)

    for c in range(n_chunks):
        state_matmuls(c)
    for c in range(n_chunks):
        head_outputs(c)

    @pl.when(pl.program_id(0) % steps_per_seq == steps_per_seq - 1)
    def _():
        for h in range(HEADS):
            c_out_ref[h] = cn_scr[h, :HEAD_DIM, :].T
            n_out_ref[h:h + 1, :] = cn_scr[h, HEAD_DIM:HEAD_DIM + 1, :]
        m_out_ref[...] = m_scr[...]


def _mixer_prompt(x2d, nmix, w_t, wgr, bgr, lng, lnb, mixw, mixb, gain_t, triu, eye, cast, batch, tb):
    n = x2d.shape[0]
    n_steps = n // tb
    steps_per_seq = n_steps // batch

    def window(w):
        rows = max(BF16_ROWS, w.shape[0] // n_steps)
        return pl.BlockSpec((rows, w.shape[1]), lambda i: (jnp.minimum(i, w.shape[0] // rows - 1), 0))

    tok = lambda w: pl.BlockSpec((tb, w), lambda i: (i, 0))
    per_seq = lambda *shape: pl.BlockSpec((None,) + shape, lambda i: (i // steps_per_seq,) + (0,) * len(shape))
    return pl.pallas_call(
        functools.partial(_mixer_prompt_kernel, steps_per_seq, len(cast)),
        grid=(n_steps,),
        in_specs=[
            tok(D_MODEL),
            _const_spec((1, D_MODEL)),
            _const_spec((W_ROWS, D_MODEL)),
            _const_spec((2 * GATE_ROWS, D_MODEL)),
            _const_spec((2 * GATE_ROWS, CHUNK)),
            _const_spec((1, GM_WIDTH)),
            _const_spec((1, GM_WIDTH)),
            _const_spec((HEADS, CHUNK, CHUNK)),
            _const_spec((HEADS, CHUNK, CHUNK)),
            _const_spec((ML_WIDTH, CHUNK)),
            _const_spec((CHUNK, CHUNK)),
            _const_spec((CHUNK, CHUNK)),
        ] + [window(w) for w in cast],
        out_specs=(tok(GM_WIDTH), pl.BlockSpec((ML_WIDTH, tb), lambda i: (0, i)), per_seq(CHUNK, GM_WIDTH),
                   per_seq(HEADS, HEAD_DIM, HEAD_DIM), per_seq(HEADS, HEAD_DIM), per_seq(HEADS, HEAD_DIM))
        + tuple(window(w) for w in cast),
        out_shape=(
            jax.ShapeDtypeStruct((n, GM_WIDTH), BF16),
            jax.ShapeDtypeStruct((ML_WIDTH, n), BF16),
            jax.ShapeDtypeStruct((batch, CHUNK, GM_WIDTH), F32),
            jax.ShapeDtypeStruct((batch, HEADS, HEAD_DIM, HEAD_DIM), F32),
            jax.ShapeDtypeStruct((batch, HEADS, HEAD_DIM), F32),
            jax.ShapeDtypeStruct((batch, HEADS, HEAD_DIM), F32),
        ) + tuple(jax.ShapeDtypeStruct(w.shape, BF16) for w in cast),
        scratch_shapes=[pltpu.VMEM((HEADS, STATE_ROWS, HEAD_DIM), F32), pltpu.VMEM((HEADS, HEAD_DIM), F32),
                        pltpu.VMEM((LANES_ROWS, D_MODEL), BF16)],
        compiler_params=pltpu.CompilerParams(dimension_semantics=("arbitrary",),
                                             vmem_limit_bytes=VMEM_LIMIT),
        name="mixer_prompt",
    )(x2d, nmix, w_t, wgr, bgr, lng, lnb, mixw, mixb, gain_t, triu, eye, *cast)


def _front_kernel(x_ref, nmix_ref, w_t_ref, wgate_ref, wgate_t_ref, bcol_ref, brow_ref,
                  lng_ref, lnb_ref, mixw_ref, mixb_ref,
                  ygm_ref, q_ref, k_ref, v_ref, og_ref, gcol_ref, grow_ref, vgn_ref):
    st = {}
    norm, u_mm, u_act, v_mm, v_act, *mix = _gmlp_pieces(x_ref, nmix_ref, w_t_ref, lng_ref, lnb_ref, mixw_ref,
                                                        mixb_ref, ygm_ref, vgn_ref, st)
    norm()
    u_mm()
    v_mm()
    a = st["a"]

    def proj(lo):
        return _dot_nt(a, w_t_ref[lo:lo + ML_WIDTH])

    q_ref[...] = proj(Q_ROW).astype(BF16)
    k_ref[...] = (proj(K_ROW) * (HEAD_DIM ** -0.5)).astype(BF16)
    v_ref[...] = proj(V_ROW).astype(BF16)
    og_ref[...] = jax.nn.sigmoid(proj(O_ROW))
    zc = _dot(a, wgate_ref[...]) + bcol_ref[...]
    lane = lax.broadcasted_iota(jnp.int32, zc.shape, 1)
    gcol_ref[...] = jnp.where(lane >= HEADS, _log_sigmoid(zc), zc)
    zr = _dot_nt(wgate_t_ref[...], a) + brow_ref[...]
    sub = lax.broadcasted_iota(jnp.int32, zr.shape, 0)
    grow_ref[...] = jnp.where(sub >= HEADS, _log_sigmoid(zr), zr)
    for piece in (u_act, v_act, *mix):
        piece()


def _front(x2d, nmix, w_t, wgate, wgate_t, bcol, brow, lng, lnb, mixw, mixb, tb):
    n = x2d.shape[0]
    tok = lambda w: pl.BlockSpec((tb, w), lambda i: (i, 0))
    out_shape = (
        jax.ShapeDtypeStruct((n, GM_WIDTH), BF16),
        jax.ShapeDtypeStruct((n, ML_WIDTH), BF16),
        jax.ShapeDtypeStruct((n, ML_WIDTH), BF16),
        jax.ShapeDtypeStruct((n, ML_WIDTH), BF16),
        jax.ShapeDtypeStruct((n, ML_WIDTH), F32),
        jax.ShapeDtypeStruct((n, GATE_LANES), F32),
        jax.ShapeDtypeStruct((GATE_ROWS, n), F32),
        jax.ShapeDtypeStruct((n, GM_WIDTH), F32),
    )
    return pl.pallas_call(
        _front_kernel,
        grid=(n // tb,),
        in_specs=[
            tok(D_MODEL),
            _const_spec((1, D_MODEL)),
            _const_spec((W_ROWS, D_MODEL)),
            _const_spec((D_MODEL, GATE_LANES)),
            _const_spec((GATE_ROWS, D_MODEL)),
            _const_spec((1, GATE_LANES)),
            _const_spec((GATE_ROWS, tb)),
            _const_spec((1, GM_WIDTH)),
            _const_spec((1, GM_WIDTH)),
            _const_spec((HEADS, CHUNK, CHUNK)),
            _const_spec((HEADS, CHUNK, CHUNK)),
        ],
        out_specs=(tok(GM_WIDTH), tok(ML_WIDTH), tok(ML_WIDTH), tok(ML_WIDTH), tok(ML_WIDTH),
                   tok(GATE_LANES), pl.BlockSpec((GATE_ROWS, tb), lambda i: (0, i)), tok(GM_WIDTH)),
        out_shape=out_shape,
        compiler_params=pltpu.CompilerParams(dimension_semantics=("arbitrary",),
                                             vmem_limit_bytes=VMEM_LIMIT),
        name="front",
    )(x2d, nmix, w_t, wgate, wgate_t, bcol, brow, lng, lnb, mixw, mixb)


def _intra(q, ks, igcol, bcol, igrow, brow, mprev, mask):
    d = bcol + (igrow - brow)
    g = bcol + mprev
    m_t = jnp.maximum(g, jnp.max(jnp.where(mask, d, -jnp.inf), axis=-1, keepdims=True))
    w_intra = jnp.where(mask, jnp.exp(d - m_t), 0.0)
    w_inter = jnp.exp(g - m_t)
    s = _dot_nt(q, ks) * w_intra
    return s, w_inter, m_t, g


def _head_out(num, den, m_t, gain, og):
    hh = num / jnp.maximum(jnp.abs(den), jnp.exp(-m_t))
    return (og * _rms(hh, gain)).astype(BF16)


def _mlstm_sample_kernel(seq_len, q_ref, k_ref, v_ref, og_ref, gcol_ref, grow_ref, mrep_ref, c_ref, n_ref,
                         mln_ref, tri_ref, triu_ref, sel_last_ref, expand_ref, pick_last_ref, seq_sum_ref,
                         yml_ref, c_out_ref, n_out_ref, m_out_ref):
    n_seq = CHUNK // seq_len
    tri = tri_ref[...]
    mask = tri.astype(F32) > 0.0
    gcol = gcol_ref[...]
    grow = grow_ref[...]
    bcol_all = _sel_dot(tri, gcol)
    brow_all = _dot_sel(grow, triu_ref[...])
    sel_last = sel_last_ref[...]
    expand = expand_ref[...]
    pick_last = pick_last_ref[...]
    seq_sum = seq_sum_ref[...]
    row = lax.broadcasted_iota(jnp.int32, (CHUNK, HEAD_DIM), 0)
    lane = lax.broadcasted_iota(jnp.int32, (CHUNK, HEAD_DIM), 1)
    seq_rows = [(row >= i * seq_len) & (row < (i + 1) * seq_len) for i in range(n_seq)]
    head_cols = [slice(h * HEAD_DIM, (h + 1) * HEAD_DIM) for h in range(HEADS)]
    every_head = range(HEADS)
    q = [q_ref[:, cols] for cols in head_cols]
    ks = [k_ref[:, cols] for cols in head_cols]
    v = [v_ref[:, cols] for cols in head_cols]
    zero = jnp.zeros_like(q[0])
    igcol = [gcol[:, h:h + 1] for h in every_head]
    bcol = [bcol_all[:, HEADS + h:HEADS + h + 1] for h in every_head]
    intra = [_intra(q[h], ks[h], igcol[h], bcol[h], grow[h:h + 1, :], brow_all[HEADS + h:HEADS + h + 1, :],
                    mrep_ref[:, h:h + 1], mask) for h in every_head]
    s, w_inter, m_t, g = zip(*intra)
    c_all = [c_ref[:, h] for h in every_head]
    n_all = [n_ref[:, h] for h in every_head]
    qc = [_dot(jnp.concatenate([jnp.where(m, q[h], zero) for m in seq_rows], axis=1),
               c_all[h].reshape(n_seq * HEAD_DIM, HEAD_DIM).astype(BF16)) for h in every_head]
    n_rows = [_sel_dot(expand, n_all[h]) for h in every_head]
    ends = [_sel_dot(sel_last, jnp.where(lane == 0, m_t[h], jnp.where(lane == 1, g[h],
                                                                        jnp.where(lane == 2, bcol[h], 0.0))))
            for h in every_head]
    sv = [_dot(s[h].astype(BF16), v[h]) for h in every_head]
    kw, dec, upd, dec_seq, n_inc, m_seq = [], [], [], [], [], []
    for h in every_head:
        m_new, g_last, b_last = ends[h][:, 0:1], ends[h][:, 1:2], ends[h][:, 2:3]
        w_end = jnp.exp(b_last - bcol[h] + igcol[h] - m_new)
        dec.append(jnp.exp(g_last - m_new))
        kw.append(ks[h].astype(F32) * w_end)
    for h in every_head:
        v_exp = jnp.concatenate([jnp.where(m, v[h], zero) for m in seq_rows], axis=1)
        upd.append(_dot(kw[h].T.astype(BF16), v_exp))
        dec_seq.append(_sel_dot(pick_last, jnp.broadcast_to(dec[h], (CHUNK, HEAD_DIM))))
        n_inc.append(_sel_dot(seq_sum, kw[h]))
        m_seq.append(_sel_dot(pick_last, jnp.broadcast_to(m_t[h], (CHUNK, HEAD_DIM))))
    for h, cols in enumerate(head_cols):
        qn = jnp.sum(q[h].astype(F32) * n_rows[h], axis=-1, keepdims=True)
        num = sv[h] + w_inter[h] * qc[h]
        den = jnp.sum(s[h], axis=-1, keepdims=True) + w_inter[h] * qn
        yml_ref[:, cols] = _head_out(num, den, m_t[h], mln_ref[:, cols], og_ref[:, cols])
        for i in range(n_seq):
            c_out_ref[i, h] = (dec_seq[h][i:i + 1, 0:1] * c_all[h][i]
                               + upd[h][:, i * HEAD_DIM:(i + 1) * HEAD_DIM])
        n_out_ref[:, h] = dec_seq[h] * n_all[h] + n_inc[h]
        m_out_ref[:, h] = m_seq[h]


def _mlstm_sample(q, k, v, og, gcol, grow, mrep, c0, n0, mln, tri, triu, sel_last, expand, pick_last,
                  seq_sum, seq_len):
    n = q.shape[0]
    n_seq = CHUNK // seq_len
    n_batch = n // seq_len
    tok = lambda w: pl.BlockSpec((CHUNK, w), lambda i: (i, 0))
    c_spec = pl.BlockSpec((n_seq, HEADS, HEAD_DIM, HEAD_DIM), lambda i: (i, 0, 0, 0))
    n_spec = pl.BlockSpec((n_seq, HEADS, HEAD_DIM), lambda i: (i, 0, 0))
    return pl.pallas_call(
        functools.partial(_mlstm_sample_kernel, seq_len),
        grid=(n // CHUNK,),
        in_specs=[
            tok(ML_WIDTH), tok(ML_WIDTH), tok(ML_WIDTH), tok(ML_WIDTH), tok(GATE_LANES),
            pl.BlockSpec((GATE_ROWS, CHUNK), lambda i: (0, i)),
            pl.BlockSpec((CHUNK, HEADS), lambda i: (i, 0)),
            c_spec, n_spec,
            _const_spec((1, ML_WIDTH)),
            _const_spec((CHUNK, CHUNK)), _const_spec((CHUNK, CHUNK)), _const_spec((CHUNK, CHUNK)),
            _const_spec((CHUNK, n_seq)), _const_spec((n_seq, CHUNK)), _const_spec((n_seq, CHUNK)),
        ],
        out_specs=(tok(ML_WIDTH), c_spec, n_spec, n_spec),
        out_shape=(
            jax.ShapeDtypeStruct((n, ML_WIDTH), BF16),
            jax.ShapeDtypeStruct((n_batch, HEADS, HEAD_DIM, HEAD_DIM), F32),
            jax.ShapeDtypeStruct((n_batch, HEADS, HEAD_DIM), F32),
            jax.ShapeDtypeStruct((n_batch, HEADS, HEAD_DIM), F32),
        ),
        compiler_params=pltpu.CompilerParams(dimension_semantics=("arbitrary",),
                                             vmem_limit_bytes=VMEM_LIMIT),
        name="mlstm_sample",
    )(q, k, v, og, gcol, grow, mrep, c0, n0, mln, tri, triu, sel_last, expand, pick_last, seq_sum)


def _back_math(final_norm, yml_transposed, x_ref, ygm_ref, yml_ref, pe_ref, wout_ref, nffn_ref,
               wup_ref, wdown_ref, nple_ref, wpg_ref, wpp_ref, nfin_ref, out_ref):
    ml_dot = _dot_tn if yml_transposed else _dot
    h = x_ref[...] + _dot(ygm_ref[...], wout_ref[:GM_WIDTH]) + ml_dot(yml_ref[...], wout_ref[GM_WIDTH:])
    a = _rms(h, nffn_ref[...]).astype(BF16)
    ff = D_FF // FF_SPLIT

    def hidden(c):
        f = _dot(a, wup_ref[:, c * ff:(c + 1) * ff])
        return jnp.square(jnp.maximum(f, 0.0)).astype(BF16)

    def down(f, c):
        return _dot(f, wdown_ref[c * ff:(c + 1) * ff, :])

    mlp = down(hidden(0), 0)
    for c in range(1, FF_SPLIT - 1):
        mlp = mlp + down(hidden(c), c)
    tb = x_ref.shape[0]
    pieces = (slice(0, tb // 2), slice(tb // 2, 3 * tb // 4), slice(3 * tb // 4, tb))
    f_last = hidden(FF_SPLIT - 1)
    h_new = []
    for r in (pieces[0], slice(pieces[0].stop, tb)):
        h_r = h[r] + (mlp[r] + down(f_last[r], FF_SPLIT - 1))
        h_new += [h_r[p.start - r.start:p.stop - r.start] for p in pieces if r.start <= p.start < r.stop]
    emb = _dot(pe_ref[...].astype(BF16), wpp_ref[...])
    gates = [_dot(_rms(h_p, nple_ref[...]).astype(BF16), wpg_ref[...]) for h_p in h_new]
    for r, h_p, g in zip(pieces, h_new, gates):
        y = h_p + jax.nn.sigmoid(g) * emb[r]
        if final_norm:
            y = _rms(y, nfin_ref[...])
        out_ref[r, :] = y


def _back_kernel(final_norm, yml_transposed, *refs):
    _back_math(final_norm, yml_transposed, *refs)


def _back(x2d, ygm, yml, pe2d, wout, nffn, wup, wdown, nple, wpg, wpp, nfin, final_norm, tb):
    n = x2d.shape[0]
    yml_transposed = yml.shape[0] != n
    tok = lambda w: pl.BlockSpec((tb, w), lambda i: (i, 0))
    yml_spec = pl.BlockSpec((ML_WIDTH, tb), lambda i: (0, i)) if yml_transposed else tok(ML_WIDTH)
    return pl.pallas_call(
        functools.partial(_back_kernel, final_norm, yml_transposed),
        grid=(n // tb,),
        in_specs=[
            tok(D_MODEL), tok(GM_WIDTH), yml_spec, tok(PLE_DIM),
            _const_spec((GM_WIDTH + ML_WIDTH, D_MODEL)),
            _const_spec((1, D_MODEL)),
            _const_spec((D_MODEL, D_FF)), _const_spec((D_FF, D_MODEL)),
            _const_spec((1, D_MODEL)),
            _const_spec((D_MODEL, D_MODEL)), _const_spec((PLE_DIM, D_MODEL)),
            _const_spec((1, D_MODEL)),
        ],
        out_specs=tok(D_MODEL),
        out_shape=jax.ShapeDtypeStruct((n, D_MODEL), F32),
        compiler_params=pltpu.CompilerParams(dimension_semantics=("arbitrary",),
                                             vmem_limit_bytes=VMEM_LIMIT),
        name="back",
    )(x2d, ygm, yml, pe2d, wout, nffn, wup, wdown, nple, wpg, wpp, nfin)


def _const(x):
    return jnp.asarray(np.asarray(x, np.float32), BF16)


def _block_tri(block):
    r = np.arange(CHUNK)[:, None]
    c = np.arange(CHUNK)[None, :]
    return (r // block == c // block) & (c <= r)


def _token_block(n):
    return 512 if n % 512 == 0 else CHUNK


def kernel(x_prompt, x_sample, p_prompt, p_sample, state_C, state_n, state_m, norm_mix, w_in, gm_ln_g,
           gm_ln_b, gm_ws, gm_bs, ml_b_i, ml_b_f, ml_norm, w_out, norm_ffn, w_up, w_down, norm_ple,
           w_ple_gate, w_ple_proj, norm_final):
    depth = w_in.shape[0]
    batch, seq, _ = x_prompt.shape
    dec_batch, dec_seq, _ = x_sample.shape
    n_p, n_s = batch * seq, dec_batch * dec_seq
    n_seq = CHUNK // dec_seq
    assert seq % CHUNK == 0 and CHUNK % dec_seq == 0 and n_s % CHUNK == 0
    tb_p, tb_s = 2 * _token_block(seq), _token_block(n_s)

    hp = x_prompt.reshape(n_p, D_MODEL)
    hs = x_sample.reshape(n_s, D_MODEL)

    tri_p, tri_s = _block_tri(CHUNK), _block_tri(dec_seq)
    triu_p, eye = _const(tri_p.T), _const(np.eye(CHUNK))
    r = np.arange(CHUNK)
    i = np.arange(n_seq)
    sel_last = _const(r[None, :] == (r[:, None] // dec_seq) * dec_seq + dec_seq - 1)
    expand = _const(r[:, None] // dec_seq == i[None, :])
    pick_last = _const(r[None, :] == i[:, None] * dec_seq + dec_seq - 1)
    seq_sum = _const(r[None, :] // dec_seq == i[:, None])

    outs = {k: [] for k in ("Cp", "Np", "Mp", "Vp", "Cs", "Ns", "Ms", "Vs")}
    for l in range(depth):
        row = lambda a: a[l].reshape(1, -1).astype(F32)
        w_t = w_in[l].T.astype(BF16)
        wg_t = w_t[MAIN_COLS:]
        b_i, b_f = ml_b_i[l].astype(F32), ml_b_f[l].astype(F32)
        row_pad = lambda a: jnp.pad(a, ((0, GATE_ROWS - HEADS), (0, 0)))
        wgr = jnp.concatenate([row_pad(wg_t[:HEADS]), row_pad(wg_t[HEADS:])], axis=0)
        bgr = jnp.broadcast_to(jnp.concatenate([row_pad(b_i[:, None]), row_pad(b_f[:, None])], axis=0),
                               (2 * GATE_ROWS, CHUNK))
        gain_t = jnp.broadcast_to(ml_norm[l].astype(F32)[:, None], (ML_WIDTH, CHUNK))
        wgate = jnp.pad(wg_t.T, ((0, 0), (0, GATE_LANES - 2 * HEADS)))
        wgate_t = jnp.pad(wg_t, ((0, GATE_ROWS - 2 * HEADS), (0, 0)))
        gbias = jnp.concatenate([b_i, b_f])
        bcol = jnp.pad(gbias, (0, GATE_LANES - 2 * HEADS)).reshape(1, GATE_LANES)
        brow = jnp.broadcast_to(jnp.pad(gbias, (0, GATE_ROWS - 2 * HEADS)).reshape(GATE_ROWS, 1),
                                (GATE_ROWS, tb_s))
        ws = gm_ws[l]
        mixw_p = (ws[:, :CHUNK, :CHUNK] * tri_p.astype(np.float32)).astype(BF16)
        mixb_p = jnp.broadcast_to(gm_bs[l][:, :CHUNK, None], (HEADS, CHUNK, CHUNK)).astype(F32)
        mixw_s = (jnp.tile(ws[:, :dec_seq, :dec_seq], (1, n_seq, n_seq)) * tri_s.astype(np.float32)).astype(BF16)
        mixb_s = jnp.broadcast_to(jnp.tile(gm_bs[l][:, :dec_seq], (1, n_seq))[:, :, None],
                                  (HEADS, CHUNK, CHUNK)).astype(F32)
        last = l == depth - 1
        nfin = norm_final.reshape(1, D_MODEL).astype(F32)

        back_f32 = [w[l].astype(F32) for w in (w_out, w_up, w_down, w_ple_gate, w_ple_proj)]
        ygm, yml_t, vlast, c_fin, n_fin, m8, wout, wup, wdown, wpg, wpp = _mixer_prompt(
            hp, row(norm_mix), w_t, wgr, bgr, row(gm_ln_g), row(gm_ln_b), mixw_p, mixb_p,
            gain_t, triu_p, eye, back_f32, batch, tb_p)

        def back(x2d, ygm, yml, pe, tb):
            return _back(x2d, ygm, yml, pe, wout, row(norm_ffn), wup, wdown, row(norm_ple),
                         wpg, wpp, nfin, last, tb)

        hp = back(hp, ygm, yml_t, p_prompt[l].reshape(n_p, PLE_DIM), tb_p)
        outs["Cp"].append(c_fin)
        outs["Np"].append(n_fin)
        outs["Mp"].append(m8[:, :, 0])
        outs["Vp"].append(vlast)

        ygm, q, k, v, og, gcol, grow, vgn = _front(
            hs, row(norm_mix), w_t, wgate, wgate_t, bcol, brow, row(gm_ln_g), row(gm_ln_b), mixw_s, mixb_s,
            tb_s)
        mrep = jnp.repeat(state_m[l].astype(F32), dec_seq, axis=0)
        yml, c_new, n_new, m_new = _mlstm_sample(
            q, k, v, og, gcol, grow, mrep, state_C[l].astype(F32), state_n[l].astype(F32), row(ml_norm),
            _const(tri_s), _const(tri_s.T), sel_last, expand, pick_last, seq_sum, dec_seq)
        hs = back(hs, ygm, yml, p_sample[l].reshape(n_s, PLE_DIM), tb_s)
        outs["Cs"].append(c_new)
        outs["Ns"].append(n_new)
        outs["Ms"].append(m_new[..., 0])
        outs["Vs"].append(vgn.reshape(dec_batch, dec_seq, GM_WIDTH))

    st = lambda k: outs[k][0][None] if depth == 1 else jnp.stack(outs[k])
    return (hp.reshape(batch, seq, D_MODEL), hs.reshape(dec_batch, dec_seq, D_MODEL),
            st("Cp"), st("Np"), st("Mp"), st("Vp"), st("Cs"), st("Ns"), st("Ms"), st("Vs"))
```

```python
import functools

import jax
import numpy as np
import jax.numpy as jnp
from jax import lax
from jax.experimental import pallas as pl
from jax.experimental.pallas import tpu as pltpu

F32 = jnp.float32
BF16 = jnp.bfloat16

D_MODEL = 1024
GM_WIDTH = 512
ML_WIDTH = 512
HEADS = 4
HEAD_DIM = 128
D_FF = 4096
PLE_DIM = 256
EPS = 1e-6
CHUNK = 128
MAIN_COLS = 2 * GM_WIDTH + 4 * ML_WIDTH
Q_ROW = 2 * GM_WIDTH
K_ROW = Q_ROW + ML_WIDTH
V_ROW = K_ROW + ML_WIDTH
O_ROW = V_ROW + ML_WIDTH
W_ROWS = MAIN_COLS + 2 * HEADS
LANES_V = ML_WIDTH
LANES_O = 2 * ML_WIDTH
LANES_G = 3 * ML_WIDTH
LANES_ROWS = LANES_G + 2 * 16
GATE_LANES = 128
GATE_ROWS = 16
STATE_ROWS = HEAD_DIM + 16
FF_SPLIT = 4
BF16_ROWS = 16
VMEM_LIMIT = 60 * 1024 * 1024


def _dot(a, b):
    return jnp.dot(a, b, preferred_element_type=F32)


def _dot_nt(a, b):
    return lax.dot_general(a, b, (((1,), (1,)), ((), ())), preferred_element_type=F32)


def _dot_tn(a, b):
    return lax.dot_general(a, b, (((0,), (0,)), ((), ())), preferred_element_type=F32)


def _split3(x):
    x1 = x.astype(BF16)
    r = x - x1.astype(F32)
    x2 = r.astype(BF16)
    r = r - x2.astype(F32)
    return x1, x2, r.astype(BF16)


def _sel_dot(sel, x):
    p1, p2, p3 = _split3(x)
    return _dot(sel, p1) + _dot(sel, p2) + _dot(sel, p3)


def _dot_sel(x, sel):
    p1, p2, p3 = _split3(x)
    return _dot(p1, sel) + _dot(p2, sel) + _dot(p3, sel)


def _sel_dot_nt(sel, x):
    p1, p2, p3 = _split3(x)
    return _dot_nt(sel, p1) + _dot_nt(sel, p2) + _dot_nt(sel, p3)


def _rms(x, g):
    return x * lax.rsqrt(jnp.mean(x * x, axis=-1, keepdims=True) + EPS) * g


def _log_sigmoid(x):
    return -(jnp.maximum(-x, 0.0) + jnp.log1p(jnp.exp(-jnp.abs(x))))


def _const_spec(shape):
    nd = len(shape)
    return pl.BlockSpec(shape, lambda *_: (0,) * nd, pipeline_mode=pl.Buffered(1))


def _gmlp_pieces(x_ref, nmix_ref, w_t_ref, lng_ref, lnb_ref, mixw_ref, mixb_ref, ygm_ref, vgn_ref, st):
    tb = x_ref.shape[0]

    def norm():
        st["a"] = _rms(x_ref[...], nmix_ref[...]).astype(BF16)

    def u_mm():
        st["u"] = _dot_nt(st["a"], w_t_ref[:GM_WIDTH])

    def u_act():
        st["u"] = jax.nn.gelu(st["u"])

    def v_mm():
        st["vg"] = _dot_nt(st["a"], w_t_ref[GM_WIDTH:2 * GM_WIDTH])

    def v_act():
        vg = jax.nn.gelu(st.pop("vg"))
        mu = jnp.mean(vg, axis=-1, keepdims=True)
        var = jnp.mean(jnp.square(vg - mu), axis=-1, keepdims=True)
        vgn = (vg - mu) * lax.rsqrt(var + EPS) * lng_ref[...] + lnb_ref[...]
        vgn_ref[...] = vgn[tb - vgn_ref.shape[0]:, :]
        st["vgb"] = vgn.astype(BF16)

    def mix(c):
        rows = slice(c * CHUNK, (c + 1) * CHUNK)
        for h in range(HEADS):
            cols = slice(h * HEAD_DIM, (h + 1) * HEAD_DIM)
            s = _dot(mixw_ref[h], st["vgb"][rows, cols]) + mixb_ref[h]
            ygm_ref[rows, cols] = (st["u"][rows, cols] * s).astype(BF16)

    return [norm, u_mm, u_act, v_mm, v_act] + [functools.partial(mix, c) for c in range(tb // CHUNK)]


def _mixer_prompt_kernel(steps_per_seq, n_cast, x_ref, nmix_ref, w_t_ref, wgr_ref, bgr_ref,
                         lng_ref, lnb_ref, mixw_ref, mixb_ref, gain_t_ref, triu_ref, eye_ref, *rest):
    cast_in, rest = rest[:n_cast], rest[n_cast:]
    ygm_ref, yml_t_ref, vlast_ref, cn_out_ref, m_out_ref = rest[:5]
    cast_out, (cn_scr, m_scr, w_lanes_scr) = rest[5:5 + n_cast], rest[5 + n_cast:]

    @pl.when(pl.program_id(0) == 0)
    def _():
        w_lanes_scr[:LANES_V] = w_t_ref[Q_ROW:K_ROW]
        w_lanes_scr[LANES_V:LANES_G] = w_t_ref[V_ROW:MAIN_COLS]
        w_lanes_scr[LANES_G:] = wgr_ref[...]

    tb = x_ref.shape[0]
    n_chunks = tb // CHUNK
    blocks = [slice(c * CHUNK, (c + 1) * CHUNK) for c in range(n_chunks)]
    heads = [slice(h * HEAD_DIM, (h + 1) * HEAD_DIM) for h in range(HEADS)]
    unit_ids = [(c, h) for c in range(n_chunks) for h in range(HEADS)]
    new_seq = pl.program_id(0) % steps_per_seq == 0
    triu = triu_ref[...]
    mask_t = triu.astype(F32) > 0.0
    eye = eye_ref[...]
    lane_g = lax.broadcasted_iota(jnp.int32, (GATE_ROWS, tb), 1) % CHUNK
    sub_g = lax.broadcasted_iota(jnp.int32, (GATE_ROWS, CHUNK), 0)
    ones_row = (sub_g == 0).astype(BF16)
    last = slice(CHUNK - 1, CHUNK)

    st = {}
    norm, u_mm, u_act, v_mm, v_act, *mix = _gmlp_pieces(x_ref, nmix_ref, w_t_ref, lng_ref, lnb_ref, mixw_ref,
                                                        mixb_ref, ygm_ref, vlast_ref, st)
    zero = jnp.zeros((HEAD_DIM, HEAD_DIM), BF16)

    def block_diag(x, y):
        return jnp.concatenate([jnp.concatenate([x, zero], axis=1), jnp.concatenate([zero, y], axis=1)], axis=0)

    def pair_dot(lhs, rhs):
        out = _dot(jnp.concatenate(lhs, axis=1), block_diag(*rhs))
        return out[:, :HEAD_DIM], out[:, HEAD_DIM:]

    norm()
    a = st["a"]
    u_mm()
    v_mm()
    on_lanes = _dot_nt(w_lanes_scr[...], a)
    u_act()
    v_act()
    k = (_dot_nt(a, w_t_ref[K_ROW:V_ROW]) * (HEAD_DIM ** -0.5)).astype(BF16)
    for src, dst in zip(cast_in, cast_out):
        dst[...] = src[...].astype(BF16)
    for piece in mix:
        piece()
    lanes = lambda ref: jnp.concatenate([ref[...]] * n_chunks, axis=1)
    zr = on_lanes[LANES_G:] + lanes(bgr_ref)
    lf_r = _log_sigmoid(zr[GATE_ROWS:])
    b_r = jnp.concatenate([_dot_sel(lf_r[:, t], triu) for t in blocks], axis=1)
    q_t = on_lanes[:ML_WIDTH].astype(BF16)
    r_r = zr[:GATE_ROWS] - b_r
    p_r = r_r
    shift = 1
    while shift < CHUNK:
        p_r = jnp.maximum(p_r, jnp.where(lane_g >= shift, pltpu.roll(p_r, shift, axis=1), -jnp.inf))
        shift *= 2
    e_end = [jnp.exp(r_r[:, t] - p_r[:, t][:, last]) for t in blocks]
    v_t = on_lanes[LANES_V:LANES_O]
    r_c = [_sel_dot_nt(eye, r_r[:, t]) for t in blocks]
    og_t = jax.nn.sigmoid(on_lanes[LANES_O:LANES_G]) * lanes(gain_t_ref)

    pair_ids = [(c, h) for c in range(n_chunks) for h in range(0, HEADS, 2)]
    kq, e_intra, x1, upd = {}, {}, {}, {}
    for c, h in unit_ids:
        e_intra[c, h] = jnp.where(mask_t, jnp.exp(r_c[c][:, h:h + 1] - p_r[h:h + 1, blocks[c]]), 0.0)
    for c, h in pair_ids:
        t = blocks[c]
        kq[c, h], kq[c, h + 1] = pair_dot([k[t, heads[h]], k[t, heads[h + 1]]],
                                          [q_t[heads[h], t], q_t[heads[h + 1], t]])
    for c, h in pair_ids:
        t = blocks[c]
        s0, vext, vw = [], [], []
        for g in (h, h + 1):
            s0.append((kq[c, g] * e_intra[c, g]).astype(BF16))
            vext.append(jnp.concatenate([v_t[heads[g], t].astype(BF16), ones_row], axis=0))
            e_row = e_end[c][g:g + 1, :]
            vw.append(jnp.concatenate([(v_t[heads[g], t] * e_row).astype(BF16),
                                       jnp.where(sub_g == 0, e_row, 0.0).astype(BF16)], axis=0))
        x1[c, h], x1[c, h + 1] = pair_dot(vext, s0)
        upd[c, h], upd[c, h + 1] = pair_dot(vw, [k[t, heads[h]], k[t, heads[h + 1]]])

    cn_in, cm, w_inter = {}, {}, {}
    for h in range(HEADS):
        cn = jnp.where(new_seq, 0.0, cn_scr[h])
        m_prev = jnp.where(new_seq, 0.0, m_scr[h:h + 1, 0:1])
        for c in range(n_chunks):
            prow = p_r[h:h + 1, blocks[c]]
            cn_in[c, h] = cn
            cm[c, h] = jnp.maximum(m_prev, prow)
            w_inter[c, h] = jnp.exp(m_prev - cm[c, h])
            cm_last = cm[c, h][:, last]
            cn = jnp.exp(m_prev - cm_last) * cn + jnp.exp(prow[:, last] - cm_last) * upd[c, h]
            m_prev = b_r[h:h + 1, blocks[c]][:, last] + cm_last
        cn_scr[h] = cn
        m_scr[h:h + 1, :] = jnp.broadcast_to(m_prev, (1, HEAD_DIM))
    x2 = {}

    def state_matmuls(c):
        t = blocks[c]
        for h in range(0, HEADS, 2):
            x2[c, h], x2[c, h + 1] = pair_dot([cn_in[c, h].astype(BF16), cn_in[c, h + 1].astype(BF16)],
                                              [q_t[heads[h], t], q_t[heads[h + 1], t]])

    def head_outputs(c):
        t = blocks[c]
        for h, hd in enumerate(heads):
            brow, prow = b_r[h:h + 1, t], p_r[h:h + 1, t]
            nd = x1[c, h] * jnp.exp(prow - cm[c, h]) + x2[c, h] * w_inter[c, h]
            num_t, den = nd[:HEAD_DIM], nd[HEAD_DIM:HEAD_DIM + 1]
            inv = 1.0 / jnp.maximum(jnp.abs(den), jnp.exp(-(brow + cm[c, h])))
            ssq = jnp.sum(num_t * num_t, axis=0, keepdims=True)
            scale = inv * lax.rsqrt(ssq * (inv * inv) * (1.0 / HEAD_DIM) + EPS)
            yml_t_ref[hd, t] = (num_t * scale * og_t[hd, t]).astype(BF16)

    for c in range(n_chunks):
        state_matmuls(c)
    for c in range(n_chunks):
        head_outputs(c)

    @pl.when(pl.program_id(0) % steps_per_seq == steps_per_seq - 1)
    def _():
        cn_out_ref[...] = cn_scr[...]
        m_out_ref[...] = m_scr[...]


def _mixer_prompt(x2d, nmix, w_t, wgr, bgr, lng, lnb, mixw, mixb, gain_t, triu, eye, cast, batch, tb):
    n = x2d.shape[0]
    n_steps = n // tb
    steps_per_seq = n_steps // batch

    def window(w):
        rows = max(BF16_ROWS, w.shape[0] // n_steps)
        return pl.BlockSpec((rows, w.shape[1]), lambda i: (jnp.minimum(i, w.shape[0] // rows - 1), 0))

    tok = lambda w: pl.BlockSpec((tb, w), lambda i: (i, 0))
    per_seq = lambda *shape: pl.BlockSpec((None,) + shape, lambda i: (i // steps_per_seq,) + (0,) * len(shape))
    return pl.pallas_call(
        functools.partial(_mixer_prompt_kernel, steps_per_seq, len(cast)),
        grid=(n_steps,),
        in_specs=[
            tok(D_MODEL),
            _const_spec((1, D_MODEL)),
            _const_spec((W_ROWS, D_MODEL)),
            _const_spec((2 * GATE_ROWS, D_MODEL)),
            _const_spec((2 * GATE_ROWS, CHUNK)),
            _const_spec((1, GM_WIDTH)),
            _const_spec((1, GM_WIDTH)),
            _const_spec((HEADS, CHUNK, CHUNK)),
            _const_spec((HEADS, CHUNK, CHUNK)),
            _const_spec((ML_WIDTH, CHUNK)),
            _const_spec((CHUNK, CHUNK)),
            _const_spec((CHUNK, CHUNK)),
        ] + [window(w) for w in cast],
        out_specs=(tok(GM_WIDTH), pl.BlockSpec((ML_WIDTH, tb), lambda i: (0, i)), per_seq(CHUNK, GM_WIDTH),
                   per_seq(HEADS, STATE_ROWS, HEAD_DIM), per_seq(HEADS, HEAD_DIM)) + tuple(window(w) for w in cast),
        out_shape=(
            jax.ShapeDtypeStruct((n, GM_WIDTH), BF16),
            jax.ShapeDtypeStruct((ML_WIDTH, n), BF16),
            jax.ShapeDtypeStruct((batch, CHUNK, GM_WIDTH), F32),
            jax.ShapeDtypeStruct((batch, HEADS, STATE_ROWS, HEAD_DIM), F32),
            jax.ShapeDtypeStruct((batch, HEADS, HEAD_DIM), F32),
        ) + tuple(jax.ShapeDtypeStruct(w.shape, BF16) for w in cast),
        scratch_shapes=[pltpu.VMEM((HEADS, STATE_ROWS, HEAD_DIM), F32), pltpu.VMEM((HEADS, HEAD_DIM), F32),
                        pltpu.VMEM((LANES_ROWS, D_MODEL), BF16)],
        compiler_params=pltpu.CompilerParams(dimension_semantics=("arbitrary",),
                                             vmem_limit_bytes=VMEM_LIMIT),
        name="mixer_prompt",
    )(x2d, nmix, w_t, wgr, bgr, lng, lnb, mixw, mixb, gain_t, triu, eye, *cast)


def _front_kernel(x_ref, nmix_ref, w_t_ref, wgate_ref, wgate_t_ref, bcol_ref, brow_ref,
                  lng_ref, lnb_ref, mixw_ref, mixb_ref,
                  ygm_ref, q_ref, k_ref, v_ref, og_ref, gcol_ref, grow_ref, vgn_ref):
    st = {}
    norm, u_mm, u_act, v_mm, v_act, *mix = _gmlp_pieces(x_ref, nmix_ref, w_t_ref, lng_ref, lnb_ref, mixw_ref,
                                                        mixb_ref, ygm_ref, vgn_ref, st)
    norm()
    u_mm()
    v_mm()
    a = st["a"]

    def proj(lo):
        return _dot_nt(a, w_t_ref[lo:lo + ML_WIDTH])

    q_ref[...] = proj(Q_ROW).astype(BF16)
    k_ref[...] = (proj(K_ROW) * (HEAD_DIM ** -0.5)).astype(BF16)
    v_ref[...] = proj(V_ROW).astype(BF16)
    og_ref[...] = jax.nn.sigmoid(proj(O_ROW))
    zc = _dot(a, wgate_ref[...]) + bcol_ref[...]
    lane = lax.broadcasted_iota(jnp.int32, zc.shape, 1)
    gcol_ref[...] = jnp.where(lane >= HEADS, _log_sigmoid(zc), zc)
    zr = _dot_nt(wgate_t_ref[...], a) + brow_ref[...]
    sub = lax.broadcasted_iota(jnp.int32, zr.shape, 0)
    grow_ref[...] = jnp.where(sub >= HEADS, _log_sigmoid(zr), zr)
    for piece in (u_act, v_act, *mix):
        piece()


def _front(x2d, nmix, w_t, wgate, wgate_t, bcol, brow, lng, lnb, mixw, mixb, tb):
    n = x2d.shape[0]
    tok = lambda w: pl.BlockSpec((tb, w), lambda i: (i, 0))
    out_shape = (
        jax.ShapeDtypeStruct((n, GM_WIDTH), BF16),
        jax.ShapeDtypeStruct((n, ML_WIDTH), BF16),
        jax.ShapeDtypeStruct((n, ML_WIDTH), BF16),
        jax.ShapeDtypeStruct((n, ML_WIDTH), BF16),
        jax.ShapeDtypeStruct((n, ML_WIDTH), F32),
        jax.ShapeDtypeStruct((n, GATE_LANES), F32),
        jax.ShapeDtypeStruct((GATE_ROWS, n), F32),
        jax.ShapeDtypeStruct((n, GM_WIDTH), F32),
    )
    return pl.pallas_call(
        _front_kernel,
        grid=(n // tb,),
        in_specs=[
            tok(D_MODEL),
            _const_spec((1, D_MODEL)),
            _const_spec((W_ROWS, D_MODEL)),
            _const_spec((D_MODEL, GATE_LANES)),
            _const_spec((GATE_ROWS, D_MODEL)),
            _const_spec((1, GATE_LANES)),
            _const_spec((GATE_ROWS, tb)),
            _const_spec((1, GM_WIDTH)),
            _const_spec((1, GM_WIDTH)),
            _const_spec((HEADS, CHUNK, CHUNK)),
            _const_spec((HEADS, CHUNK, CHUNK)),
        ],
        out_specs=(tok(GM_WIDTH), tok(ML_WIDTH), tok(ML_WIDTH), tok(ML_WIDTH), tok(ML_WIDTH),
                   tok(GATE_LANES), pl.BlockSpec((GATE_ROWS, tb), lambda i: (0, i)), tok(GM_WIDTH)),
        out_shape=out_shape,
        compiler_params=pltpu.CompilerParams(dimension_semantics=("arbitrary",),
                                             vmem_limit_bytes=VMEM_LIMIT),
        name="front",
    )(x2d, nmix, w_t, wgate, wgate_t, bcol, brow, lng, lnb, mixw, mixb)


def _intra(q, ks, igcol, bcol, igrow, brow, mprev, mask):
    d = bcol + (igrow - brow)
    g = bcol + mprev
    m_t = jnp.maximum(g, jnp.max(jnp.where(mask, d, -jnp.inf), axis=-1, keepdims=True))
    w_intra = jnp.where(mask, jnp.exp(d - m_t), 0.0)
    w_inter = jnp.exp(g - m_t)
    s = _dot_nt(q, ks) * w_intra
    return s, w_inter, m_t, g


def _head_out(num, den, m_t, gain, og):
    hh = num / jnp.maximum(jnp.abs(den), jnp.exp(-m_t))
    return (og * _rms(hh, gain)).astype(BF16)


def _mlstm_sample_kernel(seq_len, q_ref, k_ref, v_ref, og_ref, gcol_ref, grow_ref, mrep_ref, c_ref, n_ref,
                         mln_ref, tri_ref, triu_ref, sel_last_ref, expand_ref, pick_last_ref, seq_sum_ref,
                         yml_ref, c_out_ref, n_out_ref, m_out_ref):
    n_seq = CHUNK // seq_len
    tri = tri_ref[...]
    mask = tri.astype(F32) > 0.0
    gcol = gcol_ref[...]
    grow = grow_ref[...]
    bcol_all = _sel_dot(tri, gcol)
    brow_all = _dot_sel(grow, triu_ref[...])
    sel_last = sel_last_ref[...]
    expand = expand_ref[...]
    pick_last = pick_last_ref[...]
    seq_sum = seq_sum_ref[...]
    row = lax.broadcasted_iota(jnp.int32, (CHUNK, HEAD_DIM), 0)
    lane = lax.broadcasted_iota(jnp.int32, (CHUNK, HEAD_DIM), 1)
    seq_rows = [(row >= i * seq_len) & (row < (i + 1) * seq_len) for i in range(n_seq)]
    head_cols = [slice(h * HEAD_DIM, (h + 1) * HEAD_DIM) for h in range(HEADS)]
    every_head = range(HEADS)
    q = [q_ref[:, cols] for cols in head_cols]
    ks = [k_ref[:, cols] for cols in head_cols]
    v = [v_ref[:, cols] for cols in head_cols]
    zero = jnp.zeros_like(q[0])
    igcol = [gcol[:, h:h + 1] for h in every_head]
    bcol = [bcol_all[:, HEADS + h:HEADS + h + 1] for h in every_head]
    intra = [_intra(q[h], ks[h], igcol[h], bcol[h], grow[h:h + 1, :], brow_all[HEADS + h:HEADS + h + 1, :],
                    mrep_ref[:, h:h + 1], mask) for h in every_head]
    s, w_inter, m_t, g = zip(*intra)
    c_all = [c_ref[:, h] for h in every_head]
    n_all = [n_ref[:, h] for h in every_head]
    qc = [_dot(jnp.concatenate([jnp.where(m, q[h], zero) for m in seq_rows], axis=1),
               c_all[h].reshape(n_seq * HEAD_DIM, HEAD_DIM).astype(BF16)) for h in every_head]
    n_rows = [_sel_dot(expand, n_all[h]) for h in every_head]
    ends = [_sel_dot(sel_last, jnp.where(lane == 0, m_t[h], jnp.where(lane == 1, g[h],
                                                                        jnp.where(lane == 2, bcol[h], 0.0))))
            for h in every_head]
    sv = [_dot(s[h].astype(BF16), v[h]) for h in every_head]
    kw, dec, upd, dec_seq, n_inc, m_seq = [], [], [], [], [], []
    for h in every_head:
        m_new, g_last, b_last = ends[h][:, 0:1], ends[h][:, 1:2], ends[h][:, 2:3]
        w_end = jnp.exp(b_last - bcol[h] + igcol[h] - m_new)
        dec.append(jnp.exp(g_last - m_new))
        kw.append(ks[h].astype(F32) * w_end)
    for h in every_head:
        v_exp = jnp.concatenate([jnp.where(m, v[h], zero) for m in seq_rows], axis=1)
        upd.append(_dot(kw[h].T.astype(BF16), v_exp))
        dec_seq.append(_sel_dot(pick_last, jnp.broadcast_to(dec[h], (CHUNK, HEAD_DIM))))
        n_inc.append(_sel_dot(seq_sum, kw[h]))
        m_seq.append(_sel_dot(pick_last, jnp.broadcast_to(m_t[h], (CHUNK, HEAD_DIM))))
    for h, cols in enumerate(head_cols):
        qn = jnp.sum(q[h].astype(F32) * n_rows[h], axis=-1, keepdims=True)
        num = sv[h] + w_inter[h] * qc[h]
        den = jnp.sum(s[h], axis=-1, keepdims=True) + w_inter[h] * qn
        yml_ref[:, cols] = _head_out(num, den, m_t[h], mln_ref[:, cols], og_ref[:, cols])
        for i in range(n_seq):
            c_out_ref[i, h] = (dec_seq[h][i:i + 1, 0:1] * c_all[h][i]
                               + upd[h][:, i * HEAD_DIM:(i + 1) * HEAD_DIM])
        n_out_ref[:, h] = dec_seq[h] * n_all[h] + n_inc[h]
        m_out_ref[:, h] = m_seq[h]


def _mlstm_sample(q, k, v, og, gcol, grow, mrep, c0, n0, mln, tri, triu, sel_last, expand, pick_last,
                  seq_sum, seq_len):
    n = q.shape[0]
    n_seq = CHUNK // seq_len
    n_batch = n // seq_len
    tok = lambda w: pl.BlockSpec((CHUNK, w), lambda i: (i, 0))
    c_spec = pl.BlockSpec((n_seq, HEADS, HEAD_DIM, HEAD_DIM), lambda i: (i, 0, 0, 0))
    n_spec = pl.BlockSpec((n_seq, HEADS, HEAD_DIM), lambda i: (i, 0, 0))
    return pl.pallas_call(
        functools.partial(_mlstm_sample_kernel, seq_len),
        grid=(n // CHUNK,),
        in_specs=[
            tok(ML_WIDTH), tok(ML_WIDTH), tok(ML_WIDTH), tok(ML_WIDTH), tok(GATE_LANES),
            pl.BlockSpec((GATE_ROWS, CHUNK), lambda i: (0, i)),
            pl.BlockSpec((CHUNK, HEADS), lambda i: (i, 0)),
            c_spec, n_spec,
            _const_spec((1, ML_WIDTH)),
            _const_spec((CHUNK, CHUNK)), _const_spec((CHUNK, CHUNK)), _const_spec((CHUNK, CHUNK)),
            _const_spec((CHUNK, n_seq)), _const_spec((n_seq, CHUNK)), _const_spec((n_seq, CHUNK)),
        ],
        out_specs=(tok(ML_WIDTH), c_spec, n_spec, n_spec),
        out_shape=(
            jax.ShapeDtypeStruct((n, ML_WIDTH), BF16),
            jax.ShapeDtypeStruct((n_batch, HEADS, HEAD_DIM, HEAD_DIM), F32),
            jax.ShapeDtypeStruct((n_batch, HEADS, HEAD_DIM), F32),
            jax.ShapeDtypeStruct((n_batch, HEADS, HEAD_DIM), F32),
        ),
        compiler_params=pltpu.CompilerParams(dimension_semantics=("arbitrary",),
                                             vmem_limit_bytes=VMEM_LIMIT),
        name="mlstm_sample",
    )(q, k, v, og, gcol, grow, mrep, c0, n0, mln, tri, triu, sel_last, expand, pick_last, seq_sum)


def _back_math(final_norm, yml_transposed, x_ref, ygm_ref, yml_ref, pe_ref, wout_ref, nffn_ref,
               wup_ref, wdown_ref, nple_ref, wpg_ref, wpp_ref, nfin_ref, out_ref):
    ml_dot = _dot_tn if yml_transposed else _dot
    h = x_ref[...] + _dot(ygm_ref[...], wout_ref[:GM_WIDTH]) + ml_dot(yml_ref[...], wout_ref[GM_WIDTH:])
    a = _rms(h, nffn_ref[...]).astype(BF16)
    ff = D_FF // FF_SPLIT

    def hidden(c):
        f = _dot(a, wup_ref[:, c * ff:(c + 1) * ff])
        return jnp.square(jnp.maximum(f, 0.0)).astype(BF16)

    def down(f, c):
        return _dot(f, wdown_ref[c * ff:(c + 1) * ff, :])

    mlp = down(hidden(0), 0)
    for c in range(1, FF_SPLIT - 1):
        mlp = mlp + down(hidden(c), c)
    tb = x_ref.shape[0]
    pieces = (slice(0, tb // 2), slice(tb // 2, 3 * tb // 4), slice(3 * tb // 4, tb))
    f_last = hidden(FF_SPLIT - 1)
    h_new = []
    for r in (pieces[0], slice(pieces[0].stop, tb)):
        h_r = h[r] + (mlp[r] + down(f_last[r], FF_SPLIT - 1))
        h_new += [h_r[p.start - r.start:p.stop - r.start] for p in pieces if r.start <= p.start < r.stop]
    emb = _dot(pe_ref[...].astype(BF16), wpp_ref[...])
    gates = [_dot(_rms(h_p, nple_ref[...]).astype(BF16), wpg_ref[...]) for h_p in h_new]
    for r, h_p, g in zip(pieces, h_new, gates):
        y = h_p + jax.nn.sigmoid(g) * emb[r]
        if final_norm:
            y = _rms(y, nfin_ref[...])
        out_ref[r, :] = y


def _back_kernel(final_norm, yml_transposed, *refs):
    _back_math(final_norm, yml_transposed, *refs)


def _back(x2d, ygm, yml, pe2d, wout, nffn, wup, wdown, nple, wpg, wpp, nfin, final_norm, tb):
    n = x2d.shape[0]
    yml_transposed = yml.shape[0] != n
    tok = lambda w: pl.BlockSpec((tb, w), lambda i: (i, 0))
    yml_spec = pl.BlockSpec((ML_WIDTH, tb), lambda i: (0, i)) if yml_transposed else tok(ML_WIDTH)
    return pl.pallas_call(
        functools.partial(_back_kernel, final_norm, yml_transposed),
        grid=(n // tb,),
        in_specs=[
            tok(D_MODEL), tok(GM_WIDTH), yml_spec, tok(PLE_DIM),
            _const_spec((GM_WIDTH + ML_WIDTH, D_MODEL)),
            _const_spec((1, D_MODEL)),
            _const_spec((D_MODEL, D_FF)), _const_spec((D_FF, D_MODEL)),
            _const_spec((1, D_MODEL)),
            _const_spec((D_MODEL, D_MODEL)), _const_spec((PLE_DIM, D_MODEL)),
            _const_spec((1, D_MODEL)),
        ],
        out_specs=tok(D_MODEL),
        out_shape=jax.ShapeDtypeStruct((n, D_MODEL), F32),
        compiler_params=pltpu.CompilerParams(dimension_semantics=("arbitrary",),
                                             vmem_limit_bytes=VMEM_LIMIT),
        name="back",
    )(x2d, ygm, yml, pe2d, wout, nffn, wup, wdown, nple, wpg, wpp, nfin)


def _const(x):
    return jnp.asarray(np.asarray(x, np.float32), BF16)


def _block_tri(block):
    r = np.arange(CHUNK)[:, None]
    c = np.arange(CHUNK)[None, :]
    return (r // block == c // block) & (c <= r)


def _token_block(n):
    return 512 if n % 512 == 0 else CHUNK


def kernel(x_prompt, x_sample, p_prompt, p_sample, state_C, state_n, state_m, norm_mix, w_in, gm_ln_g,
           gm_ln_b, gm_ws, gm_bs, ml_b_i, ml_b_f, ml_norm, w_out, norm_ffn, w_up, w_down, norm_ple,
           w_ple_gate, w_ple_proj, norm_final):
    depth = w_in.shape[0]
    batch, seq, _ = x_prompt.shape
    dec_batch, dec_seq, _ = x_sample.shape
    n_p, n_s = batch * seq, dec_batch * dec_seq
    n_seq = CHUNK // dec_seq
    assert seq % CHUNK == 0 and CHUNK % dec_seq == 0 and n_s % CHUNK == 0
    tb_p, tb_s = 2 * _token_block(seq), _token_block(n_s)

    hp = x_prompt.reshape(n_p, D_MODEL)
    hs = x_sample.reshape(n_s, D_MODEL)

    tri_p, tri_s = _block_tri(CHUNK), _block_tri(dec_seq)
    triu_p, eye = _const(tri_p.T), _const(np.eye(CHUNK))
    r = np.arange(CHUNK)
    i = np.arange(n_seq)
    sel_last = _const(r[None, :] == (r[:, None] // dec_seq) * dec_seq + dec_seq - 1)
    expand = _const(r[:, None] // dec_seq == i[None, :])
    pick_last = _const(r[None, :] == i[:, None] * dec_seq + dec_seq - 1)
    seq_sum = _const(r[None, :] // dec_seq == i[:, None])

    outs = {k: [] for k in ("Cp", "Np", "Mp", "Vp", "Cs", "Ns", "Ms", "Vs")}
    for l in range(depth):
        row = lambda a: a[l].reshape(1, -1).astype(F32)
        w_t = w_in[l].T.astype(BF16)
        wg_t = w_t[MAIN_COLS:]
        b_i, b_f = ml_b_i[l].astype(F32), ml_b_f[l].astype(F32)
        row_pad = lambda a: jnp.pad(a, ((0, GATE_ROWS - HEADS), (0, 0)))
        wgr = jnp.concatenate([row_pad(wg_t[:HEADS]), row_pad(wg_t[HEADS:])], axis=0)
        bgr = jnp.broadcast_to(jnp.concatenate([row_pad(b_i[:, None]), row_pad(b_f[:, None])], axis=0),
                               (2 * GATE_ROWS, CHUNK))
        gain_t = jnp.broadcast_to(ml_norm[l].astype(F32)[:, None], (ML_WIDTH, CHUNK))
        wgate = jnp.pad(wg_t.T, ((0, 0), (0, GATE_LANES - 2 * HEADS)))
        wgate_t = jnp.pad(wg_t, ((0, GATE_ROWS - 2 * HEADS), (0, 0)))
        gbias = jnp.concatenate([b_i, b_f])
        bcol = jnp.pad(gbias, (0, GATE_LANES - 2 * HEADS)).reshape(1, GATE_LANES)
        brow = jnp.broadcast_to(jnp.pad(gbias, (0, GATE_ROWS - 2 * HEADS)).reshape(GATE_ROWS, 1),
                                (GATE_ROWS, tb_s))
        ws = gm_ws[l]
        mixw_p = (ws[:, :CHUNK, :CHUNK] * tri_p.astype(np.float32)).astype(BF16)
        mixb_p = jnp.broadcast_to(gm_bs[l][:, :CHUNK, None], (HEADS, CHUNK, CHUNK)).astype(F32)
        mixw_s = (jnp.tile(ws[:, :dec_seq, :dec_seq], (1, n_seq, n_seq)) * tri_s.astype(np.float32)).astype(BF16)
        mixb_s = jnp.broadcast_to(jnp.tile(gm_bs[l][:, :dec_seq], (1, n_seq))[:, :, None],
                                  (HEADS, CHUNK, CHUNK)).astype(F32)
        last = l == depth - 1
        nfin = norm_final.reshape(1, D_MODEL).astype(F32)

        back_f32 = [w[l].astype(F32) for w in (w_out, w_up, w_down, w_ple_gate, w_ple_proj)]
        ygm, yml_t, vlast, cn_t, m8, wout, wup, wdown, wpg, wpp = _mixer_prompt(
            hp, row(norm_mix), w_t, wgr, bgr, row(gm_ln_g), row(gm_ln_b), mixw_p, mixb_p,
            gain_t, triu_p, eye, back_f32, batch, tb_p)

        def back(x2d, ygm, yml, pe, tb):
            return _back(x2d, ygm, yml, pe, wout, row(norm_ffn), wup, wdown, row(norm_ple),
                         wpg, wpp, nfin, last, tb)

        hp = back(hp, ygm, yml_t, p_prompt[l].reshape(n_p, PLE_DIM), tb_p)
        outs["Cp"].append(jnp.swapaxes(cn_t[:, :, :HEAD_DIM, :], -1, -2))
        outs["Np"].append(cn_t[:, :, HEAD_DIM, :])
        outs["Mp"].append(m8[:, :, 0])
        outs["Vp"].append(vlast)

        ygm, q, k, v, og, gcol, grow, vgn = _front(
            hs, row(norm_mix), w_t, wgate, wgate_t, bcol, brow, row(gm_ln_g), row(gm_ln_b), mixw_s, mixb_s,
            tb_s)
        mrep = jnp.repeat(state_m[l].astype(F32), dec_seq, axis=0)
        yml, c_new, n_new, m_new = _mlstm_sample(
            q, k, v, og, gcol, grow, mrep, state_C[l].astype(F32), state_n[l].astype(F32), row(ml_norm),
            _const(tri_s), _const(tri_s.T), sel_last, expand, pick_last, seq_sum, dec_seq)
        hs = back(hs, ygm, yml, p_sample[l].reshape(n_s, PLE_DIM), tb_s)
        outs["Cs"].append(c_new)
        outs["Ns"].append(n_new)
        outs["Ms"].append(m_new[..., 0])
        outs["Vs"].append(vgn.reshape(dec_batch, dec_seq, GM_WIDTH))

    st = lambda k: outs[k][0][None] if depth == 1 else jnp.stack(outs[k])
    return (hp.reshape(batch, seq, D_MODEL), hs.reshape(dec_batch, dec_seq, D_MODEL),
            st("Cp"), st("Np"), st("Mp"), st("Vp"), st("Cs"), st("Ns"), st("Ms"), st("Vs"))
```

```python
import functools

import jax
import numpy as np
import jax.numpy as jnp
from jax import lax
from jax.experimental import pallas as pl
from jax.experimental.pallas import tpu as pltpu

F32 = jnp.float32
BF16 = jnp.bfloat16

D_MODEL = 1024
GM_WIDTH = 512
ML_WIDTH = 512
HEADS = 4
HEAD_DIM = 128
D_FF = 4096
PLE_DIM = 256
EPS = 1e-6
CHUNK = 128
MAIN_COLS = 2 * GM_WIDTH + 4 * ML_WIDTH
Q_ROW = 2 * GM_WIDTH
K_ROW = Q_ROW + ML_WIDTH
V_ROW = K_ROW + ML_WIDTH
O_ROW = V_ROW + ML_WIDTH
W_ROWS = MAIN_COLS + 2 * HEADS
LANES_V = ML_WIDTH
LANES_O = 2 * ML_WIDTH
LANES_G = 3 * ML_WIDTH
LANES_ROWS = LANES_G + 2 * 16
GATE_LANES = 128
GATE_ROWS = 16
STATE_ROWS = HEAD_DIM + 16
FF_SPLIT = 4
BF16_ROWS = 16
C_SLOTS = 3
VMEM_LIMIT = 60 * 1024 * 1024


def _dot(a, b):
    return jnp.dot(a, b, preferred_element_type=F32)


def _dot_nt(a, b):
    return lax.dot_general(a, b, (((1,), (1,)), ((), ())), preferred_element_type=F32)


def _dot_tn(a, b):
    return lax.dot_general(a, b, (((0,), (0,)), ((), ())), preferred_element_type=F32)


def _split3(x):
    x1 = x.astype(BF16)
    r = x - x1.astype(F32)
    x2 = r.astype(BF16)
    r = r - x2.astype(F32)
    return x1, x2, r.astype(BF16)


def _sel_dot(sel, x):
    p1, p2, p3 = _split3(x)
    return _dot(sel, p1) + _dot(sel, p2) + _dot(sel, p3)


def _dot_sel(x, sel):
    p1, p2, p3 = _split3(x)
    return _dot(p1, sel) + _dot(p2, sel) + _dot(p3, sel)


def _sel_dot_nt(sel, x):
    p1, p2, p3 = _split3(x)
    return _dot_nt(sel, p1) + _dot_nt(sel, p2) + _dot_nt(sel, p3)


def _rms(x, g):
    return x * lax.rsqrt(jnp.mean(x * x, axis=-1, keepdims=True) + EPS) * g


def _log_sigmoid(x):
    return -(jnp.maximum(-x, 0.0) + jnp.log1p(jnp.exp(-jnp.abs(x))))


def _const_spec(shape):
    nd = len(shape)
    return pl.BlockSpec(shape, lambda *_: (0,) * nd, pipeline_mode=pl.Buffered(1))


def _gmlp_pieces(x_ref, nmix_ref, w_t_ref, lng_ref, lnb_ref, mixw_ref, mixb_ref, ygm_ref, vgn_ref, st):
    tb = x_ref.shape[0]

    def norm():
        st["a"] = _rms(x_ref[...], nmix_ref[...]).astype(BF16)

    def u_mm():
        st["u"] = _dot_nt(st["a"], w_t_ref[:GM_WIDTH])

    def u_act():
        st["u"] = jax.nn.gelu(st["u"])

    def v_mm():
        st["vg"] = _dot_nt(st["a"], w_t_ref[GM_WIDTH:2 * GM_WIDTH])

    def v_act():
        vg = jax.nn.gelu(st.pop("vg"))
        mu = jnp.mean(vg, axis=-1, keepdims=True)
        var = jnp.mean(jnp.square(vg - mu), axis=-1, keepdims=True)
        vgn = (vg - mu) * lax.rsqrt(var + EPS) * lng_ref[...] + lnb_ref[...]
        vgn_ref[...] = vgn[tb - vgn_ref.shape[0]:, :]
        st["vgb"] = vgn.astype(BF16)

    def mix(c):
        rows = slice(c * CHUNK, (c + 1) * CHUNK)
        for h in range(HEADS):
            cols = slice(h * HEAD_DIM, (h + 1) * HEAD_DIM)
            s = _dot(mixw_ref[h], st["vgb"][rows, cols]) + mixb_ref[h]
            ygm_ref[rows, cols] = (st["u"][rows, cols] * s).astype(BF16)

    return [norm, u_mm, u_act, v_mm, v_act] + [functools.partial(mix, c) for c in range(tb // CHUNK)]


def _mixer_prompt_kernel(steps_per_seq, n_cast, x_ref, nmix_ref, w_t_ref, wgr_ref, bgr_ref,
                         lng_ref, lnb_ref, mixw_ref, mixb_ref, gain_t_ref, triu_ref, eye_ref, *rest):
    cast_in, rest = rest[:n_cast], rest[n_cast:]
    ygm_ref, yml_t_ref, vlast_ref, cn_out_ref, m_out_ref = rest[:5]
    cast_out, (cn_scr, m_scr, w_lanes_scr) = rest[5:5 + n_cast], rest[5 + n_cast:]

    @pl.when(pl.program_id(0) == 0)
    def _():
        w_lanes_scr[:LANES_V] = w_t_ref[Q_ROW:K_ROW]
        w_lanes_scr[LANES_V:LANES_G] = w_t_ref[V_ROW:MAIN_COLS]
        w_lanes_scr[LANES_G:] = wgr_ref[...]

    tb = x_ref.shape[0]
    n_chunks = tb // CHUNK
    blocks = [slice(c * CHUNK, (c + 1) * CHUNK) for c in range(n_chunks)]
    heads = [slice(h * HEAD_DIM, (h + 1) * HEAD_DIM) for h in range(HEADS)]
    unit_ids = [(c, h) for c in range(n_chunks) for h in range(HEADS)]
    new_seq = pl.program_id(0) % steps_per_seq == 0
    triu = triu_ref[...]
    mask_t = triu.astype(F32) > 0.0
    eye = eye_ref[...]
    lane_g = lax.broadcasted_iota(jnp.int32, (GATE_ROWS, tb), 1) % CHUNK
    sub_g = lax.broadcasted_iota(jnp.int32, (GATE_ROWS, CHUNK), 0)
    ones_row = (sub_g == 0).astype(BF16)
    last = slice(CHUNK - 1, CHUNK)

    st = {}
    norm, u_mm, u_act, v_mm, v_act, *mix = _gmlp_pieces(x_ref, nmix_ref, w_t_ref, lng_ref, lnb_ref, mixw_ref,
                                                        mixb_ref, ygm_ref, vlast_ref, st)
    zero = jnp.zeros((HEAD_DIM, HEAD_DIM), BF16)

    def block_diag(x, y):
        return jnp.concatenate([jnp.concatenate([x, zero], axis=1), jnp.concatenate([zero, y], axis=1)], axis=0)

    def pair_dot(lhs, rhs):
        out = _dot(jnp.concatenate(lhs, axis=1), block_diag(*rhs))
        return out[:, :HEAD_DIM], out[:, HEAD_DIM:]

    norm()
    a = st["a"]
    u_mm()
    v_mm()
    on_lanes = _dot_nt(w_lanes_scr[...], a)
    u_act()
    v_act()
    k = (_dot_nt(a, w_t_ref[K_ROW:V_ROW]) * (HEAD_DIM ** -0.5)).astype(BF16)
    for src, dst in zip(cast_in, cast_out):
        dst[...] = src[...].astype(BF16)
    for piece in mix:
        piece()
    lanes = lambda ref: jnp.concatenate([ref[...]] * n_chunks, axis=1)
    zr = on_lanes[LANES_G:] + lanes(bgr_ref)
    lf_r = _log_sigmoid(zr[GATE_ROWS:])
    b_r = jnp.concatenate([_dot_sel(lf_r[:, t], triu) for t in blocks], axis=1)
    q_t = on_lanes[:ML_WIDTH].astype(BF16)
    r_r = zr[:GATE_ROWS] - b_r
    p_r = r_r
    shift = 1
    while shift < CHUNK:
        p_r = jnp.maximum(p_r, jnp.where(lane_g >= shift, pltpu.roll(p_r, shift, axis=1), -jnp.inf))
        shift *= 2
    e_end = [jnp.exp(r_r[:, t] - p_r[:, t][:, last]) for t in blocks]
    v_t = on_lanes[LANES_V:LANES_O]
    r_c = [_sel_dot_nt(eye, r_r[:, t]) for t in blocks]
    og_t = jax.nn.sigmoid(on_lanes[LANES_O:LANES_G]) * lanes(gain_t_ref)

    pair_ids = [(c, h) for c in range(n_chunks) for h in range(0, HEADS, 2)]
    kq, e_intra, x1, upd = {}, {}, {}, {}
    for c, h in unit_ids:
        e_intra[c, h] = jnp.where(mask_t, jnp.exp(r_c[c][:, h:h + 1] - p_r[h:h + 1, blocks[c]]), 0.0)
    for c, h in pair_ids:
        t = blocks[c]
        kq[c, h], kq[c, h + 1] = pair_dot([k[t, heads[h]], k[t, heads[h + 1]]],
                                          [q_t[heads[h], t], q_t[heads[h + 1], t]])
    for c, h in pair_ids:
        t = blocks[c]
        s0, vext, vw = [], [], []
        for g in (h, h + 1):
            s0.append((kq[c, g] * e_intra[c, g]).astype(BF16))
            vext.append(jnp.concatenate([v_t[heads[g], t].astype(BF16), ones_row], axis=0))
            e_row = e_end[c][g:g + 1, :]
            vw.append(jnp.concatenate([(v_t[heads[g], t] * e_row).astype(BF16),
                                       jnp.where(sub_g == 0, e_row, 0.0).astype(BF16)], axis=0))
        x1[c, h], x1[c, h + 1] = pair_dot(vext, s0)
        upd[c, h], upd[c, h + 1] = pair_dot(vw, [k[t, heads[h]], k[t, heads[h + 1]]])

    cn_in, cm, w_inter = {}, {}, {}
    for h in range(HEADS):
        cn = jnp.where(new_seq, 0.0, cn_scr[h])
        m_prev = jnp.where(new_seq, 0.0, m_scr[h:h + 1, 0:1])
        for c in range(n_chunks):
            prow = p_r[h:h + 1, blocks[c]]
            cn_in[c, h] = cn
            cm[c, h] = jnp.maximum(m_prev, prow)
            w_inter[c, h] = jnp.exp(m_prev - cm[c, h])
            cm_last = cm[c, h][:, last]
            cn = jnp.exp(m_prev - cm_last) * cn + jnp.exp(prow[:, last] - cm_last) * upd[c, h]
            m_prev = b_r[h:h + 1, blocks[c]][:, last] + cm_last
        cn_scr[h] = cn
        m_scr[h:h + 1, :] = jnp.broadcast_to(m_prev, (1, HEAD_DIM))
    x2 = {}

    def state_matmuls(c):
        t = blocks[c]
        for h in range(0, HEADS, 2):
            x2[c, h], x2[c, h + 1] = pair_dot([cn_in[c, h].astype(BF16), cn_in[c, h + 1].astype(BF16)],
                                              [q_t[heads[h], t], q_t[heads[h + 1], t]])

    def head_outputs(c):
        t = blocks[c]
        for h, hd in enumerate(heads):
            brow, prow = b_r[h:h + 1, t], p_r[h:h + 1, t]
            nd = x1[c, h] * jnp.exp(prow - cm[c, h]) + x2[c, h] * w_inter[c, h]
            num_t, den = nd[:HEAD_DIM], nd[HEAD_DIM:HEAD_DIM + 1]
            inv = 1.0 / jnp.maximum(jnp.abs(den), jnp.exp(-(brow + cm[c, h])))
            ssq = jnp.sum(num_t * num_t, axis=0, keepdims=True)
            scale = inv * lax.rsqrt(ssq * (inv * inv) * (1.0 / HEAD_DIM) + EPS)
            yml_t_ref[hd, t] = (num_t * scale * og_t[hd, t]).astype(BF16)

    for c in range(n_chunks):
        state_matmuls(c)
    for c in range(n_chunks):
        head_outputs(c)

    @pl.when(pl.program_id(0) % steps_per_seq == steps_per_seq - 1)
    def _():
        cn_out_ref[...] = cn_scr[...]
        m_out_ref[...] = m_scr[...]


def _mixer_prompt(x2d, nmix, w_t, wgr, bgr, lng, lnb, mixw, mixb, gain_t, triu, eye, cast, batch, tb):
    n = x2d.shape[0]
    n_steps = n // tb
    steps_per_seq = n_steps // batch

    def window(w):
        rows = max(BF16_ROWS, w.shape[0] // n_steps)
        return pl.BlockSpec((rows, w.shape[1]), lambda i: (jnp.minimum(i, w.shape[0] // rows - 1), 0))

    tok = lambda w: pl.BlockSpec((tb, w), lambda i: (i, 0))
    per_seq = lambda *shape: pl.BlockSpec((None,) + shape, lambda i: (i // steps_per_seq,) + (0,) * len(shape))
    return pl.pallas_call(
        functools.partial(_mixer_prompt_kernel, steps_per_seq, len(cast)),
        grid=(n_steps,),
        in_specs=[
            tok(D_MODEL),
            _const_spec((1, D_MODEL)),
            _const_spec((W_ROWS, D_MODEL)),
            _const_spec((2 * GATE_ROWS, D_MODEL)),
            _const_spec((2 * GATE_ROWS, CHUNK)),
            _const_spec((1, GM_WIDTH)),
            _const_spec((1, GM_WIDTH)),
            _const_spec((HEADS, CHUNK, CHUNK)),
            _const_spec((HEADS, CHUNK, CHUNK)),
            _const_spec((ML_WIDTH, CHUNK)),
            _const_spec((CHUNK, CHUNK)),
            _const_spec((CHUNK, CHUNK)),
        ] + [window(w) for w in cast],
        out_specs=(tok(GM_WIDTH), pl.BlockSpec((ML_WIDTH, tb), lambda i: (0, i)), per_seq(CHUNK, GM_WIDTH),
                   per_seq(HEADS, STATE_ROWS, HEAD_DIM), per_seq(HEADS, HEAD_DIM)) + tuple(window(w) for w in cast),
        out_shape=(
            jax.ShapeDtypeStruct((n, GM_WIDTH), BF16),
            jax.ShapeDtypeStruct((ML_WIDTH, n), BF16),
            jax.ShapeDtypeStruct((batch, CHUNK, GM_WIDTH), F32),
            jax.ShapeDtypeStruct((batch, HEADS, STATE_ROWS, HEAD_DIM), F32),
            jax.ShapeDtypeStruct((batch, HEADS, HEAD_DIM), F32),
        ) + tuple(jax.ShapeDtypeStruct(w.shape, BF16) for w in cast),
        scratch_shapes=[pltpu.VMEM((HEADS, STATE_ROWS, HEAD_DIM), F32), pltpu.VMEM((HEADS, HEAD_DIM), F32),
                        pltpu.VMEM((LANES_ROWS, D_MODEL), BF16)],
        compiler_params=pltpu.CompilerParams(dimension_semantics=("arbitrary",),
                                             vmem_limit_bytes=VMEM_LIMIT),
        name="mixer_prompt",
    )(x2d, nmix, w_t, wgr, bgr, lng, lnb, mixw, mixb, gain_t, triu, eye, *cast)


def _front_kernel(x_ref, nmix_ref, w_t_ref, wgate_ref, wgate_t_ref, bcol_ref, brow_ref,
                  lng_ref, lnb_ref, mixw_ref, mixb_ref,
                  ygm_ref, q_ref, k_ref, v_ref, og_ref, gcol_ref, grow_ref, vgn_ref):
    st = {}
    for piece in _gmlp_pieces(x_ref, nmix_ref, w_t_ref, lng_ref, lnb_ref, mixw_ref, mixb_ref,
                              ygm_ref, vgn_ref, st):
        piece()
    a = st["a"]

    def proj(lo):
        return _dot_nt(a, w_t_ref[lo:lo + ML_WIDTH])

    q_ref[...] = proj(Q_ROW).astype(BF16)
    k_ref[...] = (proj(K_ROW) * (HEAD_DIM ** -0.5)).astype(BF16)
    v_ref[...] = proj(V_ROW).astype(BF16)
    og_ref[...] = jax.nn.sigmoid(proj(O_ROW))
    zc = _dot(a, wgate_ref[...]) + bcol_ref[...]
    lane = lax.broadcasted_iota(jnp.int32, zc.shape, 1)
    gcol_ref[...] = jnp.where(lane >= HEADS, _log_sigmoid(zc), zc)
    zr = _dot_nt(wgate_t_ref[...], a) + brow_ref[...]
    sub = lax.broadcasted_iota(jnp.int32, zr.shape, 0)
    grow_ref[...] = jnp.where(sub >= HEADS, _log_sigmoid(zr), zr)


def _front(x2d, nmix, w_t, wgate, wgate_t, bcol, brow, lng, lnb, mixw, mixb, tb):
    n = x2d.shape[0]
    tok = lambda w: pl.BlockSpec((tb, w), lambda i: (i, 0))
    out_shape = (
        jax.ShapeDtypeStruct((n, GM_WIDTH), BF16),
        jax.ShapeDtypeStruct((n, ML_WIDTH), BF16),
        jax.ShapeDtypeStruct((n, ML_WIDTH), BF16),
        jax.ShapeDtypeStruct((n, ML_WIDTH), BF16),
        jax.ShapeDtypeStruct((n, ML_WIDTH), F32),
        jax.ShapeDtypeStruct((n, GATE_LANES), F32),
        jax.ShapeDtypeStruct((GATE_ROWS, n), F32),
        jax.ShapeDtypeStruct((n, GM_WIDTH), F32),
    )
    return pl.pallas_call(
        _front_kernel,
        grid=(n // tb,),
        in_specs=[
            tok(D_MODEL),
            _const_spec((1, D_MODEL)),
            _const_spec((W_ROWS, D_MODEL)),
            _const_spec((D_MODEL, GATE_LANES)),
            _const_spec((GATE_ROWS, D_MODEL)),
            _const_spec((1, GATE_LANES)),
            _const_spec((GATE_ROWS, tb)),
            _const_spec((1, GM_WIDTH)),
            _const_spec((1, GM_WIDTH)),
            _const_spec((HEADS, CHUNK, CHUNK)),
            _const_spec((HEADS, CHUNK, CHUNK)),
        ],
        out_specs=(tok(GM_WIDTH), tok(ML_WIDTH), tok(ML_WIDTH), tok(ML_WIDTH), tok(ML_WIDTH),
                   tok(GATE_LANES), pl.BlockSpec((GATE_ROWS, tb), lambda i: (0, i)), tok(GM_WIDTH)),
        out_shape=out_shape,
        compiler_params=pltpu.CompilerParams(dimension_semantics=("arbitrary",),
                                             vmem_limit_bytes=VMEM_LIMIT),
        name="front",
    )(x2d, nmix, w_t, wgate, wgate_t, bcol, brow, lng, lnb, mixw, mixb)


def _intra(q, ks, igcol, bcol, igrow, brow, mprev, mask):
    d = bcol + (igrow - brow)
    g = bcol + mprev
    m_t = jnp.maximum(g, jnp.max(jnp.where(mask, d, -jnp.inf), axis=-1, keepdims=True))
    w_intra = jnp.where(mask, jnp.exp(d - m_t), 0.0)
    w_inter = jnp.exp(g - m_t)
    s = _dot_nt(q, ks) * w_intra
    return s, w_inter, m_t, g


def _head_out(num, den, m_t, gain, og):
    hh = num / jnp.maximum(jnp.abs(den), jnp.exp(-m_t))
    return (og * _rms(hh, gain)).astype(BF16)


def _mlstm_sample_kernel(seq_len, n_steps, q_ref, k_ref, v_ref, og_ref, gcol_ref, grow_ref, mrep_ref, c_hbm, n_ref,
                         mln_ref, tri_ref, triu_ref, sel_last_ref, expand_ref, pick_last_ref, seq_sum_ref,
                         yml_ref, c_out_ref, n_out_ref, m_out_ref, c_ring, c_sem):
    n_seq = CHUNK // seq_len
    step = pl.program_id(0)

    def c_fetch(s):
        slot = s % C_SLOTS
        return pltpu.make_async_copy(c_hbm.at[pl.ds(s * n_seq, n_seq)], c_ring.at[slot], c_sem.at[slot])

    @pl.when(step == 0)
    def _():
        for s in range(min(C_SLOTS - 1, n_steps)):
            c_fetch(s).start()

    @pl.when(step + C_SLOTS - 1 < n_steps)
    def _():
        c_fetch(step + C_SLOTS - 1).start()

    c_fetch(step).wait()
    c_ref = c_ring.at[step % C_SLOTS]
    tri = tri_ref[...]
    mask = tri.astype(F32) > 0.0
    gcol = gcol_ref[...]
    grow = grow_ref[...]
    bcol_all = _sel_dot(tri, gcol)
    brow_all = _dot_sel(grow, triu_ref[...])
    sel_last = sel_last_ref[...]
    expand = expand_ref[...]
    pick_last = pick_last_ref[...]
    seq_sum = seq_sum_ref[...]
    row = lax.broadcasted_iota(jnp.int32, (CHUNK, HEAD_DIM), 0)
    lane = lax.broadcasted_iota(jnp.int32, (CHUNK, HEAD_DIM), 1)
    seq_rows = [(row >= i * seq_len) & (row < (i + 1) * seq_len) for i in range(n_seq)]
    head_cols = [slice(h * HEAD_DIM, (h + 1) * HEAD_DIM) for h in range(HEADS)]
    every_head = range(HEADS)
    q = [q_ref[:, cols] for cols in head_cols]
    ks = [k_ref[:, cols] for cols in head_cols]
    v = [v_ref[:, cols] for cols in head_cols]
    zero = jnp.zeros_like(q[0])
    igcol = [gcol[:, h:h + 1] for h in every_head]
    bcol = [bcol_all[:, HEADS + h:HEADS + h + 1] for h in every_head]
    intra = [_intra(q[h], ks[h], igcol[h], bcol[h], grow[h:h + 1, :], brow_all[HEADS + h:HEADS + h + 1, :],
                    mrep_ref[:, h:h + 1], mask) for h in every_head]
    s, w_inter, m_t, g = zip(*intra)
    c_all = [c_ref[:, h] for h in every_head]
    n_all = [n_ref[:, h] for h in every_head]
    qc = [_dot(jnp.concatenate([jnp.where(m, q[h], zero) for m in seq_rows], axis=1),
               c_all[h].reshape(n_seq * HEAD_DIM, HEAD_DIM).astype(BF16)) for h in every_head]
    n_rows = [_sel_dot(expand, n_all[h]) for h in every_head]
    ends = [_sel_dot(sel_last, jnp.where(lane == 0, m_t[h], jnp.where(lane == 1, g[h],
                                                                        jnp.where(lane == 2, bcol[h], 0.0))))
            for h in every_head]
    sv = [_dot(s[h].astype(BF16), v[h]) for h in every_head]
    kw, dec, upd, dec_seq, n_inc, m_seq = [], [], [], [], [], []
    for h in every_head:
        m_new, g_last, b_last = ends[h][:, 0:1], ends[h][:, 1:2], ends[h][:, 2:3]
        w_end = jnp.exp(b_last - bcol[h] + igcol[h] - m_new)
        dec.append(jnp.exp(g_last - m_new))
        kw.append(ks[h].astype(F32) * w_end)
    for h in every_head:
        v_exp = jnp.concatenate([jnp.where(m, v[h], zero) for m in seq_rows], axis=1)
        upd.append(_dot(kw[h].T.astype(BF16), v_exp))
        dec_seq.append(_sel_dot(pick_last, jnp.broadcast_to(dec[h], (CHUNK, HEAD_DIM))))
        n_inc.append(_sel_dot(seq_sum, kw[h]))
        m_seq.append(_sel_dot(pick_last, jnp.broadcast_to(m_t[h], (CHUNK, HEAD_DIM))))
    for h, cols in enumerate(head_cols):
        qn = jnp.sum(q[h].astype(F32) * n_rows[h], axis=-1, keepdims=True)
        num = sv[h] + w_inter[h] * qc[h]
        den = jnp.sum(s[h], axis=-1, keepdims=True) + w_inter[h] * qn
        yml_ref[:, cols] = _head_out(num, den, m_t[h], mln_ref[:, cols], og_ref[:, cols])
        for i in range(n_seq):
            c_out_ref[i, h] = (dec_seq[h][i:i + 1, 0:1] * c_all[h][i]
                               + upd[h][:, i * HEAD_DIM:(i + 1) * HEAD_DIM])
        n_out_ref[:, h] = dec_seq[h] * n_all[h] + n_inc[h]
        m_out_ref[:, h] = m_seq[h]


def _mlstm_sample(q, k, v, og, gcol, grow, mrep, c0, n0, mln, tri, triu, sel_last, expand, pick_last,
                  seq_sum, seq_len):
    n = q.shape[0]
    n_seq = CHUNK // seq_len
    n_batch = n // seq_len
    tok = lambda w: pl.BlockSpec((CHUNK, w), lambda i: (i, 0))
    c_spec = pl.BlockSpec((n_seq, HEADS, HEAD_DIM, HEAD_DIM), lambda i: (i, 0, 0, 0))
    n_spec = pl.BlockSpec((n_seq, HEADS, HEAD_DIM), lambda i: (i, 0, 0))
    return pl.pallas_call(
        functools.partial(_mlstm_sample_kernel, seq_len, n // CHUNK),
        grid=(n // CHUNK,),
        in_specs=[
            tok(ML_WIDTH), tok(ML_WIDTH), tok(ML_WIDTH), tok(ML_WIDTH), tok(GATE_LANES),
            pl.BlockSpec((GATE_ROWS, CHUNK), lambda i: (0, i)),
            pl.BlockSpec((CHUNK, HEADS), lambda i: (i, 0)),
            pl.BlockSpec(memory_space=pl.ANY), n_spec,
            _const_spec((1, ML_WIDTH)),
            _const_spec((CHUNK, CHUNK)), _const_spec((CHUNK, CHUNK)), _const_spec((CHUNK, CHUNK)),
            _const_spec((CHUNK, n_seq)), _const_spec((n_seq, CHUNK)), _const_spec((n_seq, CHUNK)),
        ],
        out_specs=(tok(ML_WIDTH), c_spec, n_spec, n_spec),
        out_shape=(
            jax.ShapeDtypeStruct((n, ML_WIDTH), BF16),
            jax.ShapeDtypeStruct((n_batch, HEADS, HEAD_DIM, HEAD_DIM), F32),
            jax.ShapeDtypeStruct((n_batch, HEADS, HEAD_DIM), F32),
            jax.ShapeDtypeStruct((n_batch, HEADS, HEAD_DIM), F32),
        ),
        scratch_shapes=[pltpu.VMEM((C_SLOTS, n_seq, HEADS, HEAD_DIM, HEAD_DIM), F32),
                        pltpu.SemaphoreType.DMA((C_SLOTS,))],
        compiler_params=pltpu.CompilerParams(dimension_semantics=("arbitrary",),
                                             vmem_limit_bytes=VMEM_LIMIT),
        name="mlstm_sample",
    )(q, k, v, og, gcol, grow, mrep, c0, n0, mln, tri, triu, sel_last, expand, pick_last, seq_sum)


def _back_math(final_norm, yml_transposed, x_ref, ygm_ref, yml_ref, pe_ref, wout_ref, nffn_ref,
               wup_ref, wdown_ref, nple_ref, wpg_ref, wpp_ref, nfin_ref, out_ref):
    ml_dot = _dot_tn if yml_transposed else _dot
    h = x_ref[...] + _dot(ygm_ref[...], wout_ref[:GM_WIDTH]) + ml_dot(yml_ref[...], wout_ref[GM_WIDTH:])
    a = _rms(h, nffn_ref[...]).astype(BF16)
    ff = D_FF // FF_SPLIT

    def hidden(c):
        f = _dot(a, wup_ref[:, c * ff:(c + 1) * ff])
        return jnp.square(jnp.maximum(f, 0.0)).astype(BF16)

    def down(f, c):
        return _dot(f, wdown_ref[c * ff:(c + 1) * ff, :])

    mlp = down(hidden(0), 0)
    for c in range(1, FF_SPLIT - 1):
        mlp = mlp + down(hidden(c), c)
    tb = x_ref.shape[0]
    pieces = (slice(0, tb // 2), slice(tb // 2, 3 * tb // 4), slice(3 * tb // 4, tb))
    f_last = hidden(FF_SPLIT - 1)
    h_new = []
    for r in (pieces[0], slice(pieces[0].stop, tb)):
        h_r = h[r] + (mlp[r] + down(f_last[r], FF_SPLIT - 1))
        h_new += [h_r[p.start - r.start:p.stop - r.start] for p in pieces if r.start <= p.start < r.stop]
    emb = _dot(pe_ref[...].astype(BF16), wpp_ref[...])
    gates = [_dot(_rms(h_p, nple_ref[...]).astype(BF16), wpg_ref[...]) for h_p in h_new]
    for r, h_p, g in zip(pieces, h_new, gates):
        y = h_p + jax.nn.sigmoid(g) * emb[r]
        if final_norm:
            y = _rms(y, nfin_ref[...])
        out_ref[r, :] = y


def _back_kernel(final_norm, yml_transposed, *refs):
    _back_math(final_norm, yml_transposed, *refs)


def _back(x2d, ygm, yml, pe2d, wout, nffn, wup, wdown, nple, wpg, wpp, nfin, final_norm, tb):
    n = x2d.shape[0]
    yml_transposed = yml.shape[0] != n
    tok = lambda w: pl.BlockSpec((tb, w), lambda i: (i, 0))
    yml_spec = pl.BlockSpec((ML_WIDTH, tb), lambda i: (0, i)) if yml_transposed else tok(ML_WIDTH)
    return pl.pallas_call(
        functools.partial(_back_kernel, final_norm, yml_transposed),
        grid=(n // tb,),
        in_specs=[
            tok(D_MODEL), tok(GM_WIDTH), yml_spec, tok(PLE_DIM),
            _const_spec((GM_WIDTH + ML_WIDTH, D_MODEL)),
            _const_spec((1, D_MODEL)),
            _const_spec((D_MODEL, D_FF)), _const_spec((D_FF, D_MODEL)),
            _const_spec((1, D_MODEL)),
            _const_spec((D_MODEL, D_MODEL)), _const_spec((PLE_DIM, D_MODEL)),
            _const_spec((1, D_MODEL)),
        ],
        out_specs=tok(D_MODEL),
        out_shape=jax.ShapeDtypeStruct((n, D_MODEL), F32),
        compiler_params=pltpu.CompilerParams(dimension_semantics=("arbitrary",),
                                             vmem_limit_bytes=VMEM_LIMIT),
        name="back",
    )(x2d, ygm, yml, pe2d, wout, nffn, wup, wdown, nple, wpg, wpp, nfin)


def _const(x):
    return jnp.asarray(np.asarray(x, np.float32), BF16)


def _block_tri(block):
    r = np.arange(CHUNK)[:, None]
    c = np.arange(CHUNK)[None, :]
    return (r // block == c // block) & (c <= r)


def _token_block(n):
    return 512 if n % 512 == 0 else CHUNK


def kernel(x_prompt, x_sample, p_prompt, p_sample, state_C, state_n, state_m, norm_mix, w_in, gm_ln_g,
           gm_ln_b, gm_ws, gm_bs, ml_b_i, ml_b_f, ml_norm, w_out, norm_ffn, w_up, w_down, norm_ple,
           w_ple_gate, w_ple_proj, norm_final):
    depth = w_in.shape[0]
    batch, seq, _ = x_prompt.shape
    dec_batch, dec_seq, _ = x_sample.shape
    n_p, n_s = batch * seq, dec_batch * dec_seq
    n_seq = CHUNK // dec_seq
    assert seq % CHUNK == 0 and CHUNK % dec_seq == 0 and n_s % CHUNK == 0
    tb_p, tb_s = 2 * _token_block(seq), _token_block(n_s)

    hp = x_prompt.reshape(n_p, D_MODEL)
    hs = x_sample.reshape(n_s, D_MODEL)

    tri_p, tri_s = _block_tri(CHUNK), _block_tri(dec_seq)
    triu_p, eye = _const(tri_p.T), _const(np.eye(CHUNK))
    r = np.arange(CHUNK)
    i = np.arange(n_seq)
    sel_last = _const(r[None, :] == (r[:, None] // dec_seq) * dec_seq + dec_seq - 1)
    expand = _const(r[:, None] // dec_seq == i[None, :])
    pick_last = _const(r[None, :] == i[:, None] * dec_seq + dec_seq - 1)
    seq_sum = _const(r[None, :] // dec_seq == i[:, None])

    outs = {k: [] for k in ("Cp", "Np", "Mp", "Vp", "Cs", "Ns", "Ms", "Vs")}
    for l in range(depth):
        row = lambda a: a[l].reshape(1, -1).astype(F32)
        w_t = w_in[l].T.astype(BF16)
        wg_t = w_t[MAIN_COLS:]
        b_i, b_f = ml_b_i[l].astype(F32), ml_b_f[l].astype(F32)
        row_pad = lambda a: jnp.pad(a, ((0, GATE_ROWS - HEADS), (0, 0)))
        wgr = jnp.concatenate([row_pad(wg_t[:HEADS]), row_pad(wg_t[HEADS:])], axis=0)
        bgr = jnp.broadcast_to(jnp.concatenate([row_pad(b_i[:, None]), row_pad(b_f[:, None])], axis=0),
                               (2 * GATE_ROWS, CHUNK))
        gain_t = jnp.broadcast_to(ml_norm[l].astype(F32)[:, None], (ML_WIDTH, CHUNK))
        wgate = jnp.pad(wg_t.T, ((0, 0), (0, GATE_LANES - 2 * HEADS)))
        wgate_t = jnp.pad(wg_t, ((0, GATE_ROWS - 2 * HEADS), (0, 0)))
        gbias = jnp.concatenate([b_i, b_f])
        bcol = jnp.pad(gbias, (0, GATE_LANES - 2 * HEADS)).reshape(1, GATE_LANES)
        brow = jnp.broadcast_to(jnp.pad(gbias, (0, GATE_ROWS - 2 * HEADS)).reshape(GATE_ROWS, 1),
                                (GATE_ROWS, tb_s))
        ws = gm_ws[l]
        mixw_p = (ws[:, :CHUNK, :CHUNK] * tri_p.astype(np.float32)).astype(BF16)
        mixb_p = jnp.broadcast_to(gm_bs[l][:, :CHUNK, None], (HEADS, CHUNK, CHUNK)).astype(F32)
        mixw_s = (jnp.tile(ws[:, :dec_seq, :dec_seq], (1, n_seq, n_seq)) * tri_s.astype(np.float32)).astype(BF16)
        mixb_s = jnp.broadcast_to(jnp.tile(gm_bs[l][:, :dec_seq], (1, n_seq))[:, :, None],
                                  (HEADS, CHUNK, CHUNK)).astype(F32)
        last = l == depth - 1
        nfin = norm_final.reshape(1, D_MODEL).astype(F32)

        back_f32 = [w[l].astype(F32) for w in (w_out, w_up, w_down, w_ple_gate, w_ple_proj)]
        ygm, yml_t, vlast, cn_t, m8, wout, wup, wdown, wpg, wpp = _mixer_prompt(
            hp, row(norm_mix), w_t, wgr, bgr, row(gm_ln_g), row(gm_ln_b), mixw_p, mixb_p,
            gain_t, triu_p, eye, back_f32, batch, tb_p)

        def back(x2d, ygm, yml, pe, tb):
            return _back(x2d, ygm, yml, pe, wout, row(norm_ffn), wup, wdown, row(norm_ple),
                         wpg, wpp, nfin, last, tb)

        hp = back(hp, ygm, yml_t, p_prompt[l].reshape(n_p, PLE_DIM), tb_p)
        outs["Cp"].append(jnp.swapaxes(cn_t[:, :, :HEAD_DIM, :], -1, -2))
        outs["Np"].append(cn_t[:, :, HEAD_DIM, :])
        outs["Mp"].append(m8[:, :, 0])
        outs["Vp"].append(vlast)

        ygm, q, k, v, og, gcol, grow, vgn = _front(
            hs, row(norm_mix), w_t, wgate, wgate_t, bcol, brow, row(gm_ln_g), row(gm_ln_b), mixw_s, mixb_s,
            tb_s)
        mrep = jnp.repeat(state_m[l].astype(F32), dec_seq, axis=0)
        yml, c_new, n_new, m_new = _mlstm_sample(
            q, k, v, og, gcol, grow, mrep, state_C[l].astype(F32), state_n[l].astype(F32), row(ml_norm),
            _const(tri_s), _const(tri_s.T), sel_last, expand, pick_last, seq_sum, dec_seq)
        hs = back(hs, ygm, yml, p_sample[l].reshape(n_s, PLE_DIM), tb_s)
        outs["Cs"].append(c_new)
        outs["Ns"].append(n_new)
        outs["Ms"].append(m_new[..., 0])
        outs["Vs"].append(vgn.reshape(dec_batch, dec_seq, GM_WIDTH))

    st = lambda k: outs[k][0][None] if depth == 1 else jnp.stack(outs[k])
    return (hp.reshape(batch, seq, D_MODEL), hs.reshape(dec_batch, dec_seq, D_MODEL),
            st("Cp"), st("Np"), st("Mp"), st("Vp"), st("Cs"), st("Ns"), st("Ms"), st("Vs"))
```
